```python
import jax, jax.numpy as jnp
from jax import lax
import numpy as np

D_MODEL = 2048
BATCH = 8
SEQ = 4096
DEPTH = 4

N_MIXERS = 3
EPS = 1e-6
SB_HEADS = 16
SB_HEAD_DIM = D_MODEL // SB_HEADS
SB_WIDTH = SB_HEADS * SB_HEAD_DIM
Q_BLOCK = 128
GM_CHUNK = 128
GM_GROUPS = 16
GM_WIDTH = D_MODEL
GM_GROUP_DIM = GM_WIDTH // GM_GROUPS
SSD_INNER = 2 * D_MODEL
SSD_HEAD_DIM = 64
SSD_HEADS = SSD_INNER // SSD_HEAD_DIM
SSD_GROUPS = 8
SSD_HPG = SSD_HEADS // SSD_GROUPS
SSD_STATE = 128
SSD_CONV = 4
SSD_CHUNK = 128
SSD_CONV_DIM = SSD_INNER + 2 * SSD_GROUPS * SSD_STATE
SSD_PROJ = SSD_INNER + SSD_CONV_DIM + SSD_HEADS
MLP_HIDDEN = 4 * D_MODEL
N_A = len(range(0, DEPTH, N_MIXERS))
N_B = len(range(1, DEPTH, N_MIXERS))
N_C = len(range(2, DEPTH, N_MIXERS))

kernel_name = "hybrid_sb_gmlp_ssd_trunk"


def rms_norm(x, g):
    xf = x.astype(jnp.float32)
    y = xf * lax.rsqrt(jnp.mean(xf * xf, axis=-1, keepdims=True) + EPS)
    return (y * g.astype(jnp.float32)).astype(x.dtype)


def stick_breaking_attention(h, w_qkv, q_g, k_g, w_o):
    b, s, _ = h.shape
    q, k, v = jnp.split(h @ w_qkv, 3, axis=-1)
    q = rms_norm(q.reshape(b, s, SB_HEADS, SB_HEAD_DIM), q_g).transpose(0, 2, 1, 3)
    k = rms_norm(k.reshape(b, s, SB_HEADS, SB_HEAD_DIM), k_g).transpose(0, 2, 1, 3)
    v = v.reshape(b, s, SB_HEADS, SB_HEAD_DIM).transpose(0, 2, 1, 3)
    scale = SB_HEAD_DIM ** -0.5
    n_blk = s // Q_BLOCK
    q_blocks = q.reshape(b, SB_HEADS, n_blk, Q_BLOCK, SB_HEAD_DIM).transpose(2, 0, 1, 3, 4)
    k_pos = jnp.arange(s)

    def one_block(args):
        qb, start = args
        z = jnp.einsum('bhqd,bhkd->bhqk', qb, k).astype(jnp.float32) * scale
        q_pos = start + jnp.arange(Q_BLOCK)
        mask = k_pos[None, :] < q_pos[:, None]
        log_beta = jax.nn.log_sigmoid(z)
        log_1m = jnp.where(mask, jax.nn.log_sigmoid(-z), 0.0)
        suffix = lax.cumsum(log_1m, axis=3, reverse=True) - log_1m
        a = jnp.where(mask, jnp.exp(log_beta + suffix), 0.0)
        return jnp.einsum('bhqk,bhkd->bhqd', a.astype(v.dtype), v)

    starts = jnp.arange(n_blk, dtype=jnp.int32) * Q_BLOCK
    o = lax.map(one_block, (q_blocks, starts))
    o = o.transpose(1, 0, 3, 2, 4).reshape(b, s, SB_WIDTH)
    return o @ w_o


def chunked_spatial_gating(h, w_in, v_g, w_s, b_s, w_o):
    b, s, _ = h.shape
    u, v = jnp.split(jax.nn.gelu(h @ w_in, approximate=False), 2, axis=-1)
    v = rms_norm(v, v_g)
    n_chunk = s // GM_CHUNK
    v = v.reshape(b, n_chunk, GM_CHUNK, GM_GROUPS, GM_GROUP_DIM)
    u = u.reshape(b, n_chunk, GM_CHUNK, GM_GROUPS, GM_GROUP_DIM)
    causal = jnp.tril(jnp.ones((GM_CHUNK, GM_CHUNK), dtype=bool))
    w = jnp.where(causal[None], w_s, 0.0)
    mixed = jnp.einsum('gts,bcsgd->bctgd', w.astype(v.dtype), v) + b_s.T[:, :, None]
    y = (u * mixed).reshape(b, s, GM_WIDTH)
    return y @ w_o


def ssd_chunked_scan(x, dt, a, bm, cm):
    b, s = x.shape[:2]
    c, L = s // SSD_CHUNK, SSD_CHUNK
    xs = (x * dt[..., None]).reshape(b, c, L, SSD_GROUPS, SSD_HPG, SSD_HEAD_DIM)
    da = (dt * a).reshape(b, c, L, SSD_GROUPS, SSD_HPG).transpose(0, 3, 4, 1, 2)
    bc = bm.reshape(b, c, L, SSD_GROUPS, SSD_STATE)
    cc = cm.reshape(b, c, L, SSD_GROUPS, SSD_STATE)
    a_cum = jnp.cumsum(da, axis=-1)
    seg = a_cum[..., :, None] - a_cum[..., None, :]
    tri = jnp.tril(jnp.ones((L, L), dtype=bool))
    decay = jnp.exp(jnp.where(tri, seg, -jnp.inf))
    cb = jnp.einsum('bclgn,bcsgn->bgcls', cc, bc)
    y_diag = jnp.einsum('bgcls,bgrcls,bcsgrp->bclgrp', cb, decay, xs)
    decay_states = jnp.exp(a_cum[..., -1:] - a_cum)
    states = jnp.einsum('bclgn,bgrcl,bclgrp->cbgrpn', bc, decay_states, xs)
    chunk_decay = jnp.exp(a_cum[..., -1]).transpose(3, 0, 1, 2)

    def step(carry, inp):
        st, dec = inp
        return carry * dec[..., None, None] + st, carry

    _, prev = lax.scan(step, jnp.zeros_like(states[0]), (states, chunk_decay))
    y_off = jnp.einsum('bclgn,cbgrpn,bgrcl->bclgrp', cc, prev, jnp.exp(a_cum))
    return (y_diag + y_off).reshape(b, s, SSD_GROUPS, SSD_HPG, SSD_HEAD_DIM)


def ssd_mixer(h, w_in, conv_w, conv_b, dt_bias, a_log, d_skip, norm_g, w_o):
    b, s, _ = h.shape
    z, xbc, dt = jnp.split(h @ w_in, [SSD_INNER, SSD_INNER + SSD_CONV_DIM], axis=-1)
    xbc = lax.conv_general_dilated(xbc, conv_w[:, None, :].astype(xbc.dtype), window_strides=(1,),
                                   padding=[(SSD_CONV - 1, 0)],
                                   dimension_numbers=('NWC', 'WIO', 'NWC'),
                                   feature_group_count=SSD_CONV_DIM) + conv_b
    xbc = jax.nn.silu(xbc)
    xi, bm, cm = jnp.split(xbc, [SSD_INNER, SSD_INNER + SSD_GROUPS * SSD_STATE], axis=-1)
    f32 = jnp.float32
    dt = jax.nn.softplus(dt.astype(f32) + dt_bias.astype(f32))
    a = -jnp.exp(a_log.astype(f32))
    xi = xi.astype(f32).reshape(b, s, SSD_GROUPS, SSD_HPG, SSD_HEAD_DIM)
    y = ssd_chunked_scan(xi, dt.reshape(b, s, SSD_GROUPS, SSD_HPG), a.reshape(SSD_GROUPS, SSD_HPG),
                         bm.astype(f32).reshape(b, s, SSD_GROUPS, SSD_STATE),
                         cm.astype(f32).reshape(b, s, SSD_GROUPS, SSD_STATE))
    y = y + d_skip.astype(f32).reshape(SSD_GROUPS, SSD_HPG)[:, :, None] * xi
    y = y.reshape(b, s, SSD_INNER).astype(h.dtype) * jax.nn.silu(z)
    y = rms_norm(y.reshape(b, s, SSD_GROUPS, SSD_INNER // SSD_GROUPS),
                 norm_g.reshape(SSD_GROUPS, SSD_INNER // SSD_GROUPS)).reshape(b, s, SSD_INNER)
    return y @ w_o


def squared_relu_mlp(h, w_in, w_out):
    return jnp.square(jax.nn.relu(h @ w_in)) @ w_out


def _fwd_setup_inputs(seed: int = 0) -> dict:
    key = jax.random.key(seed)
    ks = jax.random.split(key, 24)
    nrm = jax.random.normal
    f32 = jnp.float32
    d = D_MODEL
    dt = jnp.exp(jax.random.uniform(ks[14], (N_C, SSD_HEADS), f32, np.log(1e-3), np.log(1e-1)))
    return {
        "x": nrm(ks[0], (BATCH, SEQ, d), f32),
        "norm_mix_g": 1.0 + 0.02 * nrm(ks[1], (DEPTH, d), f32),
        "norm_mlp_g": 1.0 + 0.02 * nrm(ks[2], (DEPTH, d), f32),
        "sb_w_qkv": nrm(ks[3], (N_A, d, 3 * SB_WIDTH), f32) * d ** -0.5,
        "sb_q_norm_g": 1.0 + 0.02 * nrm(ks[4], (N_A, SB_HEAD_DIM), f32),
        "sb_k_norm_g": 1.0 + 0.02 * nrm(ks[5], (N_A, SB_HEAD_DIM), f32),
        "sb_w_o": nrm(ks[6], (N_A, SB_WIDTH, d), f32) * SB_WIDTH ** -0.5,
        "gm_w_in": nrm(ks[7], (N_B, d, 2 * GM_WIDTH), f32) * d ** -0.5,
        "gm_v_norm_g": 1.0 + 0.02 * nrm(ks[8], (N_B, GM_WIDTH), f32),
        "gm_w_s": nrm(ks[9], (N_B, GM_GROUPS, GM_CHUNK, GM_CHUNK), f32) * (1.0 / GM_CHUNK),
        "gm_b_s": 1.0 + 0.02 * nrm(ks[10], (N_B, GM_GROUPS, GM_CHUNK), f32),
        "gm_w_o": nrm(ks[11], (N_B, GM_WIDTH, d), f32) * GM_WIDTH ** -0.5,
        "ssd_w_in": nrm(ks[12], (N_C, d, SSD_PROJ), f32) * d ** -0.5,
        "ssd_conv_w": nrm(ks[13], (N_C, SSD_CONV, SSD_CONV_DIM), f32) * SSD_CONV ** -0.5,
        "ssd_conv_b": 0.02 * nrm(ks[15], (N_C, SSD_CONV_DIM), f32),
        "ssd_dt_bias": dt + jnp.log(-jnp.expm1(-dt)),
        "ssd_a_log": jnp.log(jax.random.uniform(ks[16], (N_C, SSD_HEADS), f32, 1.0, 16.0)),
        "ssd_d": 1.0 + 0.02 * nrm(ks[17], (N_C, SSD_HEADS), f32),
        "ssd_norm_g": 1.0 + 0.02 * nrm(ks[18], (N_C, SSD_INNER), f32),
        "ssd_w_o": nrm(ks[19], (N_C, SSD_INNER, d), f32) * SSD_INNER ** -0.5,
        "mlp_w_in": nrm(ks[20], (DEPTH, d, MLP_HIDDEN), f32) * d ** -0.5,
        "mlp_w_out": nrm(ks[21], (DEPTH, MLP_HIDDEN, d), f32) * MLP_HIDDEN ** -0.5,
    }


def _fwd_reference(x, norm_mix_g, norm_mlp_g, sb_w_qkv, sb_q_norm_g, sb_k_norm_g, sb_w_o,
              gm_w_in, gm_v_norm_g, gm_w_s, gm_b_s, gm_w_o,
              ssd_w_in, ssd_conv_w, ssd_conv_b, ssd_dt_bias, ssd_a_log, ssd_d, ssd_norm_g, ssd_w_o,
              mlp_w_in, mlp_w_out):
    h = x
    for i in range(DEPTH):
        kind, j = i % N_MIXERS, i // N_MIXERS
        hn = rms_norm(h, norm_mix_g[i])
        if kind == 0:
            mix = stick_breaking_attention(hn, sb_w_qkv[j], sb_q_norm_g[j], sb_k_norm_g[j], sb_w_o[j])
        elif kind == 1:
            mix = chunked_spatial_gating(hn, gm_w_in[j], gm_v_norm_g[j], gm_w_s[j], gm_b_s[j], gm_w_o[j])
        else:
            mix = ssd_mixer(hn, ssd_w_in[j], ssd_conv_w[j], ssd_conv_b[j], ssd_dt_bias[j],
                            ssd_a_log[j], ssd_d[j], ssd_norm_g[j], ssd_w_o[j])
        h = h + mix
        h = h + squared_relu_mlp(rms_norm(h, norm_mlp_g[i]), mlp_w_in[i], mlp_w_out[i])
    return h


import jax as _jax
import jax.numpy as _jnp

TWIN_FORMAT = 'train_step'
FWD_PARAMS = ['x', 'norm_mix_g', 'norm_mlp_g', 'sb_w_qkv', 'sb_q_norm_g', 'sb_k_norm_g', 'sb_w_o', 'gm_w_in', 'gm_v_norm_g', 'gm_w_s', 'gm_b_s', 'gm_w_o', 'ssd_w_in', 'ssd_conv_w', 'ssd_conv_b', 'ssd_dt_bias', 'ssd_a_log', 'ssd_d', 'ssd_norm_g', 'ssd_w_o', 'mlp_w_in', 'mlp_w_out']
TWIN_WEIGHTS = ['norm_mix_g', 'norm_mlp_g', 'sb_w_qkv', 'sb_q_norm_g', 'sb_k_norm_g', 'sb_w_o', 'gm_w_in', 'gm_v_norm_g', 'gm_w_s', 'gm_b_s', 'gm_w_o', 'ssd_w_in', 'ssd_conv_w', 'ssd_conv_b', 'ssd_dt_bias', 'ssd_a_log', 'ssd_d', 'ssd_norm_g', 'ssd_w_o', 'mlp_w_in', 'mlp_w_out']
TWIN_DIFF_INPUT = 'x'
TWIN_INPUTS = ['x', 'norm_mix_g', 'norm_mlp_g', 'sb_w_qkv', 'sb_q_norm_g', 'sb_k_norm_g', 'sb_w_o', 'gm_w_in', 'gm_v_norm_g', 'gm_w_s', 'gm_b_s', 'gm_w_o', 'ssd_w_in', 'ssd_conv_w', 'ssd_conv_b', 'ssd_dt_bias', 'ssd_a_log', 'ssd_d', 'ssd_norm_g', 'ssd_w_o', 'mlp_w_in', 'mlp_w_out', 'loss_target', 'm_norm_mix_g', 'm_norm_mlp_g', 'm_sb_w_qkv', 'm_sb_q_norm_g', 'm_sb_k_norm_g', 'm_sb_w_o', 'm_gm_w_in', 'm_gm_v_norm_g', 'm_gm_w_s', 'm_gm_b_s', 'm_gm_w_o', 'm_ssd_w_in', 'm_ssd_conv_w', 'm_ssd_conv_b', 'm_ssd_dt_bias', 'm_ssd_a_log', 'm_ssd_d', 'm_ssd_norm_g', 'm_ssd_w_o', 'm_mlp_w_in', 'm_mlp_w_out', 'v_norm_mix_g', 'v_norm_mlp_g', 'v_sb_w_qkv', 'v_sb_q_norm_g', 'v_sb_k_norm_g', 'v_sb_w_o', 'v_gm_w_in', 'v_gm_v_norm_g', 'v_gm_w_s', 'v_gm_b_s', 'v_gm_w_o', 'v_ssd_w_in', 'v_ssd_conv_w', 'v_ssd_conv_b', 'v_ssd_dt_bias', 'v_ssd_a_log', 'v_ssd_d', 'v_ssd_norm_g', 'v_ssd_w_o', 'v_mlp_w_in', 'v_mlp_w_out']
TWIN_OUTPUTS = ['loss', 'grad_x', 'grad_norm_mix_g', 'grad_norm_mlp_g', 'grad_sb_w_qkv', 'grad_sb_q_norm_g', 'grad_sb_k_norm_g', 'grad_sb_w_o', 'grad_gm_w_in', 'grad_gm_v_norm_g', 'grad_gm_w_s', 'grad_gm_b_s', 'grad_gm_w_o', 'grad_ssd_w_in', 'grad_ssd_conv_w', 'grad_ssd_conv_b', 'grad_ssd_dt_bias', 'grad_ssd_a_log', 'grad_ssd_d', 'grad_ssd_norm_g', 'grad_ssd_w_o', 'grad_mlp_w_in', 'grad_mlp_w_out', 'delta_norm_mix_g', 'delta_norm_mlp_g', 'delta_sb_w_qkv', 'delta_sb_q_norm_g', 'delta_sb_k_norm_g', 'delta_sb_w_o', 'delta_gm_w_in', 'delta_gm_v_norm_g', 'delta_gm_w_s', 'delta_gm_b_s', 'delta_gm_w_o', 'delta_ssd_w_in', 'delta_ssd_conv_w', 'delta_ssd_conv_b', 'delta_ssd_dt_bias', 'delta_ssd_a_log', 'delta_ssd_d', 'delta_ssd_norm_g', 'delta_ssd_w_o', 'delta_mlp_w_in', 'delta_mlp_w_out', 'new_m_norm_mix_g', 'new_m_norm_mlp_g', 'new_m_sb_w_qkv', 'new_m_sb_q_norm_g', 'new_m_sb_k_norm_g', 'new_m_sb_w_o', 'new_m_gm_w_in', 'new_m_gm_v_norm_g', 'new_m_gm_w_s', 'new_m_gm_b_s', 'new_m_gm_w_o', 'new_m_ssd_w_in', 'new_m_ssd_conv_w', 'new_m_ssd_conv_b', 'new_m_ssd_dt_bias', 'new_m_ssd_a_log', 'new_m_ssd_d', 'new_m_ssd_norm_g', 'new_m_ssd_w_o', 'new_m_mlp_w_in', 'new_m_mlp_w_out', 'new_v_norm_mix_g', 'new_v_norm_mlp_g', 'new_v_sb_w_qkv', 'new_v_sb_q_norm_g', 'new_v_sb_k_norm_g', 'new_v_sb_w_o', 'new_v_gm_w_in', 'new_v_gm_v_norm_g', 'new_v_gm_w_s', 'new_v_gm_b_s', 'new_v_gm_w_o', 'new_v_ssd_w_in', 'new_v_ssd_conv_w', 'new_v_ssd_conv_b', 'new_v_ssd_dt_bias', 'new_v_ssd_a_log', 'new_v_ssd_d', 'new_v_ssd_norm_g', 'new_v_ssd_w_o', 'new_v_mlp_w_in', 'new_v_mlp_w_out']
TWIN_LEAF_KINDS = {'loss': 'loss', 'grad_x': 'grad_x', 'grad_norm_mix_g': 'grad_w', 'grad_norm_mlp_g': 'grad_w', 'grad_sb_w_qkv': 'grad_w', 'grad_sb_q_norm_g': 'grad_w', 'grad_sb_k_norm_g': 'grad_w', 'grad_sb_w_o': 'grad_w', 'grad_gm_w_in': 'grad_w', 'grad_gm_v_norm_g': 'grad_w', 'grad_gm_w_s': 'grad_w', 'grad_gm_b_s': 'grad_w', 'grad_gm_w_o': 'grad_w', 'grad_ssd_w_in': 'grad_w', 'grad_ssd_conv_w': 'grad_w', 'grad_ssd_conv_b': 'grad_w', 'grad_ssd_dt_bias': 'grad_w', 'grad_ssd_a_log': 'grad_w', 'grad_ssd_d': 'grad_w', 'grad_ssd_norm_g': 'grad_w', 'grad_ssd_w_o': 'grad_w', 'grad_mlp_w_in': 'grad_w', 'grad_mlp_w_out': 'grad_w', 'delta_norm_mix_g': 'delta_w', 'delta_norm_mlp_g': 'delta_w', 'delta_sb_w_qkv': 'delta_w', 'delta_sb_q_norm_g': 'delta_w', 'delta_sb_k_norm_g': 'delta_w', 'delta_sb_w_o': 'delta_w', 'delta_gm_w_in': 'delta_w', 'delta_gm_v_norm_g': 'delta_w', 'delta_gm_w_s': 'delta_w', 'delta_gm_b_s': 'delta_w', 'delta_gm_w_o': 'delta_w', 'delta_ssd_w_in': 'delta_w', 'delta_ssd_conv_w': 'delta_w', 'delta_ssd_conv_b': 'delta_w', 'delta_ssd_dt_bias': 'delta_w', 'delta_ssd_a_log': 'delta_w', 'delta_ssd_d': 'delta_w', 'delta_ssd_norm_g': 'delta_w', 'delta_ssd_w_o': 'delta_w', 'delta_mlp_w_in': 'delta_w', 'delta_mlp_w_out': 'delta_w', 'new_m_norm_mix_g': 'new_m', 'new_m_norm_mlp_g': 'new_m', 'new_m_sb_w_qkv': 'new_m', 'new_m_sb_q_norm_g': 'new_m', 'new_m_sb_k_norm_g': 'new_m', 'new_m_sb_w_o': 'new_m', 'new_m_gm_w_in': 'new_m', 'new_m_gm_v_norm_g': 'new_m', 'new_m_gm_w_s': 'new_m', 'new_m_gm_b_s': 'new_m', 'new_m_gm_w_o': 'new_m', 'new_m_ssd_w_in': 'new_m', 'new_m_ssd_conv_w': 'new_m', 'new_m_ssd_conv_b': 'new_m', 'new_m_ssd_dt_bias': 'new_m', 'new_m_ssd_a_log': 'new_m', 'new_m_ssd_d': 'new_m', 'new_m_ssd_norm_g': 'new_m', 'new_m_ssd_w_o': 'new_m', 'new_m_mlp_w_in': 'new_m', 'new_m_mlp_w_out': 'new_m', 'new_v_norm_mix_g': 'new_v', 'new_v_norm_mlp_g': 'new_v', 'new_v_sb_w_qkv': 'new_v', 'new_v_sb_q_norm_g': 'new_v', 'new_v_sb_k_norm_g': 'new_v', 'new_v_sb_w_o': 'new_v', 'new_v_gm_w_in': 'new_v', 'new_v_gm_v_norm_g': 'new_v', 'new_v_gm_w_s': 'new_v', 'new_v_gm_b_s': 'new_v', 'new_v_gm_w_o': 'new_v', 'new_v_ssd_w_in': 'new_v', 'new_v_ssd_conv_w': 'new_v', 'new_v_ssd_conv_b': 'new_v', 'new_v_ssd_dt_bias': 'new_v', 'new_v_ssd_a_log': 'new_v', 'new_v_ssd_d': 'new_v', 'new_v_ssd_norm_g': 'new_v', 'new_v_ssd_w_o': 'new_v', 'new_v_mlp_w_in': 'new_v', 'new_v_mlp_w_out': 'new_v'}


def _forward(args):
    return _fwd_reference(*[args[k] for k in FWD_PARAMS])


def _output_shape():
    def fwd():
        inp = _fwd_setup_inputs(0)
        return _fwd_reference(*[inp[k] for k in FWD_PARAMS])
    out = _jax.eval_shape(fwd)
    return out.shape, out.dtype

N_MICROBATCH = 1
ADAM_LR = 0.001
ADAM_B1 = 0.9
ADAM_B2 = 0.999
ADAM_EPS = 1e-08
ADAM_WD = 0.01
ADAM_STEP = 10
PER_EXAMPLE_BATCH_AXIS = {'x': 0, 'loss_target': 0}
SHARED_INPUTS = []
_WEIGHT_DTYPES = {'norm_mix_g': _jnp.float32, 'norm_mlp_g': _jnp.float32, 'sb_w_qkv': _jnp.float32, 'sb_q_norm_g': _jnp.float32, 'sb_k_norm_g': _jnp.float32, 'sb_w_o': _jnp.float32, 'gm_w_in': _jnp.float32, 'gm_v_norm_g': _jnp.float32, 'gm_w_s': _jnp.float32, 'gm_b_s': _jnp.float32, 'gm_w_o': _jnp.float32, 'ssd_w_in': _jnp.float32, 'ssd_conv_w': _jnp.float32, 'ssd_conv_b': _jnp.float32, 'ssd_dt_bias': _jnp.float32, 'ssd_a_log': _jnp.float32, 'ssd_d': _jnp.float32, 'ssd_norm_g': _jnp.float32, 'ssd_w_o': _jnp.float32, 'mlp_w_in': _jnp.float32, 'mlp_w_out': _jnp.float32}
MOMENT_SCALE = {'norm_mix_g': 9.627301e+00, 'norm_mlp_g': 5.007076e+01, 'sb_w_qkv': 3.923209e+00, 'sb_q_norm_g': 1.497080e+01, 'sb_k_norm_g': 1.494567e+01, 'sb_w_o': 6.305119e+00, 'gm_w_in': 3.691594e+00, 'gm_v_norm_g': 4.824833e-02, 'gm_w_s': 1.893802e+00, 'gm_b_s': 6.511142e+00, 'gm_w_o': 1.051964e+01, 'ssd_w_in': 3.793324e+00, 'ssd_conv_w': 4.622377e+00, 'ssd_conv_b': 8.888876e+00, 'ssd_dt_bias': 2.896623e+00, 'ssd_a_log': 1.692246e+01, 'ssd_d': 1.932814e+01, 'ssd_norm_g': 2.153156e+01, 'ssd_w_o': 9.080527e+00, 'mlp_w_in': 4.292287e+00, 'mlp_w_out': 1.543633e+01}


def _to_microbatches(a, axis):
    t = _jnp.moveaxis(a, axis, 0)
    t = t.reshape((N_MICROBATCH, t.shape[0] // N_MICROBATCH) + t.shape[1:])
    return _jnp.moveaxis(t, 1, axis + 1)


def setup_inputs(seed: int = 0) -> dict:
    inp = _fwd_setup_inputs(seed)
    key = _jax.random.fold_in(_jax.random.key(seed), 7919)
    shape, _ = _output_shape()
    out = dict(inp)
    out["loss_target"] = _jax.random.normal(_jax.random.fold_in(key, 0), shape, _jnp.float32)
    for i, name in enumerate(TWIN_WEIGHTS):
        w = inp[name].astype(_jnp.float32)
        if MOMENT_SCALE is None:
            s = _jnp.sqrt(_jnp.mean(_jnp.square(w)) + 1e-30)
        else:
            s = MOMENT_SCALE[name]
        km, kv = _jax.random.split(_jax.random.fold_in(key, i + 1))
        out[name] = w
        out["m_" + name] = s * _jax.random.normal(km, w.shape, _jnp.float32)
        out["v_" + name] = (s * s) * _jax.random.uniform(kv, w.shape, _jnp.float32, 0.5, 1.5)
    if N_MICROBATCH > 1:
        for name, axis in PER_EXAMPLE_BATCH_AXIS.items():
            out[name] = _to_microbatches(out[name], axis)
    return {'x': out['x'], 'norm_mix_g': out['norm_mix_g'], 'norm_mlp_g': out['norm_mlp_g'], 'sb_w_qkv': out['sb_w_qkv'], 'sb_q_norm_g': out['sb_q_norm_g'], 'sb_k_norm_g': out['sb_k_norm_g'], 'sb_w_o': out['sb_w_o'], 'gm_w_in': out['gm_w_in'], 'gm_v_norm_g': out['gm_v_norm_g'], 'gm_w_s': out['gm_w_s'], 'gm_b_s': out['gm_b_s'], 'gm_w_o': out['gm_w_o'], 'ssd_w_in': out['ssd_w_in'], 'ssd_conv_w': out['ssd_conv_w'], 'ssd_conv_b': out['ssd_conv_b'], 'ssd_dt_bias': out['ssd_dt_bias'], 'ssd_a_log': out['ssd_a_log'], 'ssd_d': out['ssd_d'], 'ssd_norm_g': out['ssd_norm_g'], 'ssd_w_o': out['ssd_w_o'], 'mlp_w_in': out['mlp_w_in'], 'mlp_w_out': out['mlp_w_out'], 'loss_target': out['loss_target'], 'm_norm_mix_g': out['m_norm_mix_g'], 'm_norm_mlp_g': out['m_norm_mlp_g'], 'm_sb_w_qkv': out['m_sb_w_qkv'], 'm_sb_q_norm_g': out['m_sb_q_norm_g'], 'm_sb_k_norm_g': out['m_sb_k_norm_g'], 'm_sb_w_o': out['m_sb_w_o'], 'm_gm_w_in': out['m_gm_w_in'], 'm_gm_v_norm_g': out['m_gm_v_norm_g'], 'm_gm_w_s': out['m_gm_w_s'], 'm_gm_b_s': out['m_gm_b_s'], 'm_gm_w_o': out['m_gm_w_o'], 'm_ssd_w_in': out['m_ssd_w_in'], 'm_ssd_conv_w': out['m_ssd_conv_w'], 'm_ssd_conv_b': out['m_ssd_conv_b'], 'm_ssd_dt_bias': out['m_ssd_dt_bias'], 'm_ssd_a_log': out['m_ssd_a_log'], 'm_ssd_d': out['m_ssd_d'], 'm_ssd_norm_g': out['m_ssd_norm_g'], 'm_ssd_w_o': out['m_ssd_w_o'], 'm_mlp_w_in': out['m_mlp_w_in'], 'm_mlp_w_out': out['m_mlp_w_out'], 'v_norm_mix_g': out['v_norm_mix_g'], 'v_norm_mlp_g': out['v_norm_mlp_g'], 'v_sb_w_qkv': out['v_sb_w_qkv'], 'v_sb_q_norm_g': out['v_sb_q_norm_g'], 'v_sb_k_norm_g': out['v_sb_k_norm_g'], 'v_sb_w_o': out['v_sb_w_o'], 'v_gm_w_in': out['v_gm_w_in'], 'v_gm_v_norm_g': out['v_gm_v_norm_g'], 'v_gm_w_s': out['v_gm_w_s'], 'v_gm_b_s': out['v_gm_b_s'], 'v_gm_w_o': out['v_gm_w_o'], 'v_ssd_w_in': out['v_ssd_w_in'], 'v_ssd_conv_w': out['v_ssd_conv_w'], 'v_ssd_conv_b': out['v_ssd_conv_b'], 'v_ssd_dt_bias': out['v_ssd_dt_bias'], 'v_ssd_a_log': out['v_ssd_a_log'], 'v_ssd_d': out['v_ssd_d'], 'v_ssd_norm_g': out['v_ssd_norm_g'], 'v_ssd_w_o': out['v_ssd_w_o'], 'v_mlp_w_in': out['v_mlp_w_in'], 'v_mlp_w_out': out['v_mlp_w_out']}


def _loss(weights, diff, rest, loss_target):
    with _jax.named_scope("forward"):
        args = {**rest, TWIN_DIFF_INPUT: diff, **{k: w.astype(_WEIGHT_DTYPES[k]) for k, w in weights.items()}}
        y = _forward(args)
    with _jax.named_scope("loss_head"):
        err = _jnp.square(y.astype(_jnp.float32) - loss_target)
        return 0.5 * _jnp.sum(_jnp.mean(err, axis=-1)) if err.ndim else 0.5 * err


def _adamw(w, g, m, v):
    m = ADAM_B1 * m + (1.0 - ADAM_B1) * g
    v = ADAM_B2 * v + (1.0 - ADAM_B2) * _jnp.square(g)
    m_hat = m / (1.0 - ADAM_B1 ** ADAM_STEP)
    v_hat = v / (1.0 - ADAM_B2 ** ADAM_STEP)
    delta = -ADAM_LR * (m_hat / (_jnp.sqrt(v_hat) + ADAM_EPS) + ADAM_WD * w)
    return delta, m, v


def reference(x, norm_mix_g, norm_mlp_g, sb_w_qkv, sb_q_norm_g, sb_k_norm_g, sb_w_o, gm_w_in, gm_v_norm_g, gm_w_s, gm_b_s, gm_w_o, ssd_w_in, ssd_conv_w, ssd_conv_b, ssd_dt_bias, ssd_a_log, ssd_d, ssd_norm_g, ssd_w_o, mlp_w_in, mlp_w_out, loss_target, m_norm_mix_g, m_norm_mlp_g, m_sb_w_qkv, m_sb_q_norm_g, m_sb_k_norm_g, m_sb_w_o, m_gm_w_in, m_gm_v_norm_g, m_gm_w_s, m_gm_b_s, m_gm_w_o, m_ssd_w_in, m_ssd_conv_w, m_ssd_conv_b, m_ssd_dt_bias, m_ssd_a_log, m_ssd_d, m_ssd_norm_g, m_ssd_w_o, m_mlp_w_in, m_mlp_w_out, v_norm_mix_g, v_norm_mlp_g, v_sb_w_qkv, v_sb_q_norm_g, v_sb_k_norm_g, v_sb_w_o, v_gm_w_in, v_gm_v_norm_g, v_gm_w_s, v_gm_b_s, v_gm_w_o, v_ssd_w_in, v_ssd_conv_w, v_ssd_conv_b, v_ssd_dt_bias, v_ssd_a_log, v_ssd_d, v_ssd_norm_g, v_ssd_w_o, v_mlp_w_in, v_mlp_w_out):
    given = dict(x=x, norm_mix_g=norm_mix_g, norm_mlp_g=norm_mlp_g, sb_w_qkv=sb_w_qkv, sb_q_norm_g=sb_q_norm_g, sb_k_norm_g=sb_k_norm_g, sb_w_o=sb_w_o, gm_w_in=gm_w_in, gm_v_norm_g=gm_v_norm_g, gm_w_s=gm_w_s, gm_b_s=gm_b_s, gm_w_o=gm_w_o, ssd_w_in=ssd_w_in, ssd_conv_w=ssd_conv_w, ssd_conv_b=ssd_conv_b, ssd_dt_bias=ssd_dt_bias, ssd_a_log=ssd_a_log, ssd_d=ssd_d, ssd_norm_g=ssd_norm_g, ssd_w_o=ssd_w_o, mlp_w_in=mlp_w_in, mlp_w_out=mlp_w_out, loss_target=loss_target, m_norm_mix_g=m_norm_mix_g, m_norm_mlp_g=m_norm_mlp_g, m_sb_w_qkv=m_sb_w_qkv, m_sb_q_norm_g=m_sb_q_norm_g, m_sb_k_norm_g=m_sb_k_norm_g, m_sb_w_o=m_sb_w_o, m_gm_w_in=m_gm_w_in, m_gm_v_norm_g=m_gm_v_norm_g, m_gm_w_s=m_gm_w_s, m_gm_b_s=m_gm_b_s, m_gm_w_o=m_gm_w_o, m_ssd_w_in=m_ssd_w_in, m_ssd_conv_w=m_ssd_conv_w, m_ssd_conv_b=m_ssd_conv_b, m_ssd_dt_bias=m_ssd_dt_bias, m_ssd_a_log=m_ssd_a_log, m_ssd_d=m_ssd_d, m_ssd_norm_g=m_ssd_norm_g, m_ssd_w_o=m_ssd_w_o, m_mlp_w_in=m_mlp_w_in, m_mlp_w_out=m_mlp_w_out, v_norm_mix_g=v_norm_mix_g, v_norm_mlp_g=v_norm_mlp_g, v_sb_w_qkv=v_sb_w_qkv, v_sb_q_norm_g=v_sb_q_norm_g, v_sb_k_norm_g=v_sb_k_norm_g, v_sb_w_o=v_sb_w_o, v_gm_w_in=v_gm_w_in, v_gm_v_norm_g=v_gm_v_norm_g, v_gm_w_s=v_gm_w_s, v_gm_b_s=v_gm_b_s, v_gm_w_o=v_gm_w_o, v_ssd_w_in=v_ssd_w_in, v_ssd_conv_w=v_ssd_conv_w, v_ssd_conv_b=v_ssd_conv_b, v_ssd_dt_bias=v_ssd_dt_bias, v_ssd_a_log=v_ssd_a_log, v_ssd_d=v_ssd_d, v_ssd_norm_g=v_ssd_norm_g, v_ssd_w_o=v_ssd_w_o, v_mlp_w_in=v_mlp_w_in, v_mlp_w_out=v_mlp_w_out)
    weights = {n: given[n] for n in TWIN_WEIGHTS}
    shared = {n: given[n] for n in SHARED_INPUTS}
    per_example = {n: given[n] for n in ['x']}
    grad_fn = _jax.value_and_grad(_loss, argnums=(0, 1))

    def one_microbatch(ex, loss_target):
        ex = dict(ex)
        diff = ex.pop(TWIN_DIFF_INPUT)
        return grad_fn(weights, diff, {**shared, **ex}, loss_target)

    if N_MICROBATCH == 1:
        loss, (grad_w, grad_x) = one_microbatch(per_example, given["loss_target"])
    else:
        def body(carry, xs):
            loss_sum, grad_sum = carry
            l_k, (gw_k, gx_k) = one_microbatch(xs[0], xs[1])
            with _jax.named_scope("update"):
                return (loss_sum + l_k, _jax.tree.map(_jnp.add, grad_sum, gw_k)), gx_k

        init = (_jnp.zeros((), _jnp.float32), _jax.tree.map(_jnp.zeros_like, weights))
        (loss, grad_w), grad_x = _jax.lax.scan(body, init, (per_example, given["loss_target"]))
    with _jax.named_scope("update"):
        delta_w, new_m, new_v = {}, {}, {}
        for n in TWIN_WEIGHTS:
            delta_w[n], new_m[n], new_v[n] = _adamw(weights[n], grad_w[n], given["m_" + n], given["v_" + n])
    return (loss, grad_x, *[grad_w[n] for n in TWIN_WEIGHTS], *[delta_w[n] for n in TWIN_WEIGHTS],
            *[new_m[n] for n in TWIN_WEIGHTS], *[new_v[n] for n in TWIN_WEIGHTS])
```

```python
import functools
import math

import jax
import jax.numpy as jnp
from jax import lax
from jax.experimental import pallas as pl
from jax.experimental.pallas import tpu as pltpu

F32 = jnp.float32
BF16 = jnp.bfloat16

D_MODEL = 2048
DEPTH = 4
N_MIXERS = 3
EPS = 1e-6
SB_HEADS = 16
HEAD_DIM = 128
GM_CHUNK = 128
GM_GROUPS = 16
SSD_INNER = 4096
SSD_HEADS = 64
SSD_HEAD_DIM = 64
SSD_GROUPS = 8
SSD_HPG = 8
SSD_STATE = 128
SSD_CHUNK = 128
SSD_CONV = 4
SSD_BC = SSD_GROUPS * SSD_STATE
SSD_CONV_DIM = SSD_INNER + 2 * SSD_BC
SSD_PROJ = SSD_INNER + SSD_CONV_DIM + SSD_HEADS
DT_PAD = 128
N_DEV = 8
LANES = 128

ADAM_LR = 0.001
ADAM_B1 = 0.9
ADAM_B2 = 0.999
ADAM_EPS = 1e-08
ADAM_WD = 0.01
ADAM_STEP = 10

VMEM_LIMIT_BYTES = 56 * 1024 * 1024
MESH = pl.DeviceIdType.MESH
HBM_SPEC = pl.BlockSpec(memory_space=pl.ANY)


def _cparams(*sem):
    return pltpu.CompilerParams(dimension_semantics=sem, vmem_limit_bytes=VMEM_LIMIT_BYTES)


def _tile(dim, pref):
    t = min(dim, pref)
    assert dim % t == 0, (dim, pref)
    return t


_DOT_DIMS = {"nn": ((1,), (0,)), "nt": ((1,), (1,)), "tn": ((0,), (0,))}


def matmul(a, b, form, name, *, b_layer=None, out_dtypes=(F32,), epilogue=None, extras=(),
           tm=1024, tn=1024, tk=512):
    bs = b.shape[-2:]
    if form == "nn":
        (m, k), (k2, n) = a.shape, bs
    elif form == "nt":
        (m, k), (n, k2) = a.shape, bs
    else:
        (k, m), (k2, n) = a.shape, bs
    assert k == k2, (a.shape, b.shape, form)
    tm, tn, tk = _tile(m, tm), _tile(n, tn), _tile(k, tk)
    grid = (m // tm, n // tn, k // tk)
    nk = grid[2]
    if form == "tn":
        a_spec = pl.BlockSpec((tk, tm), lambda i, j, kk: (kk, i))
    else:
        a_spec = pl.BlockSpec((tm, tk), lambda i, j, kk: (i, kk))
    if form == "nt":
        b_blk, b_idx = (tn, tk), (lambda i, j, kk: (j, kk))
    else:
        b_blk, b_idx = (tk, tn), (lambda i, j, kk: (kk, j))
    if b.ndim == 3:
        b_spec = pl.BlockSpec((None,) + b_blk, lambda i, j, kk: (b_layer,) + b_idx(i, j, kk))
    else:
        b_spec = pl.BlockSpec(b_blk, b_idx)
    o_spec = pl.BlockSpec((tm, tn), lambda i, j, kk: (i, j))
    n_extra, n_out = len(extras), len(out_dtypes)
    dims = (_DOT_DIMS[form], ((), ()))

    def body(a_ref, b_ref, *rest):
        extra_refs, out_refs, acc_ref = rest[:n_extra], rest[n_extra:n_extra + n_out], rest[-1]
        kk = pl.program_id(2)

        @pl.when(kk == 0)
        def _():
            acc_ref[...] = jnp.zeros_like(acc_ref)

        acc_ref[...] += lax.dot_general(a_ref[...], b_ref[...], dims, preferred_element_type=F32)

        @pl.when(kk == nk - 1)
        def _():
            acc = acc_ref[...]
            outs = (acc,) if epilogue is None else epilogue(acc, *[r[...] for r in extra_refs])
            for r, o in zip(out_refs, outs):
                r[...] = o.astype(r.dtype)

    res = pl.pallas_call(
        body, name=name, grid=grid,
        in_specs=[a_spec, b_spec] + [o_spec] * n_extra,
        out_specs=[o_spec] * n_out,
        out_shape=[jax.ShapeDtypeStruct((m, n), dt) for dt in out_dtypes],
        scratch_shapes=[pltpu.VMEM((tm, tn), F32)],
        compiler_params=_cparams("parallel", "parallel", "arbitrary"),
    )(a, b, *extras)
    return res[0] if n_out == 1 else res


class Data:
    def __init__(self, array, block, imap):
        self.array, self.block, self.imap = array, block, imap


class Param:
    def __init__(self, array, block, imap, group):
        self.array, self.block, self.imap, self.group = array, block, imap, group


class Out:
    def __init__(self, shape, dtype, block, imap):
        self.shape, self.dtype, self.block, self.imap = shape, dtype, block, imap


def _f32(v):
    return v.astype(F32)


def blockwise_fwd(f, grid, data, params, outs, name):
    nd, npar = len(data), len(params)

    def body(*refs):
        vals = [_f32(r[...]) for r in refs[:nd + npar]]
        res = f(*vals)
        for r, o in zip(refs[nd + npar:], res):
            r[...] = o.astype(r.dtype)

    in_specs = [pl.BlockSpec(d.block, d.imap) for d in data]
    in_specs += [pl.BlockSpec(p.block, (lambda o, i, _m=p.imap: _m(o))) for p in params]
    res = pl.pallas_call(
        body, name=name, grid=grid, in_specs=in_specs,
        out_specs=[pl.BlockSpec(o.block, o.imap) for o in outs],
        out_shape=[jax.ShapeDtypeStruct(o.shape, o.dtype) for o in outs],
        compiler_params=_cparams("parallel", "parallel"),
    )(*[d.array for d in data], *[p.array for p in params])
    return res


def blockwise_bwd(f, grid, data, params, outs, cts, name, *, grad_dtypes, accum=None):
    nd, npar, no = len(data), len(params), len(outs)
    accum = accum or {}
    want = [k for k in range(nd) if grad_dtypes[k] is not None]
    acc_keys = sorted(accum)

    def body(*refs):
        in_refs = refs[:nd + npar]
        ct_refs = refs[nd + npar:nd + npar + no]
        acc_refs = refs[nd + npar + no:nd + npar + no + len(acc_keys)]
        out_refs = refs[nd + npar + no + len(acc_keys):]
        dd_refs, dp_refs = out_refs[:len(want)], out_refs[len(want):]
        o, i = pl.program_id(0), pl.program_id(1)
        vals = [_f32(r[...]) for r in in_refs]
        _, vjp = jax.vjp(f, *vals)
        grads = vjp(tuple(_f32(r[...]) for r in ct_refs))
        for r, k in zip(dd_refs, want):
            g = grads[k]
            if k in accum:
                g = g + _f32(acc_refs[acc_keys.index(k)][...])
            r[...] = g.astype(r.dtype)
        for r, p, g in zip(dp_refs, params, grads[nd:]):
            first = jnp.logical_and(i == 0, o % p.group == 0)

            @pl.when(first)
            def _():
                r[...] = g.astype(r.dtype)

            @pl.when(jnp.logical_not(first))
            def _():
                r[...] += g.astype(r.dtype)

    d_specs = [pl.BlockSpec(d.block, d.imap) for d in data]
    p_specs = [pl.BlockSpec(p.block, (lambda o, i, _m=p.imap: _m(o))) for p in params]
    ct_specs = [pl.BlockSpec(o.block, o.imap) for o in outs]
    acc_specs = [d_specs[k] for k in acc_keys]
    res = pl.pallas_call(
        body, name=name, grid=grid,
        in_specs=d_specs + p_specs + ct_specs + acc_specs,
        out_specs=[d_specs[k] for k in want] + p_specs,
        out_shape=[jax.ShapeDtypeStruct(data[k].array.shape, grad_dtypes[k]) for k in want]
        + [jax.ShapeDtypeStruct(p.array.shape, F32) for p in params],
        compiler_params=_cparams("arbitrary", "arbitrary"),
    )(*[d.array for d in data], *[p.array for p in params], *cts, *[accum[k] for k in acc_keys])
    return list(res[:len(want)]), list(res[len(want):])


def _rmsnorm_f(x, g):
    return (x * lax.rsqrt(jnp.mean(x * x, axis=-1, keepdims=True) + EPS) * g,)


def _gelu(x):
    return 0.5 * x * (1.0 + lax.erf(x * (2.0 ** -0.5)))


def _silu(x):
    return x / (1.0 + jnp.exp(-x))


def _softplus(x):
    return jnp.maximum(x, 0.0) + jnp.log1p(jnp.exp(-jnp.abs(x)))


def _gm_act_f(pu, pv, gv):
    return (_gelu(pu),) + _rmsnorm_f(_gelu(pv), gv)


def _gm_gate_f(u, vn, w, b):
    n = w.shape[0]
    row = lax.broadcasted_iota(jnp.int32, (n, n), 0)
    col = lax.broadcasted_iota(jnp.int32, (n, n), 1)
    wm = jnp.where(row >= col, w, 0.0)
    return (u * (jnp.dot(wm, vn, preferred_element_type=F32) + b),)


def _ssd_gate_f(y, z, g):
    return _rmsnorm_f(y * _silu(z), g)


def _dt_f(raw, bias):
    return (_softplus(raw + bias),)


def _row_tile(t):
    return _tile(t, 256)


def _norm_specs(h, g_row):
    t, d = h.shape
    tr = _row_tile(t)
    data = [Data(h, (tr, d), lambda o, i: (i, 0))]
    params = [Param(g_row, (1, d), lambda o: (0, 0), 1)]
    outs = [Out((t, d), BF16, (tr, d), lambda o, i: (i, 0))]
    return (1, t // tr), data, params, outs


def rmsnorm_fwd(h, g_row, name):
    grid, data, params, outs = _norm_specs(h, g_row)
    return blockwise_fwd(_rmsnorm_f, grid, data, params, outs, name)[0]


def rmsnorm_bwd(h, g_row, ct, skip, name):
    grid, data, params, outs = _norm_specs(h, g_row)
    (dh,), (dg,) = blockwise_bwd(_rmsnorm_f, grid, data, params, outs, [ct], name,
                                 grad_dtypes=[F32], accum={0: skip})
    return dh, dg


def loss_call(y, target):
    t, d = y.shape
    tr = _row_tile(t)

    def body(y_ref, t_ref, dy_ref, loss_ref):
        e = y_ref[...] - t_ref[...]
        dy_ref[...] = e * (1.0 / d)

        @pl.when(pl.program_id(0) == 0)
        def _():
            loss_ref[...] = jnp.zeros_like(loss_ref)

        loss_ref[...] += jnp.sum(e * e).reshape(1, 1) * (0.5 / d)

    row = pl.BlockSpec((tr, d), lambda i: (i, 0))
    return pl.pallas_call(
        body, name="loss", grid=(t // tr,), in_specs=[row, row],
        out_specs=[row, pl.BlockSpec((1, 1), lambda i: (0, 0))],
        out_shape=[jax.ShapeDtypeStruct((t, d), F32), jax.ShapeDtypeStruct((1, 1), F32)],
        compiler_params=_cparams("arbitrary"),
    )(y, target)


ATT_KB = 128


def _dot_exactish(x, ones_bf):
    hi = x.astype(BF16)
    lo = (x - hi.astype(F32)).astype(BF16)
    return (jnp.dot(hi, ones_bf, preferred_element_type=F32)
            + jnp.dot(lo, ones_bf, preferred_element_type=F32))


def _att_block(q, k, off, qpos, lane, scale):
    z = lax.dot_general(q, k, (((1,), (1,)), ((), ())), preferred_element_type=F32) * scale
    mask = (off + lane) < qpos
    lb = jnp.minimum(z, 0.0) - jnp.log1p(jnp.exp(-jnp.abs(z)))
    l1 = jnp.where(mask, lb - z, 0.0)
    return mask, lb, l1


def _rowsum_b(v):
    return jnp.broadcast_to(jnp.sum(v, axis=1, keepdims=True), v.shape)


def attn_fwd(qk, qkv_bf, name, tq=256):
    t = qk.shape[0]
    tq = _tile(t, tq)
    sub = tq // ATT_KB
    scale = HEAD_DIM ** -0.5

    def body(q_ref, k_ref, v_ref, o_ref):
        qi = pl.program_id(1)
        q = q_ref[...]
        nb = (qi + 1) * sub
        qpos = qi * tq + lax.broadcasted_iota(jnp.int32, (tq, ATT_KB), 0)
        lane = lax.broadcasted_iota(jnp.int32, (tq, ATT_KB), 1)
        r_ = lax.broadcasted_iota(jnp.int32, (ATT_KB, ATT_KB), 0)
        c_ = lax.broadcasted_iota(jnp.int32, (ATT_KB, ATT_KB), 1)
        upper = (r_ > c_).astype(BF16)

        def step(s, carry):
            acc, c = carry
            off = pl.multiple_of((nb - 1 - s) * ATT_KB, ATT_KB)
            k = k_ref[pl.ds(off, ATT_KB), :]
            v = v_ref[pl.ds(off, ATT_KB), :]
            mask, lb, l1 = _att_block(q, k, off, qpos, lane, scale)
            suffix = c + _dot_exactish(l1, upper)
            a = jnp.where(mask, jnp.exp(lb + suffix), 0.0)
            acc = acc + jnp.dot(a.astype(BF16), v, preferred_element_type=F32)
            return acc, c + _rowsum_b(l1)

        zero = jnp.zeros((tq, ATT_KB), F32)
        acc, _ = lax.fori_loop(0, nb, step, (zero, zero))
        o_ref[...] = acc.astype(o_ref.dtype)

    return pl.pallas_call(
        body, name=name, grid=(SB_HEADS, t // tq),
        in_specs=[pl.BlockSpec((tq, HEAD_DIM), lambda h, i: (i, h)),
                  pl.BlockSpec((t, HEAD_DIM), lambda h, i: (0, SB_HEADS + h)),
                  pl.BlockSpec((t, HEAD_DIM), lambda h, i: (0, 2 * SB_HEADS + h))],
        out_specs=pl.BlockSpec((tq, HEAD_DIM), lambda h, i: (i, h)),
        out_shape=jax.ShapeDtypeStruct((t, SB_HEADS * HEAD_DIM), BF16),
        compiler_params=_cparams("parallel", "arbitrary"),
    )(qk, qk, qkv_bf)


def attn_bwd(qk, qkv_bf, do, name, tq=256):
    t = qk.shape[0]
    tq = _tile(t, tq)
    sub = tq // ATT_KB
    nq = t // tq
    nkb = t // ATT_KB
    scale = HEAD_DIM ** -0.5

    def body(q_ref, k_ref, v_ref, do_ref, dq_ref, dk_ref, dv_ref, cbuf, dk_acc, dv_acc):
        qi = pl.program_id(1)
        q = q_ref[...]
        do_ = do_ref[...]
        nb = (qi + 1) * sub
        qpos = qi * tq + lax.broadcasted_iota(jnp.int32, (tq, ATT_KB), 0)
        lane = lax.broadcasted_iota(jnp.int32, (tq, ATT_KB), 1)
        r_ = lax.broadcasted_iota(jnp.int32, (ATT_KB, ATT_KB), 0)
        c_ = lax.broadcasted_iota(jnp.int32, (ATT_KB, ATT_KB), 1)
        upper = (r_ > c_).astype(BF16)
        lower = (r_ < c_).astype(BF16)
        zero = jnp.zeros((tq, ATT_KB), F32)

        @pl.when(qi == 0)
        def _():
            dk_acc[...] = jnp.zeros_like(dk_acc)
            dv_acc[...] = jnp.zeros_like(dv_acc)

        def pass1(s, c):
            jb = nb - 1 - s
            off = pl.multiple_of(jb * ATT_KB, ATT_KB)
            cbuf[jb] = c
            _, _, l1 = _att_block(q, k_ref[pl.ds(off, ATT_KB), :], off, qpos, lane, scale)
            return c + _rowsum_b(l1)

        lax.fori_loop(0, nb, pass1, zero)

        def pass2(jb, carry):
            dq, ce = carry
            off = pl.multiple_of(jb * ATT_KB, ATT_KB)
            k = k_ref[pl.ds(off, ATT_KB), :]
            v = v_ref[pl.ds(off, ATT_KB), :]
            mask, lb, l1 = _att_block(q, k, off, qpos, lane, scale)
            suffix = cbuf[jb] + _dot_exactish(l1, upper)
            a = jnp.where(mask, jnp.exp(lb + suffix), 0.0)
            da = lax.dot_general(do_, v, (((1,), (1,)), ((), ())), preferred_element_type=F32)
            e = da * a
            big_e = ce + _dot_exactish(e, lower)
            sig = jnp.exp(lb)
            dz = (jnp.where(mask, e * (1.0 - sig) - big_e * sig, 0.0) * scale).astype(BF16)
            dq = dq + jnp.dot(dz, k, preferred_element_type=F32)
            dk_acc[pl.ds(off, ATT_KB), :] += lax.dot_general(
                dz, q, (((0,), (0,)), ((), ())), preferred_element_type=F32)
            dv_acc[pl.ds(off, ATT_KB), :] += lax.dot_general(
                a.astype(BF16), do_, (((0,), (0,)), ((), ())), preferred_element_type=F32)
            return dq, ce + _rowsum_b(e)

        dq, _ = lax.fori_loop(0, nb, pass2, (zero, zero))
        dq_ref[...] = dq

        @pl.when(qi == nq - 1)
        def _():
            dk_ref[...] = dk_acc[...]
            dv_ref[...] = dv_acc[...].astype(dv_ref.dtype)

    hd = SB_HEADS * HEAD_DIM
    dq, dk, dv = pl.pallas_call(
        body, name=name, grid=(SB_HEADS, nq),
        in_specs=[pl.BlockSpec((tq, HEAD_DIM), lambda h, i: (i, h)),
                  pl.BlockSpec((t, HEAD_DIM), lambda h, i: (0, SB_HEADS + h)),
                  pl.BlockSpec((t, HEAD_DIM), lambda h, i: (0, 2 * SB_HEADS + h)),
                  pl.BlockSpec((tq, HEAD_DIM), lambda h, i: (i, h))],
        out_specs=[pl.BlockSpec((tq, HEAD_DIM), lambda h, i: (i, h)),
                   pl.BlockSpec((t, HEAD_DIM), lambda h, i: (0, h)),
                   pl.BlockSpec((t, HEAD_DIM), lambda h, i: (0, h))],
        out_shape=[jax.ShapeDtypeStruct((t, hd), F32), jax.ShapeDtypeStruct((t, hd), F32),
                   jax.ShapeDtypeStruct((t, hd), BF16)],
        scratch_shapes=[pltpu.VMEM((nkb, tq, ATT_KB), F32), pltpu.VMEM((t, HEAD_DIM), F32),
                        pltpu.VMEM((t, HEAD_DIM), F32)],
        compiler_params=_cparams("arbitrary", "arbitrary"),
    )(qk, qk, qkv_bf, do)
    return jnp.concatenate([dq, dk], axis=1), dv


def _qknorm_specs(qkv, gqk):
    t = qkv.shape[0]
    tr = _row_tile(t)
    data = [Data(qkv, (tr, HEAD_DIM), lambda o, i: (i, o))]
    params = [Param(gqk, (None, 1, HEAD_DIM), lambda o: (o // SB_HEADS, 0, 0), SB_HEADS)]
    outs = [Out((t, 2 * SB_HEADS * HEAD_DIM), BF16, (tr, HEAD_DIM), lambda o, i: (i, o))]
    return (2 * SB_HEADS, t // tr), data, params, outs


CONV_COLS = 128


def _shift_down(x, s, rows):
    if s == 0:
        return x
    return jnp.where(rows < s, 0.0, pltpu.roll(x, s, 0))


def _shift_up(x, s, rows):
    if s == 0:
        return x
    n = x.shape[0]
    return jnp.where(rows >= n - s, 0.0, pltpu.roll(x, n - s, 0))


def conv_fwd(xbc, w, b, name):
    t, c = xbc.shape
    tc = _tile(c, CONV_COLS)

    def body(x_ref, w_ref, b_ref, o_ref):
        x = x_ref[...]
        rows = lax.broadcasted_iota(jnp.int32, x.shape, 0)
        y = jnp.broadcast_to(b_ref[...], x.shape)
        for kk in range(SSD_CONV):
            y = y + w_ref[kk:kk + 1, :] * _shift_down(x, SSD_CONV - 1 - kk, rows)
        o_ref[...] = _silu(y)

    col = pl.BlockSpec((t, tc), lambda j: (0, j))
    return pl.pallas_call(
        body, name=name, grid=(c // tc,),
        in_specs=[col, pl.BlockSpec((SSD_CONV, tc), lambda j: (0, j)), pl.BlockSpec((1, tc), lambda j: (0, j))],
        out_specs=col, out_shape=jax.ShapeDtypeStruct((t, c), F32),
        compiler_params=_cparams("parallel"),
    )(xbc, w, b)


def conv_bwd(xbc, w, b, dact, name):
    t, c = xbc.shape
    tc = _tile(c, CONV_COLS)

    def body(x_ref, w_ref, b_ref, g_ref, dx_ref, dw_ref, db_ref):
        x = x_ref[...]
        rows = lax.broadcasted_iota(jnp.int32, x.shape, 0)
        shifted = [_shift_down(x, SSD_CONV - 1 - kk, rows) for kk in range(SSD_CONV)]
        y = jnp.broadcast_to(b_ref[...], x.shape)
        for kk in range(SSD_CONV):
            y = y + w_ref[kk:kk + 1, :] * shifted[kk]
        sig = 1.0 / (1.0 + jnp.exp(-y))
        dy = g_ref[...] * (sig * (1.0 + y * (1.0 - sig)))
        dx = jnp.zeros_like(x)
        for kk in range(SSD_CONV):
            dx = dx + w_ref[kk:kk + 1, :] * _shift_up(dy, SSD_CONV - 1 - kk, rows)
            dw_ref[kk:kk + 1, :] = jnp.sum(dy * shifted[kk], axis=0, keepdims=True)
        dx_ref[...] = dx.astype(dx_ref.dtype)
        db_ref[...] = jnp.sum(dy, axis=0, keepdims=True)

    col = pl.BlockSpec((t, tc), lambda j: (0, j))
    wspec = pl.BlockSpec((SSD_CONV, tc), lambda j: (0, j))
    bspec = pl.BlockSpec((1, tc), lambda j: (0, j))
    return pl.pallas_call(
        body, name=name, grid=(c // tc,),
        in_specs=[col, wspec, bspec, col], out_specs=[col, wspec, bspec],
        out_shape=[jax.ShapeDtypeStruct((t, c), BF16), jax.ShapeDtypeStruct((SSD_CONV, c), F32),
                   jax.ShapeDtypeStruct((1, c), F32)],
        compiler_params=_cparams("parallel"),
    )(xbc, w, b, dact)


def _ssd_chunk(x, dt, bm, cm, sprev, alog, dskip):
    n = x.shape[0]
    hp = lax.Precision.HIGHEST
    a = -jnp.exp(alog)
    da = dt * a
    row = lax.broadcasted_iota(jnp.int32, (n, n), 0)
    col = lax.broadcasted_iota(jnp.int32, (n, n), 1)
    causal = row >= col
    dam = jnp.broadcast_to(da, (n, n))
    acum_l = jnp.dot(causal.astype(F32), dam, precision=hp, preferred_element_type=F32)
    acum_s = jnp.dot(jnp.ones((n, n), F32), jnp.where(row <= col, dam, 0.0), precision=hp,
                     preferred_element_type=F32)
    decay = jnp.exp(jnp.where(causal, acum_l - acum_s, -1e30))
    acum = jnp.sum(acum_l, axis=1, keepdims=True) * (1.0 / n)
    atot = jnp.sum(da, axis=0, keepdims=True)
    xs = x * dt
    cb = lax.dot_general(cm, bm, (((1,), (1,)), ((), ())), preferred_element_type=F32)
    y = jnp.dot(cb * decay, xs, preferred_element_type=F32)
    y = y + jnp.exp(acum) * lax.dot_general(cm, sprev, (((1,), (1,)), ((), ())), preferred_element_type=F32)
    y = y + dskip * x
    snew = jnp.exp(atot) * sprev + lax.dot_general(
        xs * jnp.exp(atot - acum), bm, (((0,), (0,)), ((), ())), preferred_element_type=F32)
    return y, snew


def _ssd_specs(t, rev):
    nc = t // SSD_CHUNK
    cidx = (lambda c: nc - 1 - c) if rev else (lambda c: c)
    x_spec = pl.BlockSpec((SSD_HPG, SSD_CHUNK, SSD_HEAD_DIM), lambda g, c: (g, cidx(c), 0))
    dt_spec = pl.BlockSpec((SSD_HPG, SSD_CHUNK, 1), lambda g, c: (g, cidx(c), 0))
    bc_spec = pl.BlockSpec((None, SSD_CHUNK, SSD_STATE), lambda g, c: (g, cidx(c), 0))
    sc_spec = pl.BlockSpec((SSD_HPG, 1, 1), lambda g, c: (g, 0, 0))
    st_spec = pl.BlockSpec((SSD_HPG, None, SSD_HEAD_DIM, SSD_STATE), lambda g, c: (g, cidx(c), 0, 0))
    return nc, x_spec, dt_spec, bc_spec, sc_spec, st_spec


def ssd_scan_fwd(x, dt, bm, cm, alog, dskip, name):
    t = x.shape[1]
    nc, x_spec, dt_spec, bc_spec, sc_spec, st_spec = _ssd_specs(t, False)

    def body(x_ref, dt_ref, b_ref, c_ref, al_ref, d_ref, y_ref, sin_ref, state):
        @pl.when(pl.program_id(1) == 0)
        def _():
            state[...] = jnp.zeros_like(state)

        bm_, cm_ = b_ref[...], c_ref[...]

        def head(r, carry):
            sprev = state[r]
            sin_ref[r] = sprev
            y, snew = _ssd_chunk(x_ref[r], dt_ref[r], bm_, cm_, sprev, al_ref[r], d_ref[r])
            y_ref[r] = y
            state[r] = snew
            return carry

        lax.fori_loop(0, SSD_HPG, head, 0)

    return pl.pallas_call(
        body, name=name, grid=(SSD_GROUPS, nc),
        in_specs=[x_spec, dt_spec, bc_spec, bc_spec, sc_spec, sc_spec],
        out_specs=[x_spec, st_spec],
        out_shape=[jax.ShapeDtypeStruct(x.shape, F32),
                   jax.ShapeDtypeStruct((SSD_HEADS, nc, SSD_HEAD_DIM, SSD_STATE), F32)],
        scratch_shapes=[pltpu.VMEM((SSD_HPG, SSD_HEAD_DIM, SSD_STATE), F32)],
        compiler_params=_cparams("parallel", "arbitrary"),
    )(x, dt, bm, cm, alog, dskip)


def ssd_scan_bwd(x, dt, bm, cm, alog, dskip, sin, dy, name):
    t = x.shape[1]
    nc, x_spec, dt_spec, bc_spec, sc_spec, st_spec = _ssd_specs(t, True)

    def body(x_ref, dt_ref, b_ref, c_ref, al_ref, d_ref, sin_ref, dy_ref,
             dx_ref, ddt_ref, db_ref, dc_ref, dal_ref, dd_ref, dstate):
        @pl.when(pl.program_id(1) == 0)
        def _():
            dstate[...] = jnp.zeros_like(dstate)
            dal_ref[...] = jnp.zeros_like(dal_ref)
            dd_ref[...] = jnp.zeros_like(dd_ref)

        db_ref[...] = jnp.zeros_like(db_ref)
        dc_ref[...] = jnp.zeros_like(dc_ref)
        bm_, cm_ = b_ref[...], c_ref[...]

        def head(r, carry):
            _, vjp = jax.vjp(_ssd_chunk, x_ref[r], dt_ref[r], bm_, cm_, sin_ref[r], al_ref[r], d_ref[r])
            dx, ddt, dbm, dcm, dsp, dal, dd = vjp((dy_ref[r], dstate[r]))
            dx_ref[r] = dx
            ddt_ref[r] = ddt
            db_ref[...] += dbm
            dc_ref[...] += dcm
            dstate[r] = dsp
            dal_ref[r] += dal
            dd_ref[r] += dd
            return carry

        lax.fori_loop(0, SSD_HPG, head, 0)

    return pl.pallas_call(
        body, name=name, grid=(SSD_GROUPS, nc),
        in_specs=[x_spec, dt_spec, bc_spec, bc_spec, sc_spec, sc_spec, st_spec, x_spec],
        out_specs=[x_spec, dt_spec, bc_spec, bc_spec, sc_spec, sc_spec],
        out_shape=[jax.ShapeDtypeStruct(x.shape, F32), jax.ShapeDtypeStruct(dt.shape, F32),
                   jax.ShapeDtypeStruct(bm.shape, F32), jax.ShapeDtypeStruct(cm.shape, F32),
                   jax.ShapeDtypeStruct(alog.shape, F32), jax.ShapeDtypeStruct(dskip.shape, F32)],
        scratch_shapes=[pltpu.VMEM((SSD_HPG, SSD_HEAD_DIM, SSD_STATE), F32)],
        compiler_params=_cparams("arbitrary", "arbitrary"),
    )(x, dt, bm, cm, alog, dskip, sin, dy)


def _to_heads(a, n, w):
    t = a.shape[0]
    return a.reshape(t, n, w).transpose(1, 0, 2)


def _from_heads(a):
    n, t, w = a.shape
    return a.transpose(1, 0, 2).reshape(t, n * w)


def _add(acc, h):
    return (acc + h,)


def sb_fwd(h, hn, w, j, tag):
    wqkv, wo, gqk = w["sb_w_qkv"], w["sb_w_o"], w["sb_gqk"][j]
    qkv, qkv_bf = matmul(hn, wqkv, "nn", f"{tag}_qkv", b_layer=j, out_dtypes=(F32, BF16),
                         epilogue=lambda acc: (acc, acc))
    grid, data, params, outs = _qknorm_specs(qkv, gqk)
    qk = blockwise_fwd(_rmsnorm_f, grid, data, params, outs, f"{tag}_qknorm")[0]
    o = attn_fwd(qk, qkv_bf, f"{tag}_attn")
    h1 = matmul(o, wo, "nn", f"{tag}_out", b_layer=j, epilogue=_add, extras=(h,))
    return h1, (qkv, qkv_bf, qk, o)


def sb_bwd(g_bf, hn, saved, w, j, tag):
    qkv, qkv_bf, qk, o = saved
    wqkv, wo, gqk = w["sb_w_qkv"], w["sb_w_o"], w["sb_gqk"][j]
    d_wo = matmul(o, g_bf, "tn", f"{tag}_dwo", out_dtypes=(BF16,))
    do = matmul(g_bf, wo, "nt", f"{tag}_do", b_layer=j, out_dtypes=(BF16,))
    dqk, dv = attn_bwd(qk, qkv_bf, do, f"{tag}_attn_bwd")
    grid, data, params, outs = _qknorm_specs(qkv, gqk)
    (draw,), (dgqk,) = blockwise_bwd(_rmsnorm_f, grid, data, params, outs, [dqk], f"{tag}_qknorm_bwd",
                                     grad_dtypes=[BF16])
    dqkv = jnp.concatenate([draw[:, :2 * SB_HEADS * HEAD_DIM], dv], axis=1)
    d_wqkv = matmul(hn, dqkv, "tn", f"{tag}_dwqkv", out_dtypes=(BF16,))
    dhn = matmul(dqkv, wqkv, "nt", f"{tag}_dhn", b_layer=j)
    return dhn, {"sb_w_qkv": d_wqkv, "sb_w_o": d_wo, "sb_gqk": dgqk}


def _gm_act_specs(p, gv):
    t = p.shape[0]
    tr = _row_tile(t)
    d = D_MODEL
    data = [Data(p, (tr, d), lambda o, i: (i, 0)), Data(p, (tr, d), lambda o, i: (i, 1))]
    params = [Param(gv, (1, d), lambda o: (0, 0), 1)]
    outs = [Out((t, d), F32, (tr, d), lambda o, i: (i, 0)), Out((t, d), BF16, (tr, d), lambda o, i: (i, 0))]
    return (1, t // tr), data, params, outs


def _gm_gate_specs(u, vn, ws, bs):
    t = u.shape[0]
    blk = (GM_CHUNK, LANES)
    data = [Data(u, blk, lambda o, i: (i, o)), Data(vn, blk, lambda o, i: (i, o))]
    params = [Param(ws, (None, GM_CHUNK, GM_CHUNK), lambda o: (o, 0, 0), 1),
              Param(bs, (None, GM_CHUNK, 1), lambda o: (o, 0, 0), 1)]
    outs = [Out((t, D_MODEL), BF16, blk, lambda o, i: (i, o))]
    return (GM_GROUPS, t // GM_CHUNK), data, params, outs


def gm_fwd(h, hn, w, j, tag):
    p = matmul(hn, w["gm_w_in"], "nn", f"{tag}_in", b_layer=j)
    grid, data, params, outs = _gm_act_specs(p, w["gm_v_norm_g"])
    u, vn = blockwise_fwd(_gm_act_f, grid, data, params, outs, f"{tag}_act")
    grid, data, params, outs = _gm_gate_specs(u, vn, w["gm_w_s"], w["gm_b_s"])
    y = blockwise_fwd(_gm_gate_f, grid, data, params, outs, f"{tag}_gate")[0]
    h1 = matmul(y, w["gm_w_o"], "nn", f"{tag}_out", b_layer=j, epilogue=_add, extras=(h,))
    return h1, (p, u, vn, y)


def gm_bwd(g_bf, hn, saved, w, j, tag):
    p, u, vn, y = saved
    d_wo = matmul(y, g_bf, "tn", f"{tag}_dwo", out_dtypes=(BF16,))
    dy = matmul(g_bf, w["gm_w_o"], "nt", f"{tag}_dy", b_layer=j)
    grid, data, params, outs = _gm_gate_specs(u, vn, w["gm_w_s"], w["gm_b_s"])
    (du, dvn), (dws, dbs) = blockwise_bwd(_gm_gate_f, grid, data, params, outs, [dy], f"{tag}_gate_bwd",
                                          grad_dtypes=[F32, F32])
    grid, data, params, outs = _gm_act_specs(p, w["gm_v_norm_g"])
    (dpu, dpv), (dgv,) = blockwise_bwd(_gm_act_f, grid, data, params, outs, [du, dvn], f"{tag}_act_bwd",
                                       grad_dtypes=[BF16, BF16])
    dp = jnp.concatenate([dpu[:, :D_MODEL], dpv[:, D_MODEL:]], axis=1)
    d_win = matmul(hn, dp, "tn", f"{tag}_dwin", out_dtypes=(BF16,))
    dhn = matmul(dp, w["gm_w_in"], "nt", f"{tag}_dhn", b_layer=j)
    return dhn, {"gm_w_in": d_win, "gm_w_o": d_wo, "gm_v_norm_g": dgv, "gm_w_s": dws, "gm_b_s": dbs}


def _ssd_gate_specs(y, z, g):
    t = y.shape[0]
    tr = _row_tile(t)
    gw = SSD_INNER // SSD_GROUPS
    blk = (tr, gw)
    data = [Data(y, blk, lambda o, i: (i, o)), Data(z, blk, lambda o, i: (i, o))]
    params = [Param(g, (None, 1, gw), lambda o: (o, 0, 0), 1)]
    outs = [Out((t, SSD_INNER), BF16, blk, lambda o, i: (i, o))]
    return (SSD_GROUPS, t // tr), data, params, outs


def _dt_specs(raw, bias):
    t = raw.shape[0]
    tr = _row_tile(t)
    data = [Data(raw, (tr, DT_PAD), lambda o, i: (i, 0))]
    params = [Param(bias, (1, DT_PAD), lambda o: (0, 0), 1)]
    outs = [Out((t, DT_PAD), F32, (tr, DT_PAD), lambda o, i: (i, 0))]
    return (1, t // tr), data, params, outs


def ssd_fwd(h, hn, w, tag):
    z = matmul(hn, w["ssd_wz"], "nn", f"{tag}_inz")
    xbc = matmul(hn, w["ssd_wxbc"], "nn", f"{tag}_inx")
    raw = matmul(hn, w["ssd_wdt"], "nn", f"{tag}_indt")
    act = conv_fwd(xbc, w["ssd_conv_w"], w["ssd_conv_b"], f"{tag}_conv")
    grid, data, params, outs = _dt_specs(raw, w["ssd_dt_bias"])
    dt = blockwise_fwd(_dt_f, grid, data, params, outs, f"{tag}_dt")[0]
    xh = _to_heads(act[:, :SSD_INNER], SSD_HEADS, SSD_HEAD_DIM)
    bh = _to_heads(act[:, SSD_INNER:SSD_INNER + SSD_BC], SSD_GROUPS, SSD_STATE)
    ch = _to_heads(act[:, SSD_INNER + SSD_BC:], SSD_GROUPS, SSD_STATE)
    dth = _to_heads(dt[:, :SSD_HEADS], SSD_HEADS, 1)
    yh, sin = ssd_scan_fwd(xh, dth, bh, ch, w["ssd_a_log"], w["ssd_d"], f"{tag}_scan")
    y = _from_heads(yh)
    grid, data, params, outs = _ssd_gate_specs(y, z, w["ssd_norm_g"])
    yn = blockwise_fwd(_ssd_gate_f, grid, data, params, outs, f"{tag}_gate")[0]
    h1 = matmul(yn, w["ssd_w_o"], "nn", f"{tag}_out", b_layer=0, epilogue=_add, extras=(h,))
    return h1, (z, xbc, raw, xh, bh, ch, dth, sin, y, yn)


def ssd_bwd(g_bf, hn, saved, w, tag):
    z, xbc, raw, xh, bh, ch, dth, sin, y, yn = saved
    t = z.shape[0]
    d_wo = matmul(yn, g_bf, "tn", f"{tag}_dwo", out_dtypes=(BF16,))
    dyn = matmul(g_bf, w["ssd_w_o"], "nt", f"{tag}_dyn", b_layer=0)
    grid, data, params, outs = _ssd_gate_specs(y, z, w["ssd_norm_g"])
    (dy, dz), (dng,) = blockwise_bwd(_ssd_gate_f, grid, data, params, outs, [dyn], f"{tag}_gate_bwd",
                                     grad_dtypes=[F32, BF16])
    dxh, ddth, dbh, dch, dal, ddk = ssd_scan_bwd(xh, dth, bh, ch, w["ssd_a_log"], w["ssd_d"], sin,
                                                 _to_heads(dy, SSD_HEADS, SSD_HEAD_DIM), f"{tag}_scan_bwd")
    dact = jnp.concatenate([_from_heads(dxh), _from_heads(dbh), _from_heads(dch)], axis=1)
    dxbc, dcw, dcb = conv_bwd(xbc, w["ssd_conv_w"], w["ssd_conv_b"], dact, f"{tag}_conv_bwd")
    ddt = jnp.pad(_from_heads(ddth), ((0, 0), (0, DT_PAD - SSD_HEADS)))
    grid, data, params, outs = _dt_specs(raw, w["ssd_dt_bias"])
    (draw,), (dbias,) = blockwise_bwd(_dt_f, grid, data, params, outs, [ddt], f"{tag}_dt_bwd",
                                      grad_dtypes=[BF16])
    d_wz = matmul(hn, dz, "tn", f"{tag}_dwz", out_dtypes=(BF16,))
    d_wx = matmul(hn, dxbc, "tn", f"{tag}_dwx", out_dtypes=(BF16,))
    d_wdt = matmul(hn, draw, "tn", f"{tag}_dwdt", out_dtypes=(BF16,))
    dhn = matmul(dz, w["ssd_wz"], "nt", f"{tag}_dhn_z")
    dhn = matmul(dxbc, w["ssd_wxbc"], "nt", f"{tag}_dhn_x", epilogue=_add, extras=(dhn,))
    dhn = matmul(draw, w["ssd_wdt"], "nt", f"{tag}_dhn_dt", epilogue=_add, extras=(dhn,))
    d_win = jnp.concatenate([d_wz, d_wx, d_wdt[:, :SSD_HEADS]], axis=1)
    return dhn, {"ssd_w_in": d_win, "ssd_w_o": d_wo, "ssd_conv_w": dcw, "ssd_conv_b": dcb,
                 "ssd_dt_bias": dbias[:, :SSD_HEADS], "ssd_a_log": dal, "ssd_d": ddk, "ssd_norm_g": dng}


def _relu2(acc):
    return (acc, jnp.square(jnp.maximum(acc, 0.0)))


def _relu2_bwd(acc, a):
    return (acc * (2.0 * jnp.maximum(a, 0.0)),)


def local_step(x, target, w):
    h = x
    saved = []
    for i in range(DEPTH):
        kind, j = i % N_MIXERS, i // N_MIXERS
        tag = f"l{i}"
        hn = rmsnorm_fwd(h, w["norm_mix_g"][i], f"{tag}_norm_mix")
        if kind == 0:
            h1, ms = sb_fwd(h, hn, w, j, f"{tag}_sb")
        elif kind == 1:
            h1, ms = gm_fwd(h, hn, w, j, f"{tag}_gm")
        else:
            h1, ms = ssd_fwd(h, hn, w, f"{tag}_ssd")
        hm = rmsnorm_fwd(h1, w["norm_mlp_g"][i], f"{tag}_norm_mlp")
        a, r = matmul(hm, w["mlp_w_in"], "nn", f"{tag}_mlp_in", b_layer=i, out_dtypes=(F32, BF16),
                      epilogue=_relu2)
        h2 = matmul(r, w["mlp_w_out"], "nn", f"{tag}_mlp_out", b_layer=i, epilogue=_add, extras=(h1,))
        saved.append((h, hn, ms, h1, hm, a, r))
        h = h2
    dh, loss = loss_call(h, target)

    grads = {"mlp_w_in": [None] * DEPTH, "mlp_w_out": [None] * DEPTH, "norm_mix_g": [None] * DEPTH,
             "norm_mlp_g": [None] * DEPTH, "sb_w_qkv": [None] * 2, "sb_w_o": [None] * 2, "sb_gqk": [None] * 2}
    for i in reversed(range(DEPTH)):
        kind, j = i % N_MIXERS, i // N_MIXERS
        tag = f"l{i}"
        h0, hn, ms, h1, hm, a, r = saved[i]
        g_bf = dh.astype(BF16)
        grads["mlp_w_out"][i] = matmul(r, g_bf, "tn", f"{tag}_mlp_dwout", out_dtypes=(BF16,))
        da = matmul(g_bf, w["mlp_w_out"], "nt", f"{tag}_mlp_da", b_layer=i, out_dtypes=(BF16,),
                    epilogue=_relu2_bwd, extras=(a,))
        grads["mlp_w_in"][i] = matmul(hm, da, "tn", f"{tag}_mlp_dwin", out_dtypes=(BF16,))
        dhm = matmul(da, w["mlp_w_in"], "nt", f"{tag}_mlp_dhm", b_layer=i)
        dh1, grads["norm_mlp_g"][i] = rmsnorm_bwd(h1, w["norm_mlp_g"][i], dhm, dh, f"{tag}_norm_mlp_bwd")
        g_bf = dh1.astype(BF16)
        if kind == 0:
            dhn, mg = sb_bwd(g_bf, hn, ms, w, j, f"{tag}_sb")
            for kname, val in mg.items():
                grads[kname][j] = val
        elif kind == 1:
            dhn, mg = gm_bwd(g_bf, hn, ms, w, j, f"{tag}_gm")
            grads.update(mg)
        else:
            dhn, mg = ssd_bwd(g_bf, hn, ms, w, f"{tag}_ssd")
            grads.update(mg)
        dh, grads["norm_mix_g"][i] = rmsnorm_bwd(h0, w["norm_mix_g"][i], dhn, dh1, f"{tag}_norm_mix_bwd")
    return loss, dh, grads


def _my_pos():
    return lax.axis_index("x"), lax.axis_index("y"), lax.axis_index("c")


def _lin(p):
    return 4 * p[0] + 2 * p[1] + p[2]


def _shard_view(ref, kind, width, idx, layer=None):
    if kind == "col":
        start = pl.multiple_of(idx * width, LANES)
        return ref.at[:, :, pl.ds(start, width)] if layer is None else ref.at[layer, :, pl.ds(start, width)]
    if kind == "row":
        start = pl.multiple_of(idx * width, 16)
        return ref.at[:, pl.ds(start, width), :] if layer is None else ref.at[layer, pl.ds(start, width), :]
    assert kind == "slot"
    return ref.at[idx]


def allgather(shards, kinds):
    n = len(shards)
    out_shapes, widths = [], []
    for s, kind in zip(shards, kinds):
        if kind == "col":
            out_shapes.append(jax.ShapeDtypeStruct(s.shape[:2] + (s.shape[2] * N_DEV,), s.dtype))
            widths.append(s.shape[2])
        elif kind == "row":
            out_shapes.append(jax.ShapeDtypeStruct((s.shape[0], s.shape[1] * N_DEV, s.shape[2]), s.dtype))
            widths.append(s.shape[1])
        else:
            out_shapes.append(jax.ShapeDtypeStruct((N_DEV,) + s.shape, s.dtype))
            widths.append(None)

    def body(*refs):
        in_refs, out_refs = refs[:n], refs[n:2 * n]
        send_sems, recv_sems, local_sems = refs[2 * n:]
        x, y, c = _my_pos()
        me, sibling = (x, y, c), (x, y, 1 - c)
        chips = [(1 - x, y), (x, 1 - y), (1 - x, 1 - y)]

        def part(a, block):
            return _shard_view(out_refs[a], kinds[a], widths[a], _lin(block))

        def copy(a, k, block, to, src=None):
            return pltpu.make_async_remote_copy(
                src_ref=part(a, block) if src is None else src, dst_ref=part(a, block),
                send_sem=send_sems.at[a * 7 + k], recv_sem=recv_sems.at[a * 7 + k],
                device_id=to, device_id_type=MESH)

        mine, first, passed = [], [], []
        for a in range(n):
            mine.append(pltpu.make_async_copy(in_refs[a], part(a, me), local_sems.at[a]))
            mine[-1].start()
            first.append(copy(a, 0, me, sibling, src=in_refs[a]))
            first += [copy(a, 1 + j, me, (*chip, c), src=in_refs[a]) for j, chip in enumerate(chips)]
        for cp in first:
            cp.start()
        for a in range(n):
            for j, chip in enumerate(chips):
                copy(a, 1 + j, (*chip, c), me).wait_recv()
                passed.append(copy(a, 4 + j, (*chip, c), sibling))
                passed[-1].start()
        for a in range(n):
            copy(a, 0, sibling, me).wait_recv()
            for j, chip in enumerate(chips):
                copy(a, 4 + j, (*chip, 1 - c), me).wait_recv()
        for cp in first + passed:
            cp.wait_send()
        for cp in mine:
            cp.wait()

    return pl.pallas_call(
        body, name="allgather_weights", in_specs=[HBM_SPEC] * n, out_specs=[HBM_SPEC] * n,
        out_shape=out_shapes,
        scratch_shapes=[pltpu.SemaphoreType.DMA((7 * n,)), pltpu.SemaphoreType.DMA((7 * n,)),
                        pltpu.SemaphoreType.DMA((n,))],
    )(*shards)


_FLIPS = [(0, 0, 1), (0, 1, 0), (0, 1, 1), (1, 0, 0), (1, 0, 1), (1, 1, 0), (1, 1, 1)]


def _flip(pos, f):
    return tuple((1 - p) if b else p for p, b in zip(pos, f))


def scatter_grads(items):
    n = len(items)
    tensors = []
    for it in items:
        if it[5] not in [tn for tn, _ in tensors]:
            tensors.append((it[5], jax.ShapeDtypeStruct((N_DEV,) + it[4], it[0].dtype)))
    tix = {tn: k for k, (tn, _) in enumerate(tensors)}

    def body(*refs):
        in_refs, out_refs = refs[:n], refs[n:n + len(tensors)]
        send_sems, recv_sems, local_sems = refs[n + len(tensors):]
        me = _my_pos()

        def src(a, dest):
            g, kind, width, layer, _, _ = items[a]
            if kind == "all":
                return in_refs[a]
            if kind == "slot":
                return in_refs[a].at[_lin(dest)]
            if kind == "col":
                return in_refs[a].at[:, pl.ds(pl.multiple_of(_lin(dest) * width, LANES), width)]
            return in_refs[a].at[pl.ds(pl.multiple_of(_lin(dest) * width, 16), width), :]

        def dst(a, sender):
            layer = items[a][3]
            buf = out_refs[tix[items[a][5]]]
            return buf.at[_lin(sender)] if layer is None else buf.at[_lin(sender), layer]

        def copy(a, k):
            peer = _flip(me, _FLIPS[k])
            return pltpu.make_async_remote_copy(
                src_ref=src(a, peer), dst_ref=dst(a, me),
                send_sem=send_sems.at[a * 7 + k], recv_sem=recv_sems.at[a * 7 + k],
                device_id=peer, device_id_type=MESH)

        def arrival(a, k):
            peer = _flip(me, _FLIPS[k])
            return pltpu.make_async_remote_copy(
                src_ref=src(a, me), dst_ref=dst(a, peer),
                send_sem=send_sems.at[a * 7 + k], recv_sem=recv_sems.at[a * 7 + k],
                device_id=peer, device_id_type=MESH)

        mine = [pltpu.make_async_copy(src(a, me), dst(a, me), local_sems.at[a]) for a in range(n)]
        for cp in mine:
            cp.start()
        sends = [copy(a, k) for a in range(n) for k in range(7)]
        for cp in sends:
            cp.start()
        for a in range(n):
            for k in range(7):
                arrival(a, k).wait_recv()
        for cp in sends:
            cp.wait_send()
        for cp in mine:
            cp.wait()

    res = pl.pallas_call(
        body, name="scatter_grads", in_specs=[HBM_SPEC] * n, out_specs=[HBM_SPEC] * len(tensors),
        out_shape=[s for _, s in tensors],
        scratch_shapes=[pltpu.SemaphoreType.DMA((7 * n,)), pltpu.SemaphoreType.DMA((7 * n,)),
                        pltpu.SemaphoreType.DMA((n,))],
    )(*[it[0] for it in items])
    return {tn: r for (tn, _), r in zip(tensors, res)}


_C1 = 1.0 - ADAM_B1 ** ADAM_STEP
_C2 = 1.0 - ADAM_B2 ** ADAM_STEP


def adamw(wt, m, v, slots, name):
    nl, r, c = wt.shape
    tr = _tile(r, 128)

    def body(w_ref, m_ref, v_ref, s_ref, g_ref, d_ref, nm_ref, nv_ref):
        g = s_ref[0].astype(F32)
        for k in range(1, N_DEV):
            g = g + s_ref[k].astype(F32)
        m_new = ADAM_B1 * m_ref[...] + (1.0 - ADAM_B1) * g
        v_new = ADAM_B2 * v_ref[...] + (1.0 - ADAM_B2) * jnp.square(g)
        m_hat = m_new / _C1
        v_hat = v_new / _C2
        g_ref[...] = g
        d_ref[...] = -ADAM_LR * (m_hat / (jnp.sqrt(v_hat) + ADAM_EPS) + ADAM_WD * w_ref[...])
        nm_ref[...] = m_new
        nv_ref[...] = v_new

    blk = pl.BlockSpec((None, tr, c), lambda l, i: (l, i, 0))
    sblk = pl.BlockSpec((N_DEV, None, tr, c), lambda l, i: (0, l, i, 0))
    return pl.pallas_call(
        body, name=name, grid=(nl, r // tr), in_specs=[blk, blk, blk, sblk], out_specs=[blk] * 4,
        out_shape=[jax.ShapeDtypeStruct(wt.shape, F32)] * 4,
        compiler_params=_cparams("parallel", "parallel"),
    )(wt, m, v, slots)


def _pack(arrays, rows):
    flat = jnp.concatenate([a.reshape(-1).astype(F32) for a in arrays])
    return jnp.pad(flat, (0, rows * LANES - flat.shape[0])).reshape(rows, LANES)


def _unpack(packed, shapes):
    flat = packed.reshape(-1)
    out, off = [], 0
    for s in shapes:
        sz = math.prod(s)
        out.append(flat[off:off + sz].reshape(s))
        off += sz
    return out


SMALL_W = 768


def _pack_small_shard(conv_w, conv_b, norm_g):
    ng = jnp.pad(norm_g, ((0, 0), (0, SMALL_W - norm_g.shape[1])))
    return jnp.concatenate([conv_w[0], conv_b, ng, jnp.zeros((2, SMALL_W), F32)], axis=0)


def _unpack_small_shard(p):
    return p[0:4][None], p[4:5], p[5:6, :SSD_INNER // N_DEV]


REPL = ["norm_mix_g", "norm_mlp_g", "sb_q_norm_g", "sb_k_norm_g", "gm_v_norm_g", "gm_w_s", "gm_b_s",
        "ssd_dt_bias", "ssd_a_log", "ssd_d"]
BIG = ["sb_w_qkv", "sb_w_o", "gm_w_in", "gm_w_o", "ssd_w_in", "ssd_w_o", "mlp_w_in", "mlp_w_out"]
BIG_KIND = {"sb_w_qkv": "col", "sb_w_o": "row", "gm_w_in": "col", "gm_w_o": "row", "ssd_w_in": "slot",
            "ssd_w_o": "row", "mlp_w_in": "col", "mlp_w_out": "row"}
SMALL_SHARDED = ["ssd_conv_w", "ssd_conv_b", "ssd_norm_g"]
WEIGHTS = ["norm_mix_g", "norm_mlp_g", "sb_w_qkv", "sb_q_norm_g", "sb_k_norm_g", "sb_w_o", "gm_w_in",
           "gm_v_norm_g", "gm_w_s", "gm_b_s", "gm_w_o", "ssd_w_in", "ssd_conv_w", "ssd_conv_b", "ssd_dt_bias",
           "ssd_a_log", "ssd_d", "ssd_norm_g", "ssd_w_o", "mlp_w_in", "mlp_w_out"]


def _model_weights(p, gathered, small):
    w = {k: gathered[k] for k in BIG if k != "ssd_w_in"}
    win = jnp.moveaxis(gathered["ssd_w_in"][:, 0], 0, 1).reshape(D_MODEL, SSD_PROJ)
    w["ssd_wz"] = win[:, :SSD_INNER]
    w["ssd_wxbc"] = win[:, SSD_INNER:SSD_INNER + SSD_CONV_DIM]
    w["ssd_wdt"] = jnp.pad(win[:, SSD_INNER + SSD_CONV_DIM:], ((0, 0), (0, DT_PAD - SSD_HEADS)))
    w["ssd_conv_w"] = small[:, 0:4].transpose(1, 0, 2).reshape(SSD_CONV, SSD_CONV_DIM)
    w["ssd_conv_b"] = small[:, 4].reshape(1, SSD_CONV_DIM)
    w["ssd_norm_g"] = small[:, 5:6, :SSD_INNER // N_DEV]
    w["norm_mix_g"] = [p["norm_mix_g"][i:i + 1] for i in range(DEPTH)]
    w["norm_mlp_g"] = [p["norm_mlp_g"][i:i + 1] for i in range(DEPTH)]
    w["sb_gqk"] = [jnp.stack([p["sb_q_norm_g"][j:j + 1], p["sb_k_norm_g"][j:j + 1]]) for j in range(2)]
    w["gm_v_norm_g"] = p["gm_v_norm_g"]
    w["gm_w_s"] = p["gm_w_s"][0]
    w["gm_b_s"] = p["gm_b_s"][0][:, :, None]
    w["ssd_dt_bias"] = jnp.pad(p["ssd_dt_bias"], ((0, 0), (0, DT_PAD - SSD_HEADS)))
    w["ssd_a_log"] = p["ssd_a_log"].reshape(SSD_HEADS, 1, 1)
    w["ssd_d"] = p["ssd_d"].reshape(SSD_HEADS, 1, 1)
    return w


def _repl_grads(g):
    return [jnp.concatenate(g["norm_mix_g"], axis=0), jnp.concatenate(g["norm_mlp_g"], axis=0),
            jnp.concatenate([g["sb_gqk"][0][0], g["sb_gqk"][1][0]], axis=0),
            jnp.concatenate([g["sb_gqk"][0][1], g["sb_gqk"][1][1]], axis=0),
            g["gm_v_norm_g"], g["gm_w_s"][None], g["gm_b_s"][None, :, :, 0],
            g["ssd_dt_bias"], g["ssd_a_log"].reshape(1, SSD_HEADS), g["ssd_d"].reshape(1, SSD_HEADS)]


def kernel(x, norm_mix_g, norm_mlp_g, sb_w_qkv, sb_q_norm_g, sb_k_norm_g, sb_w_o, gm_w_in, gm_v_norm_g, gm_w_s, gm_b_s, gm_w_o, ssd_w_in, ssd_conv_w, ssd_conv_b, ssd_dt_bias, ssd_a_log, ssd_d, ssd_norm_g, ssd_w_o, mlp_w_in, mlp_w_out, loss_target, m_norm_mix_g, m_norm_mlp_g, m_sb_w_qkv, m_sb_q_norm_g, m_sb_k_norm_g, m_sb_w_o, m_gm_w_in, m_gm_v_norm_g, m_gm_w_s, m_gm_b_s, m_gm_w_o, m_ssd_w_in, m_ssd_conv_w, m_ssd_conv_b, m_ssd_dt_bias, m_ssd_a_log, m_ssd_d, m_ssd_norm_g, m_ssd_w_o, m_mlp_w_in, m_mlp_w_out, v_norm_mix_g, v_norm_mlp_g, v_sb_w_qkv, v_sb_q_norm_g, v_sb_k_norm_g, v_sb_w_o, v_gm_w_in, v_gm_v_norm_g, v_gm_w_s, v_gm_b_s, v_gm_w_o, v_ssd_w_in, v_ssd_conv_w, v_ssd_conv_b, v_ssd_dt_bias, v_ssd_a_log, v_ssd_d, v_ssd_norm_g, v_ssd_w_o, v_mlp_w_in, v_mlp_w_out):
    args = dict(locals())
    p = {k: args[k] for k in WEIGHTS}
    pm = {k: args["m_" + k] for k in WEIGHTS}
    pv = {k: args["v_" + k] for k in WEIGHTS}

    shards = [p[k].astype(BF16) for k in BIG] + [_pack_small_shard(*[p[k] for k in SMALL_SHARDED])]
    gathered = allgather(shards, [BIG_KIND[k] for k in BIG] + ["slot"])
    w = _model_weights(p, dict(zip(BIG, gathered[:-1])), gathered[-1])

    loss, grad_x, g = local_step(x[0], loss_target[0], w)
    loss = lax.psum(loss[0, 0], ("x", "y", "c"))

    items = []
    for k in BIG:
        kind = BIG_KIND[k]
        shard = p[k].shape
        if kind == "slot":
            gi = g[k].reshape(D_MODEL, N_DEV, SSD_PROJ // N_DEV).transpose(1, 0, 2)
            items.append((gi, "slot", None, 0, shard, k))
            continue
        per_layer = g[k] if isinstance(g[k], list) else [g[k]]
        width = shard[2] if kind == "col" else shard[1]
        for layer, gl in enumerate(per_layer):
            items.append((gl, kind, width, layer, shard, k))
    dcw = g["ssd_conv_w"].reshape(SSD_CONV, N_DEV, SMALL_W).transpose(1, 0, 2)
    dcb = g["ssd_conv_b"].reshape(N_DEV, 1, SMALL_W)
    dng = jnp.pad(g["ssd_norm_g"], ((0, 0), (0, 0), (0, SMALL_W - SSD_INNER // N_DEV)))
    small_g = jnp.concatenate([dcw, dcb, dng, jnp.zeros((N_DEV, 2, SMALL_W), F32)], axis=1)
    items.append((small_g, "slot", None, None, (8, SMALL_W), "small"))
    repl_shapes = [p[k].shape for k in REPL]
    n_repl = sum(math.prod(s) for s in repl_shapes)
    repl_rows = -(-n_repl // (LANES * LANES)) * LANES
    items.append((_pack(_repl_grads(g), repl_rows), "all", None, None, (repl_rows, LANES), "repl"))
    recv = scatter_grads(items)

    out = {}
    for k in BIG:
        s = p[k].shape
        out[k] = adamw(p[k], pm[k], pv[k], recv[k], f"adamw_{k}")
    sm = adamw(*[_pack_small_shard(*[d[k] for k in SMALL_SHARDED])[None] for d in (p, pm, pv)],
               recv["small"][:, None], "adamw_small")
    for k, vals in zip(SMALL_SHARDED, zip(*[_unpack_small_shard(r[0]) for r in sm])):
        out[k] = list(vals)
    rp = adamw(*[_pack([d[k] for k in REPL], repl_rows)[None] for d in (p, pm, pv)],
               recv["repl"][:, None], "adamw_repl")
    for k, vals in zip(REPL, zip(*[_unpack(r[0], repl_shapes) for r in rp])):
        out[k] = list(vals)
    res = [loss, grad_x[None]]
    for q in range(4):
        res += [out[k][q] for k in WEIGHTS]
    return tuple(res)
```

```python
import functools
import math

import jax
import jax.numpy as jnp
from jax import lax
from jax.experimental import pallas as pl
from jax.experimental.pallas import tpu as pltpu

F32 = jnp.float32
BF16 = jnp.bfloat16

D_MODEL = 2048
DEPTH = 4
N_MIXERS = 3
EPS = 1e-6
SB_HEADS = 16
HEAD_DIM = 128
GM_CHUNK = 128
GM_GROUPS = 16
SSD_INNER = 4096
SSD_HEADS = 64
SSD_HEAD_DIM = 64
SSD_GROUPS = 8
SSD_HPG = 8
SSD_STATE = 128
SSD_CHUNK = 128
SSD_CONV = 4
SSD_BC = SSD_GROUPS * SSD_STATE
SSD_CONV_DIM = SSD_INNER + 2 * SSD_BC
SSD_PROJ = SSD_INNER + SSD_CONV_DIM + SSD_HEADS
DT_PAD = 128
N_DEV = 8
LANES = 128

ADAM_LR = 0.001
ADAM_B1 = 0.9
ADAM_B2 = 0.999
ADAM_EPS = 1e-08
ADAM_WD = 0.01
ADAM_STEP = 10

VMEM_LIMIT_BYTES = 56 * 1024 * 1024
MESH = pl.DeviceIdType.MESH
HBM_SPEC = pl.BlockSpec(memory_space=pl.ANY)


def _cparams(*sem):
    return pltpu.CompilerParams(dimension_semantics=sem, vmem_limit_bytes=VMEM_LIMIT_BYTES)


def _tile(dim, pref):
    t = min(dim, pref)
    assert dim % t == 0, (dim, pref)
    return t


_DOT_DIMS = {"nn": ((1,), (0,)), "nt": ((1,), (1,)), "tn": ((0,), (0,))}


def matmul(a, b, form, name, *, b_layer=None, out_dtypes=(F32,), epilogue=None, extras=(),
           tm=1024, tn=1024, tk=2048):
    bs = b.shape[-2:]
    if form == "nn":
        (m, k), (k2, n) = a.shape, bs
    elif form == "nt":
        (m, k), (n, k2) = a.shape, bs
    else:
        (k, m), (k2, n) = a.shape, bs
    assert k == k2, (a.shape, b.shape, form)
    tm, tn, tk = _tile(m, tm), _tile(n, tn), _tile(k, tk)
    grid = (m // tm, n // tn, k // tk)
    nk = grid[2]
    if form == "tn":
        a_spec = pl.BlockSpec((tk, tm), lambda i, j, kk: (kk, i))
    else:
        a_spec = pl.BlockSpec((tm, tk), lambda i, j, kk: (i, kk))
    if form == "nt":
        b_blk, b_idx = (tn, tk), (lambda i, j, kk: (j, kk))
    else:
        b_blk, b_idx = (tk, tn), (lambda i, j, kk: (kk, j))
    if b.ndim == 3:
        b_spec = pl.BlockSpec((None,) + b_blk, lambda i, j, kk: (b_layer,) + b_idx(i, j, kk))
    else:
        b_spec = pl.BlockSpec(b_blk, b_idx)
    o_spec = pl.BlockSpec((tm, tn), lambda i, j, kk: (i, j))
    n_extra, n_out = len(extras), len(out_dtypes)
    dims = (_DOT_DIMS[form], ((), ()))

    def body(a_ref, b_ref, *rest):
        extra_refs, out_refs = rest[:n_extra], rest[n_extra:n_extra + n_out]
        kk = pl.program_id(2)

        def product():
            return lax.dot_general(a_ref[...], b_ref[...], dims, preferred_element_type=F32)

        def finish(acc):
            outs = (acc,) if epilogue is None else epilogue(acc, *[r[...] for r in extra_refs])
            for r, o in zip(out_refs, outs):
                r[...] = o.astype(r.dtype)

        if nk == 1:
            finish(product())
            return
        acc_ref = rest[-1]

        @pl.when(kk == 0)
        def _():
            acc_ref[...] = product()

        @pl.when(jnp.logical_and(kk > 0, kk < nk - 1))
        def _():
            acc_ref[...] += product()

        @pl.when(kk == nk - 1)
        def _():
            finish(acc_ref[...] + product())

    res = pl.pallas_call(
        body, name=name, grid=grid,
        in_specs=[a_spec, b_spec] + [o_spec] * n_extra,
        out_specs=[o_spec] * n_out,
        out_shape=[jax.ShapeDtypeStruct((m, n), dt) for dt in out_dtypes],
        scratch_shapes=[pltpu.VMEM((tm, tn), F32)] if nk > 1 else [],
        compiler_params=_cparams("parallel", "parallel", "arbitrary"),
    )(a, b, *extras)
    return res[0] if n_out == 1 else res


class Data:
    def __init__(self, array, block, imap):
        self.array, self.block, self.imap = array, block, imap


class Param:
    def __init__(self, array, block, imap, group):
        self.array, self.block, self.imap, self.group = array, block, imap, group


class Out:
    def __init__(self, shape, dtype, block, imap):
        self.shape, self.dtype, self.block, self.imap = shape, dtype, block, imap


def _f32(v):
    return v.astype(F32)


def blockwise_fwd(f, grid, data, params, outs, name):
    nd, npar = len(data), len(params)

    def body(*refs):
        vals = [_f32(r[...]) for r in refs[:nd + npar]]
        res = f(*vals)
        for r, o in zip(refs[nd + npar:], res):
            r[...] = o.astype(r.dtype)

    in_specs = [pl.BlockSpec(d.block, d.imap) for d in data]
    in_specs += [pl.BlockSpec(p.block, (lambda o, i, _m=p.imap: _m(o))) for p in params]
    res = pl.pallas_call(
        body, name=name, grid=grid, in_specs=in_specs,
        out_specs=[pl.BlockSpec(o.block, o.imap) for o in outs],
        out_shape=[jax.ShapeDtypeStruct(o.shape, o.dtype) for o in outs],
        compiler_params=_cparams("parallel", "parallel"),
    )(*[d.array for d in data], *[p.array for p in params])
    return res


def blockwise_bwd(f, grid, data, params, outs, cts, name, *, grad_dtypes, accum=None):
    nd, npar, no = len(data), len(params), len(outs)
    accum = accum or {}
    want = [k for k in range(nd) if grad_dtypes[k] is not None]
    acc_keys = sorted(accum)

    def body(*refs):
        in_refs = refs[:nd + npar]
        ct_refs = refs[nd + npar:nd + npar + no]
        acc_refs = refs[nd + npar + no:nd + npar + no + len(acc_keys)]
        out_refs = refs[nd + npar + no + len(acc_keys):]
        dd_refs, dp_refs = out_refs[:len(want)], out_refs[len(want):]
        o, i = pl.program_id(0), pl.program_id(1)
        vals = [_f32(r[...]) for r in in_refs]
        _, vjp = jax.vjp(f, *vals)
        grads = vjp(tuple(_f32(r[...]) for r in ct_refs))
        for r, k in zip(dd_refs, want):
            g = grads[k]
            if k in accum:
                g = g + _f32(acc_refs[acc_keys.index(k)][...])
            r[...] = g.astype(r.dtype)
        for r, p, g in zip(dp_refs, params, grads[nd:]):
            first = jnp.logical_and(i == 0, o % p.group == 0)

            @pl.when(first)
            def _():
                r[...] = g.astype(r.dtype)

            @pl.when(jnp.logical_not(first))
            def _():
                r[...] += g.astype(r.dtype)

    d_specs = [pl.BlockSpec(d.block, d.imap) for d in data]
    p_specs = [pl.BlockSpec(p.block, (lambda o, i, _m=p.imap: _m(o))) for p in params]
    ct_specs = [pl.BlockSpec(o.block, o.imap) for o in outs]
    acc_specs = [d_specs[k] for k in acc_keys]
    res = pl.pallas_call(
        body, name=name, grid=grid,
        in_specs=d_specs + p_specs + ct_specs + acc_specs,
        out_specs=[d_specs[k] for k in want] + p_specs,
        out_shape=[jax.ShapeDtypeStruct(data[k].array.shape, grad_dtypes[k]) for k in want]
        + [jax.ShapeDtypeStruct(p.array.shape, F32) for p in params],
        compiler_params=_cparams("arbitrary", "arbitrary"),
    )(*[d.array for d in data], *[p.array for p in params], *cts, *[accum[k] for k in acc_keys])
    return list(res[:len(want)]), list(res[len(want):])


def _rmsnorm_f(x, g):
    return (x * lax.rsqrt(jnp.mean(x * x, axis=-1, keepdims=True) + EPS) * g,)


def _gelu(x):
    return 0.5 * x * (1.0 + lax.erf(x * (2.0 ** -0.5)))


def _silu(x):
    return x / (1.0 + jnp.exp(-x))


def _softplus(x):
    return jnp.maximum(x, 0.0) + jnp.log1p(jnp.exp(-jnp.abs(x)))


def _gm_act_f(pu, pv, gv):
    return (_gelu(pu),) + _rmsnorm_f(_gelu(pv), gv)


def _gm_gate_f(u, vn, w, b):
    n = w.shape[0]
    row = lax.broadcasted_iota(jnp.int32, (n, n), 0)
    col = lax.broadcasted_iota(jnp.int32, (n, n), 1)
    wm = jnp.where(row >= col, w, 0.0)
    return (u * (jnp.dot(wm, vn, preferred_element_type=F32) + b),)


def _ssd_gate_f(y, z, g):
    return _rmsnorm_f(y * _silu(z), g)


def _dt_f(raw, bias):
    return (_softplus(raw + bias),)


def _row_tile(t):
    return _tile(t, 256)


def _norm_specs(h, g_row):
    t, d = h.shape
    tr = _row_tile(t)
    data = [Data(h, (tr, d), lambda o, i: (i, 0))]
    params = [Param(g_row, (1, d), lambda o: (0, 0), 1)]
    outs = [Out((t, d), BF16, (tr, d), lambda o, i: (i, 0))]
    return (1, t // tr), data, params, outs


def rmsnorm_fwd(h, g_row, name):
    grid, data, params, outs = _norm_specs(h, g_row)
    return blockwise_fwd(_rmsnorm_f, grid, data, params, outs, name)[0]


def rmsnorm_bwd(h, g_row, ct, skip, name):
    grid, data, params, outs = _norm_specs(h, g_row)
    (dh,), (dg,) = blockwise_bwd(_rmsnorm_f, grid, data, params, outs, [ct], name,
                                 grad_dtypes=[F32], accum={0: skip})
    return dh, dg


def loss_call(y, target):
    t, d = y.shape
    tr = _row_tile(t)

    def body(y_ref, t_ref, dy_ref, loss_ref):
        e = y_ref[...] - t_ref[...]
        dy_ref[...] = e * (1.0 / d)

        @pl.when(pl.program_id(0) == 0)
        def _():
            loss_ref[...] = jnp.zeros_like(loss_ref)

        loss_ref[...] += jnp.sum(e * e).reshape(1, 1) * (0.5 / d)

    row = pl.BlockSpec((tr, d), lambda i: (i, 0))
    return pl.pallas_call(
        body, name="loss", grid=(t // tr,), in_specs=[row, row],
        out_specs=[row, pl.BlockSpec((1, 1), lambda i: (0, 0))],
        out_shape=[jax.ShapeDtypeStruct((t, d), F32), jax.ShapeDtypeStruct((1, 1), F32)],
        compiler_params=_cparams("arbitrary"),
    )(y, target)


ATT_TQ = 1024
ATT_KB = 256
ATT_RC = 1024
ATT_DIAG = ATT_TQ // ATT_KB


def _split2(x):
    hi = x.astype(BF16)
    return hi, (x - hi.astype(F32)).astype(BF16)


def _dot2(hi, lo, ones_bf):
    return (jnp.dot(hi, ones_bf, preferred_element_type=F32)
            + jnp.dot(lo, ones_bf, preferred_element_type=F32))


def _att_consts():
    r_ = lax.broadcasted_iota(jnp.int32, (ATT_KB, ATT_KB), 0)
    c_ = lax.broadcasted_iota(jnp.int32, (ATT_KB, ATT_KB), 1)
    upper = (r_ > c_).astype(BF16)
    lower = (r_ < c_).astype(BF16)
    ones = jnp.ones((ATT_KB, LANES), BF16)
    return upper, lower, ones


def _att_logits(q, k, scale, mask):
    z = lax.dot_general(q, k, (((1,), (1,)), ((), ())), preferred_element_type=F32) * scale
    lb = jnp.minimum(z, 0.0) - jnp.log1p(jnp.exp(-jnp.abs(z)))
    l1 = lb - z
    if mask is not None:
        l1 = jnp.where(mask, l1, 0.0)
    return lb, l1


def _diag_mask(rc, d):
    if d is None:
        return None
    row = rc * ATT_RC + lax.broadcasted_iota(jnp.int32, (ATT_RC, ATT_KB), 0)
    col = d * ATT_KB + lax.broadcasted_iota(jnp.int32, (ATT_RC, ATT_KB), 1)
    return col < row


def _wide(c):
    return jnp.concatenate([c] * (ATT_KB // LANES), axis=1)


def attn_fwd(qk, qkv_bf, name):
    t = qk.shape[0]
    tq = ATT_TQ
    assert t % tq == 0
    scale = HEAD_DIM ** -0.5

    def body(q_ref, k_ref, v_ref, o_ref, acc_ref, c_ref):
        qi = pl.program_id(1)
        upper, _, ones = _att_consts()
        acc_ref[...] = jnp.zeros_like(acc_ref)
        c_ref[...] = jnp.zeros_like(c_ref)

        def block(jb, diag):
            off = pl.multiple_of(jb * ATT_KB, ATT_KB)
            k = k_ref[pl.ds(off, ATT_KB), :]
            v = v_ref[pl.ds(off, ATT_KB), :]
            for rc in range(tq // ATT_RC):
                rows = pl.ds(rc * ATT_RC, ATT_RC)
                mask = _diag_mask(rc, diag)
                lb, l1 = _att_logits(q_ref[rows, :], k, scale, mask)
                hi, lo = _split2(l1)
                c = c_ref[rows, :]
                a = jnp.exp(lb + _wide(c) + _dot2(hi, lo, upper))
                if mask is not None:
                    a = jnp.where(mask, a, 0.0)
                acc_ref[rows, :] += jnp.dot(a.astype(BF16), v, preferred_element_type=F32)
                c_ref[rows, :] = c + _dot2(hi, lo, ones)

        for d in reversed(range(ATT_DIAG)):
            block(qi * ATT_DIAG + d, d)

        def step(s, carry):
            block(qi * ATT_DIAG - 1 - s, None)
            return carry

        lax.fori_loop(0, qi * ATT_DIAG, step, 0)
        o_ref[...] = acc_ref[...].astype(o_ref.dtype)

    return pl.pallas_call(
        body, name=name, grid=(SB_HEADS, t // tq),
        in_specs=[pl.BlockSpec((tq, HEAD_DIM), lambda h, i: (i, h)),
                  pl.BlockSpec((t, HEAD_DIM), lambda h, i: (0, SB_HEADS + h)),
                  pl.BlockSpec((t, HEAD_DIM), lambda h, i: (0, 2 * SB_HEADS + h))],
        out_specs=pl.BlockSpec((tq, HEAD_DIM), lambda h, i: (i, h)),
        out_shape=jax.ShapeDtypeStruct((t, SB_HEADS * HEAD_DIM), BF16),
        scratch_shapes=[pltpu.VMEM((tq, HEAD_DIM), F32), pltpu.VMEM((tq, LANES), F32)],
        compiler_params=_cparams("parallel", "arbitrary"),
    )(qk, qk, qkv_bf)


def attn_bwd(qk, qkv_bf, do, name):
    t = qk.shape[0]
    tq = ATT_TQ
    assert t % tq == 0
    nq = t // tq
    nkb = t // ATT_KB
    scale = HEAD_DIM ** -0.5
    tn_dims = (((0,), (0,)), ((), ()))

    def body(q_ref, k_ref, v_ref, do_ref, dq_ref, dk_ref, dv_ref,
             cbuf, c_ref, ce_ref, dq_acc, a_buf, dz_buf, dk_acc, dv_acc):
        qi = pl.program_id(1)
        upper, lower, ones = _att_consts()

        @pl.when(qi == 0)
        def _():
            dk_acc[...] = jnp.zeros_like(dk_acc)
            dv_acc[...] = jnp.zeros_like(dv_acc)

        c_ref[...] = jnp.zeros_like(c_ref)
        ce_ref[...] = jnp.zeros_like(ce_ref)
        dq_acc[...] = jnp.zeros_like(dq_acc)

        def pass1(jb, diag):
            off = pl.multiple_of(jb * ATT_KB, ATT_KB)
            k = k_ref[pl.ds(off, ATT_KB), :]
            for rc in range(tq // ATT_RC):
                rows = pl.ds(rc * ATT_RC, ATT_RC)
                _, l1 = _att_logits(q_ref[rows, :], k, scale, _diag_mask(rc, diag))
                hi, lo = _split2(l1)
                c = c_ref[rows, :]
                cbuf[jb, rows, :] = c
                c_ref[rows, :] = c + _dot2(hi, lo, ones)

        for d in reversed(range(ATT_DIAG)):
            pass1(qi * ATT_DIAG + d, d)

        def step1(s, carry):
            pass1(qi * ATT_DIAG - 1 - s, None)
            return carry

        lax.fori_loop(0, qi * ATT_DIAG, step1, 0)

        def pass2(jb, diag):
            off = pl.multiple_of(jb * ATT_KB, ATT_KB)
            k = k_ref[pl.ds(off, ATT_KB), :]
            v = v_ref[pl.ds(off, ATT_KB), :]
            for rc in range(tq // ATT_RC):
                rows = pl.ds(rc * ATT_RC, ATT_RC)
                mask = _diag_mask(rc, diag)
                lb, l1 = _att_logits(q_ref[rows, :], k, scale, mask)
                hi, lo = _split2(l1)
                a = jnp.exp(lb + _wide(cbuf[jb, rows, :]) + _dot2(hi, lo, upper))
                if mask is not None:
                    a = jnp.where(mask, a, 0.0)
                da = lax.dot_general(do_ref[rows, :], v, (((1,), (1,)), ((), ())), preferred_element_type=F32)
                e = da * a
                ehi, elo = _split2(e)
                ce = ce_ref[rows, :]
                big_e = _wide(ce) + _dot2(ehi, elo, lower)
                dz = (e - (e + big_e) * jnp.exp(lb)) * scale
                if mask is not None:
                    dz = jnp.where(mask, dz, 0.0)
                dz = dz.astype(BF16)
                a_buf[rows, :] = a.astype(BF16)
                dz_buf[rows, :] = dz
                dq_acc[rows, :] += jnp.dot(dz, k, preferred_element_type=F32)
                ce_ref[rows, :] = ce + _dot2(ehi, elo, ones)
            dk_acc[pl.ds(off, ATT_KB), :] += lax.dot_general(dz_buf[...], q_ref[...], tn_dims,
                                                             preferred_element_type=F32)
            dv_acc[pl.ds(off, ATT_KB), :] += lax.dot_general(a_buf[...], do_ref[...], tn_dims,
                                                             preferred_element_type=F32)

        def step2(jb, carry):
            pass2(jb, None)
            return carry

        lax.fori_loop(0, qi * ATT_DIAG, step2, 0)
        for d in range(ATT_DIAG):
            pass2(qi * ATT_DIAG + d, d)
        dq_ref[...] = dq_acc[...]

        @pl.when(qi == nq - 1)
        def _():
            dk_ref[...] = dk_acc[...]
            dv_ref[...] = dv_acc[...].astype(dv_ref.dtype)

    hd = SB_HEADS * HEAD_DIM
    dq, dk, dv = pl.pallas_call(
        body, name=name, grid=(SB_HEADS, nq),
        in_specs=[pl.BlockSpec((tq, HEAD_DIM), lambda h, i: (i, h)),
                  pl.BlockSpec((t, HEAD_DIM), lambda h, i: (0, SB_HEADS + h)),
                  pl.BlockSpec((t, HEAD_DIM), lambda h, i: (0, 2 * SB_HEADS + h)),
                  pl.BlockSpec((tq, HEAD_DIM), lambda h, i: (i, h))],
        out_specs=[pl.BlockSpec((tq, HEAD_DIM), lambda h, i: (i, h)),
                   pl.BlockSpec((t, HEAD_DIM), lambda h, i: (0, h)),
                   pl.BlockSpec((t, HEAD_DIM), lambda h, i: (0, h))],
        out_shape=[jax.ShapeDtypeStruct((t, hd), F32), jax.ShapeDtypeStruct((t, hd), F32),
                   jax.ShapeDtypeStruct((t, hd), BF16)],
        scratch_shapes=[pltpu.VMEM((nkb, tq, LANES), F32), pltpu.VMEM((tq, LANES), F32),
                        pltpu.VMEM((tq, LANES), F32), pltpu.VMEM((tq, HEAD_DIM), F32),
                        pltpu.VMEM((tq, ATT_KB), BF16), pltpu.VMEM((tq, ATT_KB), BF16),
                        pltpu.VMEM((t, HEAD_DIM), F32), pltpu.VMEM((t, HEAD_DIM), F32)],
        compiler_params=_cparams("arbitrary", "arbitrary"),
    )(qk, qk, qkv_bf, do)
    return jnp.concatenate([dq, dk], axis=1), dv


def _qknorm_specs(qkv, gqk):
    t = qkv.shape[0]
    tr = _tile(t, 1024)
    data = [Data(qkv, (tr, HEAD_DIM), lambda o, i: (i, o))]
    params = [Param(gqk, (None, 1, HEAD_DIM), lambda o: (o // SB_HEADS, 0, 0), SB_HEADS)]
    outs = [Out((t, 2 * SB_HEADS * HEAD_DIM), BF16, (tr, HEAD_DIM), lambda o, i: (i, o))]
    return (2 * SB_HEADS, t // tr), data, params, outs


CONV_COLS = 128


def _shift_down(x, s, rows):
    if s == 0:
        return x
    return jnp.where(rows < s, 0.0, pltpu.roll(x, s, 0))


def _shift_up(x, s, rows):
    if s == 0:
        return x
    n = x.shape[0]
    return jnp.where(rows >= n - s, 0.0, pltpu.roll(x, n - s, 0))


def conv_fwd(xbc, w, b, name):
    t, c = xbc.shape
    tc = _tile(c, CONV_COLS)

    def body(x_ref, w_ref, b_ref, o_ref):
        x = x_ref[...]
        rows = lax.broadcasted_iota(jnp.int32, x.shape, 0)
        y = jnp.broadcast_to(b_ref[...], x.shape)
        for kk in range(SSD_CONV):
            y = y + w_ref[kk:kk + 1, :] * _shift_down(x, SSD_CONV - 1 - kk, rows)
        o_ref[...] = _silu(y)

    col = pl.BlockSpec((t, tc), lambda j: (0, j))
    return pl.pallas_call(
        body, name=name, grid=(c // tc,),
        in_specs=[col, pl.BlockSpec((SSD_CONV, tc), lambda j: (0, j)), pl.BlockSpec((1, tc), lambda j: (0, j))],
        out_specs=col, out_shape=jax.ShapeDtypeStruct((t, c), F32),
        compiler_params=_cparams("parallel"),
    )(xbc, w, b)


def conv_bwd(xbc, w, b, dact, name):
    t, c = xbc.shape
    tc = _tile(c, CONV_COLS)

    def body(x_ref, w_ref, b_ref, g_ref, dx_ref, dw_ref, db_ref):
        x = x_ref[...]
        rows = lax.broadcasted_iota(jnp.int32, x.shape, 0)
        shifted = [_shift_down(x, SSD_CONV - 1 - kk, rows) for kk in range(SSD_CONV)]
        y = jnp.broadcast_to(b_ref[...], x.shape)
        for kk in range(SSD_CONV):
            y = y + w_ref[kk:kk + 1, :] * shifted[kk]
        sig = 1.0 / (1.0 + jnp.exp(-y))
        dy = g_ref[...] * (sig * (1.0 + y * (1.0 - sig)))
        dx = jnp.zeros_like(x)
        for kk in range(SSD_CONV):
            dx = dx + w_ref[kk:kk + 1, :] * _shift_up(dy, SSD_CONV - 1 - kk, rows)
            dw_ref[kk:kk + 1, :] = jnp.sum(dy * shifted[kk], axis=0, keepdims=True)
        dx_ref[...] = dx.astype(dx_ref.dtype)
        db_ref[...] = jnp.sum(dy, axis=0, keepdims=True)

    col = pl.BlockSpec((t, tc), lambda j: (0, j))
    wspec = pl.BlockSpec((SSD_CONV, tc), lambda j: (0, j))
    bspec = pl.BlockSpec((1, tc), lambda j: (0, j))
    return pl.pallas_call(
        body, name=name, grid=(c // tc,),
        in_specs=[col, wspec, bspec, col], out_specs=[col, wspec, bspec],
        out_shape=[jax.ShapeDtypeStruct((t, c), BF16), jax.ShapeDtypeStruct((SSD_CONV, c), F32),
                   jax.ShapeDtypeStruct((1, c), F32)],
        compiler_params=_cparams("parallel"),
    )(xbc, w, b, dact)


def _ssd_chunk(x, dt, bm, cm, sprev, alog, dskip):
    n = x.shape[0]
    hp = lax.Precision.HIGHEST
    a = -jnp.exp(alog)
    da = dt * a
    row = lax.broadcasted_iota(jnp.int32, (n, n), 0)
    col = lax.broadcasted_iota(jnp.int32, (n, n), 1)
    causal = row >= col
    dam = jnp.broadcast_to(da, (n, n))
    acum_l = jnp.dot(causal.astype(F32), dam, precision=hp, preferred_element_type=F32)
    acum_s = jnp.dot(jnp.ones((n, n), F32), jnp.where(row <= col, dam, 0.0), precision=hp,
                     preferred_element_type=F32)
    decay = jnp.exp(jnp.where(causal, acum_l - acum_s, -1e30))
    acum = jnp.sum(acum_l, axis=1, keepdims=True) * (1.0 / n)
    atot = jnp.sum(da, axis=0, keepdims=True)
    xs = x * dt
    cb = lax.dot_general(cm, bm, (((1,), (1,)), ((), ())), preferred_element_type=F32)
    y = jnp.dot(cb * decay, xs, preferred_element_type=F32)
    y = y + jnp.exp(acum) * lax.dot_general(cm, sprev, (((1,), (1,)), ((), ())), preferred_element_type=F32)
    y = y + dskip * x
    snew = jnp.exp(atot) * sprev + lax.dot_general(
        xs * jnp.exp(atot - acum), bm, (((0,), (0,)), ((), ())), preferred_element_type=F32)
    return y, snew


def _ssd_specs(t, rev):
    nc = t // SSD_CHUNK
    cidx = (lambda c: nc - 1 - c) if rev else (lambda c: c)
    x_spec = pl.BlockSpec((SSD_HPG, SSD_CHUNK, SSD_HEAD_DIM), lambda g, c: (g, cidx(c), 0))
    dt_spec = pl.BlockSpec((SSD_HPG, SSD_CHUNK, 1), lambda g, c: (g, cidx(c), 0))
    bc_spec = pl.BlockSpec((None, SSD_CHUNK, SSD_STATE), lambda g, c: (g, cidx(c), 0))
    sc_spec = pl.BlockSpec((SSD_HPG, 1, 1), lambda g, c: (g, 0, 0))
    st_spec = pl.BlockSpec((SSD_HPG, None, SSD_HEAD_DIM, SSD_STATE), lambda g, c: (g, cidx(c), 0, 0))
    return nc, x_spec, dt_spec, bc_spec, sc_spec, st_spec


def ssd_scan_fwd(x, dt, bm, cm, alog, dskip, name):
    t = x.shape[1]
    nc, x_spec, dt_spec, bc_spec, sc_spec, st_spec = _ssd_specs(t, False)

    def body(x_ref, dt_ref, b_ref, c_ref, al_ref, d_ref, y_ref, sin_ref, state):
        @pl.when(pl.program_id(1) == 0)
        def _():
            state[...] = jnp.zeros_like(state)

        bm_, cm_ = b_ref[...], c_ref[...]

        def head(r, carry):
            sprev = state[r]
            sin_ref[r] = sprev
            y, snew = _ssd_chunk(x_ref[r], dt_ref[r], bm_, cm_, sprev, al_ref[r], d_ref[r])
            y_ref[r] = y
            state[r] = snew
            return carry

        lax.fori_loop(0, SSD_HPG, head, 0, unroll=2)

    return pl.pallas_call(
        body, name=name, grid=(SSD_GROUPS, nc),
        in_specs=[x_spec, dt_spec, bc_spec, bc_spec, sc_spec, sc_spec],
        out_specs=[x_spec, st_spec],
        out_shape=[jax.ShapeDtypeStruct(x.shape, F32),
                   jax.ShapeDtypeStruct((SSD_HEADS, nc, SSD_HEAD_DIM, SSD_STATE), F32)],
        scratch_shapes=[pltpu.VMEM((SSD_HPG, SSD_HEAD_DIM, SSD_STATE), F32)],
        compiler_params=_cparams("parallel", "arbitrary"),
    )(x, dt, bm, cm, alog, dskip)


def ssd_scan_bwd(x, dt, bm, cm, alog, dskip, sin, dy, name):
    t = x.shape[1]
    nc, x_spec, dt_spec, bc_spec, sc_spec, st_spec = _ssd_specs(t, True)

    def body(x_ref, dt_ref, b_ref, c_ref, al_ref, d_ref, sin_ref, dy_ref,
             dx_ref, ddt_ref, db_ref, dc_ref, dal_ref, dd_ref, dstate):
        @pl.when(pl.program_id(1) == 0)
        def _():
            dstate[...] = jnp.zeros_like(dstate)
            dal_ref[...] = jnp.zeros_like(dal_ref)
            dd_ref[...] = jnp.zeros_like(dd_ref)

        db_ref[...] = jnp.zeros_like(db_ref)
        dc_ref[...] = jnp.zeros_like(dc_ref)
        bm_, cm_ = b_ref[...], c_ref[...]

        def head(r, carry):
            _, vjp = jax.vjp(_ssd_chunk, x_ref[r], dt_ref[r], bm_, cm_, sin_ref[r], al_ref[r], d_ref[r])
            dx, ddt, dbm, dcm, dsp, dal, dd = vjp((dy_ref[r], dstate[r]))
            dx_ref[r] = dx
            ddt_ref[r] = ddt
            db_ref[...] += dbm
            dc_ref[...] += dcm
            dstate[r] = dsp
            dal_ref[r] += dal
            dd_ref[r] += dd
            return carry

        lax.fori_loop(0, SSD_HPG, head, 0, unroll=2)

    return pl.pallas_call(
        body, name=name, grid=(SSD_GROUPS, nc),
        in_specs=[x_spec, dt_spec, bc_spec, bc_spec, sc_spec, sc_spec, st_spec, x_spec],
        out_specs=[x_spec, dt_spec, bc_spec, bc_spec, sc_spec, sc_spec],
        out_shape=[jax.ShapeDtypeStruct(x.shape, F32), jax.ShapeDtypeStruct(dt.shape, F32),
                   jax.ShapeDtypeStruct(bm.shape, F32), jax.ShapeDtypeStruct(cm.shape, F32),
                   jax.ShapeDtypeStruct(alog.shape, F32), jax.ShapeDtypeStruct(dskip.shape, F32)],
        scratch_shapes=[pltpu.VMEM((SSD_HPG, SSD_HEAD_DIM, SSD_STATE), F32)],
        compiler_params=_cparams("arbitrary", "arbitrary"),
    )(x, dt, bm, cm, alog, dskip, sin, dy)


def _to_heads(a, n, w):
    t = a.shape[0]
    return a.reshape(t, n, w).transpose(1, 0, 2)


def _from_heads(a):
    n, t, w = a.shape
    return a.transpose(1, 0, 2).reshape(t, n * w)


def _add(acc, h):
    return (acc + h,)


def sb_fwd(h, hn, w, j, tag):
    wqkv, wo, gqk = w["sb_w_qkv"], w["sb_w_o"], w["sb_gqk"][j]
    qkv, qkv_bf = matmul(hn, wqkv, "nn", f"{tag}_qkv", b_layer=j, out_dtypes=(F32, BF16),
                         epilogue=lambda acc: (acc, acc))
    grid, data, params, outs = _qknorm_specs(qkv, gqk)
    qk = blockwise_fwd(_rmsnorm_f, grid, data, params, outs, f"{tag}_qknorm")[0]
    o = attn_fwd(qk, qkv_bf, f"{tag}_attn")
    h1 = matmul(o, wo, "nn", f"{tag}_out", b_layer=j, epilogue=_add, extras=(h,))
    return h1, (qkv, qkv_bf, qk, o)


def sb_bwd(g_bf, hn, saved, w, j, tag):
    qkv, qkv_bf, qk, o = saved
    wqkv, wo, gqk = w["sb_w_qkv"], w["sb_w_o"], w["sb_gqk"][j]
    d_wo = matmul(o, g_bf, "tn", f"{tag}_dwo", out_dtypes=(BF16,))
    do = matmul(g_bf, wo, "nt", f"{tag}_do", b_layer=j, out_dtypes=(BF16,))
    dqk, dv = attn_bwd(qk, qkv_bf, do, f"{tag}_attn_bwd")
    grid, data, params, outs = _qknorm_specs(qkv, gqk)
    (draw,), (dgqk,) = blockwise_bwd(_rmsnorm_f, grid, data, params, outs, [dqk], f"{tag}_qknorm_bwd",
                                     grad_dtypes=[BF16])
    dqkv = jnp.concatenate([draw[:, :2 * SB_HEADS * HEAD_DIM], dv], axis=1)
    d_wqkv = matmul(hn, dqkv, "tn", f"{tag}_dwqkv", out_dtypes=(BF16,))
    dhn = matmul(dqkv, wqkv, "nt", f"{tag}_dhn", b_layer=j)
    return dhn, {"sb_w_qkv": d_wqkv, "sb_w_o": d_wo, "sb_gqk": dgqk}


def _gm_act_specs(p, gv):
    t = p.shape[0]
    tr = _row_tile(t)
    d = D_MODEL
    data = [Data(p, (tr, d), lambda o, i: (i, 0)), Data(p, (tr, d), lambda o, i: (i, 1))]
    params = [Param(gv, (1, d), lambda o: (0, 0), 1)]
    outs = [Out((t, d), F32, (tr, d), lambda o, i: (i, 0)), Out((t, d), BF16, (tr, d), lambda o, i: (i, 0))]
    return (1, t // tr), data, params, outs


def _gm_gate_specs(u, vn, ws, bs):
    t = u.shape[0]
    blk = (GM_CHUNK, LANES)
    data = [Data(u, blk, lambda o, i: (i, o)), Data(vn, blk, lambda o, i: (i, o))]
    params = [Param(ws, (None, GM_CHUNK, GM_CHUNK), lambda o: (o, 0, 0), 1),
              Param(bs, (None, GM_CHUNK, 1), lambda o: (o, 0, 0), 1)]
    outs = [Out((t, D_MODEL), BF16, blk, lambda o, i: (i, o))]
    return (GM_GROUPS, t // GM_CHUNK), data, params, outs


def gm_fwd(h, hn, w, j, tag):
    p = matmul(hn, w["gm_w_in"], "nn", f"{tag}_in", b_layer=j)
    grid, data, params, outs = _gm_act_specs(p, w["gm_v_norm_g"])
    u, vn = blockwise_fwd(_gm_act_f, grid, data, params, outs, f"{tag}_act")
    grid, data, params, outs = _gm_gate_specs(u, vn, w["gm_w_s"], w["gm_b_s"])
    y = blockwise_fwd(_gm_gate_f, grid, data, params, outs, f"{tag}_gate")[0]
    h1 = matmul(y, w["gm_w_o"], "nn", f"{tag}_out", b_layer=j, epilogue=_add, extras=(h,))
    return h1, (p, u, vn, y)


def gm_bwd(g_bf, hn, saved, w, j, tag):
    p, u, vn, y = saved
    d_wo = matmul(y, g_bf, "tn", f"{tag}_dwo", out_dtypes=(BF16,))
    dy = matmul(g_bf, w["gm_w_o"], "nt", f"{tag}_dy", b_layer=j)
    grid, data, params, outs = _gm_gate_specs(u, vn, w["gm_w_s"], w["gm_b_s"])
    (du, dvn), (dws, dbs) = blockwise_bwd(_gm_gate_f, grid, data, params, outs, [dy], f"{tag}_gate_bwd",
                                          grad_dtypes=[F32, F32])
    grid, data, params, outs = _gm_act_specs(p, w["gm_v_norm_g"])
    (dpu, dpv), (dgv,) = blockwise_bwd(_gm_act_f, grid, data, params, outs, [du, dvn], f"{tag}_act_bwd",
                                       grad_dtypes=[BF16, BF16])
    dp = jnp.concatenate([dpu[:, :D_MODEL], dpv[:, D_MODEL:]], axis=1)
    d_win = matmul(hn, dp, "tn", f"{tag}_dwin", out_dtypes=(BF16,))
    dhn = matmul(dp, w["gm_w_in"], "nt", f"{tag}_dhn", b_layer=j)
    return dhn, {"gm_w_in": d_win, "gm_w_o": d_wo, "gm_v_norm_g": dgv, "gm_w_s": dws, "gm_b_s": dbs}


def _ssd_gate_specs(y, z, g):
    t = y.shape[0]
    tr = _row_tile(t)
    gw = SSD_INNER // SSD_GROUPS
    blk = (tr, gw)
    data = [Data(y, blk, lambda o, i: (i, o)), Data(z, blk, lambda o, i: (i, o))]
    params = [Param(g, (None, 1, gw), lambda o: (o, 0, 0), 1)]
    outs = [Out((t, SSD_INNER), BF16, blk, lambda o, i: (i, o))]
    return (SSD_GROUPS, t // tr), data, params, outs


def _dt_specs(raw, bias):
    t = raw.shape[0]
    tr = _row_tile(t)
    data = [Data(raw, (tr, DT_PAD), lambda o, i: (i, 0))]
    params = [Param(bias, (1, DT_PAD), lambda o: (0, 0), 1)]
    outs = [Out((t, DT_PAD), F32, (tr, DT_PAD), lambda o, i: (i, 0))]
    return (1, t // tr), data, params, outs


def ssd_fwd(h, hn, w, tag):
    z = matmul(hn, w["ssd_wz"], "nn", f"{tag}_inz")
    xbc = matmul(hn, w["ssd_wxbc"], "nn", f"{tag}_inx")
    raw = matmul(hn, w["ssd_wdt"], "nn", f"{tag}_indt")
    act = conv_fwd(xbc, w["ssd_conv_w"], w["ssd_conv_b"], f"{tag}_conv")
    grid, data, params, outs = _dt_specs(raw, w["ssd_dt_bias"])
    dt = blockwise_fwd(_dt_f, grid, data, params, outs, f"{tag}_dt")[0]
    xh = _to_heads(act[:, :SSD_INNER], SSD_HEADS, SSD_HEAD_DIM)
    bh = _to_heads(act[:, SSD_INNER:SSD_INNER + SSD_BC], SSD_GROUPS, SSD_STATE)
    ch = _to_heads(act[:, SSD_INNER + SSD_BC:], SSD_GROUPS, SSD_STATE)
    dth = _to_heads(dt[:, :SSD_HEADS], SSD_HEADS, 1)
    yh, sin = ssd_scan_fwd(xh, dth, bh, ch, w["ssd_a_log"], w["ssd_d"], f"{tag}_scan")
    y = _from_heads(yh)
    grid, data, params, outs = _ssd_gate_specs(y, z, w["ssd_norm_g"])
    yn = blockwise_fwd(_ssd_gate_f, grid, data, params, outs, f"{tag}_gate")[0]
    h1 = matmul(yn, w["ssd_w_o"], "nn", f"{tag}_out", b_layer=0, epilogue=_add, extras=(h,))
    return h1, (z, xbc, raw, xh, bh, ch, dth, sin, y, yn)


def ssd_bwd(g_bf, hn, saved, w, tag):
    z, xbc, raw, xh, bh, ch, dth, sin, y, yn = saved
    t = z.shape[0]
    d_wo = matmul(yn, g_bf, "tn", f"{tag}_dwo", out_dtypes=(BF16,))
    dyn = matmul(g_bf, w["ssd_w_o"], "nt", f"{tag}_dyn", b_layer=0)
    grid, data, params, outs = _ssd_gate_specs(y, z, w["ssd_norm_g"])
    (dy, dz), (dng,) = blockwise_bwd(_ssd_gate_f, grid, data, params, outs, [dyn], f"{tag}_gate_bwd",
                                     grad_dtypes=[F32, BF16])
    dxh, ddth, dbh, dch, dal, ddk = ssd_scan_bwd(xh, dth, bh, ch, w["ssd_a_log"], w["ssd_d"], sin,
                                                 _to_heads(dy, SSD_HEADS, SSD_HEAD_DIM), f"{tag}_scan_bwd")
    dact = jnp.concatenate([_from_heads(dxh), _from_heads(dbh), _from_heads(dch)], axis=1)
    dxbc, dcw, dcb = conv_bwd(xbc, w["ssd_conv_w"], w["ssd_conv_b"], dact, f"{tag}_conv_bwd")
    ddt = jnp.pad(_from_heads(ddth), ((0, 0), (0, DT_PAD - SSD_HEADS)))
    grid, data, params, outs = _dt_specs(raw, w["ssd_dt_bias"])
    (draw,), (dbias,) = blockwise_bwd(_dt_f, grid, data, params, outs, [ddt], f"{tag}_dt_bwd",
                                      grad_dtypes=[BF16])
    d_wz = matmul(hn, dz, "tn", f"{tag}_dwz", out_dtypes=(BF16,))
    d_wx = matmul(hn, dxbc, "tn", f"{tag}_dwx", out_dtypes=(BF16,))
    d_wdt = matmul(hn, draw, "tn", f"{tag}_dwdt", out_dtypes=(BF16,))
    dhn = matmul(dz, w["ssd_wz"], "nt", f"{tag}_dhn_z")
    dhn = matmul(dxbc, w["ssd_wxbc"], "nt", f"{tag}_dhn_x", epilogue=_add, extras=(dhn,))
    dhn = matmul(draw, w["ssd_wdt"], "nt", f"{tag}_dhn_dt", epilogue=_add, extras=(dhn,))
    d_win = jnp.concatenate([d_wz, d_wx, d_wdt[:, :SSD_HEADS]], axis=1)
    return dhn, {"ssd_w_in": d_win, "ssd_w_o": d_wo, "ssd_conv_w": dcw, "ssd_conv_b": dcb,
                 "ssd_dt_bias": dbias[:, :SSD_HEADS], "ssd_a_log": dal, "ssd_d": ddk, "ssd_norm_g": dng}


def _relu2(acc):
    return (acc, jnp.square(jnp.maximum(acc, 0.0)))


def _relu2_bwd(acc, a):
    return (acc * (2.0 * jnp.maximum(a, 0.0)),)


def local_step(x, target, w):
    h = x
    saved = []
    for i in range(DEPTH):
        kind, j = i % N_MIXERS, i // N_MIXERS
        tag = f"l{i}"
        hn = rmsnorm_fwd(h, w["norm_mix_g"][i], f"{tag}_norm_mix")
        if kind == 0:
            h1, ms = sb_fwd(h, hn, w, j, f"{tag}_sb")
        elif kind == 1:
            h1, ms = gm_fwd(h, hn, w, j, f"{tag}_gm")
        else:
            h1, ms = ssd_fwd(h, hn, w, f"{tag}_ssd")
        hm = rmsnorm_fwd(h1, w["norm_mlp_g"][i], f"{tag}_norm_mlp")
        a, r = matmul(hm, w["mlp_w_in"], "nn", f"{tag}_mlp_in", b_layer=i, out_dtypes=(F32, BF16),
                      epilogue=_relu2)
        h2 = matmul(r, w["mlp_w_out"], "nn", f"{tag}_mlp_out", b_layer=i, epilogue=_add, extras=(h1,))
        saved.append((h, hn, ms, h1, hm, a, r))
        h = h2
    dh, loss = loss_call(h, target)

    grads = {"mlp_w_in": [None] * DEPTH, "mlp_w_out": [None] * DEPTH, "norm_mix_g": [None] * DEPTH,
             "norm_mlp_g": [None] * DEPTH, "sb_w_qkv": [None] * 2, "sb_w_o": [None] * 2, "sb_gqk": [None] * 2}
    for i in reversed(range(DEPTH)):
        kind, j = i % N_MIXERS, i // N_MIXERS
        tag = f"l{i}"
        h0, hn, ms, h1, hm, a, r = saved[i]
        g_bf = dh.astype(BF16)
        grads["mlp_w_out"][i] = matmul(r, g_bf, "tn", f"{tag}_mlp_dwout", out_dtypes=(BF16,))
        da = matmul(g_bf, w["mlp_w_out"], "nt", f"{tag}_mlp_da", b_layer=i, out_dtypes=(BF16,),
                    epilogue=_relu2_bwd, extras=(a,))
        grads["mlp_w_in"][i] = matmul(hm, da, "tn", f"{tag}_mlp_dwin", out_dtypes=(BF16,))
        dhm = matmul(da, w["mlp_w_in"], "nt", f"{tag}_mlp_dhm", b_layer=i)
        dh1, grads["norm_mlp_g"][i] = rmsnorm_bwd(h1, w["norm_mlp_g"][i], dhm, dh, f"{tag}_norm_mlp_bwd")
        g_bf = dh1.astype(BF16)
        if kind == 0:
            dhn, mg = sb_bwd(g_bf, hn, ms, w, j, f"{tag}_sb")
            for kname, val in mg.items():
                grads[kname][j] = val
        elif kind == 1:
            dhn, mg = gm_bwd(g_bf, hn, ms, w, j, f"{tag}_gm")
            grads.update(mg)
        else:
            dhn, mg = ssd_bwd(g_bf, hn, ms, w, f"{tag}_ssd")
            grads.update(mg)
        dh, grads["norm_mix_g"][i] = rmsnorm_bwd(h0, w["norm_mix_g"][i], dhn, dh1, f"{tag}_norm_mix_bwd")
    return loss, dh, grads


def _my_pos():
    return lax.axis_index("x"), lax.axis_index("y"), lax.axis_index("c")


def _lin(p):
    return 4 * p[0] + 2 * p[1] + p[2]


def _shard_view(ref, kind, width, idx, layer=None):
    if kind == "col":
        start = pl.multiple_of(idx * width, LANES)
        return ref.at[:, :, pl.ds(start, width)] if layer is None else ref.at[layer, :, pl.ds(start, width)]
    if kind == "row":
        start = pl.multiple_of(idx * width, 16)
        return ref.at[:, pl.ds(start, width), :] if layer is None else ref.at[layer, pl.ds(start, width), :]
    assert kind == "slot"
    return ref.at[idx]


def allgather(shards, kinds):
    n = len(shards)
    out_shapes, widths = [], []
    for s, kind in zip(shards, kinds):
        if kind == "col":
            out_shapes.append(jax.ShapeDtypeStruct(s.shape[:2] + (s.shape[2] * N_DEV,), s.dtype))
            widths.append(s.shape[2])
        elif kind == "row":
            out_shapes.append(jax.ShapeDtypeStruct((s.shape[0], s.shape[1] * N_DEV, s.shape[2]), s.dtype))
            widths.append(s.shape[1])
        else:
            out_shapes.append(jax.ShapeDtypeStruct((N_DEV,) + s.shape, s.dtype))
            widths.append(None)

    def body(*refs):
        in_refs, out_refs = refs[:n], refs[n:2 * n]
        send_sems, recv_sems, local_sems = refs[2 * n:]
        x, y, c = _my_pos()
        me, sibling = (x, y, c), (x, y, 1 - c)
        chips = [(1 - x, y), (x, 1 - y), (1 - x, 1 - y)]

        def part(a, block):
            return _shard_view(out_refs[a], kinds[a], widths[a], _lin(block))

        def copy(a, k, block, to, src=None):
            return pltpu.make_async_remote_copy(
                src_ref=part(a, block) if src is None else src, dst_ref=part(a, block),
                send_sem=send_sems.at[a * 7 + k], recv_sem=recv_sems.at[a * 7 + k],
                device_id=to, device_id_type=MESH)

        mine, first, passed = [], [], []
        for a in range(n):
            mine.append(pltpu.make_async_copy(in_refs[a], part(a, me), local_sems.at[a]))
            mine[-1].start()
            first.append(copy(a, 0, me, sibling, src=in_refs[a]))
            first += [copy(a, 1 + j, me, (*chip, c), src=in_refs[a]) for j, chip in enumerate(chips)]
        for cp in first:
            cp.start()
        for a in range(n):
            for j, chip in enumerate(chips):
                copy(a, 1 + j, (*chip, c), me).wait_recv()
                passed.append(copy(a, 4 + j, (*chip, c), sibling))
                passed[-1].start()
        for a in range(n):
            copy(a, 0, sibling, me).wait_recv()
            for j, chip in enumerate(chips):
                copy(a, 4 + j, (*chip, 1 - c), me).wait_recv()
        for cp in first + passed:
            cp.wait_send()
        for cp in mine:
            cp.wait()

    return pl.pallas_call(
        body, name="allgather_weights", in_specs=[HBM_SPEC] * n, out_specs=[HBM_SPEC] * n,
        out_shape=out_shapes,
        scratch_shapes=[pltpu.SemaphoreType.DMA((7 * n,)), pltpu.SemaphoreType.DMA((7 * n,)),
                        pltpu.SemaphoreType.DMA((n,))],
    )(*shards)


_FLIPS = [(0, 0, 1), (0, 1, 0), (0, 1, 1), (1, 0, 0), (1, 0, 1), (1, 1, 0), (1, 1, 1)]


def _flip(pos, f):
    return tuple((1 - p) if b else p for p, b in zip(pos, f))


def scatter_grads(items):
    n = len(items)
    tensors = []
    for it in items:
        if it[5] not in [tn for tn, _ in tensors]:
            tensors.append((it[5], jax.ShapeDtypeStruct((N_DEV,) + it[4], it[0].dtype)))
    tix = {tn: k for k, (tn, _) in enumerate(tensors)}

    def body(*refs):
        in_refs, out_refs = refs[:n], refs[n:n + len(tensors)]
        send_sems, recv_sems, local_sems = refs[n + len(tensors):]
        me = _my_pos()

        def src(a, dest):
            g, kind, width, layer, _, _ = items[a]
            if kind == "all":
                return in_refs[a]
            if kind == "slot":
                return in_refs[a].at[_lin(dest)]
            if kind == "col":
                return in_refs[a].at[:, pl.ds(pl.multiple_of(_lin(dest) * width, LANES), width)]
            return in_refs[a].at[pl.ds(pl.multiple_of(_lin(dest) * width, 16), width), :]

        def dst(a, sender):
            layer = items[a][3]
            buf = out_refs[tix[items[a][5]]]
            return buf.at[_lin(sender)] if layer is None else buf.at[_lin(sender), layer]

        def copy(a, k):
            peer = _flip(me, _FLIPS[k])
            return pltpu.make_async_remote_copy(
                src_ref=src(a, peer), dst_ref=dst(a, me),
                send_sem=send_sems.at[a * 7 + k], recv_sem=recv_sems.at[a * 7 + k],
                device_id=peer, device_id_type=MESH)

        def arrival(a, k):
            peer = _flip(me, _FLIPS[k])
            return pltpu.make_async_remote_copy(
                src_ref=src(a, me), dst_ref=dst(a, peer),
                send_sem=send_sems.at[a * 7 + k], recv_sem=recv_sems.at[a * 7 + k],
                device_id=peer, device_id_type=MESH)

        mine = [pltpu.make_async_copy(src(a, me), dst(a, me), local_sems.at[a]) for a in range(n)]
        for cp in mine:
            cp.start()
        sends = [copy(a, k) for a in range(n) for k in range(7)]
        for cp in sends:
            cp.start()
        for a in range(n):
            for k in range(7):
                arrival(a, k).wait_recv()
        for cp in sends:
            cp.wait_send()
        for cp in mine:
            cp.wait()

    res = pl.pallas_call(
        body, name="scatter_grads", in_specs=[HBM_SPEC] * n, out_specs=[HBM_SPEC] * len(tensors),
        out_shape=[s for _, s in tensors],
        scratch_shapes=[pltpu.SemaphoreType.DMA((7 * n,)), pltpu.SemaphoreType.DMA((7 * n,)),
                        pltpu.SemaphoreType.DMA((n,))],
    )(*[it[0] for it in items])
    return {tn: r for (tn, _), r in zip(tensors, res)}


_C1 = 1.0 - ADAM_B1 ** ADAM_STEP
_C2 = 1.0 - ADAM_B2 ** ADAM_STEP


def adamw(wt, m, v, slots, name):
    nl, r, c = wt.shape
    tr = _tile(r, 128)

    def body(w_ref, m_ref, v_ref, s_ref, g_ref, d_ref, nm_ref, nv_ref):
        g = s_ref[0].astype(F32)
        for k in range(1, N_DEV):
            g = g + s_ref[k].astype(F32)
        m_new = ADAM_B1 * m_ref[...] + (1.0 - ADAM_B1) * g
        v_new = ADAM_B2 * v_ref[...] + (1.0 - ADAM_B2) * jnp.square(g)
        m_hat = m_new / _C1
        v_hat = v_new / _C2
        g_ref[...] = g
        d_ref[...] = -ADAM_LR * (m_hat / (jnp.sqrt(v_hat) + ADAM_EPS) + ADAM_WD * w_ref[...])
        nm_ref[...] = m_new
        nv_ref[...] = v_new

    blk = pl.BlockSpec((None, tr, c), lambda l, i: (l, i, 0))
    sblk = pl.BlockSpec((N_DEV, None, tr, c), lambda l, i: (0, l, i, 0))
    return pl.pallas_call(
        body, name=name, grid=(nl, r // tr), in_specs=[blk, blk, blk, sblk], out_specs=[blk] * 4,
        out_shape=[jax.ShapeDtypeStruct(wt.shape, F32)] * 4,
        compiler_params=_cparams("parallel", "parallel"),
    )(wt, m, v, slots)


def _pack(arrays, rows):
    flat = jnp.concatenate([a.reshape(-1).astype(F32) for a in arrays])
    return jnp.pad(flat, (0, rows * LANES - flat.shape[0])).reshape(rows, LANES)


def _unpack(packed, shapes):
    flat = packed.reshape(-1)
    out, off = [], 0
    for s in shapes:
        sz = math.prod(s)
        out.append(flat[off:off + sz].reshape(s))
        off += sz
    return out


SMALL_W = 768


def _pack_small_shard(conv_w, conv_b, norm_g):
    ng = jnp.pad(norm_g, ((0, 0), (0, SMALL_W - norm_g.shape[1])))
    return jnp.concatenate([conv_w[0], conv_b, ng, jnp.zeros((2, SMALL_W), F32)], axis=0)


def _unpack_small_shard(p):
    return p[0:4][None], p[4:5], p[5:6, :SSD_INNER // N_DEV]


REPL = ["norm_mix_g", "norm_mlp_g", "sb_q_norm_g", "sb_k_norm_g", "gm_v_norm_g", "gm_w_s", "gm_b_s",
        "ssd_dt_bias", "ssd_a_log", "ssd_d"]
BIG = ["sb_w_qkv", "sb_w_o", "gm_w_in", "gm_w_o", "ssd_w_in", "ssd_w_o", "mlp_w_in", "mlp_w_out"]
BIG_KIND = {"sb_w_qkv": "col", "sb_w_o": "row", "gm_w_in": "col", "gm_w_o": "row", "ssd_w_in": "slot",
            "ssd_w_o": "row", "mlp_w_in": "col", "mlp_w_out": "row"}
SMALL_SHARDED = ["ssd_conv_w", "ssd_conv_b", "ssd_norm_g"]
WEIGHTS = ["norm_mix_g", "norm_mlp_g", "sb_w_qkv", "sb_q_norm_g", "sb_k_norm_g", "sb_w_o", "gm_w_in",
           "gm_v_norm_g", "gm_w_s", "gm_b_s", "gm_w_o", "ssd_w_in", "ssd_conv_w", "ssd_conv_b", "ssd_dt_bias",
           "ssd_a_log", "ssd_d", "ssd_norm_g", "ssd_w_o", "mlp_w_in", "mlp_w_out"]


def _model_weights(p, gathered, small):
    w = {k: gathered[k] for k in BIG if k != "ssd_w_in"}
    win = jnp.moveaxis(gathered["ssd_w_in"][:, 0], 0, 1).reshape(D_MODEL, SSD_PROJ)
    w["ssd_wz"] = win[:, :SSD_INNER]
    w["ssd_wxbc"] = win[:, SSD_INNER:SSD_INNER + SSD_CONV_DIM]
    w["ssd_wdt"] = jnp.pad(win[:, SSD_INNER + SSD_CONV_DIM:], ((0, 0), (0, DT_PAD - SSD_HEADS)))
    w["ssd_conv_w"] = small[:, 0:4].transpose(1, 0, 2).reshape(SSD_CONV, SSD_CONV_DIM)
    w["ssd_conv_b"] = small[:, 4].reshape(1, SSD_CONV_DIM)
    w["ssd_norm_g"] = small[:, 5:6, :SSD_INNER // N_DEV]
    w["norm_mix_g"] = [p["norm_mix_g"][i:i + 1] for i in range(DEPTH)]
    w["norm_mlp_g"] = [p["norm_mlp_g"][i:i + 1] for i in range(DEPTH)]
    w["sb_gqk"] = [jnp.stack([p["sb_q_norm_g"][j:j + 1], p["sb_k_norm_g"][j:j + 1]]) for j in range(2)]
    w["gm_v_norm_g"] = p["gm_v_norm_g"]
    w["gm_w_s"] = p["gm_w_s"][0]
    w["gm_b_s"] = p["gm_b_s"][0][:, :, None]
    w["ssd_dt_bias"] = jnp.pad(p["ssd_dt_bias"], ((0, 0), (0, DT_PAD - SSD_HEADS)))
    w["ssd_a_log"] = p["ssd_a_log"].reshape(SSD_HEADS, 1, 1)
    w["ssd_d"] = p["ssd_d"].reshape(SSD_HEADS, 1, 1)
    return w


def _repl_grads(g):
    return [jnp.concatenate(g["norm_mix_g"], axis=0), jnp.concatenate(g["norm_mlp_g"], axis=0),
            jnp.concatenate([g["sb_gqk"][0][0], g["sb_gqk"][1][0]], axis=0),
            jnp.concatenate([g["sb_gqk"][0][1], g["sb_gqk"][1][1]], axis=0),
            g["gm_v_norm_g"], g["gm_w_s"][None], g["gm_b_s"][None, :, :, 0],
            g["ssd_dt_bias"], g["ssd_a_log"].reshape(1, SSD_HEADS), g["ssd_d"].reshape(1, SSD_HEADS)]


def kernel(x, norm_mix_g, norm_mlp_g, sb_w_qkv, sb_q_norm_g, sb_k_norm_g, sb_w_o, gm_w_in, gm_v_norm_g, gm_w_s, gm_b_s, gm_w_o, ssd_w_in, ssd_conv_w, ssd_conv_b, ssd_dt_bias, ssd_a_log, ssd_d, ssd_norm_g, ssd_w_o, mlp_w_in, mlp_w_out, loss_target, m_norm_mix_g, m_norm_mlp_g, m_sb_w_qkv, m_sb_q_norm_g, m_sb_k_norm_g, m_sb_w_o, m_gm_w_in, m_gm_v_norm_g, m_gm_w_s, m_gm_b_s, m_gm_w_o, m_ssd_w_in, m_ssd_conv_w, m_ssd_conv_b, m_ssd_dt_bias, m_ssd_a_log, m_ssd_d, m_ssd_norm_g, m_ssd_w_o, m_mlp_w_in, m_mlp_w_out, v_norm_mix_g, v_norm_mlp_g, v_sb_w_qkv, v_sb_q_norm_g, v_sb_k_norm_g, v_sb_w_o, v_gm_w_in, v_gm_v_norm_g, v_gm_w_s, v_gm_b_s, v_gm_w_o, v_ssd_w_in, v_ssd_conv_w, v_ssd_conv_b, v_ssd_dt_bias, v_ssd_a_log, v_ssd_d, v_ssd_norm_g, v_ssd_w_o, v_mlp_w_in, v_mlp_w_out):
    args = dict(locals())
    p = {k: args[k] for k in WEIGHTS}
    pm = {k: args["m_" + k] for k in WEIGHTS}
    pv = {k: args["v_" + k] for k in WEIGHTS}

    shards = [p[k].astype(BF16) for k in BIG] + [_pack_small_shard(*[p[k] for k in SMALL_SHARDED])]
    gathered = allgather(shards, [BIG_KIND[k] for k in BIG] + ["slot"])
    w = _model_weights(p, dict(zip(BIG, gathered[:-1])), gathered[-1])

    loss, grad_x, g = local_step(x[0], loss_target[0], w)
    loss = lax.psum(loss[0, 0], ("x", "y", "c"))

    items = []
    for k in BIG:
        kind = BIG_KIND[k]
        shard = p[k].shape
        if kind == "slot":
            gi = g[k].reshape(D_MODEL, N_DEV, SSD_PROJ // N_DEV).transpose(1, 0, 2)
            items.append((gi, "slot", None, 0, shard, k))
            continue
        per_layer = g[k] if isinstance(g[k], list) else [g[k]]
        width = shard[2] if kind == "col" else shard[1]
        for layer, gl in enumerate(per_layer):
            items.append((gl, kind, width, layer, shard, k))
    dcw = g["ssd_conv_w"].reshape(SSD_CONV, N_DEV, SMALL_W).transpose(1, 0, 2)
    dcb = g["ssd_conv_b"].reshape(N_DEV, 1, SMALL_W)
    dng = jnp.pad(g["ssd_norm_g"], ((0, 0), (0, 0), (0, SMALL_W - SSD_INNER // N_DEV)))
    small_g = jnp.concatenate([dcw, dcb, dng, jnp.zeros((N_DEV, 2, SMALL_W), F32)], axis=1)
    items.append((small_g, "slot", None, None, (8, SMALL_W), "small"))
    repl_shapes = [p[k].shape for k in REPL]
    n_repl = sum(math.prod(s) for s in repl_shapes)
    repl_rows = -(-n_repl // (LANES * LANES)) * LANES
    items.append((_pack(_repl_grads(g), repl_rows), "all", None, None, (repl_rows, LANES), "repl"))
    recv = scatter_grads(items)

    out = {}
    for k in BIG:
        s = p[k].shape
        out[k] = adamw(p[k], pm[k], pv[k], recv[k], f"adamw_{k}")
    sm = adamw(*[_pack_small_shard(*[d[k] for k in SMALL_SHARDED])[None] for d in (p, pm, pv)],
               recv["small"][:, None], "adamw_small")
    for k, vals in zip(SMALL_SHARDED, zip(*[_unpack_small_shard(r[0]) for r in sm])):
        out[k] = list(vals)
    rp = adamw(*[_pack([d[k] for k in REPL], repl_rows)[None] for d in (p, pm, pv)],
               recv["repl"][:, None], "adamw_repl")
    for k, vals in zip(REPL, zip(*[_unpack(r[0], repl_shapes) for r in rp])):
        out[k] = list(vals)
    res = [loss, grad_x[None]]
    for q in range(4):
        res += [out[k][q] for k in WEIGHTS]
    return tuple(res)
```

```python
import functools
import math

import jax
import jax.numpy as jnp
from jax import lax
from jax.experimental import pallas as pl
from jax.experimental.pallas import tpu as pltpu

F32 = jnp.float32
BF16 = jnp.bfloat16

D_MODEL = 2048
DEPTH = 4
N_MIXERS = 3
EPS = 1e-6
SB_HEADS = 16
HEAD_DIM = 128
GM_CHUNK = 128
GM_GROUPS = 16
SSD_INNER = 4096
SSD_HEADS = 64
SSD_HEAD_DIM = 64
SSD_GROUPS = 8
SSD_HPG = 8
SSD_STATE = 128
SSD_CHUNK = 128
SSD_CONV = 4
SSD_BC = SSD_GROUPS * SSD_STATE
SSD_CONV_DIM = SSD_INNER + 2 * SSD_BC
SSD_PROJ = SSD_INNER + SSD_CONV_DIM + SSD_HEADS
DT_PAD = 128
N_DEV = 8
LANES = 128

ADAM_LR = 0.001
ADAM_B1 = 0.9
ADAM_B2 = 0.999
ADAM_EPS = 1e-08
ADAM_WD = 0.01
ADAM_STEP = 10

VMEM_LIMIT_BYTES = 56 * 1024 * 1024
MESH = pl.DeviceIdType.MESH
HBM_SPEC = pl.BlockSpec(memory_space=pl.ANY)


def _cparams(*sem):
    return pltpu.CompilerParams(dimension_semantics=sem, vmem_limit_bytes=VMEM_LIMIT_BYTES)


def _tile(dim, pref):
    t = min(dim, pref)
    assert dim % t == 0, (dim, pref)
    return t


_DOT_DIMS = {"nn": ((1,), (0,)), "nt": ((1,), (1,)), "tn": ((0,), (0,))}


def matmul(a, b, form, name, *, b_layer=None, out_dtypes=(F32,), epilogue=None, extras=(),
           tm=1024, tn=1024, tk=2048):
    bs = b.shape[-2:]
    if form == "nn":
        (m, k), (k2, n) = a.shape, bs
    elif form == "nt":
        (m, k), (n, k2) = a.shape, bs
    else:
        (k, m), (k2, n) = a.shape, bs
    assert k == k2, (a.shape, b.shape, form)
    tm, tn, tk = _tile(m, tm), _tile(n, tn), _tile(k, tk)
    grid = (m // tm, n // tn, k // tk)
    nk = grid[2]
    if form == "tn":
        a_spec = pl.BlockSpec((tk, tm), lambda i, j, kk: (kk, i))
    else:
        a_spec = pl.BlockSpec((tm, tk), lambda i, j, kk: (i, kk))
    if form == "nt":
        b_blk, b_idx = (tn, tk), (lambda i, j, kk: (j, kk))
    else:
        b_blk, b_idx = (tk, tn), (lambda i, j, kk: (kk, j))
    if b.ndim == 3:
        b_spec = pl.BlockSpec((None,) + b_blk, lambda i, j, kk: (b_layer,) + b_idx(i, j, kk))
    else:
        b_spec = pl.BlockSpec(b_blk, b_idx)
    o_spec = pl.BlockSpec((tm, tn), lambda i, j, kk: (i, j))
    n_extra, n_out = len(extras), len(out_dtypes)
    dims = (_DOT_DIMS[form], ((), ()))

    def body(a_ref, b_ref, *rest):
        extra_refs, out_refs = rest[:n_extra], rest[n_extra:n_extra + n_out]
        kk = pl.program_id(2)

        def product():
            return lax.dot_general(a_ref[...], b_ref[...], dims, preferred_element_type=F32)

        def finish(acc):
            outs = (acc,) if epilogue is None else epilogue(acc, *[r[...] for r in extra_refs])
            for r, o in zip(out_refs, outs):
                r[...] = o.astype(r.dtype)

        if nk == 1:
            finish(product())
            return
        acc_ref = rest[-1]

        @pl.when(kk == 0)
        def _():
            acc_ref[...] = product()

        @pl.when(jnp.logical_and(kk > 0, kk < nk - 1))
        def _():
            acc_ref[...] += product()

        @pl.when(kk == nk - 1)
        def _():
            finish(acc_ref[...] + product())

    res = pl.pallas_call(
        body, name=name, grid=grid,
        in_specs=[a_spec, b_spec] + [o_spec] * n_extra,
        out_specs=[o_spec] * n_out,
        out_shape=[jax.ShapeDtypeStruct((m, n), dt) for dt in out_dtypes],
        scratch_shapes=[pltpu.VMEM((tm, tn), F32)] if nk > 1 else [],
        compiler_params=_cparams("parallel", "parallel", "arbitrary"),
    )(a, b, *extras)
    return res[0] if n_out == 1 else res


class Data:
    def __init__(self, array, block, imap):
        self.array, self.block, self.imap = array, block, imap


class Param:
    def __init__(self, array, block, imap, group):
        self.array, self.block, self.imap, self.group = array, block, imap, group


class Out:
    def __init__(self, shape, dtype, block, imap):
        self.shape, self.dtype, self.block, self.imap = shape, dtype, block, imap


def _f32(v):
    return v.astype(F32)


def blockwise_fwd(f, grid, data, params, outs, name):
    nd, npar = len(data), len(params)

    def body(*refs):
        vals = [_f32(r[...]) for r in refs[:nd + npar]]
        res = f(*vals)
        for r, o in zip(refs[nd + npar:], res):
            r[...] = o.astype(r.dtype)

    in_specs = [pl.BlockSpec(d.block, d.imap) for d in data]
    in_specs += [pl.BlockSpec(p.block, (lambda o, i, _m=p.imap: _m(o))) for p in params]
    res = pl.pallas_call(
        body, name=name, grid=grid, in_specs=in_specs,
        out_specs=[pl.BlockSpec(o.block, o.imap) for o in outs],
        out_shape=[jax.ShapeDtypeStruct(o.shape, o.dtype) for o in outs],
        compiler_params=_cparams("parallel", "parallel"),
    )(*[d.array for d in data], *[p.array for p in params])
    return res


def blockwise_bwd(f, grid, data, params, outs, cts, name, *, grad_dtypes, accum=None):
    nd, npar, no = len(data), len(params), len(outs)
    accum = accum or {}
    want = [k for k in range(nd) if grad_dtypes[k] is not None]
    acc_keys = sorted(accum)

    def body(*refs):
        in_refs = refs[:nd + npar]
        ct_refs = refs[nd + npar:nd + npar + no]
        acc_refs = refs[nd + npar + no:nd + npar + no + len(acc_keys)]
        out_refs = refs[nd + npar + no + len(acc_keys):]
        dd_refs, dp_refs = out_refs[:len(want)], out_refs[len(want):]
        o, i = pl.program_id(0), pl.program_id(1)
        vals = [_f32(r[...]) for r in in_refs]
        _, vjp = jax.vjp(f, *vals)
        grads = vjp(tuple(_f32(r[...]) for r in ct_refs))
        for r, k in zip(dd_refs, want):
            g = grads[k]
            if k in accum:
                g = g + _f32(acc_refs[acc_keys.index(k)][...])
            r[...] = g.astype(r.dtype)
        for r, p, g in zip(dp_refs, params, grads[nd:]):
            first = jnp.logical_and(i == 0, o % p.group == 0)

            @pl.when(first)
            def _():
                r[...] = g.astype(r.dtype)

            @pl.when(jnp.logical_not(first))
            def _():
                r[...] += g.astype(r.dtype)

    d_specs = [pl.BlockSpec(d.block, d.imap) for d in data]
    p_specs = [pl.BlockSpec(p.block, (lambda o, i, _m=p.imap: _m(o))) for p in params]
    ct_specs = [pl.BlockSpec(o.block, o.imap) for o in outs]
    acc_specs = [d_specs[k] for k in acc_keys]
    res = pl.pallas_call(
        body, name=name, grid=grid,
        in_specs=d_specs + p_specs + ct_specs + acc_specs,
        out_specs=[d_specs[k] for k in want] + p_specs,
        out_shape=[jax.ShapeDtypeStruct(data[k].array.shape, grad_dtypes[k]) for k in want]
        + [jax.ShapeDtypeStruct(p.array.shape, F32) for p in params],
        compiler_params=_cparams("arbitrary", "arbitrary"),
    )(*[d.array for d in data], *[p.array for p in params], *cts, *[accum[k] for k in acc_keys])
    return list(res[:len(want)]), list(res[len(want):])


def _rmsnorm_f(x, g):
    return (x * lax.rsqrt(jnp.mean(x * x, axis=-1, keepdims=True) + EPS) * g,)


def _gelu(x):
    return 0.5 * x * (1.0 + lax.erf(x * (2.0 ** -0.5)))


def _silu(x):
    return x / (1.0 + jnp.exp(-x))


def _softplus(x):
    return jnp.maximum(x, 0.0) + jnp.log1p(jnp.exp(-jnp.abs(x)))


def _gm_act_f(pu, pv, gv):
    return (_gelu(pu),) + _rmsnorm_f(_gelu(pv), gv)


def _gm_gate_f(u, vn, w, b):
    n = w.shape[0]
    row = lax.broadcasted_iota(jnp.int32, (n, n), 0)
    col = lax.broadcasted_iota(jnp.int32, (n, n), 1)
    wm = jnp.where(row >= col, w, 0.0)
    return (u * (jnp.dot(wm, vn, preferred_element_type=F32) + b),)


def _ssd_gate_f(y, z, g):
    return _rmsnorm_f(y * _silu(z), g)


def _dt_f(raw, bias):
    return (_softplus(raw + bias),)


def _row_tile(t):
    return _tile(t, 256)


def _norm_specs(h, g_row):
    t, d = h.shape
    tr = _row_tile(t)
    data = [Data(h, (tr, d), lambda o, i: (i, 0))]
    params = [Param(g_row, (1, d), lambda o: (0, 0), 1)]
    outs = [Out((t, d), BF16, (tr, d), lambda o, i: (i, 0))]
    return (1, t // tr), data, params, outs


def rmsnorm_fwd(h, g_row, name):
    grid, data, params, outs = _norm_specs(h, g_row)
    return blockwise_fwd(_rmsnorm_f, grid, data, params, outs, name)[0]


def rmsnorm_bwd(h, g_row, ct, skip, name):
    grid, data, params, outs = _norm_specs(h, g_row)
    (dh,), (dg,) = blockwise_bwd(_rmsnorm_f, grid, data, params, outs, [ct], name,
                                 grad_dtypes=[F32], accum={0: skip})
    return dh, dg


def loss_call(y, target):
    t, d = y.shape
    tr = _row_tile(t)

    def body(y_ref, t_ref, dy_ref, loss_ref):
        e = y_ref[...] - t_ref[...]
        dy_ref[...] = e * (1.0 / d)

        @pl.when(pl.program_id(0) == 0)
        def _():
            loss_ref[...] = jnp.zeros_like(loss_ref)

        loss_ref[...] += jnp.sum(e * e).reshape(1, 1) * (0.5 / d)

    row = pl.BlockSpec((tr, d), lambda i: (i, 0))
    return pl.pallas_call(
        body, name="loss", grid=(t // tr,), in_specs=[row, row],
        out_specs=[row, pl.BlockSpec((1, 1), lambda i: (0, 0))],
        out_shape=[jax.ShapeDtypeStruct((t, d), F32), jax.ShapeDtypeStruct((1, 1), F32)],
        compiler_params=_cparams("arbitrary"),
    )(y, target)


ATT_TQ = 1024
ATT_KB = 256
ATT_RC = 1024
ATT_DIAG = ATT_TQ // ATT_KB


def _split2(x):
    hi = x.astype(BF16)
    return hi, (x - hi.astype(F32)).astype(BF16)


def _dot2(hi, lo, ones_bf):
    return (jnp.dot(hi, ones_bf, preferred_element_type=F32)
            + jnp.dot(lo, ones_bf, preferred_element_type=F32))


def _att_consts():
    r_ = lax.broadcasted_iota(jnp.int32, (ATT_KB, ATT_KB), 0)
    c_ = lax.broadcasted_iota(jnp.int32, (ATT_KB, ATT_KB), 1)
    upper = (r_ > c_).astype(BF16)
    lower = (r_ < c_).astype(BF16)
    ones = jnp.ones((ATT_KB, LANES), BF16)
    return upper, lower, ones


def _att_logits(q, k, scale, mask):
    z = lax.dot_general(q, k, (((1,), (1,)), ((), ())), preferred_element_type=F32) * scale
    lb = jnp.minimum(z, 0.0) - jnp.log1p(jnp.exp(-jnp.abs(z)))
    l1 = lb - z
    if mask is not None:
        l1 = jnp.where(mask, l1, 0.0)
    return lb, l1


def _diag_mask(rc, d):
    if d is None:
        return None
    row = rc * ATT_RC + lax.broadcasted_iota(jnp.int32, (ATT_RC, ATT_KB), 0)
    col = d * ATT_KB + lax.broadcasted_iota(jnp.int32, (ATT_RC, ATT_KB), 1)
    return col < row


def _wide(c):
    return jnp.concatenate([c] * (ATT_KB // LANES), axis=1)


def attn_fwd(qk, qkv_bf, name, ex=None):
    t = qk.shape[0]
    tq = ATT_TQ
    assert t % tq == 0
    scale = HEAD_DIM ** -0.5

    def body(q_ref, k_ref, v_ref, o_ref, acc_ref, c_ref):
        qi = pl.program_id(1)
        upper, _, ones = _att_consts()
        acc_ref[...] = jnp.zeros_like(acc_ref)
        c_ref[...] = jnp.zeros_like(c_ref)

        def block(jb, diag):
            off = pl.multiple_of(jb * ATT_KB, ATT_KB)
            k = k_ref[pl.ds(off, ATT_KB), :]
            v = v_ref[pl.ds(off, ATT_KB), :]
            for rc in range(tq // ATT_RC):
                rows = pl.ds(rc * ATT_RC, ATT_RC)
                mask = _diag_mask(rc, diag)
                lb, l1 = _att_logits(q_ref[rows, :], k, scale, mask)
                hi, lo = _split2(l1)
                c = c_ref[rows, :]
                a = jnp.exp(lb + _wide(c) + _dot2(hi, lo, upper))
                if mask is not None:
                    a = jnp.where(mask, a, 0.0)
                acc_ref[rows, :] += jnp.dot(a.astype(BF16), v, preferred_element_type=F32)
                c_ref[rows, :] = c + _dot2(hi, lo, ones)

        for d in reversed(range(ATT_DIAG)):
            block(qi * ATT_DIAG + d, d)

        def step(s, carry):
            block(qi * ATT_DIAG - 1 - s, None)
            return carry

        lax.fori_loop(0, qi * ATT_DIAG, step, 0)
        o_ref[...] = acc_ref[...].astype(o_ref.dtype)

    (o,), bufs = comm_call(
        body, name=name, grid=(SB_HEADS, t // tq),
        in_specs=[pl.BlockSpec((tq, HEAD_DIM), lambda h, i: (i, h)),
                  pl.BlockSpec((t, HEAD_DIM), lambda h, i: (0, SB_HEADS + h)),
                  pl.BlockSpec((t, HEAD_DIM), lambda h, i: (0, 2 * SB_HEADS + h))],
        out_specs=[pl.BlockSpec((tq, HEAD_DIM), lambda h, i: (i, h))],
        out_shape=[jax.ShapeDtypeStruct((t, SB_HEADS * HEAD_DIM), BF16)],
        scratch_shapes=[pltpu.VMEM((tq, HEAD_DIM), F32), pltpu.VMEM((tq, LANES), F32)],
        inputs=[qk, qk, qkv_bf], sem=("parallel", "arbitrary"), ex=ex)
    return o, bufs


def attn_bwd(qk, qkv_bf, do, name, ex=None):
    t = qk.shape[0]
    tq = ATT_TQ
    assert t % tq == 0
    nq = t // tq
    nkb = t // ATT_KB
    scale = HEAD_DIM ** -0.5
    tn_dims = (((0,), (0,)), ((), ()))

    def body(q_ref, k_ref, v_ref, do_ref, dq_ref, dk_ref, dv_ref,
             cbuf, c_ref, ce_ref, dq_acc, a_buf, dz_buf, dk_acc, dv_acc):
        qi = pl.program_id(1)
        upper, lower, ones = _att_consts()

        @pl.when(qi == 0)
        def _():
            dk_acc[...] = jnp.zeros_like(dk_acc)
            dv_acc[...] = jnp.zeros_like(dv_acc)

        c_ref[...] = jnp.zeros_like(c_ref)
        ce_ref[...] = jnp.zeros_like(ce_ref)
        dq_acc[...] = jnp.zeros_like(dq_acc)

        def pass1(jb, diag):
            off = pl.multiple_of(jb * ATT_KB, ATT_KB)
            k = k_ref[pl.ds(off, ATT_KB), :]
            for rc in range(tq // ATT_RC):
                rows = pl.ds(rc * ATT_RC, ATT_RC)
                _, l1 = _att_logits(q_ref[rows, :], k, scale, _diag_mask(rc, diag))
                hi, lo = _split2(l1)
                c = c_ref[rows, :]
                cbuf[jb, rows, :] = c
                c_ref[rows, :] = c + _dot2(hi, lo, ones)

        for d in reversed(range(ATT_DIAG)):
            pass1(qi * ATT_DIAG + d, d)

        def step1(s, carry):
            pass1(qi * ATT_DIAG - 1 - s, None)
            return carry

        lax.fori_loop(0, qi * ATT_DIAG, step1, 0)

        def pass2(jb, diag):
            off = pl.multiple_of(jb * ATT_KB, ATT_KB)
            k = k_ref[pl.ds(off, ATT_KB), :]
            v = v_ref[pl.ds(off, ATT_KB), :]
            for rc in range(tq // ATT_RC):
                rows = pl.ds(rc * ATT_RC, ATT_RC)
                mask = _diag_mask(rc, diag)
                lb, l1 = _att_logits(q_ref[rows, :], k, scale, mask)
                hi, lo = _split2(l1)
                a = jnp.exp(lb + _wide(cbuf[jb, rows, :]) + _dot2(hi, lo, upper))
                if mask is not None:
                    a = jnp.where(mask, a, 0.0)
                da = lax.dot_general(do_ref[rows, :], v, (((1,), (1,)), ((), ())), preferred_element_type=F32)
                e = da * a
                ehi, elo = _split2(e)
                ce = ce_ref[rows, :]
                big_e = _wide(ce) + _dot2(ehi, elo, lower)
                dz = (e - (e + big_e) * jnp.exp(lb)) * scale
                if mask is not None:
                    dz = jnp.where(mask, dz, 0.0)
                dz = dz.astype(BF16)
                a_buf[rows, :] = a.astype(BF16)
                dz_buf[rows, :] = dz
                dq_acc[rows, :] += jnp.dot(dz, k, preferred_element_type=F32)
                ce_ref[rows, :] = ce + _dot2(ehi, elo, ones)
            dk_acc[pl.ds(off, ATT_KB), :] += lax.dot_general(dz_buf[...], q_ref[...], tn_dims,
                                                             preferred_element_type=F32)
            dv_acc[pl.ds(off, ATT_KB), :] += lax.dot_general(a_buf[...], do_ref[...], tn_dims,
                                                             preferred_element_type=F32)

        def step2(jb, carry):
            pass2(jb, None)
            return carry

        lax.fori_loop(0, qi * ATT_DIAG, step2, 0)
        for d in range(ATT_DIAG):
            pass2(qi * ATT_DIAG + d, d)
        dq_ref[...] = dq_acc[...]

        @pl.when(qi == nq - 1)
        def _():
            dk_ref[...] = dk_acc[...]
            dv_ref[...] = dv_acc[...].astype(dv_ref.dtype)

    hd = SB_HEADS * HEAD_DIM
    (dq, dk, dv), bufs = comm_call(
        body, name=name, grid=(SB_HEADS, nq),
        in_specs=[pl.BlockSpec((tq, HEAD_DIM), lambda h, i: (i, h)),
                  pl.BlockSpec((t, HEAD_DIM), lambda h, i: (0, SB_HEADS + h)),
                  pl.BlockSpec((t, HEAD_DIM), lambda h, i: (0, 2 * SB_HEADS + h)),
                  pl.BlockSpec((tq, HEAD_DIM), lambda h, i: (i, h))],
        out_specs=[pl.BlockSpec((tq, HEAD_DIM), lambda h, i: (i, h)),
                   pl.BlockSpec((t, HEAD_DIM), lambda h, i: (0, h)),
                   pl.BlockSpec((t, HEAD_DIM), lambda h, i: (0, h))],
        out_shape=[jax.ShapeDtypeStruct((t, hd), F32), jax.ShapeDtypeStruct((t, hd), F32),
                   jax.ShapeDtypeStruct((t, hd), BF16)],
        scratch_shapes=[pltpu.VMEM((nkb, tq, LANES), F32), pltpu.VMEM((tq, LANES), F32),
                        pltpu.VMEM((tq, LANES), F32), pltpu.VMEM((tq, HEAD_DIM), F32),
                        pltpu.VMEM((tq, ATT_KB), BF16), pltpu.VMEM((tq, ATT_KB), BF16),
                        pltpu.VMEM((t, HEAD_DIM), F32), pltpu.VMEM((t, HEAD_DIM), F32)],
        inputs=[qk, qk, qkv_bf, do], sem=("arbitrary", "arbitrary"), ex=ex)
    return jnp.concatenate([dq, dk], axis=1), dv, bufs


def _qknorm_specs(qkv, gqk):
    t = qkv.shape[0]
    tr = _tile(t, 1024)
    data = [Data(qkv, (tr, HEAD_DIM), lambda o, i: (i, o))]
    params = [Param(gqk, (None, 1, HEAD_DIM), lambda o: (o // SB_HEADS, 0, 0), SB_HEADS)]
    outs = [Out((t, 2 * SB_HEADS * HEAD_DIM), BF16, (tr, HEAD_DIM), lambda o, i: (i, o))]
    return (2 * SB_HEADS, t // tr), data, params, outs


CONV_COLS = 128


def _shift_down(x, s, rows):
    if s == 0:
        return x
    return jnp.where(rows < s, 0.0, pltpu.roll(x, s, 0))


def _shift_up(x, s, rows):
    if s == 0:
        return x
    n = x.shape[0]
    return jnp.where(rows >= n - s, 0.0, pltpu.roll(x, n - s, 0))


def conv_fwd(xbc, w, b, name):
    t, c = xbc.shape
    tc = _tile(c, CONV_COLS)

    def body(x_ref, w_ref, b_ref, o_ref):
        x = x_ref[...]
        rows = lax.broadcasted_iota(jnp.int32, x.shape, 0)
        y = jnp.broadcast_to(b_ref[...], x.shape)
        for kk in range(SSD_CONV):
            y = y + w_ref[kk:kk + 1, :] * _shift_down(x, SSD_CONV - 1 - kk, rows)
        o_ref[...] = _silu(y)

    col = pl.BlockSpec((t, tc), lambda j: (0, j))
    return pl.pallas_call(
        body, name=name, grid=(c // tc,),
        in_specs=[col, pl.BlockSpec((SSD_CONV, tc), lambda j: (0, j)), pl.BlockSpec((1, tc), lambda j: (0, j))],
        out_specs=col, out_shape=jax.ShapeDtypeStruct((t, c), F32),
        compiler_params=_cparams("parallel"),
    )(xbc, w, b)


def conv_bwd(xbc, w, b, dact, name):
    t, c = xbc.shape
    tc = _tile(c, CONV_COLS)

    def body(x_ref, w_ref, b_ref, g_ref, dx_ref, dw_ref, db_ref):
        x = x_ref[...]
        rows = lax.broadcasted_iota(jnp.int32, x.shape, 0)
        shifted = [_shift_down(x, SSD_CONV - 1 - kk, rows) for kk in range(SSD_CONV)]
        y = jnp.broadcast_to(b_ref[...], x.shape)
        for kk in range(SSD_CONV):
            y = y + w_ref[kk:kk + 1, :] * shifted[kk]
        sig = 1.0 / (1.0 + jnp.exp(-y))
        dy = g_ref[...] * (sig * (1.0 + y * (1.0 - sig)))
        dx = jnp.zeros_like(x)
        for kk in range(SSD_CONV):
            dx = dx + w_ref[kk:kk + 1, :] * _shift_up(dy, SSD_CONV - 1 - kk, rows)
            dw_ref[kk:kk + 1, :] = jnp.sum(dy * shifted[kk], axis=0, keepdims=True)
        dx_ref[...] = dx.astype(dx_ref.dtype)
        db_ref[...] = jnp.sum(dy, axis=0, keepdims=True)

    col = pl.BlockSpec((t, tc), lambda j: (0, j))
    wspec = pl.BlockSpec((SSD_CONV, tc), lambda j: (0, j))
    bspec = pl.BlockSpec((1, tc), lambda j: (0, j))
    return pl.pallas_call(
        body, name=name, grid=(c // tc,),
        in_specs=[col, wspec, bspec, col], out_specs=[col, wspec, bspec],
        out_shape=[jax.ShapeDtypeStruct((t, c), BF16), jax.ShapeDtypeStruct((SSD_CONV, c), F32),
                   jax.ShapeDtypeStruct((1, c), F32)],
        compiler_params=_cparams("parallel"),
    )(xbc, w, b, dact)


def _ssd_chunk(x, dt, bm, cm, sprev, alog, dskip):
    n = x.shape[0]
    hp = lax.Precision.HIGHEST
    a = -jnp.exp(alog)
    da = dt * a
    row = lax.broadcasted_iota(jnp.int32, (n, n), 0)
    col = lax.broadcasted_iota(jnp.int32, (n, n), 1)
    causal = row >= col
    dam = jnp.broadcast_to(da, (n, n))
    acum_l = jnp.dot(causal.astype(F32), dam, precision=hp, preferred_element_type=F32)
    acum_s = jnp.dot(jnp.ones((n, n), F32), jnp.where(row <= col, dam, 0.0), precision=hp,
                     preferred_element_type=F32)
    decay = jnp.exp(jnp.where(causal, acum_l - acum_s, -1e30))
    acum = jnp.sum(acum_l, axis=1, keepdims=True) * (1.0 / n)
    atot = jnp.sum(da, axis=0, keepdims=True)
    xs = x * dt
    cb = lax.dot_general(cm, bm, (((1,), (1,)), ((), ())), preferred_element_type=F32)
    y = jnp.dot(cb * decay, xs, preferred_element_type=F32)
    y = y + jnp.exp(acum) * lax.dot_general(cm, sprev, (((1,), (1,)), ((), ())), preferred_element_type=F32)
    y = y + dskip * x
    snew = jnp.exp(atot) * sprev + lax.dot_general(
        xs * jnp.exp(atot - acum), bm, (((0,), (0,)), ((), ())), preferred_element_type=F32)
    return y, snew


def _ssd_specs(t, rev):
    nc = t // SSD_CHUNK
    cidx = (lambda c: nc - 1 - c) if rev else (lambda c: c)
    x_spec = pl.BlockSpec((SSD_HPG, SSD_CHUNK, SSD_HEAD_DIM), lambda g, c: (g, cidx(c), 0))
    dt_spec = pl.BlockSpec((SSD_HPG, SSD_CHUNK, 1), lambda g, c: (g, cidx(c), 0))
    bc_spec = pl.BlockSpec((None, SSD_CHUNK, SSD_STATE), lambda g, c: (g, cidx(c), 0))
    sc_spec = pl.BlockSpec((SSD_HPG, 1, 1), lambda g, c: (g, 0, 0))
    st_spec = pl.BlockSpec((SSD_HPG, None, SSD_HEAD_DIM, SSD_STATE), lambda g, c: (g, cidx(c), 0, 0))
    return nc, x_spec, dt_spec, bc_spec, sc_spec, st_spec


def ssd_scan_fwd(x, dt, bm, cm, alog, dskip, name, ex=None):
    t = x.shape[1]
    nc, x_spec, dt_spec, bc_spec, sc_spec, st_spec = _ssd_specs(t, False)

    def body(x_ref, dt_ref, b_ref, c_ref, al_ref, d_ref, y_ref, sin_ref, state):
        @pl.when(pl.program_id(1) == 0)
        def _():
            state[...] = jnp.zeros_like(state)

        bm_, cm_ = b_ref[...], c_ref[...]

        def head(r, carry):
            sprev = state[r]
            sin_ref[r] = sprev
            y, snew = _ssd_chunk(x_ref[r], dt_ref[r], bm_, cm_, sprev, al_ref[r], d_ref[r])
            y_ref[r] = y
            state[r] = snew
            return carry

        lax.fori_loop(0, SSD_HPG, head, 0, unroll=2)

    (y, sin), bufs = comm_call(
        body, name=name, grid=(SSD_GROUPS, nc),
        in_specs=[x_spec, dt_spec, bc_spec, bc_spec, sc_spec, sc_spec],
        out_specs=[x_spec, st_spec],
        out_shape=[jax.ShapeDtypeStruct(x.shape, F32),
                   jax.ShapeDtypeStruct((SSD_HEADS, nc, SSD_HEAD_DIM, SSD_STATE), F32)],
        scratch_shapes=[pltpu.VMEM((SSD_HPG, SSD_HEAD_DIM, SSD_STATE), F32)],
        inputs=[x, dt, bm, cm, alog, dskip], sem=("parallel", "arbitrary"), ex=ex)
    return y, sin, bufs


def ssd_scan_bwd(x, dt, bm, cm, alog, dskip, sin, dy, name, ex=None):
    t = x.shape[1]
    nc, x_spec, dt_spec, bc_spec, sc_spec, st_spec = _ssd_specs(t, True)

    def body(x_ref, dt_ref, b_ref, c_ref, al_ref, d_ref, sin_ref, dy_ref,
             dx_ref, ddt_ref, db_ref, dc_ref, dal_ref, dd_ref, dstate):
        @pl.when(pl.program_id(1) == 0)
        def _():
            dstate[...] = jnp.zeros_like(dstate)
            dal_ref[...] = jnp.zeros_like(dal_ref)
            dd_ref[...] = jnp.zeros_like(dd_ref)

        db_ref[...] = jnp.zeros_like(db_ref)
        dc_ref[...] = jnp.zeros_like(dc_ref)
        bm_, cm_ = b_ref[...], c_ref[...]

        def head(r, carry):
            _, vjp = jax.vjp(_ssd_chunk, x_ref[r], dt_ref[r], bm_, cm_, sin_ref[r], al_ref[r], d_ref[r])
            dx, ddt, dbm, dcm, dsp, dal, dd = vjp((dy_ref[r], dstate[r]))
            dx_ref[r] = dx
            ddt_ref[r] = ddt
            db_ref[...] += dbm
            dc_ref[...] += dcm
            dstate[r] = dsp
            dal_ref[r] += dal
            dd_ref[r] += dd
            return carry

        lax.fori_loop(0, SSD_HPG, head, 0, unroll=2)

    return comm_call(
        body, name=name, grid=(SSD_GROUPS, nc),
        in_specs=[x_spec, dt_spec, bc_spec, bc_spec, sc_spec, sc_spec, st_spec, x_spec],
        out_specs=[x_spec, dt_spec, bc_spec, bc_spec, sc_spec, sc_spec],
        out_shape=[jax.ShapeDtypeStruct(x.shape, F32), jax.ShapeDtypeStruct(dt.shape, F32),
                   jax.ShapeDtypeStruct(bm.shape, F32), jax.ShapeDtypeStruct(cm.shape, F32),
                   jax.ShapeDtypeStruct(alog.shape, F32), jax.ShapeDtypeStruct(dskip.shape, F32)],
        scratch_shapes=[pltpu.VMEM((SSD_HPG, SSD_HEAD_DIM, SSD_STATE), F32)],
        inputs=[x, dt, bm, cm, alog, dskip, sin, dy], sem=("arbitrary", "arbitrary"), ex=ex)


def _to_heads(a, n, w):
    t = a.shape[0]
    return a.reshape(t, n, w).transpose(1, 0, 2)


def _from_heads(a):
    n, t, w = a.shape
    return a.transpose(1, 0, 2).reshape(t, n * w)


def _add(acc, h):
    return (acc + h,)


def sb_fwd(h, hn, w, j, tag, ex=None):
    gqk = w["sb_gqk"][j]
    qkv, qkv_bf = matmul(hn, w["sb_w_qkv"], "nn", f"{tag}_qkv", b_layer=j, out_dtypes=(F32, BF16),
                         epilogue=lambda acc: (acc, acc))
    grid, data, params, outs = _qknorm_specs(qkv, gqk)
    qk = blockwise_fwd(_rmsnorm_f, grid, data, params, outs, f"{tag}_qknorm")[0]
    o, bufs = attn_fwd(qk, qkv_bf, f"{tag}_attn", ex)
    w.update(bufs)
    h1 = matmul(o, w["sb_w_o"], "nn", f"{tag}_out", b_layer=j, epilogue=_add, extras=(h,))
    return h1, (qkv, qkv_bf, qk, o)


def sb_bwd(g_bf, hn, saved, w, j, tag, ex=None):
    qkv, qkv_bf, qk, o = saved
    wqkv, wo, gqk = w["sb_w_qkv"], w["sb_w_o"], w["sb_gqk"][j]
    d_wo = matmul(o, g_bf, "tn", f"{tag}_dwo", out_dtypes=(BF16,))
    do = matmul(g_bf, wo, "nt", f"{tag}_do", b_layer=j, out_dtypes=(BF16,))
    dqk, dv, bufs = attn_bwd(qk, qkv_bf, do, f"{tag}_attn_bwd", ex)
    grid, data, params, outs = _qknorm_specs(qkv, gqk)
    (draw,), (dgqk,) = blockwise_bwd(_rmsnorm_f, grid, data, params, outs, [dqk], f"{tag}_qknorm_bwd",
                                     grad_dtypes=[BF16])
    dqkv = jnp.concatenate([draw[:, :2 * SB_HEADS * HEAD_DIM], dv], axis=1)
    d_wqkv = matmul(hn, dqkv, "tn", f"{tag}_dwqkv", out_dtypes=(BF16,))
    dhn = matmul(dqkv, wqkv, "nt", f"{tag}_dhn", b_layer=j)
    return dhn, {"sb_w_qkv": d_wqkv, "sb_w_o": d_wo, "sb_gqk": dgqk}, bufs


def _gm_act_specs(p, gv):
    t = p.shape[0]
    tr = _row_tile(t)
    d = D_MODEL
    data = [Data(p, (tr, d), lambda o, i: (i, 0)), Data(p, (tr, d), lambda o, i: (i, 1))]
    params = [Param(gv, (1, d), lambda o: (0, 0), 1)]
    outs = [Out((t, d), F32, (tr, d), lambda o, i: (i, 0)), Out((t, d), BF16, (tr, d), lambda o, i: (i, 0))]
    return (1, t // tr), data, params, outs


def _gm_gate_specs(u, vn, ws, bs):
    t = u.shape[0]
    blk = (GM_CHUNK, LANES)
    data = [Data(u, blk, lambda o, i: (i, o)), Data(vn, blk, lambda o, i: (i, o))]
    params = [Param(ws, (None, GM_CHUNK, GM_CHUNK), lambda o: (o, 0, 0), 1),
              Param(bs, (None, GM_CHUNK, 1), lambda o: (o, 0, 0), 1)]
    outs = [Out((t, D_MODEL), BF16, blk, lambda o, i: (i, o))]
    return (GM_GROUPS, t // GM_CHUNK), data, params, outs


def gm_fwd(h, hn, w, j, tag):
    p = matmul(hn, w["gm_w_in"], "nn", f"{tag}_in", b_layer=j)
    grid, data, params, outs = _gm_act_specs(p, w["gm_v_norm_g"])
    u, vn = blockwise_fwd(_gm_act_f, grid, data, params, outs, f"{tag}_act")
    grid, data, params, outs = _gm_gate_specs(u, vn, w["gm_w_s"], w["gm_b_s"])
    y = blockwise_fwd(_gm_gate_f, grid, data, params, outs, f"{tag}_gate")[0]
    h1 = matmul(y, w["gm_w_o"], "nn", f"{tag}_out", b_layer=j, epilogue=_add, extras=(h,))
    return h1, (p, u, vn, y)


def gm_bwd(g_bf, hn, saved, w, j, tag):
    p, u, vn, y = saved
    d_wo = matmul(y, g_bf, "tn", f"{tag}_dwo", out_dtypes=(BF16,))
    dy = matmul(g_bf, w["gm_w_o"], "nt", f"{tag}_dy", b_layer=j)
    grid, data, params, outs = _gm_gate_specs(u, vn, w["gm_w_s"], w["gm_b_s"])
    (du, dvn), (dws, dbs) = blockwise_bwd(_gm_gate_f, grid, data, params, outs, [dy], f"{tag}_gate_bwd",
                                          grad_dtypes=[F32, F32])
    grid, data, params, outs = _gm_act_specs(p, w["gm_v_norm_g"])
    (dpu, dpv), (dgv,) = blockwise_bwd(_gm_act_f, grid, data, params, outs, [du, dvn], f"{tag}_act_bwd",
                                       grad_dtypes=[BF16, BF16])
    dp = jnp.concatenate([dpu[:, :D_MODEL], dpv[:, D_MODEL:]], axis=1)
    d_win = matmul(hn, dp, "tn", f"{tag}_dwin", out_dtypes=(BF16,))
    dhn = matmul(dp, w["gm_w_in"], "nt", f"{tag}_dhn", b_layer=j)
    return dhn, {"gm_w_in": d_win, "gm_w_o": d_wo, "gm_v_norm_g": dgv, "gm_w_s": dws, "gm_b_s": dbs}


def _ssd_gate_specs(y, z, g):
    t = y.shape[0]
    tr = _row_tile(t)
    gw = SSD_INNER // SSD_GROUPS
    blk = (tr, gw)
    data = [Data(y, blk, lambda o, i: (i, o)), Data(z, blk, lambda o, i: (i, o))]
    params = [Param(g, (None, 1, gw), lambda o: (o, 0, 0), 1)]
    outs = [Out((t, SSD_INNER), BF16, blk, lambda o, i: (i, o))]
    return (SSD_GROUPS, t // tr), data, params, outs


def _dt_specs(raw, bias):
    t = raw.shape[0]
    tr = _row_tile(t)
    data = [Data(raw, (tr, DT_PAD), lambda o, i: (i, 0))]
    params = [Param(bias, (1, DT_PAD), lambda o: (0, 0), 1)]
    outs = [Out((t, DT_PAD), F32, (tr, DT_PAD), lambda o, i: (i, 0))]
    return (1, t // tr), data, params, outs


def ssd_fwd(h, hn, w, tag, ex=None):
    z = matmul(hn, w["ssd_wz"], "nn", f"{tag}_inz")
    xbc = matmul(hn, w["ssd_wxbc"], "nn", f"{tag}_inx")
    raw = matmul(hn, w["ssd_wdt"], "nn", f"{tag}_indt")
    act = conv_fwd(xbc, w["ssd_conv_w"], w["ssd_conv_b"], f"{tag}_conv")
    grid, data, params, outs = _dt_specs(raw, w["ssd_dt_bias"])
    dt = blockwise_fwd(_dt_f, grid, data, params, outs, f"{tag}_dt")[0]
    xh = _to_heads(act[:, :SSD_INNER], SSD_HEADS, SSD_HEAD_DIM)
    bh = _to_heads(act[:, SSD_INNER:SSD_INNER + SSD_BC], SSD_GROUPS, SSD_STATE)
    ch = _to_heads(act[:, SSD_INNER + SSD_BC:], SSD_GROUPS, SSD_STATE)
    dth = _to_heads(dt[:, :SSD_HEADS], SSD_HEADS, 1)
    yh, sin, bufs = ssd_scan_fwd(xh, dth, bh, ch, w["ssd_a_log"], w["ssd_d"], f"{tag}_scan", ex)
    w.update(bufs)
    y = _from_heads(yh)
    grid, data, params, outs = _ssd_gate_specs(y, z, w["ssd_norm_g"])
    yn = blockwise_fwd(_ssd_gate_f, grid, data, params, outs, f"{tag}_gate")[0]
    h1 = matmul(yn, w["ssd_w_o"], "nn", f"{tag}_out", b_layer=0, epilogue=_add, extras=(h,))
    return h1, (z, xbc, raw, xh, bh, ch, dth, sin, y, yn)


def ssd_bwd(g_bf, hn, saved, w, tag, ex=None):
    z, xbc, raw, xh, bh, ch, dth, sin, y, yn = saved
    t = z.shape[0]
    d_wo = matmul(yn, g_bf, "tn", f"{tag}_dwo", out_dtypes=(BF16,))
    dyn = matmul(g_bf, w["ssd_w_o"], "nt", f"{tag}_dyn", b_layer=0)
    grid, data, params, outs = _ssd_gate_specs(y, z, w["ssd_norm_g"])
    (dy, dz), (dng,) = blockwise_bwd(_ssd_gate_f, grid, data, params, outs, [dyn], f"{tag}_gate_bwd",
                                     grad_dtypes=[F32, BF16])
    (dxh, ddth, dbh, dch, dal, ddk), bufs = ssd_scan_bwd(
        xh, dth, bh, ch, w["ssd_a_log"], w["ssd_d"], sin, _to_heads(dy, SSD_HEADS, SSD_HEAD_DIM),
        f"{tag}_scan_bwd", ex)
    dact = jnp.concatenate([_from_heads(dxh), _from_heads(dbh), _from_heads(dch)], axis=1)
    dxbc, dcw, dcb = conv_bwd(xbc, w["ssd_conv_w"], w["ssd_conv_b"], dact, f"{tag}_conv_bwd")
    ddt = jnp.pad(_from_heads(ddth), ((0, 0), (0, DT_PAD - SSD_HEADS)))
    grid, data, params, outs = _dt_specs(raw, w["ssd_dt_bias"])
    (draw,), (dbias,) = blockwise_bwd(_dt_f, grid, data, params, outs, [ddt], f"{tag}_dt_bwd",
                                      grad_dtypes=[BF16])
    d_wz = matmul(hn, dz, "tn", f"{tag}_dwz", out_dtypes=(BF16,))
    d_wx = matmul(hn, dxbc, "tn", f"{tag}_dwx", out_dtypes=(BF16,))
    d_wdt = matmul(hn, draw, "tn", f"{tag}_dwdt", out_dtypes=(BF16,))
    dhn = matmul(dz, w["ssd_wz"], "nt", f"{tag}_dhn_z")
    dhn = matmul(dxbc, w["ssd_wxbc"], "nt", f"{tag}_dhn_x", epilogue=_add, extras=(dhn,))
    dhn = matmul(draw, w["ssd_wdt"], "nt", f"{tag}_dhn_dt", epilogue=_add, extras=(dhn,))
    d_win = jnp.concatenate([d_wz, d_wx, d_wdt[:, :SSD_HEADS]], axis=1)
    d_win = d_win.reshape(D_MODEL, N_DEV, SSD_PROJ // N_DEV).transpose(1, 0, 2)
    return dhn, {"ssd_w_in": d_win, "ssd_w_o": d_wo, "ssd_conv_w": dcw, "ssd_conv_b": dcb,
                 "ssd_dt_bias": dbias[:, :SSD_HEADS], "ssd_a_log": dal, "ssd_d": ddk, "ssd_norm_g": dng}, bufs


def _relu2(acc):
    return (acc, jnp.square(jnp.maximum(acc, 0.0)))


def _relu2_bwd(acc, a):
    return (acc * (2.0 * jnp.maximum(a, 0.0)),)


def device_step(x, target, w, shards, shard_shapes):
    def gather(group, phase, name):
        w.update(run_exchange(ag_exchange(AG_GROUPS[group], shards, w, phase), name))

    gather(0, "A", "gather0_ici")
    gather(0, "B", "gather0_d2d")
    h = x
    saved = []
    for i in range(DEPTH):
        kind, j = i % N_MIXERS, i // N_MIXERS
        tag = f"l{i}"
        hn = rmsnorm_fwd(h, w["norm_mix_g"][i], f"{tag}_norm_mix")
        if kind == 0:
            ex = ag_exchange(AG_GROUPS[1], shards, w, "A") if i == 0 else None
            h1, ms = sb_fwd(h, hn, w, j, f"{tag}_sb", ex)
        elif kind == 1:
            h1, ms = gm_fwd(h, hn, w, j, f"{tag}_gm")
        else:
            h1, ms = ssd_fwd(h, hn, w, f"{tag}_ssd", ag_exchange(AG_GROUPS[2], shards, w, "A"))
        hm = rmsnorm_fwd(h1, w["norm_mlp_g"][i], f"{tag}_norm_mlp")
        a, r = matmul(hm, w["mlp_w_in"], "nn", f"{tag}_mlp_in", b_layer=i, out_dtypes=(F32, BF16),
                      epilogue=_relu2)
        h2 = matmul(r, w["mlp_w_out"], "nn", f"{tag}_mlp_out", b_layer=i, epilogue=_add, extras=(h1,))
        saved.append((h, hn, ms, h1, hm, a, r))
        h = h2
        if i == 0:
            gather(1, "B", "gather1_d2d")
            _ssd_weights(w)
        elif i == 2:
            gather(2, "B", "gather2_d2d")
    dh, loss = loss_call(h, target)

    recv, pending = {}, []
    small = {"norm_mix_g": [None] * DEPTH, "norm_mlp_g": [None] * DEPTH, "sb_gqk": [None] * 2}
    for i in reversed(range(DEPTH)):
        kind, j = i % N_MIXERS, i // N_MIXERS
        tag = f"l{i}"
        h0, hn, ms, h1, hm, a, r = saved[i]
        g_bf = dh.astype(BF16)
        pending.append(("mlp_w_out", i, matmul(r, g_bf, "tn", f"{tag}_mlp_dwout", out_dtypes=(BF16,))))
        da = matmul(g_bf, w["mlp_w_out"], "nt", f"{tag}_mlp_da", b_layer=i, out_dtypes=(BF16,),
                    epilogue=_relu2_bwd, extras=(a,))
        pending.append(("mlp_w_in", i, matmul(hm, da, "tn", f"{tag}_mlp_dwin", out_dtypes=(BF16,))))
        dhm = matmul(da, w["mlp_w_in"], "nt", f"{tag}_mlp_dhm", b_layer=i)
        dh1, small["norm_mlp_g"][i] = rmsnorm_bwd(h1, w["norm_mlp_g"][i], dhm, dh, f"{tag}_norm_mlp_bwd")
        g_bf = dh1.astype(BF16)
        if kind == 1:
            dhn, mg = gm_bwd(g_bf, hn, ms, w, j, f"{tag}_gm")
        else:
            ex, pending = rs_exchange(pending, recv, shard_shapes), []
            if kind == 0:
                dhn, mg, bufs = sb_bwd(g_bf, hn, ms, w, j, f"{tag}_sb", ex)
                small["sb_gqk"][j] = mg.pop("sb_gqk")
            else:
                dhn, mg, bufs = ssd_bwd(g_bf, hn, ms, w, f"{tag}_ssd", ex)
            recv.update(bufs)
        for kname in list(mg):
            if kname in BIG:
                pending.append((kname, j, mg.pop(kname)))
        small.update(mg)
        dh, small["norm_mix_g"][i] = rmsnorm_bwd(h0, w["norm_mix_g"][i], dhn, dh1, f"{tag}_norm_mix_bwd")
    dcw = small["ssd_conv_w"].reshape(SSD_CONV, N_DEV, SMALL_W).transpose(1, 0, 2)
    dcb = small["ssd_conv_b"].reshape(N_DEV, 1, SMALL_W)
    dng = jnp.pad(small["ssd_norm_g"], ((0, 0), (0, 0), (0, SMALL_W - SSD_INNER // N_DEV)))
    pending.append(("small", None, jnp.concatenate([dcw, dcb, dng, jnp.zeros((N_DEV, 2, SMALL_W), F32)], axis=1)))
    pending.append(("repl", None, _pack(_repl_grads(small), shard_shapes["repl"][0])))
    recv.update(run_exchange(rs_exchange(pending, recv, shard_shapes), "scatter_rest"))
    return loss, dh, recv


def _my_pos():
    return lax.axis_index("x"), lax.axis_index("y"), lax.axis_index("c")


def _lin(p):
    return 4 * p[0] + 2 * p[1] + p[2]


def _shard_view(ref, kind, width, idx, layer=None):
    if kind == "col":
        start = pl.multiple_of(idx * width, LANES)
        return ref.at[:, :, pl.ds(start, width)] if layer is None else ref.at[layer, :, pl.ds(start, width)]
    if kind == "row":
        start = pl.multiple_of(idx * width, 16)
        return ref.at[:, pl.ds(start, width), :] if layer is None else ref.at[layer, pl.ds(start, width), :]
    assert kind == "slot"
    return ref.at[idx]


_FLIPS = [(0, 0, 1), (0, 1, 0), (0, 1, 1), (1, 0, 0), (1, 0, 1), (1, 1, 0), (1, 1, 1)]
_SIBLING = (0, 0, 1)
_AG_FLIPS = [_SIBLING, (1, 0, 0), (0, 1, 0), (1, 1, 0)]


def _flip(pos, f):
    return tuple((1 - p) if b else p for p, b in zip(pos, f))


def _other_chip(pos, j):
    x, y = pos[0], pos[1]
    return [(1 - x, y), (x, 1 - y), (1 - x, 1 - y)][j]


class Exchange:
    def __init__(self):
        self.sources, self.buffers, self.items, self.locals = [], [], [], []

    def source(self, array):
        self.sources.append(array)
        return len(self.sources) - 1

    def buffer(self, name, bufs, shape, dtype):
        if name not in [b[0] for b in self.buffers]:
            self.buffers.append((name, bufs.get(name), jax.ShapeDtypeStruct(shape, dtype)))

    def n_copies(self):
        return sum(len(flips) for _, _, flips in self.items)


def _exchange_fns(ex, srcs, bufs, send_sems, recv_sems, local_sems):
    def local(k):
        sf, df = ex.locals[k]
        me = _my_pos()
        return pltpu.make_async_copy(sf(srcs, bufs, me, me), df(bufs, me), local_sems.at[k])

    def remote(sending):
        me = _my_pos()
        k = 0
        for sf, df, flips in ex.items:
            for f in flips:
                peer = _flip(me, f)
                yield pltpu.make_async_remote_copy(
                    src_ref=sf(srcs, bufs, me, peer), dst_ref=df(bufs, me if sending else peer),
                    send_sem=send_sems.at[k], recv_sem=recv_sems.at[k], device_id=peer, device_id_type=MESH)
                k += 1

    def start():
        for k in range(len(ex.locals)):
            local(k).start()
        for cp in remote(True):
            cp.start()

    def wait():
        arrivals = list(remote(False))
        for cp in arrivals:
            cp.wait_recv()
        for cp in arrivals:
            cp.wait_send()
        for k in range(len(ex.locals)):
            local(k).wait()

    return start, wait


def comm_call(core, *, name, grid, in_specs, out_specs, out_shape, scratch_shapes, inputs, sem, ex=None):
    n_in, n_out, n_scr = len(in_specs), len(out_specs), len(scratch_shapes)
    if ex is None:
        res = pl.pallas_call(core, name=name, grid=grid, in_specs=in_specs, out_specs=out_specs,
                             out_shape=out_shape, scratch_shapes=scratch_shapes,
                             compiler_params=_cparams(*sem))(*inputs)
        return list(res), {}
    existing = [(k, b[1]) for k, b in enumerate(ex.buffers) if b[1] is not None]
    names = [b[0] for b in ex.buffers]
    n_src, n_old, n_buf = len(ex.sources), len(existing), len(names)

    def body(*refs):
        cuts = [n_in, n_src, n_old, n_out, n_buf, n_scr, 3]
        parts, pos = [], 0
        for c in cuts:
            parts.append(refs[pos:pos + c])
            pos += c
        core_in, srcs, _, core_out, buf_refs, core_scr, sems = parts
        start, wait = _exchange_fns(ex, srcs, dict(zip(names, buf_refs)), *sems)
        if not grid:
            start()
            wait()
            return
        ids = [pl.program_id(d) for d in range(len(grid))]
        pl.when(functools.reduce(jnp.logical_and, [i == 0 for i in ids]))(start)
        core(*core_in, *core_out, *core_scr)
        pl.when(functools.reduce(jnp.logical_and, [i == g - 1 for i, g in zip(ids, grid)]))(wait)

    dma = pltpu.SemaphoreType.DMA
    res = pl.pallas_call(
        body, name=name, **({"grid": grid} if grid else {}),
        in_specs=list(in_specs) + [HBM_SPEC] * (n_src + n_old),
        out_specs=list(out_specs) + [HBM_SPEC] * n_buf,
        out_shape=list(out_shape) + [b[2] for b in ex.buffers],
        scratch_shapes=list(scratch_shapes) + [dma((max(ex.n_copies(), 1),)), dma((max(ex.n_copies(), 1),)),
                                               dma((max(len(ex.locals), 1),))],
        input_output_aliases={n_in + n_src + e: n_out + k for e, (k, _) in enumerate(existing)},
        compiler_params=_cparams(*(["arbitrary"] * len(grid))),
    )(*inputs, *ex.sources, *[b for _, b in existing])
    return list(res[:n_out]), dict(zip(names, res[n_out:]))


def run_exchange(ex, name):
    return comm_call(None, name=name, grid=(), in_specs=[], out_specs=[], out_shape=[], scratch_shapes=[],
                     inputs=[], sem=(), ex=ex)[1]


def _gathered_shape(shard, kind):
    s = shard.shape
    if kind == "col":
        return s[:2] + (s[2] * N_DEV,)
    if kind == "row":
        return (s[0], s[1] * N_DEV, s[2])
    return (N_DEV,) + s


def ag_exchange(group, shards, bufs, phase):
    ex = Exchange()
    for tn, layer in group:
        shard, kind = shards[tn], SHARD_KIND[tn]
        width = {"col": shard.shape[-1], "row": shard.shape[-2], "slot": None}[kind]
        ex.buffer(tn, bufs, _gathered_shape(shard, kind), shard.dtype)

        def part(b, who, tn=tn, kind=kind, width=width, layer=layer):
            return _shard_view(b[tn], kind, width, _lin(who), layer)

        if phase == "A":
            def src(s, b, me, peer, si=ex.source(shard), layer=layer):
                return s[si] if layer is None else s[si].at[layer]

            ex.items.append((src, part, _AG_FLIPS))
            ex.locals.append((src, part))
        else:
            for j in range(3):
                def held(who, j=j):
                    return (*_other_chip(who, j), who[2])

                ex.items.append((lambda s, b, me, peer, part=part, held=held: part(b, held(me)),
                                 lambda b, sender, part=part, held=held: part(b, held(sender)), [_SIBLING]))
    return ex


def rs_exchange(pending, bufs, shard_shapes):
    ex = Exchange()
    for tn, layer, g in pending:
        kind = SHARD_KIND[tn]
        shape = shard_shapes[tn]
        width = {"col": shape[-1], "row": shape[-2]}.get(kind)
        ex.buffer(tn, bufs, (N_DEV,) + tuple(shape), g.dtype)

        def src(s, b, me, peer, si=ex.source(g), kind=kind, width=width):
            idx = _lin(peer)
            if kind == "all":
                return s[si]
            if kind == "slot":
                return s[si].at[idx]
            if kind == "col":
                return s[si].at[:, pl.ds(pl.multiple_of(idx * width, LANES), width)]
            return s[si].at[pl.ds(pl.multiple_of(idx * width, 16), width), :]

        def dst(b, sender, tn=tn, layer=layer):
            return b[tn].at[_lin(sender)] if layer is None else b[tn].at[_lin(sender), layer]

        ex.items.append((src, dst, _FLIPS))
        ex.locals.append((src, dst))
    return ex


_C1 = 1.0 - ADAM_B1 ** ADAM_STEP
_C2 = 1.0 - ADAM_B2 ** ADAM_STEP


def adamw(wt, m, v, slots, name):
    nl, r, c = wt.shape
    tr = _tile(r, 128)

    def body(w_ref, m_ref, v_ref, s_ref, g_ref, d_ref, nm_ref, nv_ref):
        g = s_ref[0].astype(F32)
        for k in range(1, N_DEV):
            g = g + s_ref[k].astype(F32)
        m_new = ADAM_B1 * m_ref[...] + (1.0 - ADAM_B1) * g
        v_new = ADAM_B2 * v_ref[...] + (1.0 - ADAM_B2) * jnp.square(g)
        m_hat = m_new / _C1
        v_hat = v_new / _C2
        g_ref[...] = g
        d_ref[...] = -ADAM_LR * (m_hat / (jnp.sqrt(v_hat) + ADAM_EPS) + ADAM_WD * w_ref[...])
        nm_ref[...] = m_new
        nv_ref[...] = v_new

    blk = pl.BlockSpec((None, tr, c), lambda l, i: (l, i, 0))
    sblk = pl.BlockSpec((N_DEV, None, tr, c), lambda l, i: (0, l, i, 0))
    return pl.pallas_call(
        body, name=name, grid=(nl, r // tr), in_specs=[blk, blk, blk, sblk], out_specs=[blk] * 4,
        out_shape=[jax.ShapeDtypeStruct(wt.shape, F32)] * 4,
        compiler_params=_cparams("parallel", "parallel"),
    )(wt, m, v, slots)


def _pack(arrays, rows):
    flat = jnp.concatenate([a.reshape(-1).astype(F32) for a in arrays])
    return jnp.pad(flat, (0, rows * LANES - flat.shape[0])).reshape(rows, LANES)


def _unpack(packed, shapes):
    flat = packed.reshape(-1)
    out, off = [], 0
    for s in shapes:
        sz = math.prod(s)
        out.append(flat[off:off + sz].reshape(s))
        off += sz
    return out


SMALL_W = 768


def _pack_small_shard(conv_w, conv_b, norm_g):
    ng = jnp.pad(norm_g, ((0, 0), (0, SMALL_W - norm_g.shape[1])))
    return jnp.concatenate([conv_w[0], conv_b, ng, jnp.zeros((2, SMALL_W), F32)], axis=0)


def _unpack_small_shard(p):
    return p[0:4][None], p[4:5], p[5:6, :SSD_INNER // N_DEV]


REPL = ["norm_mix_g", "norm_mlp_g", "sb_q_norm_g", "sb_k_norm_g", "gm_v_norm_g", "gm_w_s", "gm_b_s",
        "ssd_dt_bias", "ssd_a_log", "ssd_d"]
BIG = ["sb_w_qkv", "sb_w_o", "gm_w_in", "gm_w_o", "ssd_w_in", "ssd_w_o", "mlp_w_in", "mlp_w_out"]
BIG_KIND = {"sb_w_qkv": "col", "sb_w_o": "row", "gm_w_in": "col", "gm_w_o": "row", "ssd_w_in": "slot",
            "ssd_w_o": "row", "mlp_w_in": "col", "mlp_w_out": "row"}
SMALL_SHARDED = ["ssd_conv_w", "ssd_conv_b", "ssd_norm_g"]
WEIGHTS = ["norm_mix_g", "norm_mlp_g", "sb_w_qkv", "sb_q_norm_g", "sb_k_norm_g", "sb_w_o", "gm_w_in",
           "gm_v_norm_g", "gm_w_s", "gm_b_s", "gm_w_o", "ssd_w_in", "ssd_conv_w", "ssd_conv_b", "ssd_dt_bias",
           "ssd_a_log", "ssd_d", "ssd_norm_g", "ssd_w_o", "mlp_w_in", "mlp_w_out"]


SHARD_KIND = dict(BIG_KIND, small="slot", repl="all")
AG_GROUPS = [
    [("sb_w_qkv", 0), ("sb_w_o", 0), ("mlp_w_in", 0), ("mlp_w_out", 0)],
    [("gm_w_in", 0), ("gm_w_o", 0), ("mlp_w_in", 1), ("mlp_w_out", 1), ("ssd_w_in", None), ("small", None),
     ("ssd_w_o", 0), ("mlp_w_in", 2), ("mlp_w_out", 2)],
    [("sb_w_qkv", 1), ("sb_w_o", 1), ("mlp_w_in", 3), ("mlp_w_out", 3)],
]


def _ssd_weights(w):
    win = jnp.moveaxis(w["ssd_w_in"][:, 0], 0, 1).reshape(D_MODEL, SSD_PROJ)
    small = w["small"]
    w["ssd_wz"] = win[:, :SSD_INNER]
    w["ssd_wxbc"] = win[:, SSD_INNER:SSD_INNER + SSD_CONV_DIM]
    w["ssd_wdt"] = jnp.pad(win[:, SSD_INNER + SSD_CONV_DIM:], ((0, 0), (0, DT_PAD - SSD_HEADS)))
    w["ssd_conv_w"] = small[:, 0:4].transpose(1, 0, 2).reshape(SSD_CONV, SSD_CONV_DIM)
    w["ssd_conv_b"] = small[:, 4].reshape(1, SSD_CONV_DIM)
    w["ssd_norm_g"] = small[:, 5:6, :SSD_INNER // N_DEV]


def _small_weights(p):
    w = {}
    w["norm_mix_g"] = [p["norm_mix_g"][i:i + 1] for i in range(DEPTH)]
    w["norm_mlp_g"] = [p["norm_mlp_g"][i:i + 1] for i in range(DEPTH)]
    w["sb_gqk"] = [jnp.stack([p["sb_q_norm_g"][j:j + 1], p["sb_k_norm_g"][j:j + 1]]) for j in range(2)]
    w["gm_v_norm_g"] = p["gm_v_norm_g"]
    w["gm_w_s"] = p["gm_w_s"][0]
    w["gm_b_s"] = p["gm_b_s"][0][:, :, None]
    w["ssd_dt_bias"] = jnp.pad(p["ssd_dt_bias"], ((0, 0), (0, DT_PAD - SSD_HEADS)))
    w["ssd_a_log"] = p["ssd_a_log"].reshape(SSD_HEADS, 1, 1)
    w["ssd_d"] = p["ssd_d"].reshape(SSD_HEADS, 1, 1)
    return w


def _repl_grads(g):
    return [jnp.concatenate(g["norm_mix_g"], axis=0), jnp.concatenate(g["norm_mlp_g"], axis=0),
            jnp.concatenate([g["sb_gqk"][0][0], g["sb_gqk"][1][0]], axis=0),
            jnp.concatenate([g["sb_gqk"][0][1], g["sb_gqk"][1][1]], axis=0),
            g["gm_v_norm_g"], g["gm_w_s"][None], g["gm_b_s"][None, :, :, 0],
            g["ssd_dt_bias"], g["ssd_a_log"].reshape(1, SSD_HEADS), g["ssd_d"].reshape(1, SSD_HEADS)]


def kernel(x, norm_mix_g, norm_mlp_g, sb_w_qkv, sb_q_norm_g, sb_k_norm_g, sb_w_o, gm_w_in, gm_v_norm_g, gm_w_s, gm_b_s, gm_w_o, ssd_w_in, ssd_conv_w, ssd_conv_b, ssd_dt_bias, ssd_a_log, ssd_d, ssd_norm_g, ssd_w_o, mlp_w_in, mlp_w_out, loss_target, m_norm_mix_g, m_norm_mlp_g, m_sb_w_qkv, m_sb_q_norm_g, m_sb_k_norm_g, m_sb_w_o, m_gm_w_in, m_gm_v_norm_g, m_gm_w_s, m_gm_b_s, m_gm_w_o, m_ssd_w_in, m_ssd_conv_w, m_ssd_conv_b, m_ssd_dt_bias, m_ssd_a_log, m_ssd_d, m_ssd_norm_g, m_ssd_w_o, m_mlp_w_in, m_mlp_w_out, v_norm_mix_g, v_norm_mlp_g, v_sb_w_qkv, v_sb_q_norm_g, v_sb_k_norm_g, v_sb_w_o, v_gm_w_in, v_gm_v_norm_g, v_gm_w_s, v_gm_b_s, v_gm_w_o, v_ssd_w_in, v_ssd_conv_w, v_ssd_conv_b, v_ssd_dt_bias, v_ssd_a_log, v_ssd_d, v_ssd_norm_g, v_ssd_w_o, v_mlp_w_in, v_mlp_w_out):
    args = dict(locals())
    p = {k: args[k] for k in WEIGHTS}
    pm = {k: args["m_" + k] for k in WEIGHTS}
    pv = {k: args["v_" + k] for k in WEIGHTS}

    shards = {k: p[k].astype(BF16) for k in BIG}
    shards["small"] = _pack_small_shard(*[p[k] for k in SMALL_SHARDED])
    repl_shapes = [p[k].shape for k in REPL]
    n_repl = sum(math.prod(s) for s in repl_shapes)
    repl_rows = -(-n_repl // (LANES * LANES)) * LANES
    shard_shapes = {k: p[k].shape for k in BIG}
    shard_shapes.update(small=(8, SMALL_W), repl=(repl_rows, LANES))

    loss, grad_x, recv = device_step(x[0], loss_target[0], _small_weights(p), shards, shard_shapes)
    loss = lax.psum(loss[0, 0], ("x", "y", "c"))

    out = {}
    for k in BIG:
        out[k] = adamw(p[k], pm[k], pv[k], recv[k], f"adamw_{k}")
    sm = adamw(*[_pack_small_shard(*[d[k] for k in SMALL_SHARDED])[None] for d in (p, pm, pv)],
               recv["small"][:, None], "adamw_small")
    for k, vals in zip(SMALL_SHARDED, zip(*[_unpack_small_shard(r[0]) for r in sm])):
        out[k] = list(vals)
    rp = adamw(*[_pack([d[k] for k in REPL], repl_rows)[None] for d in (p, pm, pv)],
               recv["repl"][:, None], "adamw_repl")
    for k, vals in zip(REPL, zip(*[_unpack(r[0], repl_shapes) for r in rp])):
        out[k] = list(vals)
    res = [loss, grad_x[None]]
    for q in range(4):
        res += [out[k][q] for k in WEIGHTS]
    return tuple(res)
```

```python
import functools
import math

import jax
import jax.numpy as jnp
from jax import lax
from jax.experimental import pallas as pl
from jax.experimental.pallas import tpu as pltpu

F32 = jnp.float32
BF16 = jnp.bfloat16

D_MODEL = 2048
DEPTH = 4
N_MIXERS = 3
EPS = 1e-6
SB_HEADS = 16
HEAD_DIM = 128
GM_CHUNK = 128
GM_GROUPS = 16
GM_STEP_CHUNKS = 4
SSD_INNER = 4096
SSD_HEADS = 64
SSD_HEAD_DIM = 64
SSD_GROUPS = 8
SSD_HPG = 8
SSD_STATE = 128
SSD_CHUNK = 128
SSD_CONV = 4
SSD_BC = SSD_GROUPS * SSD_STATE
SSD_CONV_DIM = SSD_INNER + 2 * SSD_BC
SSD_PROJ = SSD_INNER + SSD_CONV_DIM + SSD_HEADS
DT_PAD = 128
N_DEV = 8
LANES = 128

ADAM_LR = 0.001
ADAM_B1 = 0.9
ADAM_B2 = 0.999
ADAM_EPS = 1e-08
ADAM_WD = 0.01
ADAM_STEP = 10

VMEM_LIMIT_BYTES = 56 * 1024 * 1024
MESH = pl.DeviceIdType.MESH
HBM_SPEC = pl.BlockSpec(memory_space=pl.ANY)


def _cparams(*sem):
    return pltpu.CompilerParams(dimension_semantics=sem, vmem_limit_bytes=VMEM_LIMIT_BYTES)


def _tile(dim, pref):
    t = min(dim, pref)
    assert dim % t == 0, (dim, pref)
    return t


_DOT_DIMS = {"nn": ((1,), (0,)), "nt": ((1,), (1,)), "tn": ((0,), (0,))}


def matmul(a, b, form, name, *, b_layer=None, out_dtypes=(F32,), epilogue=None, extras=(),
           tm=1024, tn=1024, tk=2048):
    bs = b.shape[-2:]
    if form == "nn":
        (m, k), (k2, n) = a.shape, bs
    elif form == "nt":
        (m, k), (n, k2) = a.shape, bs
    else:
        (k, m), (k2, n) = a.shape, bs
    assert k == k2, (a.shape, b.shape, form)
    tm, tn, tk = _tile(m, tm), _tile(n, tn), _tile(k, tk)
    grid = (m // tm, n // tn, k // tk)
    nk = grid[2]
    if form == "tn":
        a_spec = pl.BlockSpec((tk, tm), lambda i, j, kk: (kk, i))
    else:
        a_spec = pl.BlockSpec((tm, tk), lambda i, j, kk: (i, kk))
    if form == "nt":
        b_blk, b_idx = (tn, tk), (lambda i, j, kk: (j, kk))
    else:
        b_blk, b_idx = (tk, tn), (lambda i, j, kk: (kk, j))
    if b.ndim == 3:
        b_spec = pl.BlockSpec((None,) + b_blk, lambda i, j, kk: (b_layer,) + b_idx(i, j, kk))
    else:
        b_spec = pl.BlockSpec(b_blk, b_idx)
    o_spec = pl.BlockSpec((tm, tn), lambda i, j, kk: (i, j))
    n_extra, n_out = len(extras), len(out_dtypes)
    dims = (_DOT_DIMS[form], ((), ()))

    def body(a_ref, b_ref, *rest):
        extra_refs, out_refs = rest[:n_extra], rest[n_extra:n_extra + n_out]
        kk = pl.program_id(2)

        def product():
            return lax.dot_general(a_ref[...], b_ref[...], dims, preferred_element_type=F32)

        def finish(acc):
            outs = (acc,) if epilogue is None else epilogue(acc, *[r[...] for r in extra_refs])
            for r, o in zip(out_refs, outs):
                r[...] = o.astype(r.dtype)

        if nk == 1:
            finish(product())
            return
        acc_ref = rest[-1]

        @pl.when(kk == 0)
        def _():
            acc_ref[...] = product()

        @pl.when(jnp.logical_and(kk > 0, kk < nk - 1))
        def _():
            acc_ref[...] += product()

        @pl.when(kk == nk - 1)
        def _():
            finish(acc_ref[...] + product())

    res = pl.pallas_call(
        body, name=name, grid=grid,
        in_specs=[a_spec, b_spec] + [o_spec] * n_extra,
        out_specs=[o_spec] * n_out,
        out_shape=[jax.ShapeDtypeStruct((m, n), dt) for dt in out_dtypes],
        scratch_shapes=[pltpu.VMEM((tm, tn), F32)] if nk > 1 else [],
        compiler_params=_cparams("parallel", "parallel", "arbitrary"),
    )(a, b, *extras)
    return res[0] if n_out == 1 else res


class Data:
    def __init__(self, array, block, imap):
        self.array, self.block, self.imap = array, block, imap


class Param:
    def __init__(self, array, block, imap, group):
        self.array, self.block, self.imap, self.group = array, block, imap, group


class Out:
    def __init__(self, shape, dtype, block, imap):
        self.shape, self.dtype, self.block, self.imap = shape, dtype, block, imap


def _f32(v):
    return v.astype(F32)


def blockwise_fwd(f, grid, data, params, outs, name):
    nd, npar = len(data), len(params)

    def body(*refs):
        vals = [_f32(r[...]) for r in refs[:nd + npar]]
        res = f(*vals)
        for r, o in zip(refs[nd + npar:], res):
            r[...] = o.astype(r.dtype)

    in_specs = [pl.BlockSpec(d.block, d.imap) for d in data]
    in_specs += [pl.BlockSpec(p.block, (lambda o, i, _m=p.imap: _m(o))) for p in params]
    res = pl.pallas_call(
        body, name=name, grid=grid, in_specs=in_specs,
        out_specs=[pl.BlockSpec(o.block, o.imap) for o in outs],
        out_shape=[jax.ShapeDtypeStruct(o.shape, o.dtype) for o in outs],
        compiler_params=_cparams("parallel", "parallel"),
    )(*[d.array for d in data], *[p.array for p in params])
    return res


def blockwise_bwd(f, grid, data, params, outs, cts, name, *, grad_dtypes, accum=None, also_bf16=()):
    nd, npar, no = len(data), len(params), len(outs)
    accum = accum or {}
    want = [k for k in range(nd) if grad_dtypes[k] is not None] + list(also_bf16)
    want_dtypes = [grad_dtypes[k] for k in want[:len(want) - len(also_bf16)]] + [BF16] * len(also_bf16)
    acc_keys = sorted(accum)

    def body(*refs):
        in_refs = refs[:nd + npar]
        ct_refs = refs[nd + npar:nd + npar + no]
        acc_refs = refs[nd + npar + no:nd + npar + no + len(acc_keys)]
        out_refs = refs[nd + npar + no + len(acc_keys):]
        dd_refs, dp_refs = out_refs[:len(want)], out_refs[len(want):]
        o, i = pl.program_id(0), pl.program_id(1)
        vals = [_f32(r[...]) for r in in_refs]
        _, vjp = jax.vjp(f, *vals)
        grads = vjp(tuple(_f32(r[...]) for r in ct_refs))
        for r, k in zip(dd_refs, want):
            g = grads[k]
            if k in accum:
                g = g + _f32(acc_refs[acc_keys.index(k)][...])
            r[...] = g.astype(r.dtype)
        for r, p, g in zip(dp_refs, params, grads[nd:]):
            first = jnp.logical_and(i == 0, o % p.group == 0)

            @pl.when(first)
            def _():
                r[...] = g.astype(r.dtype)

            @pl.when(jnp.logical_not(first))
            def _():
                r[...] += g.astype(r.dtype)

    d_specs = [pl.BlockSpec(d.block, d.imap) for d in data]
    p_specs = [pl.BlockSpec(p.block, (lambda o, i, _m=p.imap: _m(o))) for p in params]
    ct_specs = [pl.BlockSpec(o.block, o.imap) for o in outs]
    acc_specs = [d_specs[k] for k in acc_keys]
    res = pl.pallas_call(
        body, name=name, grid=grid,
        in_specs=d_specs + p_specs + ct_specs + acc_specs,
        out_specs=[d_specs[k] for k in want] + p_specs,
        out_shape=[jax.ShapeDtypeStruct(data[k].array.shape, dt) for k, dt in zip(want, want_dtypes)]
        + [jax.ShapeDtypeStruct(p.array.shape, F32) for p in params],
        compiler_params=_cparams("arbitrary", "arbitrary"),
    )(*[d.array for d in data], *[p.array for p in params], *cts, *[accum[k] for k in acc_keys])
    return list(res[:len(want)]), list(res[len(want):])


def _rmsnorm_f(x, g):
    return (x * lax.rsqrt(jnp.mean(x * x, axis=-1, keepdims=True) + EPS) * g,)


def _gelu(x):
    return 0.5 * x * (1.0 + lax.erf(x * (2.0 ** -0.5)))


def _silu(x):
    return x / (1.0 + jnp.exp(-x))


def _softplus(x):
    return jnp.maximum(x, 0.0) + jnp.log1p(jnp.exp(-jnp.abs(x)))


def _gm_act_f(pu, pv, gv):
    return (_gelu(pu),) + _rmsnorm_f(_gelu(pv), gv)


def _gm_gate_f(u, vn, w, b):
    n = w.shape[0]
    c = u.shape[0] // n
    row = lax.broadcasted_iota(jnp.int32, (n, n), 0)
    col = lax.broadcasted_iota(jnp.int32, (n, n), 1)
    wm = jnp.broadcast_to(jnp.where(row >= col, w, 0.0), (c, n, n))
    mixed = lax.dot_general(wm, vn.reshape(c, n, LANES), (((2,), (1,)), ((0,), (0,))),
                            preferred_element_type=F32)
    return (u * (mixed + b[None]).reshape(c * n, LANES),)


def _ssd_gate_f(y, z, g):
    return _rmsnorm_f(y * _silu(z), g)


def _dt_f(raw, bias):
    return (_softplus(raw + bias),)


def _row_tile(t):
    return _tile(t, 256)


def _norm_specs(h, g_row):
    t, d = h.shape
    tr = _row_tile(t)
    data = [Data(h, (tr, d), lambda o, i: (i, 0))]
    params = [Param(g_row, (1, d), lambda o: (0, 0), 1)]
    outs = [Out((t, d), BF16, (tr, d), lambda o, i: (i, 0))]
    return (1, t // tr), data, params, outs


def rmsnorm_fwd(h, g_row, name):
    grid, data, params, outs = _norm_specs(h, g_row)
    return blockwise_fwd(_rmsnorm_f, grid, data, params, outs, name)[0]


def rmsnorm_bwd(h, g_row, ct, skip, name, with_bf16=True):
    grid, data, params, outs = _norm_specs(h, g_row)
    dd, (dg,) = blockwise_bwd(_rmsnorm_f, grid, data, params, outs, [ct], name, grad_dtypes=[F32],
                              accum={0: skip}, also_bf16=(0,) if with_bf16 else ())
    return dd[0], (dd[1] if with_bf16 else None), dg


def loss_call(y, target):
    t, d = y.shape
    tr = _row_tile(t)

    def body(y_ref, t_ref, dy_ref, dyb_ref, loss_ref):
        e = y_ref[...] - t_ref[...]
        dy_ref[...] = e * (1.0 / d)
        dyb_ref[...] = (e * (1.0 / d)).astype(BF16)

        @pl.when(pl.program_id(0) == 0)
        def _():
            loss_ref[...] = jnp.zeros_like(loss_ref)

        loss_ref[...] += jnp.sum(e * e).reshape(1, 1) * (0.5 / d)

    row = pl.BlockSpec((tr, d), lambda i: (i, 0))
    return pl.pallas_call(
        body, name="loss", grid=(t // tr,), in_specs=[row, row],
        out_specs=[row, row, pl.BlockSpec((1, 1), lambda i: (0, 0))],
        out_shape=[jax.ShapeDtypeStruct((t, d), F32), jax.ShapeDtypeStruct((t, d), BF16),
                   jax.ShapeDtypeStruct((1, 1), F32)],
        compiler_params=_cparams("arbitrary"),
    )(y, target)


ATT_TQ = 1024
ATT_KB = 256
ATT_DIAG = ATT_TQ // ATT_KB


def _split2(x):
    hi = x.astype(BF16)
    return hi, (x - hi.astype(F32)).astype(BF16)


def _dot2(hi, lo, ones_bf):
    return (jnp.dot(hi, ones_bf, preferred_element_type=F32)
            + jnp.dot(lo, ones_bf, preferred_element_type=F32))


def _att_consts():
    r_ = lax.broadcasted_iota(jnp.int32, (ATT_KB, ATT_KB), 0)
    c_ = lax.broadcasted_iota(jnp.int32, (ATT_KB, ATT_KB), 1)
    upper = (r_ > c_).astype(BF16)
    lower = (r_ < c_).astype(BF16)
    return upper, lower


def _att_logits(q, k, scale, mask):
    z = lax.dot_general(q, k, (((1,), (1,)), ((), ())), preferred_element_type=F32) * scale
    lb = jnp.minimum(z, 0.0) - jnp.log(1.0 + jnp.exp(-jnp.abs(z)))
    l1 = lb - z
    if mask is not None:
        l1 = jnp.where(mask, l1, 0.0)
    return lb, l1


def _live_rows(d):
    if d is None:
        return pl.ds(0, ATT_TQ), None
    first, n = d * ATT_KB, ATT_TQ - d * ATT_KB
    row = first + lax.broadcasted_iota(jnp.int32, (n, ATT_KB), 0)
    col = first + lax.broadcasted_iota(jnp.int32, (n, ATT_KB), 1)
    return pl.ds(first, n), col < row


def _wide(c):
    return jnp.concatenate([c] * (ATT_KB // LANES), axis=1)


def attn_fwd(qk, qkv_bf, name, ex=None):
    t = qk.shape[0]
    tq = ATT_TQ
    assert t % tq == 0
    scale = HEAD_DIM ** -0.5

    def body(q_ref, k_ref, v_ref, o_ref, acc_ref, c_ref):
        qi = pl.program_id(1)
        upper, _ = _att_consts()
        acc_ref[...] = jnp.zeros_like(acc_ref)
        c_ref[...] = jnp.zeros_like(c_ref)

        def block(jb, diag):
            off = pl.multiple_of(jb * ATT_KB, ATT_KB)
            k = k_ref[pl.ds(off, ATT_KB), :]
            v = v_ref[pl.ds(off, ATT_KB), :]
            rows, mask = _live_rows(diag)
            lb, l1 = _att_logits(q_ref[rows, :], k, scale, mask)
            hi, lo = _split2(l1)
            c = c_ref[rows, :]
            a = jnp.exp(lb + _wide(c) + _dot2(hi, lo, upper))
            if mask is not None:
                a = jnp.where(mask, a, 0.0)
            acc_ref[rows, :] += jnp.dot(a.astype(BF16), v, preferred_element_type=F32)
            c_ref[rows, :] = c + jnp.sum(l1, axis=1, keepdims=True)

        for d in reversed(range(ATT_DIAG)):
            block(qi * ATT_DIAG + d, d)

        def step(s, carry):
            block(qi * ATT_DIAG - 1 - s, None)
            return carry

        lax.fori_loop(0, qi * ATT_DIAG, step, 0)
        o_ref[...] = acc_ref[...].astype(o_ref.dtype)

    (o,), bufs = comm_call(
        body, name=name, grid=(SB_HEADS, t // tq),
        in_specs=[pl.BlockSpec((tq, HEAD_DIM), lambda h, i: (i, h)),
                  pl.BlockSpec((t, HEAD_DIM), lambda h, i: (0, SB_HEADS + h)),
                  pl.BlockSpec((t, HEAD_DIM), lambda h, i: (0, 2 * SB_HEADS + h))],
        out_specs=[pl.BlockSpec((tq, HEAD_DIM), lambda h, i: (i, h))],
        out_shape=[jax.ShapeDtypeStruct((t, SB_HEADS * HEAD_DIM), BF16)],
        scratch_shapes=[pltpu.VMEM((tq, HEAD_DIM), F32), pltpu.VMEM((tq, LANES), F32)],
        inputs=[qk, qk, qkv_bf], sem=("parallel", "arbitrary"), ex=ex)
    return o, bufs


def attn_bwd(qk, qkv_bf, do, name, ex=None):
    t = qk.shape[0]
    tq = ATT_TQ
    assert t % tq == 0
    nq = t // tq
    nkb = t // ATT_KB
    scale = HEAD_DIM ** -0.5
    tn_dims = (((0,), (0,)), ((), ()))

    def body(q_ref, k_ref, v_ref, do_ref, dq_ref, dk_ref, dv_ref,
             cbuf, c_ref, ce_ref, dq_acc, dk_acc, dv_acc):
        qi = pl.program_id(1)
        upper, lower = _att_consts()

        @pl.when(qi == 0)
        def _():
            dk_acc[...] = jnp.zeros_like(dk_acc)
            dv_acc[...] = jnp.zeros_like(dv_acc)

        c_ref[...] = jnp.zeros_like(c_ref)
        ce_ref[...] = jnp.zeros_like(ce_ref)
        dq_acc[...] = jnp.zeros_like(dq_acc)

        def pass1(jb, diag):
            off = pl.multiple_of(jb * ATT_KB, ATT_KB)
            rows, mask = _live_rows(diag)
            _, l1 = _att_logits(q_ref[rows, :], k_ref[pl.ds(off, ATT_KB), :], scale, mask)
            c = c_ref[rows, :]
            cbuf[jb, rows, :] = c
            c_ref[rows, :] = c + jnp.sum(l1, axis=1, keepdims=True)

        for d in reversed(range(ATT_DIAG)):
            pass1(qi * ATT_DIAG + d, d)

        def step1(s, carry):
            pass1(qi * ATT_DIAG - 1 - s, None)
            return carry

        lax.fori_loop(0, qi * ATT_DIAG, step1, 0)

        def pass2(jb, diag):
            off = pl.multiple_of(jb * ATT_KB, ATT_KB)
            k = k_ref[pl.ds(off, ATT_KB), :]
            v = v_ref[pl.ds(off, ATT_KB), :]
            rows, mask = _live_rows(diag)
            q, do_ = q_ref[rows, :], do_ref[rows, :]
            lb, l1 = _att_logits(q, k, scale, mask)
            hi, lo = _split2(l1)
            a = jnp.exp(lb + _wide(cbuf[jb, rows, :]) + _dot2(hi, lo, upper))
            if mask is not None:
                a = jnp.where(mask, a, 0.0)
            da = lax.dot_general(do_, v, (((1,), (1,)), ((), ())), preferred_element_type=F32)
            e = da * a
            ehi, elo = _split2(e)
            ce = ce_ref[rows, :]
            big_e = _wide(ce) + _dot2(ehi, elo, lower)
            dz = (e - (e + big_e) * jnp.exp(lb)) * scale
            if mask is not None:
                dz = jnp.where(mask, dz, 0.0)
            dz = dz.astype(BF16)
            dq_acc[rows, :] += jnp.dot(dz, k, preferred_element_type=F32)
            ce_ref[rows, :] = ce + jnp.sum(e, axis=1, keepdims=True)
            dk_acc[pl.ds(off, ATT_KB), :] += lax.dot_general(dz, q, tn_dims, preferred_element_type=F32)
            dv_acc[pl.ds(off, ATT_KB), :] += lax.dot_general(a.astype(BF16), do_, tn_dims,
                                                             preferred_element_type=F32)

        def step2(jb, carry):
            pass2(jb, None)
            return carry

        lax.fori_loop(0, qi * ATT_DIAG, step2, 0)
        for d in range(ATT_DIAG):
            pass2(qi * ATT_DIAG + d, d)
        dq_ref[...] = dq_acc[...]

        @pl.when(qi == nq - 1)
        def _():
            dk_ref[...] = dk_acc[...]
            dv_ref[...] = dv_acc[...].astype(dv_ref.dtype)

    hd = SB_HEADS * HEAD_DIM
    (dq, dk, dv), bufs = comm_call(
        body, name=name, grid=(SB_HEADS, nq),
        in_specs=[pl.BlockSpec((tq, HEAD_DIM), lambda h, i: (i, h)),
                  pl.BlockSpec((t, HEAD_DIM), lambda h, i: (0, SB_HEADS + h)),
                  pl.BlockSpec((t, HEAD_DIM), lambda h, i: (0, 2 * SB_HEADS + h)),
                  pl.BlockSpec((tq, HEAD_DIM), lambda h, i: (i, h))],
        out_specs=[pl.BlockSpec((tq, HEAD_DIM), lambda h, i: (i, h)),
                   pl.BlockSpec((t, HEAD_DIM), lambda h, i: (0, h)),
                   pl.BlockSpec((t, HEAD_DIM), lambda h, i: (0, h))],
        out_shape=[jax.ShapeDtypeStruct((t, hd), F32), jax.ShapeDtypeStruct((t, hd), F32),
                   jax.ShapeDtypeStruct((t, hd), BF16)],
        scratch_shapes=[pltpu.VMEM((nkb, tq, LANES), F32), pltpu.VMEM((tq, LANES), F32),
                        pltpu.VMEM((tq, LANES), F32), pltpu.VMEM((tq, HEAD_DIM), F32),
                        pltpu.VMEM((t, HEAD_DIM), F32), pltpu.VMEM((t, HEAD_DIM), F32)],
        inputs=[qk, qk, qkv_bf, do], sem=("arbitrary", "arbitrary"), ex=ex)
    return jnp.concatenate([dq, dk], axis=1), dv, bufs


def _qknorm_specs(qkv, gqk):
    t = qkv.shape[0]
    tr = _tile(t, 1024)
    data = [Data(qkv, (tr, HEAD_DIM), lambda o, i: (i, o))]
    params = [Param(gqk, (None, 1, HEAD_DIM), lambda o: (o // SB_HEADS, 0, 0), SB_HEADS)]
    outs = [Out((t, 2 * SB_HEADS * HEAD_DIM), BF16, (tr, HEAD_DIM), lambda o, i: (i, o))]
    return (2 * SB_HEADS, t // tr), data, params, outs


CONV_COLS = 128


def _shift_down(x, s, rows):
    if s == 0:
        return x
    return jnp.where(rows < s, 0.0, pltpu.roll(x, s, 0))


def _shift_up(x, s, rows):
    if s == 0:
        return x
    n = x.shape[0]
    return jnp.where(rows >= n - s, 0.0, pltpu.roll(x, n - s, 0))


def conv_fwd(xbc, w, b, name):
    t, c = xbc.shape
    tc = _tile(c, CONV_COLS)

    def body(x_ref, w_ref, b_ref, o_ref):
        x = x_ref[...]
        rows = lax.broadcasted_iota(jnp.int32, x.shape, 0)
        y = jnp.broadcast_to(b_ref[...], x.shape)
        for kk in range(SSD_CONV):
            y = y + w_ref[kk:kk + 1, :] * _shift_down(x, SSD_CONV - 1 - kk, rows)
        o_ref[...] = _silu(y)

    col = pl.BlockSpec((t, tc), lambda j: (0, j))
    return pl.pallas_call(
        body, name=name, grid=(c // tc,),
        in_specs=[col, pl.BlockSpec((SSD_CONV, tc), lambda j: (0, j)), pl.BlockSpec((1, tc), lambda j: (0, j))],
        out_specs=col, out_shape=jax.ShapeDtypeStruct((t, c), F32),
        compiler_params=_cparams("parallel"),
    )(xbc, w, b)


def conv_bwd(xbc, w, b, dact, name):
    t, c = xbc.shape
    tc = _tile(c, CONV_COLS)

    def body(x_ref, w_ref, b_ref, g_ref, dx_ref, dw_ref, db_ref):
        x = x_ref[...]
        rows = lax.broadcasted_iota(jnp.int32, x.shape, 0)
        shifted = [_shift_down(x, SSD_CONV - 1 - kk, rows) for kk in range(SSD_CONV)]
        y = jnp.broadcast_to(b_ref[...], x.shape)
        for kk in range(SSD_CONV):
            y = y + w_ref[kk:kk + 1, :] * shifted[kk]
        sig = 1.0 / (1.0 + jnp.exp(-y))
        dy = g_ref[...] * (sig * (1.0 + y * (1.0 - sig)))
        dx = jnp.zeros_like(x)
        for kk in range(SSD_CONV):
            dx = dx + w_ref[kk:kk + 1, :] * _shift_up(dy, SSD_CONV - 1 - kk, rows)
            dw_ref[kk:kk + 1, :] = jnp.sum(dy * shifted[kk], axis=0, keepdims=True)
        dx_ref[...] = dx.astype(dx_ref.dtype)
        db_ref[...] = jnp.sum(dy, axis=0, keepdims=True)

    col = pl.BlockSpec((t, tc), lambda j: (0, j))
    wspec = pl.BlockSpec((SSD_CONV, tc), lambda j: (0, j))
    bspec = pl.BlockSpec((1, tc), lambda j: (0, j))
    return pl.pallas_call(
        body, name=name, grid=(c // tc,),
        in_specs=[col, wspec, bspec, col], out_specs=[col, wspec, bspec],
        out_shape=[jax.ShapeDtypeStruct((t, c), BF16), jax.ShapeDtypeStruct((SSD_CONV, c), F32),
                   jax.ShapeDtypeStruct((1, c), F32)],
        compiler_params=_cparams("parallel"),
    )(xbc, w, b, dact)


@functools.partial(jax.custom_vjp, nondiff_argnums=(2,))
def _bdot(a, b, dims):
    return lax.dot_general(a.astype(BF16), b.astype(BF16), (dims, ((), ())), preferred_element_type=F32)


def _bdot_fwd(a, b, dims):
    return _bdot(a, b, dims), (a, b)


def _bdot_bwd(dims, res, g):
    a, b = res
    (ca,), (cb,) = dims
    fa, fb = 1 - ca, 1 - cb
    gb, ab, bb = g.astype(BF16), a.astype(BF16), b.astype(BF16)

    def dn(u, v, cu, cv):
        return lax.dot_general(u, v, (((cu,), (cv,)), ((), ())), preferred_element_type=F32)

    da = dn(gb, bb, 1, fb) if ca == 1 else dn(bb, gb, fb, 1)
    db = dn(ab, gb, fa, 0) if cb == 0 else dn(gb, ab, 0, fa)
    return da, db


_bdot.defvjp(_bdot_fwd, _bdot_bwd)
_NN, _NT, _TN = ((1,), (0,)), ((1,), (1,)), ((0,), (0,))


def _split3_dot(m_bf, x, dims):
    out = None
    for _ in range(3):
        t = x.astype(BF16)
        part = lax.dot_general(m_bf, t, (dims, ((), ())), preferred_element_type=F32)
        out = part if out is None else out + part
        x = x - t.astype(F32)
    return out


@jax.custom_vjp
def _tri_dot(m_bf, x):
    return _split3_dot(m_bf, x, _NN)


def _tri_dot_fwd(m_bf, x):
    return _tri_dot(m_bf, x), m_bf


def _tri_dot_bwd(m_bf, g):
    return jnp.zeros_like(m_bf), _split3_dot(m_bf, g, _TN)


_tri_dot.defvjp(_tri_dot_fwd, _tri_dot_bwd)


def _ssd_chunk(x, dt, bm, cm, sprev, alog, dskip):
    n = x.shape[0]
    a = -jnp.exp(alog)
    da = dt * a
    row = lax.broadcasted_iota(jnp.int32, (n, n), 0)
    col = lax.broadcasted_iota(jnp.int32, (n, n), 1)
    causal = row >= col
    acum_l = _tri_dot(causal.astype(BF16), jnp.broadcast_to(da, (n, n)))
    acum_s = acum_l.T
    decay = jnp.exp(jnp.where(causal, acum_l - acum_s, -1e30))
    acum = jnp.sum(acum_l, axis=1, keepdims=True) * (1.0 / n)
    atot = jnp.sum(da, axis=0, keepdims=True)
    xs = x * dt
    y = _bdot(_bdot(cm, bm, _NT) * decay, xs, _NN)
    y = y + jnp.exp(acum) * _bdot(cm, sprev, _NT)
    y = y + dskip * x
    snew = jnp.exp(atot) * sprev + _bdot(xs * jnp.exp(atot - acum), bm, _TN)
    return y, snew


def _ssd_specs(t, rev):
    nc = t // SSD_CHUNK
    cidx = (lambda c: nc - 1 - c) if rev else (lambda c: c)
    x_spec = pl.BlockSpec((SSD_HPG, SSD_CHUNK, SSD_HEAD_DIM), lambda g, c: (g, cidx(c), 0))
    dt_spec = pl.BlockSpec((SSD_HPG, SSD_CHUNK, 1), lambda g, c: (g, cidx(c), 0))
    bc_spec = pl.BlockSpec((None, SSD_CHUNK, SSD_STATE), lambda g, c: (g, cidx(c), 0))
    sc_spec = pl.BlockSpec((SSD_HPG, 1, 1), lambda g, c: (g, 0, 0))
    st_spec = pl.BlockSpec((SSD_HPG, None, SSD_HEAD_DIM, SSD_STATE), lambda g, c: (g, cidx(c), 0, 0))
    return nc, x_spec, dt_spec, bc_spec, sc_spec, st_spec


def ssd_scan_fwd(x, dt, bm, cm, alog, dskip, name, ex=None):
    t = x.shape[1]
    nc, x_spec, dt_spec, bc_spec, sc_spec, st_spec = _ssd_specs(t, False)

    def body(x_ref, dt_ref, b_ref, c_ref, al_ref, d_ref, y_ref, sin_ref, state):
        @pl.when(pl.program_id(1) == 0)
        def _():
            state[...] = jnp.zeros_like(state)

        bm_, cm_ = b_ref[...], c_ref[...]

        def head(r, carry):
            sprev = state[r]
            sin_ref[r] = sprev
            y, snew = _ssd_chunk(x_ref[r], dt_ref[r], bm_, cm_, sprev, al_ref[r], d_ref[r])
            y_ref[r] = y
            state[r] = snew
            return carry

        lax.fori_loop(0, SSD_HPG, head, 0, unroll=SSD_HPG)

    (y, sin), bufs = comm_call(
        body, name=name, grid=(SSD_GROUPS, nc),
        in_specs=[x_spec, dt_spec, bc_spec, bc_spec, sc_spec, sc_spec],
        out_specs=[x_spec, st_spec],
        out_shape=[jax.ShapeDtypeStruct(x.shape, F32),
                   jax.ShapeDtypeStruct((SSD_HEADS, nc, SSD_HEAD_DIM, SSD_STATE), F32)],
        scratch_shapes=[pltpu.VMEM((SSD_HPG, SSD_HEAD_DIM, SSD_STATE), F32)],
        inputs=[x, dt, bm, cm, alog, dskip], sem=("parallel", "arbitrary"), ex=ex)
    return y, sin, bufs


def ssd_scan_bwd(x, dt, bm, cm, alog, dskip, sin, dy, name, ex=None):
    t = x.shape[1]
    nc, x_spec, dt_spec, bc_spec, sc_spec, st_spec = _ssd_specs(t, True)

    def body(x_ref, dt_ref, b_ref, c_ref, al_ref, d_ref, sin_ref, dy_ref,
             dx_ref, ddt_ref, db_ref, dc_ref, dal_ref, dd_ref, dstate):
        @pl.when(pl.program_id(1) == 0)
        def _():
            dstate[...] = jnp.zeros_like(dstate)
            dal_ref[...] = jnp.zeros_like(dal_ref)
            dd_ref[...] = jnp.zeros_like(dd_ref)

        db_ref[...] = jnp.zeros_like(db_ref)
        dc_ref[...] = jnp.zeros_like(dc_ref)
        bm_, cm_ = b_ref[...], c_ref[...]

        def head(r, carry):
            _, vjp = jax.vjp(_ssd_chunk, x_ref[r], dt_ref[r], bm_, cm_, sin_ref[r], al_ref[r], d_ref[r])
            dx, ddt, dbm, dcm, dsp, dal, dd = vjp((dy_ref[r], dstate[r]))
            dx_ref[r] = dx
            ddt_ref[r] = ddt
            db_ref[...] += dbm
            dc_ref[...] += dcm
            dstate[r] = dsp
            dal_ref[r] += dal
            dd_ref[r] += dd
            return carry

        lax.fori_loop(0, SSD_HPG, head, 0, unroll=SSD_HPG)

    return comm_call(
        body, name=name, grid=(SSD_GROUPS, nc),
        in_specs=[x_spec, dt_spec, bc_spec, bc_spec, sc_spec, sc_spec, st_spec, x_spec],
        out_specs=[x_spec, dt_spec, bc_spec, bc_spec, sc_spec, sc_spec],
        out_shape=[jax.ShapeDtypeStruct(x.shape, F32), jax.ShapeDtypeStruct(dt.shape, F32),
                   jax.ShapeDtypeStruct(bm.shape, F32), jax.ShapeDtypeStruct(cm.shape, F32),
                   jax.ShapeDtypeStruct(alog.shape, F32), jax.ShapeDtypeStruct(dskip.shape, F32)],
        scratch_shapes=[pltpu.VMEM((SSD_HPG, SSD_HEAD_DIM, SSD_STATE), F32)],
        inputs=[x, dt, bm, cm, alog, dskip, sin, dy], sem=("arbitrary", "arbitrary"), ex=ex)


def _to_heads(a, n, w):
    t = a.shape[0]
    return a.reshape(t, n, w).transpose(1, 0, 2)


def _from_heads(a):
    n, t, w = a.shape
    return a.transpose(1, 0, 2).reshape(t, n * w)


def _add(acc, h):
    return (acc + h,)


def sb_fwd(h, hn, w, j, tag, ex=None):
    gqk = w["sb_gqk"][j]
    qkv, qkv_bf = matmul(hn, w["sb_w_qkv"], "nn", f"{tag}_qkv", b_layer=j, out_dtypes=(F32, BF16),
                         epilogue=lambda acc: (acc, acc))
    grid, data, params, outs = _qknorm_specs(qkv, gqk)
    qk = blockwise_fwd(_rmsnorm_f, grid, data, params, outs, f"{tag}_qknorm")[0]
    o, bufs = attn_fwd(qk, qkv_bf, f"{tag}_attn", ex)
    w.update(bufs)
    h1 = matmul(o, w["sb_w_o"], "nn", f"{tag}_out", b_layer=j, epilogue=_add, extras=(h,))
    return h1, (qkv, qkv_bf, qk, o)


def sb_bwd(g_bf, hn, saved, w, j, tag, host):
    qkv, qkv_bf, qk, o = saved
    wqkv, wo, gqk = w["sb_w_qkv"], w["sb_w_o"], w["sb_gqk"][j]
    d_wo = matmul(o, g_bf, "tn", f"{tag}_dwo", out_dtypes=(BF16,))
    do = matmul(g_bf, wo, "nt", f"{tag}_do", b_layer=j, out_dtypes=(BF16,))
    dqk, dv, bufs = attn_bwd(qk, qkv_bf, do, f"{tag}_attn_bwd", host([("sb_w_o", j, d_wo)]))
    grid, data, params, outs = _qknorm_specs(qkv, gqk)
    (draw,), (dgqk,) = blockwise_bwd(_rmsnorm_f, grid, data, params, outs, [dqk], f"{tag}_qknorm_bwd",
                                     grad_dtypes=[BF16])
    dqkv = jnp.concatenate([draw[:, :2 * SB_HEADS * HEAD_DIM], dv], axis=1)
    d_wqkv = matmul(hn, dqkv, "tn", f"{tag}_dwqkv", out_dtypes=(BF16,))
    dhn = matmul(dqkv, wqkv, "nt", f"{tag}_dhn", b_layer=j)
    return dhn, {"sb_w_qkv": d_wqkv, "sb_gqk": dgqk}, bufs


def _gm_act_specs(p, gv):
    t = p.shape[0]
    tr = _row_tile(t)
    d = D_MODEL
    data = [Data(p, (tr, d), lambda o, i: (i, 0)), Data(p, (tr, d), lambda o, i: (i, 1))]
    params = [Param(gv, (1, d), lambda o: (0, 0), 1)]
    outs = [Out((t, d), F32, (tr, d), lambda o, i: (i, 0)), Out((t, d), BF16, (tr, d), lambda o, i: (i, 0))]
    return (1, t // tr), data, params, outs


def _gm_gate_specs(u, vn, ws, bs):
    t = u.shape[0]
    rows = _tile(t, GM_STEP_CHUNKS * GM_CHUNK)
    blk = (rows, LANES)
    data = [Data(u, blk, lambda o, i: (i, o)), Data(vn, blk, lambda o, i: (i, o))]
    params = [Param(ws, (None, GM_CHUNK, GM_CHUNK), lambda o: (o, 0, 0), 1),
              Param(bs, (None, GM_CHUNK, 1), lambda o: (o, 0, 0), 1)]
    outs = [Out((t, D_MODEL), BF16, blk, lambda o, i: (i, o))]
    return (GM_GROUPS, t // rows), data, params, outs


def gm_fwd(h, hn, w, j, tag):
    p = matmul(hn, w["gm_w_in"], "nn", f"{tag}_in", b_layer=j)
    grid, data, params, outs = _gm_act_specs(p, w["gm_v_norm_g"])
    u, vn = blockwise_fwd(_gm_act_f, grid, data, params, outs, f"{tag}_act")
    grid, data, params, outs = _gm_gate_specs(u, vn, w["gm_w_s"], w["gm_b_s"])
    y = blockwise_fwd(_gm_gate_f, grid, data, params, outs, f"{tag}_gate")[0]
    h1 = matmul(y, w["gm_w_o"], "nn", f"{tag}_out", b_layer=j, epilogue=_add, extras=(h,))
    return h1, (p, u, vn, y)


def gm_bwd(g_bf, hn, saved, w, j, tag):
    p, u, vn, y = saved
    d_wo = matmul(y, g_bf, "tn", f"{tag}_dwo", out_dtypes=(BF16,))
    dy = matmul(g_bf, w["gm_w_o"], "nt", f"{tag}_dy", b_layer=j)
    grid, data, params, outs = _gm_gate_specs(u, vn, w["gm_w_s"], w["gm_b_s"])
    (du, dvn), (dws, dbs) = blockwise_bwd(_gm_gate_f, grid, data, params, outs, [dy], f"{tag}_gate_bwd",
                                          grad_dtypes=[F32, F32])
    grid, data, params, outs = _gm_act_specs(p, w["gm_v_norm_g"])
    (dpu, dpv), (dgv,) = blockwise_bwd(_gm_act_f, grid, data, params, outs, [du, dvn], f"{tag}_act_bwd",
                                       grad_dtypes=[BF16, BF16])
    dp = jnp.concatenate([dpu[:, :D_MODEL], dpv[:, D_MODEL:]], axis=1)
    d_win = matmul(hn, dp, "tn", f"{tag}_dwin", out_dtypes=(BF16,))
    dhn = matmul(dp, w["gm_w_in"], "nt", f"{tag}_dhn", b_layer=j)
    return dhn, {"gm_w_in": d_win, "gm_w_o": d_wo, "gm_v_norm_g": dgv, "gm_w_s": dws, "gm_b_s": dbs}


def _ssd_gate_specs(y, z, g):
    t = y.shape[0]
    tr = _row_tile(t)
    gw = SSD_INNER // SSD_GROUPS
    blk = (tr, gw)
    data = [Data(y, blk, lambda o, i: (i, o)), Data(z, blk, lambda o, i: (i, o))]
    params = [Param(g, (None, 1, gw), lambda o: (o, 0, 0), 1)]
    outs = [Out((t, SSD_INNER), BF16, blk, lambda o, i: (i, o))]
    return (SSD_GROUPS, t // tr), data, params, outs


def _dt_specs(raw, bias):
    t = raw.shape[0]
    tr = _row_tile(t)
    data = [Data(raw, (tr, DT_PAD), lambda o, i: (i, 0))]
    params = [Param(bias, (1, DT_PAD), lambda o: (0, 0), 1)]
    outs = [Out((t, DT_PAD), F32, (tr, DT_PAD), lambda o, i: (i, 0))]
    return (1, t // tr), data, params, outs


def ssd_fwd(h, hn, w, tag, ex=None):
    z = matmul(hn, w["ssd_wz"], "nn", f"{tag}_inz")
    xbc = matmul(hn, w["ssd_wxbc"], "nn", f"{tag}_inx")
    raw = matmul(hn, w["ssd_wdt"], "nn", f"{tag}_indt")
    act = conv_fwd(xbc, w["ssd_conv_w"], w["ssd_conv_b"], f"{tag}_conv")
    grid, data, params, outs = _dt_specs(raw, w["ssd_dt_bias"])
    dt = blockwise_fwd(_dt_f, grid, data, params, outs, f"{tag}_dt")[0]
    xh = _to_heads(act[:, :SSD_INNER], SSD_HEADS, SSD_HEAD_DIM)
    bh = _to_heads(act[:, SSD_INNER:SSD_INNER + SSD_BC], SSD_GROUPS, SSD_STATE)
    ch = _to_heads(act[:, SSD_INNER + SSD_BC:], SSD_GROUPS, SSD_STATE)
    dth = _to_heads(dt[:, :SSD_HEADS], SSD_HEADS, 1)
    yh, sin, bufs = ssd_scan_fwd(xh, dth, bh, ch, w["ssd_a_log"], w["ssd_d"], f"{tag}_scan", ex)
    w.update(bufs)
    y = _from_heads(yh)
    grid, data, params, outs = _ssd_gate_specs(y, z, w["ssd_norm_g"])
    yn = blockwise_fwd(_ssd_gate_f, grid, data, params, outs, f"{tag}_gate")[0]
    h1 = matmul(yn, w["ssd_w_o"], "nn", f"{tag}_out", b_layer=0, epilogue=_add, extras=(h,))
    return h1, (z, xbc, raw, xh, bh, ch, dth, sin, y, yn)


def ssd_bwd(g_bf, hn, saved, w, tag, host):
    z, xbc, raw, xh, bh, ch, dth, sin, y, yn = saved
    t = z.shape[0]
    d_wo = matmul(yn, g_bf, "tn", f"{tag}_dwo", out_dtypes=(BF16,))
    dyn = matmul(g_bf, w["ssd_w_o"], "nt", f"{tag}_dyn", b_layer=0)
    grid, data, params, outs = _ssd_gate_specs(y, z, w["ssd_norm_g"])
    (dy, dz), (dng,) = blockwise_bwd(_ssd_gate_f, grid, data, params, outs, [dyn], f"{tag}_gate_bwd",
                                     grad_dtypes=[F32, BF16])
    (dxh, ddth, dbh, dch, dal, ddk), bufs = ssd_scan_bwd(
        xh, dth, bh, ch, w["ssd_a_log"], w["ssd_d"], sin, _to_heads(dy, SSD_HEADS, SSD_HEAD_DIM),
        f"{tag}_scan_bwd", host([("ssd_w_o", 0, d_wo)]))
    dact = jnp.concatenate([_from_heads(dxh), _from_heads(dbh), _from_heads(dch)], axis=1)
    dxbc, dcw, dcb = conv_bwd(xbc, w["ssd_conv_w"], w["ssd_conv_b"], dact, f"{tag}_conv_bwd")
    ddt = jnp.pad(_from_heads(ddth), ((0, 0), (0, DT_PAD - SSD_HEADS)))
    grid, data, params, outs = _dt_specs(raw, w["ssd_dt_bias"])
    (draw,), (dbias,) = blockwise_bwd(_dt_f, grid, data, params, outs, [ddt], f"{tag}_dt_bwd",
                                      grad_dtypes=[BF16])
    d_wz = matmul(hn, dz, "tn", f"{tag}_dwz", out_dtypes=(BF16,))
    d_wx = matmul(hn, dxbc, "tn", f"{tag}_dwx", out_dtypes=(BF16,))
    d_wdt = matmul(hn, draw, "tn", f"{tag}_dwdt", out_dtypes=(BF16,))
    dhn = matmul(dz, w["ssd_wz"], "nt", f"{tag}_dhn_z")
    dhn = matmul(dxbc, w["ssd_wxbc"], "nt", f"{tag}_dhn_x", epilogue=_add, extras=(dhn,))
    dhn = matmul(draw, w["ssd_wdt"], "nt", f"{tag}_dhn_dt", epilogue=_add, extras=(dhn,))
    d_win = jnp.concatenate([d_wz, d_wx, d_wdt[:, :SSD_HEADS]], axis=1)
    d_win = d_win.reshape(D_MODEL, N_DEV, SSD_PROJ // N_DEV).transpose(1, 0, 2)
    return dhn, {"ssd_w_in": d_win, "ssd_conv_w": dcw, "ssd_conv_b": dcb,
                 "ssd_dt_bias": dbias[:, :SSD_HEADS], "ssd_a_log": dal, "ssd_d": ddk, "ssd_norm_g": dng}, bufs


def _relu2(acc):
    return (acc, jnp.square(jnp.maximum(acc, 0.0)))


def _relu2_bwd(acc, a):
    return (acc * (2.0 * jnp.maximum(a, 0.0)),)


def device_step(x, target, w, shards, shard_shapes):
    def gather(group, phase, name):
        w.update(run_exchange(ag_exchange(AG_GROUPS[group], shards, w, phase), name))

    gather(0, "A", "gather0_ici")
    gather(0, "B", "gather0_d2d")
    h = x
    saved = []
    for i in range(DEPTH):
        kind, j = i % N_MIXERS, i // N_MIXERS
        tag = f"l{i}"
        hn = rmsnorm_fwd(h, w["norm_mix_g"][i], f"{tag}_norm_mix")
        if kind == 0:
            ex = ag_exchange(AG_GROUPS[1], shards, w, "A") if i == 0 else None
            h1, ms = sb_fwd(h, hn, w, j, f"{tag}_sb", ex)
        elif kind == 1:
            h1, ms = gm_fwd(h, hn, w, j, f"{tag}_gm")
        else:
            h1, ms = ssd_fwd(h, hn, w, f"{tag}_ssd", ag_exchange(AG_GROUPS[2], shards, w, "A"))
        hm = rmsnorm_fwd(h1, w["norm_mlp_g"][i], f"{tag}_norm_mlp")
        a, r = matmul(hm, w["mlp_w_in"], "nn", f"{tag}_mlp_in", b_layer=i, out_dtypes=(F32, BF16),
                      epilogue=_relu2)
        h2 = matmul(r, w["mlp_w_out"], "nn", f"{tag}_mlp_out", b_layer=i, epilogue=_add, extras=(h1,))
        saved.append((h, hn, ms, h1, hm, a, r))
        h = h2
        if i == 0:
            gather(1, "B", "gather1_d2d")
            _ssd_weights(w)
        elif i == 2:
            gather(2, "B", "gather2_d2d")
    dh, g_bf, loss = loss_call(h, target)

    recv, pending = {}, []
    small = {"norm_mix_g": [None] * DEPTH, "norm_mlp_g": [None] * DEPTH, "sb_gqk": [None] * 2}
    for i in reversed(range(DEPTH)):
        kind, j = i % N_MIXERS, i // N_MIXERS
        tag = f"l{i}"
        h0, hn, ms, h1, hm, a, r = saved[i]
        pending.append(("mlp_w_out", i, matmul(r, g_bf, "tn", f"{tag}_mlp_dwout", out_dtypes=(BF16,))))
        da = matmul(g_bf, w["mlp_w_out"], "nt", f"{tag}_mlp_da", b_layer=i, out_dtypes=(BF16,),
                    epilogue=_relu2_bwd, extras=(a,))
        pending.append(("mlp_w_in", i, matmul(hm, da, "tn", f"{tag}_mlp_dwin", out_dtypes=(BF16,))))
        dhm = matmul(da, w["mlp_w_in"], "nt", f"{tag}_mlp_dhm", b_layer=i)
        dh1, g_bf, small["norm_mlp_g"][i] = rmsnorm_bwd(h1, w["norm_mlp_g"][i], dhm, dh, f"{tag}_norm_mlp_bwd")
        if kind == 1:
            dhn, mg = gm_bwd(g_bf, hn, ms, w, j, f"{tag}_gm")
        else:
            def host(extra, _pending=tuple(pending)):
                return rs_exchange(list(_pending) + extra, recv, shard_shapes)

            pending = []
            if kind == 0:
                dhn, mg, bufs = sb_bwd(g_bf, hn, ms, w, j, f"{tag}_sb", host)
                small["sb_gqk"][j] = mg.pop("sb_gqk")
            else:
                dhn, mg, bufs = ssd_bwd(g_bf, hn, ms, w, f"{tag}_ssd", host)
            recv.update(bufs)
        for kname in list(mg):
            if kname in BIG:
                pending.append((kname, j, mg.pop(kname)))
        small.update(mg)
        dh, g_bf, small["norm_mix_g"][i] = rmsnorm_bwd(h0, w["norm_mix_g"][i], dhn, dh1, f"{tag}_norm_mix_bwd",
                                                       with_bf16=i > 0)
    dcw = small["ssd_conv_w"].reshape(SSD_CONV, N_DEV, SMALL_W).transpose(1, 0, 2)
    dcb = small["ssd_conv_b"].reshape(N_DEV, 1, SMALL_W)
    dng = jnp.pad(small["ssd_norm_g"], ((0, 0), (0, 0), (0, SMALL_W - SSD_INNER // N_DEV)))
    pending.append(("small", None, jnp.concatenate([dcw, dcb, dng, jnp.zeros((N_DEV, 2, SMALL_W), F32)], axis=1)))
    pending.append(("repl", None, _pack(_repl_grads(small), shard_shapes["repl"][0])))
    recv.update(run_exchange(rs_exchange(pending, recv, shard_shapes), "scatter_rest"))
    return loss, dh, recv


def _my_pos():
    return lax.axis_index("x"), lax.axis_index("y"), lax.axis_index("c")


def _lin(p):
    return 4 * p[0] + 2 * p[1] + p[2]


def _shard_view(ref, kind, width, idx, layer=None):
    if kind == "col":
        start = pl.multiple_of(idx * width, LANES)
        return ref.at[:, :, pl.ds(start, width)] if layer is None else ref.at[layer, :, pl.ds(start, width)]
    if kind == "row":
        start = pl.multiple_of(idx * width, 16)
        return ref.at[:, pl.ds(start, width), :] if layer is None else ref.at[layer, pl.ds(start, width), :]
    assert kind == "slot"
    return ref.at[idx]


_FLIPS = [(0, 0, 1), (0, 1, 0), (0, 1, 1), (1, 0, 0), (1, 0, 1), (1, 1, 0), (1, 1, 1)]
_SIBLING = (0, 0, 1)
_AG_FLIPS = [_SIBLING, (1, 0, 0), (0, 1, 0), (1, 1, 0)]


def _flip(pos, f):
    return tuple((1 - p) if b else p for p, b in zip(pos, f))


def _other_chip(pos, j):
    x, y = pos[0], pos[1]
    return [(1 - x, y), (x, 1 - y), (1 - x, 1 - y)][j]


class Exchange:
    def __init__(self):
        self.sources, self.buffers, self.items, self.locals = [], [], [], []

    def source(self, array):
        self.sources.append(array)
        return len(self.sources) - 1

    def buffer(self, name, bufs, shape, dtype):
        if name not in [b[0] for b in self.buffers]:
            self.buffers.append((name, bufs.get(name), jax.ShapeDtypeStruct(shape, dtype)))

    def n_copies(self):
        return sum(len(flips) for _, _, flips in self.items)


def _exchange_fns(ex, srcs, bufs, send_sems, recv_sems, local_sems):
    def local(k):
        sf, df = ex.locals[k]
        me = _my_pos()
        return pltpu.make_async_copy(sf(srcs, bufs, me, me), df(bufs, me), local_sems.at[k])

    def remote(sending):
        me = _my_pos()
        k = 0
        for sf, df, flips in ex.items:
            for f in flips:
                peer = _flip(me, f)
                yield pltpu.make_async_remote_copy(
                    src_ref=sf(srcs, bufs, me, peer), dst_ref=df(bufs, me if sending else peer),
                    send_sem=send_sems.at[k], recv_sem=recv_sems.at[k], device_id=peer, device_id_type=MESH)
                k += 1

    def start():
        for k in range(len(ex.locals)):
            local(k).start()
        for cp in remote(True):
            cp.start()

    def wait():
        arrivals = list(remote(False))
        for cp in arrivals:
            cp.wait_recv()
        for cp in arrivals:
            cp.wait_send()
        for k in range(len(ex.locals)):
            local(k).wait()

    return start, wait


def comm_call(core, *, name, grid, in_specs, out_specs, out_shape, scratch_shapes, inputs, sem, ex=None):
    n_in, n_out, n_scr = len(in_specs), len(out_specs), len(scratch_shapes)
    if ex is None:
        res = pl.pallas_call(core, name=name, grid=grid, in_specs=in_specs, out_specs=out_specs,
                             out_shape=out_shape, scratch_shapes=scratch_shapes,
                             compiler_params=_cparams(*sem))(*inputs)
        return list(res), {}
    existing = [(k, b[1]) for k, b in enumerate(ex.buffers) if b[1] is not None]
    names = [b[0] for b in ex.buffers]
    n_src, n_old, n_buf = len(ex.sources), len(existing), len(names)

    def body(*refs):
        cuts = [n_in, n_src, n_old, n_out, n_buf, n_scr, 3]
        parts, pos = [], 0
        for c in cuts:
            parts.append(refs[pos:pos + c])
            pos += c
        core_in, srcs, _, core_out, buf_refs, core_scr, sems = parts
        start, wait = _exchange_fns(ex, srcs, dict(zip(names, buf_refs)), *sems)
        if not grid:
            start()
            wait()
            return
        ids = [pl.program_id(d) for d in range(len(grid))]
        pl.when(functools.reduce(jnp.logical_and, [i == 0 for i in ids]))(start)
        core(*core_in, *core_out, *core_scr)
        pl.when(functools.reduce(jnp.logical_and, [i == g - 1 for i, g in zip(ids, grid)]))(wait)

    dma = pltpu.SemaphoreType.DMA
    res = pl.pallas_call(
        body, name=name, **({"grid": grid} if grid else {}),
        in_specs=list(in_specs) + [HBM_SPEC] * (n_src + n_old),
        out_specs=list(out_specs) + [HBM_SPEC] * n_buf,
        out_shape=list(out_shape) + [b[2] for b in ex.buffers],
        scratch_shapes=list(scratch_shapes) + [dma((max(ex.n_copies(), 1),)), dma((max(ex.n_copies(), 1),)),
                                               dma((max(len(ex.locals), 1),))],
        input_output_aliases={n_in + n_src + e: n_out + k for e, (k, _) in enumerate(existing)},
        compiler_params=_cparams(*(["arbitrary"] * len(grid))),
    )(*inputs, *ex.sources, *[b for _, b in existing])
    return list(res[:n_out]), dict(zip(names, res[n_out:]))


def run_exchange(ex, name):
    return comm_call(None, name=name, grid=(), in_specs=[], out_specs=[], out_shape=[], scratch_shapes=[],
                     inputs=[], sem=(), ex=ex)[1]


def _gathered_shape(shard, kind):
    s = shard.shape
    if kind == "col":
        return s[:2] + (s[2] * N_DEV,)
    if kind == "row":
        return (s[0], s[1] * N_DEV, s[2])
    return (N_DEV,) + s


def ag_exchange(group, shards, bufs, phase):
    ex = Exchange()
    for tn, layer in group:
        shard, kind = shards[tn], SHARD_KIND[tn]
        width = {"col": shard.shape[-1], "row": shard.shape[-2], "slot": None}[kind]
        ex.buffer(tn, bufs, _gathered_shape(shard, kind), shard.dtype)

        def part(b, who, tn=tn, kind=kind, width=width, layer=layer):
            return _shard_view(b[tn], kind, width, _lin(who), layer)

        if phase == "A":
            def src(s, b, me, peer, si=ex.source(shard), layer=layer):
                return s[si] if layer is None else s[si].at[layer]

            ex.items.append((src, part, _AG_FLIPS))
            ex.locals.append((src, part))
        else:
            for j in range(3):
                def held(who, j=j):
                    return (*_other_chip(who, j), who[2])

                ex.items.append((lambda s, b, me, peer, part=part, held=held: part(b, held(me)),
                                 lambda b, sender, part=part, held=held: part(b, held(sender)), [_SIBLING]))
    return ex


def rs_exchange(pending, bufs, shard_shapes):
    ex = Exchange()
    for tn, layer, g in pending:
        kind = SHARD_KIND[tn]
        shape = shard_shapes[tn]
        width = {"col": shape[-1], "row": shape[-2]}.get(kind)
        ex.buffer(tn, bufs, (N_DEV,) + tuple(shape), g.dtype)

        def src(s, b, me, peer, si=ex.source(g), kind=kind, width=width):
            idx = _lin(peer)
            if kind == "all":
                return s[si]
            if kind == "slot":
                return s[si].at[idx]
            if kind == "col":
                return s[si].at[:, pl.ds(pl.multiple_of(idx * width, LANES), width)]
            return s[si].at[pl.ds(pl.multiple_of(idx * width, 16), width), :]

        def dst(b, sender, tn=tn, layer=layer):
            return b[tn].at[_lin(sender)] if layer is None else b[tn].at[_lin(sender), layer]

        ex.items.append((src, dst, _FLIPS))
        ex.locals.append((src, dst))
    return ex


_C1 = 1.0 - ADAM_B1 ** ADAM_STEP
_C2 = 1.0 - ADAM_B2 ** ADAM_STEP


def adamw(wt, m, v, slots, name):
    nl, r, c = wt.shape
    tr = _tile(r, 128)

    def body(w_ref, m_ref, v_ref, s_ref, g_ref, d_ref, nm_ref, nv_ref):
        g = s_ref[0].astype(F32)
        for k in range(1, N_DEV):
            g = g + s_ref[k].astype(F32)
        m_new = ADAM_B1 * m_ref[...] + (1.0 - ADAM_B1) * g
        v_new = ADAM_B2 * v_ref[...] + (1.0 - ADAM_B2) * jnp.square(g)
        m_hat = m_new / _C1
        v_hat = v_new / _C2
        g_ref[...] = g
        d_ref[...] = -ADAM_LR * (m_hat / (jnp.sqrt(v_hat) + ADAM_EPS) + ADAM_WD * w_ref[...])
        nm_ref[...] = m_new
        nv_ref[...] = v_new

    blk = pl.BlockSpec((None, tr, c), lambda l, i: (l, i, 0))
    sblk = pl.BlockSpec((N_DEV, None, tr, c), lambda l, i: (0, l, i, 0))
    return pl.pallas_call(
        body, name=name, grid=(nl, r // tr), in_specs=[blk, blk, blk, sblk], out_specs=[blk] * 4,
        out_shape=[jax.ShapeDtypeStruct(wt.shape, F32)] * 4,
        compiler_params=_cparams("parallel", "parallel"),
    )(wt, m, v, slots)


def _pack(arrays, rows):
    flat = jnp.concatenate([a.reshape(-1).astype(F32) for a in arrays])
    return jnp.pad(flat, (0, rows * LANES - flat.shape[0])).reshape(rows, LANES)


def _unpack(packed, shapes):
    flat = packed.reshape(-1)
    out, off = [], 0
    for s in shapes:
        sz = math.prod(s)
        out.append(flat[off:off + sz].reshape(s))
        off += sz
    return out


SMALL_W = 768


def _pack_small_shard(conv_w, conv_b, norm_g):
    ng = jnp.pad(norm_g, ((0, 0), (0, SMALL_W - norm_g.shape[1])))
    return jnp.concatenate([conv_w[0], conv_b, ng, jnp.zeros((2, SMALL_W), F32)], axis=0)


def _unpack_small_shard(p):
    return p[0:4][None], p[4:5], p[5:6, :SSD_INNER // N_DEV]


REPL = ["norm_mix_g", "norm_mlp_g", "sb_q_norm_g", "sb_k_norm_g", "gm_v_norm_g", "gm_w_s", "gm_b_s",
        "ssd_dt_bias", "ssd_a_log", "ssd_d"]
BIG = ["sb_w_qkv", "sb_w_o", "gm_w_in", "gm_w_o", "ssd_w_in", "ssd_w_o", "mlp_w_in", "mlp_w_out"]
BIG_KIND = {"sb_w_qkv": "col", "sb_w_o": "row", "gm_w_in": "col", "gm_w_o": "row", "ssd_w_in": "slot",
            "ssd_w_o": "row", "mlp_w_in": "col", "mlp_w_out": "row"}
SMALL_SHARDED = ["ssd_conv_w", "ssd_conv_b", "ssd_norm_g"]
WEIGHTS = ["norm_mix_g", "norm_mlp_g", "sb_w_qkv", "sb_q_norm_g", "sb_k_norm_g", "sb_w_o", "gm_w_in",
           "gm_v_norm_g", "gm_w_s", "gm_b_s", "gm_w_o", "ssd_w_in", "ssd_conv_w", "ssd_conv_b", "ssd_dt_bias",
           "ssd_a_log", "ssd_d", "ssd_norm_g", "ssd_w_o", "mlp_w_in", "mlp_w_out"]


SHARD_KIND = dict(BIG_KIND, small="slot", repl="all")
AG_GROUPS = [
    [("sb_w_qkv", 0), ("sb_w_o", 0), ("mlp_w_in", 0), ("mlp_w_out", 0)],
    [("gm_w_in", 0), ("gm_w_o", 0), ("mlp_w_in", 1), ("mlp_w_out", 1), ("ssd_w_in", None), ("small", None),
     ("ssd_w_o", 0), ("mlp_w_in", 2), ("mlp_w_out", 2)],
    [("sb_w_qkv", 1), ("sb_w_o", 1), ("mlp_w_in", 3), ("mlp_w_out", 3)],
]


def _ssd_weights(w):
    win = jnp.moveaxis(w["ssd_w_in"][:, 0], 0, 1).reshape(D_MODEL, SSD_PROJ)
    small = w["small"]
    w["ssd_wz"] = win[:, :SSD_INNER]
    w["ssd_wxbc"] = win[:, SSD_INNER:SSD_INNER + SSD_CONV_DIM]
    w["ssd_wdt"] = jnp.pad(win[:, SSD_INNER + SSD_CONV_DIM:], ((0, 0), (0, DT_PAD - SSD_HEADS)))
    w["ssd_conv_w"] = small[:, 0:4].transpose(1, 0, 2).reshape(SSD_CONV, SSD_CONV_DIM)
    w["ssd_conv_b"] = small[:, 4].reshape(1, SSD_CONV_DIM)
    w["ssd_norm_g"] = small[:, 5:6, :SSD_INNER // N_DEV]


def _small_weights(p):
    w = {}
    w["norm_mix_g"] = [p["norm_mix_g"][i:i + 1] for i in range(DEPTH)]
    w["norm_mlp_g"] = [p["norm_mlp_g"][i:i + 1] for i in range(DEPTH)]
    w["sb_gqk"] = [jnp.stack([p["sb_q_norm_g"][j:j + 1], p["sb_k_norm_g"][j:j + 1]]) for j in range(2)]
    w["gm_v_norm_g"] = p["gm_v_norm_g"]
    w["gm_w_s"] = p["gm_w_s"][0]
    w["gm_b_s"] = p["gm_b_s"][0][:, :, None]
    w["ssd_dt_bias"] = jnp.pad(p["ssd_dt_bias"], ((0, 0), (0, DT_PAD - SSD_HEADS)))
    w["ssd_a_log"] = p["ssd_a_log"].reshape(SSD_HEADS, 1, 1)
    w["ssd_d"] = p["ssd_d"].reshape(SSD_HEADS, 1, 1)
    return w


def _repl_grads(g):
    return [jnp.concatenate(g["norm_mix_g"], axis=0), jnp.concatenate(g["norm_mlp_g"], axis=0),
            jnp.concatenate([g["sb_gqk"][0][0], g["sb_gqk"][1][0]], axis=0),
            jnp.concatenate([g["sb_gqk"][0][1], g["sb_gqk"][1][1]], axis=0),
            g["gm_v_norm_g"], g["gm_w_s"][None], g["gm_b_s"][None, :, :, 0],
            g["ssd_dt_bias"], g["ssd_a_log"].reshape(1, SSD_HEADS), g["ssd_d"].reshape(1, SSD_HEADS)]


def kernel(x, norm_mix_g, norm_mlp_g, sb_w_qkv, sb_q_norm_g, sb_k_norm_g, sb_w_o, gm_w_in, gm_v_norm_g, gm_w_s, gm_b_s, gm_w_o, ssd_w_in, ssd_conv_w, ssd_conv_b, ssd_dt_bias, ssd_a_log, ssd_d, ssd_norm_g, ssd_w_o, mlp_w_in, mlp_w_out, loss_target, m_norm_mix_g, m_norm_mlp_g, m_sb_w_qkv, m_sb_q_norm_g, m_sb_k_norm_g, m_sb_w_o, m_gm_w_in, m_gm_v_norm_g, m_gm_w_s, m_gm_b_s, m_gm_w_o, m_ssd_w_in, m_ssd_conv_w, m_ssd_conv_b, m_ssd_dt_bias, m_ssd_a_log, m_ssd_d, m_ssd_norm_g, m_ssd_w_o, m_mlp_w_in, m_mlp_w_out, v_norm_mix_g, v_norm_mlp_g, v_sb_w_qkv, v_sb_q_norm_g, v_sb_k_norm_g, v_sb_w_o, v_gm_w_in, v_gm_v_norm_g, v_gm_w_s, v_gm_b_s, v_gm_w_o, v_ssd_w_in, v_ssd_conv_w, v_ssd_conv_b, v_ssd_dt_bias, v_ssd_a_log, v_ssd_d, v_ssd_norm_g, v_ssd_w_o, v_mlp_w_in, v_mlp_w_out):
    args = dict(locals())
    p = {k: args[k] for k in WEIGHTS}
    pm = {k: args["m_" + k] for k in WEIGHTS}
    pv = {k: args["v_" + k] for k in WEIGHTS}

    shards = {k: p[k].astype(BF16) for k in BIG}
    shards["small"] = _pack_small_shard(*[p[k] for k in SMALL_SHARDED])
    repl_shapes = [p[k].shape for k in REPL]
    n_repl = sum(math.prod(s) for s in repl_shapes)
    repl_rows = -(-n_repl // (LANES * LANES)) * LANES
    shard_shapes = {k: p[k].shape for k in BIG}
    shard_shapes.update(small=(8, SMALL_W), repl=(repl_rows, LANES))

    loss, grad_x, recv = device_step(x[0], loss_target[0], _small_weights(p), shards, shard_shapes)
    loss = lax.psum(loss[0, 0], ("x", "y", "c"))

    out = {}
    for k in BIG:
        out[k] = adamw(p[k], pm[k], pv[k], recv[k], f"adamw_{k}")
    sm = adamw(*[_pack_small_shard(*[d[k] for k in SMALL_SHARDED])[None] for d in (p, pm, pv)],
               recv["small"][:, None], "adamw_small")
    for k, vals in zip(SMALL_SHARDED, zip(*[_unpack_small_shard(r[0]) for r in sm])):
        out[k] = list(vals)
    rp = adamw(*[_pack([d[k] for k in REPL], repl_rows)[None] for d in (p, pm, pv)],
               recv["repl"][:, None], "adamw_repl")
    for k, vals in zip(REPL, zip(*[_unpack(r[0], repl_shapes) for r in rp])):
        out[k] = list(vals)
    res = [loss, grad_x[None]]
    for q in range(4):
        res += [out[k][q] for k in WEIGHTS]
    return tuple(res)
```

```python
import functools
import math

import jax
import jax.numpy as jnp
from jax import lax
from jax.experimental import pallas as pl
from jax.experimental.pallas import tpu as pltpu

F32 = jnp.float32
BF16 = jnp.bfloat16

D_MODEL = 2048
DEPTH = 4
N_MIXERS = 3
EPS = 1e-6
SB_HEADS = 16
HEAD_DIM = 128
GM_CHUNK = 128
GM_GROUPS = 16
GM_STEP_CHUNKS = 4
SSD_INNER = 4096
SSD_HEADS = 64
SSD_HEAD_DIM = 64
SSD_GROUPS = 8
SSD_HPG = 8
SSD_STATE = 128
SSD_CHUNK = 128
SSD_CONV = 4
SSD_BC = SSD_GROUPS * SSD_STATE
SSD_CONV_DIM = SSD_INNER + 2 * SSD_BC
SSD_PROJ = SSD_INNER + SSD_CONV_DIM + SSD_HEADS
DT_PAD = 128
N_DEV = 8
LANES = 128

ADAM_LR = 0.001
ADAM_B1 = 0.9
ADAM_B2 = 0.999
ADAM_EPS = 1e-08
ADAM_WD = 0.01
ADAM_STEP = 10

VMEM_LIMIT_BYTES = 56 * 1024 * 1024
MESH = pl.DeviceIdType.MESH
HBM_SPEC = pl.BlockSpec(memory_space=pl.ANY)


def _cparams(*sem):
    return pltpu.CompilerParams(dimension_semantics=sem, vmem_limit_bytes=VMEM_LIMIT_BYTES)


def _tile(dim, pref):
    t = min(dim, pref)
    assert dim % t == 0, (dim, pref)
    return t


_DOT_DIMS = {"nn": ((1,), (0,)), "nt": ((1,), (1,)), "tn": ((0,), (0,))}


def matmul(a, b, form, name, *, b_layer=None, out_dtypes=(F32,), epilogue=None, extras=(),
           tm=1024, tn=1024, tk=2048, ex=None):
    bs = b.shape[-2:]
    if form == "nn":
        (m, k), (k2, n) = a.shape, bs
    elif form == "nt":
        (m, k), (n, k2) = a.shape, bs
    else:
        (k, m), (k2, n) = a.shape, bs
    assert k == k2, (a.shape, b.shape, form)
    tm, tn, tk = _tile(m, tm), _tile(n, tn), _tile(k, tk)
    grid = (m // tm, n // tn, k // tk)
    nk = grid[2]
    if form == "tn":
        a_spec = pl.BlockSpec((tk, tm), lambda i, j, kk: (kk, i))
    else:
        a_spec = pl.BlockSpec((tm, tk), lambda i, j, kk: (i, kk))
    if form == "nt":
        b_blk, b_idx = (tn, tk), (lambda i, j, kk: (j, kk))
    else:
        b_blk, b_idx = (tk, tn), (lambda i, j, kk: (kk, j))
    if b.ndim == 3:
        b_spec = pl.BlockSpec((None,) + b_blk, lambda i, j, kk: (b_layer,) + b_idx(i, j, kk))
    else:
        b_spec = pl.BlockSpec(b_blk, b_idx)
    o_spec = pl.BlockSpec((tm, tn), lambda i, j, kk: (i, j))
    n_extra, n_out = len(extras), len(out_dtypes)
    dims = (_DOT_DIMS[form], ((), ()))

    def body(a_ref, b_ref, *rest):
        extra_refs, out_refs = rest[:n_extra], rest[n_extra:n_extra + n_out]
        kk = pl.program_id(2)

        def product():
            return lax.dot_general(a_ref[...], b_ref[...], dims, preferred_element_type=F32)

        def finish(acc):
            outs = (acc,) if epilogue is None else epilogue(acc, *[r[...] for r in extra_refs])
            for r, o in zip(out_refs, outs):
                r[...] = o.astype(r.dtype)

        if nk == 1:
            finish(product())
            return
        acc_ref = rest[-1]

        @pl.when(kk == 0)
        def _():
            acc_ref[...] = product()

        @pl.when(jnp.logical_and(kk > 0, kk < nk - 1))
        def _():
            acc_ref[...] += product()

        @pl.when(kk == nk - 1)
        def _():
            finish(acc_ref[...] + product())

    res, bufs = comm_call(
        body, name=name, grid=grid,
        in_specs=[a_spec, b_spec] + [o_spec] * n_extra,
        out_specs=[o_spec] * n_out,
        out_shape=[jax.ShapeDtypeStruct((m, n), dt) for dt in out_dtypes],
        scratch_shapes=[pltpu.VMEM((tm, tn), F32)] if nk > 1 else [],
        inputs=[a, b, *extras], sem=("parallel", "parallel", "arbitrary"), ex=ex)
    res = res[0] if n_out == 1 else res
    return res if ex is None else (res, bufs)


class Data:
    def __init__(self, array, block, imap):
        self.array, self.block, self.imap = array, block, imap


class Param:
    def __init__(self, array, block, imap, group):
        self.array, self.block, self.imap, self.group = array, block, imap, group


class Out:
    def __init__(self, shape, dtype, block, imap):
        self.shape, self.dtype, self.block, self.imap = shape, dtype, block, imap


def _f32(v):
    return v.astype(F32)


def blockwise_fwd(f, grid, data, params, outs, name):
    nd, npar = len(data), len(params)

    def body(*refs):
        vals = [_f32(r[...]) for r in refs[:nd + npar]]
        res = f(*vals)
        for r, o in zip(refs[nd + npar:], res):
            r[...] = o.astype(r.dtype)

    in_specs = [pl.BlockSpec(d.block, d.imap) for d in data]
    in_specs += [pl.BlockSpec(p.block, (lambda o, i, _m=p.imap: _m(o))) for p in params]
    res = pl.pallas_call(
        body, name=name, grid=grid, in_specs=in_specs,
        out_specs=[pl.BlockSpec(o.block, o.imap) for o in outs],
        out_shape=[jax.ShapeDtypeStruct(o.shape, o.dtype) for o in outs],
        compiler_params=_cparams("parallel", "parallel"),
    )(*[d.array for d in data], *[p.array for p in params])
    return res


def blockwise_bwd(f, grid, data, params, outs, cts, name, *, grad_dtypes, accum=None, also_bf16=()):
    nd, npar, no = len(data), len(params), len(outs)
    accum = accum or {}
    want = [k for k in range(nd) if grad_dtypes[k] is not None] + list(also_bf16)
    want_dtypes = [grad_dtypes[k] for k in want[:len(want) - len(also_bf16)]] + [BF16] * len(also_bf16)
    acc_keys = sorted(accum)

    def body(*refs):
        in_refs = refs[:nd + npar]
        ct_refs = refs[nd + npar:nd + npar + no]
        acc_refs = refs[nd + npar + no:nd + npar + no + len(acc_keys)]
        out_refs = refs[nd + npar + no + len(acc_keys):]
        dd_refs, dp_refs = out_refs[:len(want)], out_refs[len(want):]
        o, i = pl.program_id(0), pl.program_id(1)
        vals = [_f32(r[...]) for r in in_refs]
        _, vjp = jax.vjp(f, *vals)
        grads = vjp(tuple(_f32(r[...]) for r in ct_refs))
        for r, k in zip(dd_refs, want):
            g = grads[k]
            if k in accum:
                g = g + _f32(acc_refs[acc_keys.index(k)][...])
            r[...] = g.astype(r.dtype)
        for r, p, g in zip(dp_refs, params, grads[nd:]):
            first = jnp.logical_and(i == 0, o % p.group == 0)

            @pl.when(first)
            def _():
                r[...] = g.astype(r.dtype)

            @pl.when(jnp.logical_not(first))
            def _():
                r[...] += g.astype(r.dtype)

    d_specs = [pl.BlockSpec(d.block, d.imap) for d in data]
    p_specs = [pl.BlockSpec(p.block, (lambda o, i, _m=p.imap: _m(o))) for p in params]
    ct_specs = [pl.BlockSpec(o.block, o.imap) for o in outs]
    acc_specs = [d_specs[k] for k in acc_keys]
    res = pl.pallas_call(
        body, name=name, grid=grid,
        in_specs=d_specs + p_specs + ct_specs + acc_specs,
        out_specs=[d_specs[k] for k in want] + p_specs,
        out_shape=[jax.ShapeDtypeStruct(data[k].array.shape, dt) for k, dt in zip(want, want_dtypes)]
        + [jax.ShapeDtypeStruct(p.array.shape, F32) for p in params],
        compiler_params=_cparams("arbitrary", "arbitrary"),
    )(*[d.array for d in data], *[p.array for p in params], *cts, *[accum[k] for k in acc_keys])
    return list(res[:len(want)]), list(res[len(want):])


def _rmsnorm_f(x, g):
    return (x * lax.rsqrt(jnp.mean(x * x, axis=-1, keepdims=True) + EPS) * g,)


def _gelu(x):
    return 0.5 * x * (1.0 + lax.erf(x * (2.0 ** -0.5)))


def _silu(x):
    return x / (1.0 + jnp.exp(-x))


def _softplus(x):
    return jnp.maximum(x, 0.0) + jnp.log1p(jnp.exp(-jnp.abs(x)))


def _gm_act_f(pu, pv, gv):
    return (_gelu(pu),) + _rmsnorm_f(_gelu(pv), gv)


def _gm_gate_f(u, vn, w, b):
    n = w.shape[0]
    c = u.shape[0] // n
    row = lax.broadcasted_iota(jnp.int32, (n, n), 0)
    col = lax.broadcasted_iota(jnp.int32, (n, n), 1)
    wm = jnp.broadcast_to(jnp.where(row >= col, w, 0.0), (c, n, n))
    mixed = lax.dot_general(wm, vn.reshape(c, n, LANES), (((2,), (1,)), ((0,), (0,))),
                            preferred_element_type=F32)
    return (u * (mixed + b[None]).reshape(c * n, LANES),)


def _ssd_gate_f(y, z, g):
    return _rmsnorm_f(y * _silu(z), g)


def _dt_f(raw, bias):
    return (_softplus(raw + bias),)


def _row_tile(t):
    return _tile(t, 256)


def _norm_specs(h, g_row):
    t, d = h.shape
    tr = _row_tile(t)
    data = [Data(h, (tr, d), lambda o, i: (i, 0))]
    params = [Param(g_row, (1, d), lambda o: (0, 0), 1)]
    outs = [Out((t, d), BF16, (tr, d), lambda o, i: (i, 0))]
    return (1, t // tr), data, params, outs


def rmsnorm_fwd(h, g_row, name):
    grid, data, params, outs = _norm_specs(h, g_row)
    return blockwise_fwd(_rmsnorm_f, grid, data, params, outs, name)[0]


def rmsnorm_bwd(h, g_row, ct, skip, name, with_bf16=True):
    grid, data, params, outs = _norm_specs(h, g_row)
    dd, (dg,) = blockwise_bwd(_rmsnorm_f, grid, data, params, outs, [ct], name, grad_dtypes=[F32],
                              accum={0: skip}, also_bf16=(0,) if with_bf16 else ())
    return dd[0], (dd[1] if with_bf16 else None), dg


def loss_call(y, target):
    t, d = y.shape
    tr = _row_tile(t)

    def body(y_ref, t_ref, dy_ref, dyb_ref, loss_ref):
        e = y_ref[...] - t_ref[...]
        dy_ref[...] = e * (1.0 / d)
        dyb_ref[...] = (e * (1.0 / d)).astype(BF16)

        @pl.when(pl.program_id(0) == 0)
        def _():
            loss_ref[...] = jnp.zeros_like(loss_ref)

        loss_ref[...] += jnp.sum(e * e).reshape(1, 1) * (0.5 / d)

    row = pl.BlockSpec((tr, d), lambda i: (i, 0))
    return pl.pallas_call(
        body, name="loss", grid=(t // tr,), in_specs=[row, row],
        out_specs=[row, row, pl.BlockSpec((1, 1), lambda i: (0, 0))],
        out_shape=[jax.ShapeDtypeStruct((t, d), F32), jax.ShapeDtypeStruct((t, d), BF16),
                   jax.ShapeDtypeStruct((1, 1), F32)],
        compiler_params=_cparams("arbitrary"),
    )(y, target)


ATT_TQ = 1024
ATT_KB = 256
ATT_DIAG = ATT_TQ // ATT_KB


def _split2(x):
    hi = x.astype(BF16)
    return hi, (x - hi.astype(F32)).astype(BF16)


def _dot2(hi, lo, ones_bf):
    return (jnp.dot(hi, ones_bf, preferred_element_type=F32)
            + jnp.dot(lo, ones_bf, preferred_element_type=F32))


def _att_consts():
    r_ = lax.broadcasted_iota(jnp.int32, (ATT_KB, ATT_KB), 0)
    c_ = lax.broadcasted_iota(jnp.int32, (ATT_KB, ATT_KB), 1)
    upper = (r_ > c_).astype(BF16)
    lower = (r_ < c_).astype(BF16)
    return upper, lower


def _att_logits(q, k, scale, mask):
    z = lax.dot_general(q, k, (((1,), (1,)), ((), ())), preferred_element_type=F32) * scale
    lb = jnp.minimum(z, 0.0) - jnp.log(1.0 + jnp.exp(-jnp.abs(z)))
    l1 = lb - z
    if mask is not None:
        l1 = jnp.where(mask, l1, 0.0)
    return lb, l1


def _live_rows(d):
    if d is None:
        return pl.ds(0, ATT_TQ), None
    first, n = d * ATT_KB, ATT_TQ - d * ATT_KB
    row = first + lax.broadcasted_iota(jnp.int32, (n, ATT_KB), 0)
    col = first + lax.broadcasted_iota(jnp.int32, (n, ATT_KB), 1)
    return pl.ds(first, n), col < row


def _wide(c):
    return jnp.concatenate([c] * (ATT_KB // LANES), axis=1)


def attn_fwd(qk, qkv_bf, name, ex=None):
    t = qk.shape[0]
    tq = ATT_TQ
    assert t % tq == 0
    scale = HEAD_DIM ** -0.5

    def body(q_ref, k_ref, v_ref, o_ref, acc_ref, c_ref):
        qi = pl.program_id(1)
        upper, _ = _att_consts()
        acc_ref[...] = jnp.zeros_like(acc_ref)
        c_ref[...] = jnp.zeros_like(c_ref)

        def block(jb, diag):
            off = pl.multiple_of(jb * ATT_KB, ATT_KB)
            k = k_ref[pl.ds(off, ATT_KB), :]
            v = v_ref[pl.ds(off, ATT_KB), :]
            rows, mask = _live_rows(diag)
            lb, l1 = _att_logits(q_ref[rows, :], k, scale, mask)
            hi, lo = _split2(l1)
            c = c_ref[rows, :]
            a = jnp.exp(lb + _wide(c) + _dot2(hi, lo, upper))
            if mask is not None:
                a = jnp.where(mask, a, 0.0)
            acc_ref[rows, :] += jnp.dot(a.astype(BF16), v, preferred_element_type=F32)
            c_ref[rows, :] = c + jnp.sum(l1, axis=1, keepdims=True)

        for d in reversed(range(ATT_DIAG)):
            block(qi * ATT_DIAG + d, d)

        def step(s, carry):
            block(qi * ATT_DIAG - 1 - s, None)
            return carry

        lax.fori_loop(0, qi * ATT_DIAG, step, 0)
        o_ref[...] = acc_ref[...].astype(o_ref.dtype)

    (o,), bufs = comm_call(
        body, name=name, grid=(SB_HEADS, t // tq),
        in_specs=[pl.BlockSpec((tq, HEAD_DIM), lambda h, i: (i, h)),
                  pl.BlockSpec((t, HEAD_DIM), lambda h, i: (0, SB_HEADS + h)),
                  pl.BlockSpec((t, HEAD_DIM), lambda h, i: (0, 2 * SB_HEADS + h))],
        out_specs=[pl.BlockSpec((tq, HEAD_DIM), lambda h, i: (i, h))],
        out_shape=[jax.ShapeDtypeStruct((t, SB_HEADS * HEAD_DIM), BF16)],
        scratch_shapes=[pltpu.VMEM((tq, HEAD_DIM), F32), pltpu.VMEM((tq, LANES), F32)],
        inputs=[qk, qk, qkv_bf], sem=("parallel", "arbitrary"), ex=ex)
    return o, bufs


def attn_bwd(qk, qkv_bf, do, name, ex=None):
    t = qk.shape[0]
    tq = ATT_TQ
    assert t % tq == 0
    nq = t // tq
    nkb = t // ATT_KB
    scale = HEAD_DIM ** -0.5
    tn_dims = (((0,), (0,)), ((), ()))

    def body(q_ref, k_ref, v_ref, do_ref, dq_ref, dk_ref, dv_ref,
             cbuf, c_ref, ce_ref, dq_acc, dk_acc, dv_acc):
        qi = pl.program_id(1)
        upper, lower = _att_consts()

        @pl.when(qi == 0)
        def _():
            dk_acc[...] = jnp.zeros_like(dk_acc)
            dv_acc[...] = jnp.zeros_like(dv_acc)

        c_ref[...] = jnp.zeros_like(c_ref)
        ce_ref[...] = jnp.zeros_like(ce_ref)
        dq_acc[...] = jnp.zeros_like(dq_acc)

        def pass1(jb, diag):
            off = pl.multiple_of(jb * ATT_KB, ATT_KB)
            rows, mask = _live_rows(diag)
            _, l1 = _att_logits(q_ref[rows, :], k_ref[pl.ds(off, ATT_KB), :], scale, mask)
            c = c_ref[rows, :]
            cbuf[jb, rows, :] = c
            c_ref[rows, :] = c + jnp.sum(l1, axis=1, keepdims=True)

        for d in reversed(range(ATT_DIAG)):
            pass1(qi * ATT_DIAG + d, d)

        def step1(s, carry):
            pass1(qi * ATT_DIAG - 1 - s, None)
            return carry

        lax.fori_loop(0, qi * ATT_DIAG, step1, 0)

        def pass2(jb, diag):
            off = pl.multiple_of(jb * ATT_KB, ATT_KB)
            k = k_ref[pl.ds(off, ATT_KB), :]
            v = v_ref[pl.ds(off, ATT_KB), :]
            rows, mask = _live_rows(diag)
            q, do_ = q_ref[rows, :], do_ref[rows, :]
            lb, l1 = _att_logits(q, k, scale, mask)
            hi, lo = _split2(l1)
            a = jnp.exp(lb + _wide(cbuf[jb, rows, :]) + _dot2(hi, lo, upper))
            if mask is not None:
                a = jnp.where(mask, a, 0.0)
            da = lax.dot_general(do_, v, (((1,), (1,)), ((), ())), preferred_element_type=F32)
            e = da * a
            ehi, elo = _split2(e)
            ce = ce_ref[rows, :]
            big_e = _wide(ce) + _dot2(ehi, elo, lower)
            dz = (e - (e + big_e) * jnp.exp(lb)) * scale
            if mask is not None:
                dz = jnp.where(mask, dz, 0.0)
            dz = dz.astype(BF16)
            dq_acc[rows, :] += jnp.dot(dz, k, preferred_element_type=F32)
            ce_ref[rows, :] = ce + jnp.sum(e, axis=1, keepdims=True)
            dk_acc[pl.ds(off, ATT_KB), :] += lax.dot_general(dz, q, tn_dims, preferred_element_type=F32)
            dv_acc[pl.ds(off, ATT_KB), :] += lax.dot_general(a.astype(BF16), do_, tn_dims,
                                                             preferred_element_type=F32)

        def step2(jb, carry):
            pass2(jb, None)
            return carry

        lax.fori_loop(0, qi * ATT_DIAG, step2, 0)
        for d in range(ATT_DIAG):
            pass2(qi * ATT_DIAG + d, d)
        dq_ref[...] = dq_acc[...]

        @pl.when(qi == nq - 1)
        def _():
            dk_ref[...] = dk_acc[...]
            dv_ref[...] = dv_acc[...].astype(dv_ref.dtype)

    hd = SB_HEADS * HEAD_DIM
    (dq, dk, dv), bufs = comm_call(
        body, name=name, grid=(SB_HEADS, nq),
        in_specs=[pl.BlockSpec((tq, HEAD_DIM), lambda h, i: (i, h)),
                  pl.BlockSpec((t, HEAD_DIM), lambda h, i: (0, SB_HEADS + h)),
                  pl.BlockSpec((t, HEAD_DIM), lambda h, i: (0, 2 * SB_HEADS + h)),
                  pl.BlockSpec((tq, HEAD_DIM), lambda h, i: (i, h))],
        out_specs=[pl.BlockSpec((tq, HEAD_DIM), lambda h, i: (i, h)),
                   pl.BlockSpec((t, HEAD_DIM), lambda h, i: (0, h)),
                   pl.BlockSpec((t, HEAD_DIM), lambda h, i: (0, h))],
        out_shape=[jax.ShapeDtypeStruct((t, hd), F32), jax.ShapeDtypeStruct((t, hd), F32),
                   jax.ShapeDtypeStruct((t, hd), BF16)],
        scratch_shapes=[pltpu.VMEM((nkb, tq, LANES), F32), pltpu.VMEM((tq, LANES), F32),
                        pltpu.VMEM((tq, LANES), F32), pltpu.VMEM((tq, HEAD_DIM), F32),
                        pltpu.VMEM((t, HEAD_DIM), F32), pltpu.VMEM((t, HEAD_DIM), F32)],
        inputs=[qk, qk, qkv_bf, do], sem=("arbitrary", "arbitrary"), ex=ex)
    return jnp.concatenate([dq, dk], axis=1), dv, bufs


def _qknorm_specs(qkv, gqk):
    t = qkv.shape[0]
    tr = _tile(t, 1024)
    data = [Data(qkv, (tr, HEAD_DIM), lambda o, i: (i, o))]
    params = [Param(gqk, (None, 1, HEAD_DIM), lambda o: (o // SB_HEADS, 0, 0), SB_HEADS)]
    outs = [Out((t, 2 * SB_HEADS * HEAD_DIM), BF16, (tr, HEAD_DIM), lambda o, i: (i, o))]
    return (2 * SB_HEADS, t // tr), data, params, outs


CONV_COLS = 128


def _shift_down(x, s, rows):
    if s == 0:
        return x
    return jnp.where(rows < s, 0.0, pltpu.roll(x, s, 0))


def _shift_up(x, s, rows):
    if s == 0:
        return x
    n = x.shape[0]
    return jnp.where(rows >= n - s, 0.0, pltpu.roll(x, n - s, 0))


def conv_fwd(xbc, w, b, name):
    t, c = xbc.shape
    tc = _tile(c, CONV_COLS)

    def body(x_ref, w_ref, b_ref, o_ref):
        x = x_ref[...]
        rows = lax.broadcasted_iota(jnp.int32, x.shape, 0)
        y = jnp.broadcast_to(b_ref[...], x.shape)
        for kk in range(SSD_CONV):
            y = y + w_ref[kk:kk + 1, :] * _shift_down(x, SSD_CONV - 1 - kk, rows)
        o_ref[...] = _silu(y)

    col = pl.BlockSpec((t, tc), lambda j: (0, j))
    return pl.pallas_call(
        body, name=name, grid=(c // tc,),
        in_specs=[col, pl.BlockSpec((SSD_CONV, tc), lambda j: (0, j)), pl.BlockSpec((1, tc), lambda j: (0, j))],
        out_specs=col, out_shape=jax.ShapeDtypeStruct((t, c), F32),
        compiler_params=_cparams("parallel"),
    )(xbc, w, b)


def conv_bwd(xbc, w, b, dact, name):
    t, c = xbc.shape
    tc = _tile(c, CONV_COLS)

    def body(x_ref, w_ref, b_ref, g_ref, dx_ref, dw_ref, db_ref):
        x = x_ref[...]
        rows = lax.broadcasted_iota(jnp.int32, x.shape, 0)
        shifted = [_shift_down(x, SSD_CONV - 1 - kk, rows) for kk in range(SSD_CONV)]
        y = jnp.broadcast_to(b_ref[...], x.shape)
        for kk in range(SSD_CONV):
            y = y + w_ref[kk:kk + 1, :] * shifted[kk]
        sig = 1.0 / (1.0 + jnp.exp(-y))
        dy = g_ref[...] * (sig * (1.0 + y * (1.0 - sig)))
        dx = jnp.zeros_like(x)
        for kk in range(SSD_CONV):
            dx = dx + w_ref[kk:kk + 1, :] * _shift_up(dy, SSD_CONV - 1 - kk, rows)
            dw_ref[kk:kk + 1, :] = jnp.sum(dy * shifted[kk], axis=0, keepdims=True)
        dx_ref[...] = dx.astype(dx_ref.dtype)
        db_ref[...] = jnp.sum(dy, axis=0, keepdims=True)

    col = pl.BlockSpec((t, tc), lambda j: (0, j))
    wspec = pl.BlockSpec((SSD_CONV, tc), lambda j: (0, j))
    bspec = pl.BlockSpec((1, tc), lambda j: (0, j))
    return pl.pallas_call(
        body, name=name, grid=(c // tc,),
        in_specs=[col, wspec, bspec, col], out_specs=[col, wspec, bspec],
        out_shape=[jax.ShapeDtypeStruct((t, c), BF16), jax.ShapeDtypeStruct((SSD_CONV, c), F32),
                   jax.ShapeDtypeStruct((1, c), F32)],
        compiler_params=_cparams("parallel"),
    )(xbc, w, b, dact)


@functools.partial(jax.custom_vjp, nondiff_argnums=(2,))
def _bdot(a, b, dims):
    return lax.dot_general(a.astype(BF16), b.astype(BF16), (dims, ((), ())), preferred_element_type=F32)


def _bdot_fwd(a, b, dims):
    return _bdot(a, b, dims), (a, b)


def _bdot_bwd(dims, res, g):
    a, b = res
    (ca,), (cb,) = dims
    fa, fb = 1 - ca, 1 - cb
    gb, ab, bb = g.astype(BF16), a.astype(BF16), b.astype(BF16)

    def dn(u, v, cu, cv):
        return lax.dot_general(u, v, (((cu,), (cv,)), ((), ())), preferred_element_type=F32)

    da = dn(gb, bb, 1, fb) if ca == 1 else dn(bb, gb, fb, 1)
    db = dn(ab, gb, fa, 0) if cb == 0 else dn(gb, ab, 0, fa)
    return da, db


_bdot.defvjp(_bdot_fwd, _bdot_bwd)
_NN, _NT, _TN = ((1,), (0,)), ((1,), (1,)), ((0,), (0,))


def _split3_dot(m_bf, x, dims):
    out = None
    for _ in range(3):
        t = x.astype(BF16)
        part = lax.dot_general(m_bf, t, (dims, ((), ())), preferred_element_type=F32)
        out = part if out is None else out + part
        x = x - t.astype(F32)
    return out


@jax.custom_vjp
def _tri_dot(m_bf, x):
    return _split3_dot(m_bf, x, _NN)


def _tri_dot_fwd(m_bf, x):
    return _tri_dot(m_bf, x), m_bf


def _tri_dot_bwd(m_bf, g):
    return jnp.zeros_like(m_bf), _split3_dot(m_bf, g, _TN)


_tri_dot.defvjp(_tri_dot_fwd, _tri_dot_bwd)


def _ssd_chunk(x, dt, bm, cm, sprev, alog, dskip):
    n = x.shape[0]
    a = -jnp.exp(alog)
    da = dt * a
    row = lax.broadcasted_iota(jnp.int32, (n, n), 0)
    col = lax.broadcasted_iota(jnp.int32, (n, n), 1)
    causal = row >= col
    acum_l = _tri_dot(causal.astype(BF16), jnp.broadcast_to(da, (n, n)))
    acum_s = acum_l.T
    decay = jnp.exp(jnp.where(causal, acum_l - acum_s, -1e30))
    acum = jnp.sum(acum_l, axis=1, keepdims=True) * (1.0 / n)
    atot = jnp.sum(da, axis=0, keepdims=True)
    xs = x * dt
    y = _bdot(_bdot(cm, bm, _NT) * decay, xs, _NN)
    y = y + jnp.exp(acum) * _bdot(cm, sprev, _NT)
    y = y + dskip * x
    snew = jnp.exp(atot) * sprev + _bdot(xs * jnp.exp(atot - acum), bm, _TN)
    return y, snew


def _ssd_specs(t, rev):
    nc = t // SSD_CHUNK
    cidx = (lambda c: nc - 1 - c) if rev else (lambda c: c)
    x_spec = pl.BlockSpec((SSD_HPG, SSD_CHUNK, SSD_HEAD_DIM), lambda g, c: (g, cidx(c), 0))
    dt_spec = pl.BlockSpec((SSD_HPG, SSD_CHUNK, 1), lambda g, c: (g, cidx(c), 0))
    bc_spec = pl.BlockSpec((None, SSD_CHUNK, SSD_STATE), lambda g, c: (g, cidx(c), 0))
    sc_spec = pl.BlockSpec((SSD_HPG, 1, 1), lambda g, c: (g, 0, 0))
    st_spec = pl.BlockSpec((SSD_HPG, None, SSD_HEAD_DIM, SSD_STATE), lambda g, c: (g, cidx(c), 0, 0))
    return nc, x_spec, dt_spec, bc_spec, sc_spec, st_spec


def ssd_scan_fwd(x, dt, bm, cm, alog, dskip, name, ex=None):
    t = x.shape[1]
    nc, x_spec, dt_spec, bc_spec, sc_spec, st_spec = _ssd_specs(t, False)

    def body(x_ref, dt_ref, b_ref, c_ref, al_ref, d_ref, y_ref, sin_ref, state):
        @pl.when(pl.program_id(1) == 0)
        def _():
            state[...] = jnp.zeros_like(state)

        bm_, cm_ = b_ref[...], c_ref[...]

        def head(r, carry):
            sprev = state[r]
            sin_ref[r] = sprev
            y, snew = _ssd_chunk(x_ref[r], dt_ref[r], bm_, cm_, sprev, al_ref[r], d_ref[r])
            y_ref[r] = y
            state[r] = snew
            return carry

        lax.fori_loop(0, SSD_HPG, head, 0, unroll=SSD_HPG)

    (y, sin), bufs = comm_call(
        body, name=name, grid=(SSD_GROUPS, nc),
        in_specs=[x_spec, dt_spec, bc_spec, bc_spec, sc_spec, sc_spec],
        out_specs=[x_spec, st_spec],
        out_shape=[jax.ShapeDtypeStruct(x.shape, F32),
                   jax.ShapeDtypeStruct((SSD_HEADS, nc, SSD_HEAD_DIM, SSD_STATE), F32)],
        scratch_shapes=[pltpu.VMEM((SSD_HPG, SSD_HEAD_DIM, SSD_STATE), F32)],
        inputs=[x, dt, bm, cm, alog, dskip], sem=("parallel", "arbitrary"), ex=ex)
    return y, sin, bufs


def ssd_scan_bwd(x, dt, bm, cm, alog, dskip, sin, dy, name, ex=None):
    t = x.shape[1]
    nc, x_spec, dt_spec, bc_spec, sc_spec, st_spec = _ssd_specs(t, True)

    def body(x_ref, dt_ref, b_ref, c_ref, al_ref, d_ref, sin_ref, dy_ref,
             dx_ref, ddt_ref, db_ref, dc_ref, dal_ref, dd_ref, dstate):
        @pl.when(pl.program_id(1) == 0)
        def _():
            dstate[...] = jnp.zeros_like(dstate)
            dal_ref[...] = jnp.zeros_like(dal_ref)
            dd_ref[...] = jnp.zeros_like(dd_ref)

        db_ref[...] = jnp.zeros_like(db_ref)
        dc_ref[...] = jnp.zeros_like(dc_ref)
        bm_, cm_ = b_ref[...], c_ref[...]

        def head(r, carry):
            _, vjp = jax.vjp(_ssd_chunk, x_ref[r], dt_ref[r], bm_, cm_, sin_ref[r], al_ref[r], d_ref[r])
            dx, ddt, dbm, dcm, dsp, dal, dd = vjp((dy_ref[r], dstate[r]))
            dx_ref[r] = dx
            ddt_ref[r] = ddt
            db_ref[...] += dbm
            dc_ref[...] += dcm
            dstate[r] = dsp
            dal_ref[r] += dal
            dd_ref[r] += dd
            return carry

        lax.fori_loop(0, SSD_HPG, head, 0, unroll=SSD_HPG)

    return comm_call(
        body, name=name, grid=(SSD_GROUPS, nc),
        in_specs=[x_spec, dt_spec, bc_spec, bc_spec, sc_spec, sc_spec, st_spec, x_spec],
        out_specs=[x_spec, dt_spec, bc_spec, bc_spec, sc_spec, sc_spec],
        out_shape=[jax.ShapeDtypeStruct(x.shape, F32), jax.ShapeDtypeStruct(dt.shape, F32),
                   jax.ShapeDtypeStruct(bm.shape, F32), jax.ShapeDtypeStruct(cm.shape, F32),
                   jax.ShapeDtypeStruct(alog.shape, F32), jax.ShapeDtypeStruct(dskip.shape, F32)],
        scratch_shapes=[pltpu.VMEM((SSD_HPG, SSD_HEAD_DIM, SSD_STATE), F32)],
        inputs=[x, dt, bm, cm, alog, dskip, sin, dy], sem=("arbitrary", "arbitrary"), ex=ex)


def _to_heads(a, n, w):
    t = a.shape[0]
    return a.reshape(t, n, w).transpose(1, 0, 2)


def _from_heads(a):
    n, t, w = a.shape
    return a.transpose(1, 0, 2).reshape(t, n * w)


def _add(acc, h):
    return (acc + h,)


ATTN_FWD_US, ATTN_BWD_US, SCAN_FWD_US, SCAN_BWD_US = 700, 1600, 560, 1280


def sb_fwd(h, hn, w, j, tag):
    gqk = w["sb_gqk"][j]
    qkv, qkv_bf = w.mm_fwd(hn, "sb_w_qkv", j, "nn", f"{tag}_qkv", out_dtypes=(F32, BF16),
                           epilogue=lambda acc: (acc, acc))
    grid, data, params, outs = _qknorm_specs(qkv, gqk)
    qk = blockwise_fwd(_rmsnorm_f, grid, data, params, outs, f"{tag}_qknorm")[0]
    o, bufs = attn_fwd(qk, qkv_bf, f"{tag}_attn", w.host_fwd(ATTN_FWD_US))
    w.done_fwd(bufs)
    h1 = w.mm_fwd(o, "sb_w_o", j, "nn", f"{tag}_out", epilogue=_add, extras=(h,))
    return h1, (qkv, qkv_bf, qk, o)


def sb_bwd(g_bf, hn, saved, w, j, tag):
    qkv, qkv_bf, qk, o = saved
    gqk = w["sb_gqk"][j]
    w.add_grad("sb_w_o", j, w.mm_bwd(o, g_bf, "tn", f"{tag}_dwo", out_dtypes=(BF16,)))
    do = w.mm_bwd(g_bf, w["sb_w_o"], "nt", f"{tag}_do", b_layer=j, out_dtypes=(BF16,))
    dqk, dv, bufs = attn_bwd(qk, qkv_bf, do, f"{tag}_attn_bwd", w.host_bwd(ATTN_BWD_US))
    w.recv.update(bufs)
    grid, data, params, outs = _qknorm_specs(qkv, gqk)
    (draw,), (dgqk,) = blockwise_bwd(_rmsnorm_f, grid, data, params, outs, [dqk], f"{tag}_qknorm_bwd",
                                     grad_dtypes=[BF16])
    dqkv = jnp.concatenate([draw[:, :2 * SB_HEADS * HEAD_DIM], dv], axis=1)
    w.add_grad("sb_w_qkv", j, w.mm_bwd(hn, dqkv, "tn", f"{tag}_dwqkv", out_dtypes=(BF16,)))
    dhn = w.mm_bwd(dqkv, w["sb_w_qkv"], "nt", f"{tag}_dhn", b_layer=j)
    return dhn, {"sb_gqk": dgqk}


def _gm_act_specs(p, gv):
    t = p.shape[0]
    tr = _row_tile(t)
    d = D_MODEL
    data = [Data(p, (tr, d), lambda o, i: (i, 0)), Data(p, (tr, d), lambda o, i: (i, 1))]
    params = [Param(gv, (1, d), lambda o: (0, 0), 1)]
    outs = [Out((t, d), F32, (tr, d), lambda o, i: (i, 0)), Out((t, d), BF16, (tr, d), lambda o, i: (i, 0))]
    return (1, t // tr), data, params, outs


def _gm_gate_specs(u, vn, ws, bs):
    t = u.shape[0]
    rows = _tile(t, GM_STEP_CHUNKS * GM_CHUNK)
    blk = (rows, LANES)
    data = [Data(u, blk, lambda o, i: (i, o)), Data(vn, blk, lambda o, i: (i, o))]
    params = [Param(ws, (None, GM_CHUNK, GM_CHUNK), lambda o: (o, 0, 0), 1),
              Param(bs, (None, GM_CHUNK, 1), lambda o: (o, 0, 0), 1)]
    outs = [Out((t, D_MODEL), BF16, blk, lambda o, i: (i, o))]
    return (GM_GROUPS, t // rows), data, params, outs


def gm_fwd(h, hn, w, j, tag):
    p = w.mm_fwd(hn, "gm_w_in", j, "nn", f"{tag}_in")
    grid, data, params, outs = _gm_act_specs(p, w["gm_v_norm_g"])
    u, vn = blockwise_fwd(_gm_act_f, grid, data, params, outs, f"{tag}_act")
    grid, data, params, outs = _gm_gate_specs(u, vn, w["gm_w_s"], w["gm_b_s"])
    y = blockwise_fwd(_gm_gate_f, grid, data, params, outs, f"{tag}_gate")[0]
    h1 = w.mm_fwd(y, "gm_w_o", j, "nn", f"{tag}_out", epilogue=_add, extras=(h,))
    return h1, (p, u, vn, y)


def gm_bwd(g_bf, hn, saved, w, j, tag):
    p, u, vn, y = saved
    w.add_grad("gm_w_o", j, w.mm_bwd(y, g_bf, "tn", f"{tag}_dwo", out_dtypes=(BF16,)))
    dy = w.mm_bwd(g_bf, w["gm_w_o"], "nt", f"{tag}_dy", b_layer=j)
    grid, data, params, outs = _gm_gate_specs(u, vn, w["gm_w_s"], w["gm_b_s"])
    (du, dvn), (dws, dbs) = blockwise_bwd(_gm_gate_f, grid, data, params, outs, [dy], f"{tag}_gate_bwd",
                                          grad_dtypes=[F32, F32])
    grid, data, params, outs = _gm_act_specs(p, w["gm_v_norm_g"])
    (dpu, dpv), (dgv,) = blockwise_bwd(_gm_act_f, grid, data, params, outs, [du, dvn], f"{tag}_act_bwd",
                                       grad_dtypes=[BF16, BF16])
    dp = jnp.concatenate([dpu[:, :D_MODEL], dpv[:, D_MODEL:]], axis=1)
    w.add_grad("gm_w_in", j, w.mm_bwd(hn, dp, "tn", f"{tag}_dwin", out_dtypes=(BF16,)))
    dhn = w.mm_bwd(dp, w["gm_w_in"], "nt", f"{tag}_dhn", b_layer=j)
    return dhn, {"gm_v_norm_g": dgv, "gm_w_s": dws, "gm_b_s": dbs}


def _ssd_gate_specs(y, z, g):
    t = y.shape[0]
    tr = _row_tile(t)
    gw = SSD_INNER // SSD_GROUPS
    blk = (tr, gw)
    data = [Data(y, blk, lambda o, i: (i, o)), Data(z, blk, lambda o, i: (i, o))]
    params = [Param(g, (None, 1, gw), lambda o: (o, 0, 0), 1)]
    outs = [Out((t, SSD_INNER), BF16, blk, lambda o, i: (i, o))]
    return (SSD_GROUPS, t // tr), data, params, outs


def _dt_specs(raw, bias):
    t = raw.shape[0]
    tr = _row_tile(t)
    data = [Data(raw, (tr, DT_PAD), lambda o, i: (i, 0))]
    params = [Param(bias, (1, DT_PAD), lambda o: (0, 0), 1)]
    outs = [Out((t, DT_PAD), F32, (tr, DT_PAD), lambda o, i: (i, 0))]
    return (1, t // tr), data, params, outs


def ssd_fwd(h, hn, w, tag):
    w.need("ssd_w_in", None)
    z = w.mm_fwd(hn, "ssd_wz", None, "nn", f"{tag}_inz")
    xbc = w.mm_fwd(hn, "ssd_wxbc", None, "nn", f"{tag}_inx")
    raw = w.mm_fwd(hn, "ssd_wdt", None, "nn", f"{tag}_indt")
    act = conv_fwd(xbc, w["ssd_conv_w"], w["ssd_conv_b"], f"{tag}_conv")
    grid, data, params, outs = _dt_specs(raw, w["ssd_dt_bias"])
    dt = blockwise_fwd(_dt_f, grid, data, params, outs, f"{tag}_dt")[0]
    xh = _to_heads(act[:, :SSD_INNER], SSD_HEADS, SSD_HEAD_DIM)
    bh = _to_heads(act[:, SSD_INNER:SSD_INNER + SSD_BC], SSD_GROUPS, SSD_STATE)
    ch = _to_heads(act[:, SSD_INNER + SSD_BC:], SSD_GROUPS, SSD_STATE)
    dth = _to_heads(dt[:, :SSD_HEADS], SSD_HEADS, 1)
    yh, sin, bufs = ssd_scan_fwd(xh, dth, bh, ch, w["ssd_a_log"], w["ssd_d"], f"{tag}_scan",
                                 w.host_fwd(SCAN_FWD_US))
    w.done_fwd(bufs)
    y = _from_heads(yh)
    grid, data, params, outs = _ssd_gate_specs(y, z, w["ssd_norm_g"])
    yn = blockwise_fwd(_ssd_gate_f, grid, data, params, outs, f"{tag}_gate")[0]
    h1 = w.mm_fwd(yn, "ssd_w_o", 0, "nn", f"{tag}_out", epilogue=_add, extras=(h,))
    return h1, (z, xbc, raw, xh, bh, ch, dth, sin, y, yn)


def ssd_bwd(g_bf, hn, saved, w, tag):
    z, xbc, raw, xh, bh, ch, dth, sin, y, yn = saved
    w.add_grad("ssd_w_o", 0, w.mm_bwd(yn, g_bf, "tn", f"{tag}_dwo", out_dtypes=(BF16,)))
    dyn = w.mm_bwd(g_bf, w["ssd_w_o"], "nt", f"{tag}_dyn", b_layer=0)
    grid, data, params, outs = _ssd_gate_specs(y, z, w["ssd_norm_g"])
    (dy, dz), (dng,) = blockwise_bwd(_ssd_gate_f, grid, data, params, outs, [dyn], f"{tag}_gate_bwd",
                                     grad_dtypes=[F32, BF16])
    (dxh, ddth, dbh, dch, dal, ddk), bufs = ssd_scan_bwd(
        xh, dth, bh, ch, w["ssd_a_log"], w["ssd_d"], sin, _to_heads(dy, SSD_HEADS, SSD_HEAD_DIM),
        f"{tag}_scan_bwd", w.host_bwd(SCAN_BWD_US))
    w.recv.update(bufs)
    dact = jnp.concatenate([_from_heads(dxh), _from_heads(dbh), _from_heads(dch)], axis=1)
    dxbc, dcw, dcb = conv_bwd(xbc, w["ssd_conv_w"], w["ssd_conv_b"], dact, f"{tag}_conv_bwd")
    ddt = jnp.pad(_from_heads(ddth), ((0, 0), (0, DT_PAD - SSD_HEADS)))
    grid, data, params, outs = _dt_specs(raw, w["ssd_dt_bias"])
    (draw,), (dbias,) = blockwise_bwd(_dt_f, grid, data, params, outs, [ddt], f"{tag}_dt_bwd",
                                      grad_dtypes=[BF16])
    d_wz = w.mm_bwd(hn, dz, "tn", f"{tag}_dwz", out_dtypes=(BF16,))
    d_wx = w.mm_bwd(hn, dxbc, "tn", f"{tag}_dwx", out_dtypes=(BF16,))
    d_wdt = w.mm_bwd(hn, draw, "tn", f"{tag}_dwdt", out_dtypes=(BF16,))
    d_win = jnp.concatenate([d_wz, d_wx, d_wdt[:, :SSD_HEADS]], axis=1)
    d_win = d_win.reshape(D_MODEL, N_DEV, SSD_PROJ // N_DEV).transpose(1, 0, 2)
    w.add_grad("ssd_w_in", 0, d_win)
    dhn = w.mm_bwd(dz, w["ssd_wz"], "nt", f"{tag}_dhn_z")
    dhn = w.mm_bwd(dxbc, w["ssd_wxbc"], "nt", f"{tag}_dhn_x", epilogue=_add, extras=(dhn,))
    dhn = w.mm_bwd(draw, w["ssd_wdt"], "nt", f"{tag}_dhn_dt", epilogue=_add, extras=(dhn,))
    return dhn, {"ssd_conv_w": dcw, "ssd_conv_b": dcb, "ssd_dt_bias": dbias[:, :SSD_HEADS], "ssd_a_log": dal,
                 "ssd_d": ddk, "ssd_norm_g": dng}


def _relu2(acc):
    return (acc, jnp.square(jnp.maximum(acc, 0.0)))


def _relu2_bwd(acc, a):
    return (acc * (2.0 * jnp.maximum(a, 0.0)),)


def device_step(x, target, w):
    h = x
    saved = []
    for i in range(DEPTH):
        kind, j = i % N_MIXERS, i // N_MIXERS
        tag = f"l{i}"
        hn = rmsnorm_fwd(h, w["norm_mix_g"][i], f"{tag}_norm_mix")
        if kind == 0:
            h1, ms = sb_fwd(h, hn, w, j, f"{tag}_sb")
        elif kind == 1:
            h1, ms = gm_fwd(h, hn, w, j, f"{tag}_gm")
        else:
            h1, ms = ssd_fwd(h, hn, w, f"{tag}_ssd")
        hm = rmsnorm_fwd(h1, w["norm_mlp_g"][i], f"{tag}_norm_mlp")
        a, r = w.mm_fwd(hm, "mlp_w_in", i, "nn", f"{tag}_mlp_in", out_dtypes=(F32, BF16), epilogue=_relu2)
        h2 = w.mm_fwd(r, "mlp_w_out", i, "nn", f"{tag}_mlp_out", epilogue=_add, extras=(h1,))
        saved.append((h, hn, ms, h1, hm, a, r))
        h = h2
    dh, g_bf, loss = loss_call(h, target)

    small = {"norm_mix_g": [None] * DEPTH, "norm_mlp_g": [None] * DEPTH, "sb_gqk": [None] * 2}
    for i in reversed(range(DEPTH)):
        kind, j = i % N_MIXERS, i // N_MIXERS
        tag = f"l{i}"
        h0, hn, ms, h1, hm, a, r = saved[i]
        w.add_grad("mlp_w_out", i, w.mm_bwd(r, g_bf, "tn", f"{tag}_mlp_dwout", out_dtypes=(BF16,)))
        da = w.mm_bwd(g_bf, w["mlp_w_out"], "nt", f"{tag}_mlp_da", b_layer=i, out_dtypes=(BF16,),
                      epilogue=_relu2_bwd, extras=(a,))
        w.add_grad("mlp_w_in", i, w.mm_bwd(hm, da, "tn", f"{tag}_mlp_dwin", out_dtypes=(BF16,)))
        dhm = w.mm_bwd(da, w["mlp_w_in"], "nt", f"{tag}_mlp_dhm", b_layer=i)
        dh1, g_bf, small["norm_mlp_g"][i] = rmsnorm_bwd(h1, w["norm_mlp_g"][i], dhm, dh, f"{tag}_norm_mlp_bwd")
        if kind == 0:
            dhn, mg = sb_bwd(g_bf, hn, ms, w, j, f"{tag}_sb")
            small["sb_gqk"][j] = mg.pop("sb_gqk")
        elif kind == 1:
            dhn, mg = gm_bwd(g_bf, hn, ms, w, j, f"{tag}_gm")
        else:
            dhn, mg = ssd_bwd(g_bf, hn, ms, w, f"{tag}_ssd")
        small.update(mg)
        dh, g_bf, small["norm_mix_g"][i] = rmsnorm_bwd(h0, w["norm_mix_g"][i], dhn, dh1, f"{tag}_norm_mix_bwd",
                                                       with_bf16=i > 0)
    dcw = small["ssd_conv_w"].reshape(SSD_CONV, N_DEV, SMALL_W).transpose(1, 0, 2)
    dcb = small["ssd_conv_b"].reshape(N_DEV, 1, SMALL_W)
    dng = jnp.pad(small["ssd_norm_g"], ((0, 0), (0, 0), (0, SMALL_W - SSD_INNER // N_DEV)))
    w.add_grad("small", None, jnp.concatenate([dcw, dcb, dng, jnp.zeros((N_DEV, 2, SMALL_W), F32)], axis=1))
    w.add_grad("repl", None, _pack(_repl_grads(small), w.shapes["repl"][0]))
    return loss, dh


def _my_pos():
    return lax.axis_index("x"), lax.axis_index("y"), lax.axis_index("c")


def _lin(p):
    return 4 * p[0] + 2 * p[1] + p[2]


def _shard_view(ref, kind, width, idx, layer=None):
    if kind == "col":
        start = pl.multiple_of(idx * width, LANES)
        return ref.at[:, :, pl.ds(start, width)] if layer is None else ref.at[layer, :, pl.ds(start, width)]
    if kind == "row":
        start = pl.multiple_of(idx * width, 16)
        return ref.at[:, pl.ds(start, width), :] if layer is None else ref.at[layer, pl.ds(start, width), :]
    assert kind == "slot"
    return ref.at[idx]


_FLIPS = [(0, 0, 1), (0, 1, 0), (0, 1, 1), (1, 0, 0), (1, 0, 1), (1, 1, 0), (1, 1, 1)]
_SIBLING = (0, 0, 1)
_AG_FLIPS = [_SIBLING, (1, 0, 0), (0, 1, 0), (1, 1, 0)]


def _flip(pos, f):
    return tuple((1 - p) if b else p for p, b in zip(pos, f))


def _other_chip(pos, j):
    x, y = pos[0], pos[1]
    return [(1 - x, y), (x, 1 - y), (1 - x, 1 - y)][j]


class Exchange:
    def __init__(self):
        self.sources, self.buffers, self.items, self.locals = [], [], [], []

    def source(self, array):
        self.sources.append(array)
        return len(self.sources) - 1

    def buffer(self, name, bufs, shape, dtype):
        if name not in [b[0] for b in self.buffers]:
            self.buffers.append((name, bufs.get(name), jax.ShapeDtypeStruct(shape, dtype)))

    def n_copies(self):
        return sum(len(flips) for _, _, flips in self.items)


def _exchange_fns(ex, srcs, bufs, send_sems, recv_sems, local_sems):
    def local(k):
        sf, df = ex.locals[k]
        me = _my_pos()
        return pltpu.make_async_copy(sf(srcs, bufs, me, me), df(bufs, me), local_sems.at[k])

    def remote(sending):
        me = _my_pos()
        k = 0
        for sf, df, flips in ex.items:
            for f in flips:
                peer = _flip(me, f)
                yield pltpu.make_async_remote_copy(
                    src_ref=sf(srcs, bufs, me, peer), dst_ref=df(bufs, me if sending else peer),
                    send_sem=send_sems.at[k], recv_sem=recv_sems.at[k], device_id=peer, device_id_type=MESH)
                k += 1

    def start():
        for k in range(len(ex.locals)):
            local(k).start()
        for cp in remote(True):
            cp.start()

    def wait():
        arrivals = list(remote(False))
        for cp in arrivals:
            cp.wait_recv()
        for cp in arrivals:
            cp.wait_send()
        for k in range(len(ex.locals)):
            local(k).wait()

    return start, wait


def comm_call(core, *, name, grid, in_specs, out_specs, out_shape, scratch_shapes, inputs, sem, ex=None):
    n_in, n_out, n_scr = len(in_specs), len(out_specs), len(scratch_shapes)
    if ex is None:
        res = pl.pallas_call(core, name=name, grid=grid, in_specs=in_specs, out_specs=out_specs,
                             out_shape=out_shape, scratch_shapes=scratch_shapes,
                             compiler_params=_cparams(*sem))(*inputs)
        return list(res), {}
    existing = [(k, b[1]) for k, b in enumerate(ex.buffers) if b[1] is not None]
    names = [b[0] for b in ex.buffers]
    n_src, n_old, n_buf = len(ex.sources), len(existing), len(names)

    def body(*refs):
        cuts = [n_in, n_src, n_old, n_out, n_buf, n_scr, 3]
        parts, pos = [], 0
        for c in cuts:
            parts.append(refs[pos:pos + c])
            pos += c
        core_in, srcs, _, core_out, buf_refs, core_scr, sems = parts
        start, wait = _exchange_fns(ex, srcs, dict(zip(names, buf_refs)), *sems)
        if not grid:
            start()
            wait()
            return
        ids = [pl.program_id(d) for d in range(len(grid))]
        pl.when(functools.reduce(jnp.logical_and, [i == 0 for i in ids]))(start)
        core(*core_in, *core_out, *core_scr)
        pl.when(functools.reduce(jnp.logical_and, [i == g - 1 for i, g in zip(ids, grid)]))(wait)

    dma = pltpu.SemaphoreType.DMA
    res = pl.pallas_call(
        body, name=name, **({"grid": grid} if grid else {}),
        in_specs=list(in_specs) + [HBM_SPEC] * (n_src + n_old),
        out_specs=list(out_specs) + [HBM_SPEC] * n_buf,
        out_shape=list(out_shape) + [b[2] for b in ex.buffers],
        scratch_shapes=list(scratch_shapes) + [dma((max(ex.n_copies(), 1),)), dma((max(ex.n_copies(), 1),)),
                                               dma((max(len(ex.locals), 1),))],
        input_output_aliases={n_in + n_src + e: n_out + k for e, (k, _) in enumerate(existing)},
        compiler_params=_cparams(*(["arbitrary"] * len(grid))),
    )(*inputs, *ex.sources, *[b for _, b in existing])
    return list(res[:n_out]), dict(zip(names, res[n_out:]))


def run_exchange(ex, name):
    return comm_call(None, name=name, grid=(), in_specs=[], out_specs=[], out_shape=[], scratch_shapes=[],
                     inputs=[], sem=(), ex=ex)[1]


def _gathered_shape(shard, kind):
    s = shard.shape
    if kind == "col":
        return s[:2] + (s[2] * N_DEV,)
    if kind == "row":
        return (s[0], s[1] * N_DEV, s[2])
    return (N_DEV,) + s


def _n_pieces(nbytes, rows, target):
    n = 1
    while nbytes > n * target and rows % (32 * n) == 0 and n < 8:
        n *= 2
    return n


def _ag_part(buf, tn, layer, r0, nr, who, shards):
    kind, shard, idx = SHARD_KIND[tn], shards[tn], _lin(who)
    if kind == "col":
        return buf.at[layer, pl.ds(r0, nr), pl.ds(pl.multiple_of(idx * shard.shape[2], LANES), shard.shape[2])]
    if kind == "row":
        return buf.at[layer, pl.ds(pl.multiple_of(idx * shard.shape[1] + r0, 16), nr), :]
    return buf.at[idx] if tn == "small" else buf.at[idx, 0, pl.ds(r0, nr), :]


def _ag_items(ex, piece, phase, shards, bufs):
    tn, layer, r0, nr, _ = piece
    shard = shards[tn]
    ex.buffer(tn, bufs, _gathered_shape(shard, SHARD_KIND[tn]), shard.dtype)

    def part(b, who):
        return _ag_part(b[tn], tn, layer, r0, nr, who, shards)

    if phase == "A":
        def src(s, b, me, peer, si=ex.source(shard)):
            return s[si] if tn == "small" else s[si].at[0 if layer is None else layer, pl.ds(r0, nr), :]

        ex.items.append((src, part, _AG_FLIPS))
        ex.locals.append((src, part))
        return
    for j in range(3):
        def held(who, j=j):
            return (*_other_chip(who, j), who[2])

        ex.items.append((lambda s, b, me, peer, held=held: part(b, held(me)),
                         lambda b, sender, held=held: part(b, held(sender)), [_SIBLING]))


def _rs_items(ex, piece, bufs, shapes):
    tn, layer, r0, nr, _, g = piece
    kind, shape = SHARD_KIND[tn], shapes[tn]
    ex.buffer(tn, bufs, (N_DEV,) + tuple(shape), g.dtype)

    def src(s, b, me, peer, si=ex.source(g)):
        idx = _lin(peer)
        if kind == "all":
            return s[si]
        if tn == "small":
            return s[si].at[idx]
        if kind == "slot":
            return s[si].at[idx, pl.ds(r0, nr), :]
        if kind == "col":
            return s[si].at[pl.ds(r0, nr), pl.ds(pl.multiple_of(idx * shape[2], LANES), shape[2])]
        return s[si].at[pl.ds(pl.multiple_of(idx * shape[1] + r0, 16), nr), :]

    def dst(b, sender):
        idx = _lin(sender)
        return b[tn].at[idx] if layer is None else b[tn].at[idx, layer, pl.ds(r0, nr), :]

    ex.items.append((src, dst, _FLIPS))
    ex.locals.append((src, dst))


AG_BYTES_PER_US = 22e3
RS_BYTES_PER_US = 75e3
AG_PIECE_BYTES = 2.3e6
RS_PIECE_BYTES = 9e6
MXU_FLOPS_PER_US = 8.6e8


class Traffic(dict):
    def __init__(self, small_weights, shards, shapes):
        super().__init__(small_weights)
        self.shards, self.shapes, self.recv = shards, shapes, {}
        self.wait_a, self.wait_b, self.riding, self.unfinished = [], [], ([], []), {}
        for group in AG_GROUPS:
            for tn, layer in group:
                shard = shards[tn]
                rows = shard.shape[-2]
                nbytes = rows * shard.shape[-1] * shard.dtype.itemsize
                n = 1 if tn == "small" else _n_pieces(nbytes, rows, AG_PIECE_BYTES)
                self.unfinished[(tn, layer)] = n
                self.wait_a += [(tn, layer, k * (rows // n), rows // n, nbytes / n) for k in range(n)]
        self.grads, self.grads_left, self.n_calls = [], {}, 0

    def host_fwd(self, us, exclude=(), only=None):
        ex, budget = Exchange(), us * AG_BYTES_PER_US
        ok = (lambda p: p[:2] == only) if only else (lambda p: p[0] not in exclude)
        take_b = [p for p in self.wait_b if ok(p)]
        take_a = []
        for p in [p for p in self.wait_a if ok(p)]:
            if budget <= 0 and not only:
                break
            take_a.append(p)
            budget -= p[4]
        for p in take_b:
            self.wait_b.remove(p)
            _ag_items(ex, p, "B", self.shards, self)
        for p in take_a:
            self.wait_a.remove(p)
            _ag_items(ex, p, "A", self.shards, self)
        self.riding = (take_a, take_b)
        return ex if ex.items else None

    def done_fwd(self, bufs):
        self.update(bufs)
        take_a, take_b = self.riding
        self.wait_b += take_a
        for p in take_b:
            self.unfinished[p[:2]] -= 1
        self.riding = ([], [])

    def need(self, tn, layer):
        while self.unfinished.get((tn, layer), 0) > 0:
            self.n_calls += 1
            self.done_fwd(run_exchange(self.host_fwd(0, only=(tn, layer)), f"gather_now{self.n_calls}"))
        if tn == "ssd_w_in" and "ssd_wz" not in self:
            self.need("small", None)
            _ssd_weights(self)

    def mm_fwd(self, a, tn, layer, form, name, **kw):
        self.need(tn, layer)
        ex = self.host_fwd(_matmul_us(a, self[tn], form), exclude=(tn,))
        if ex is None:
            return matmul(a, self[tn], form, name, b_layer=layer, **kw)
        out, bufs = matmul(a, self[tn], form, name, b_layer=layer, ex=ex, **kw)
        self.done_fwd(bufs)
        return out

    def add_grad(self, tn, layer, g):
        rows = self.shapes[tn][-2]
        nbytes = g.size * g.dtype.itemsize
        n = 1 if layer is None else _n_pieces(nbytes, rows, RS_PIECE_BYTES)
        self.grads += [(tn, layer, k * (rows // n), rows // n, nbytes / n, g) for k in range(n)]
        self.grads_left[tn] = self.grads_left.get(tn, 0) + n

    def host_bwd(self, us, exclude=(), only=None):
        ex, budget = Exchange(), us * RS_BYTES_PER_US
        for p in [p for p in self.grads if (p[0] == only if only else p[0] not in exclude)]:
            if budget <= 0 and not only:
                break
            self.grads.remove(p)
            self.grads_left[p[0]] -= 1
            _rs_items(ex, p, self.recv, self.shapes)
            budget -= p[4]
        return ex if ex.items else None

    def need_grad(self, tn):
        if self.grads_left.get(tn, 0) > 0:
            self.n_calls += 1
            self.recv.update(run_exchange(self.host_bwd(0, only=tn), f"scatter_now{self.n_calls}"))

    def mm_bwd(self, a, b, form, name, **kw):
        ex = self.host_bwd(_matmul_us(a, b, form))
        if ex is None:
            return matmul(a, b, form, name, **kw)
        out, bufs = matmul(a, b, form, name, ex=ex, **kw)
        self.recv.update(bufs)
        return out


def _matmul_us(a, b, form):
    k = a.shape[0] if form == "tn" else a.shape[1]
    return 2.0 * a.size // k * b.shape[-2] * b.shape[-1] / MXU_FLOPS_PER_US


_C1 = 1.0 - ADAM_B1 ** ADAM_STEP
_C2 = 1.0 - ADAM_B2 ** ADAM_STEP
ADAMW_BYTES_PER_ELEMENT = 7 * 4 + N_DEV * 2
HBM_BYTES_PER_US = 2.9e6


def adamw(wt, m, v, slots, name, ex=None):
    nl, r, c = wt.shape
    tr = _tile(r, 128)

    def body(w_ref, m_ref, v_ref, s_ref, g_ref, d_ref, nm_ref, nv_ref):
        g = s_ref[0].astype(F32)
        for k in range(1, N_DEV):
            g = g + s_ref[k].astype(F32)
        m_new = ADAM_B1 * m_ref[...] + (1.0 - ADAM_B1) * g
        v_new = ADAM_B2 * v_ref[...] + (1.0 - ADAM_B2) * jnp.square(g)
        m_hat = m_new / _C1
        v_hat = v_new / _C2
        g_ref[...] = g
        d_ref[...] = -ADAM_LR * (m_hat / (jnp.sqrt(v_hat) + ADAM_EPS) + ADAM_WD * w_ref[...])
        nm_ref[...] = m_new
        nv_ref[...] = v_new

    blk = pl.BlockSpec((None, tr, c), lambda l, i: (l, i, 0))
    sblk = pl.BlockSpec((N_DEV, None, tr, c), lambda l, i: (0, l, i, 0))
    return comm_call(
        body, name=name, grid=(nl, r // tr), in_specs=[blk, blk, blk, sblk], out_specs=[blk] * 4,
        out_shape=[jax.ShapeDtypeStruct(wt.shape, F32)] * 4, scratch_shapes=[],
        inputs=[wt, m, v, slots], sem=("parallel", "parallel"), ex=ex)


def _pack(arrays, rows):
    flat = jnp.concatenate([a.reshape(-1).astype(F32) for a in arrays])
    return jnp.pad(flat, (0, rows * LANES - flat.shape[0])).reshape(rows, LANES)


def _unpack(packed, shapes):
    flat = packed.reshape(-1)
    out, off = [], 0
    for s in shapes:
        sz = math.prod(s)
        out.append(flat[off:off + sz].reshape(s))
        off += sz
    return out


SMALL_W = 768


def _pack_small_shard(conv_w, conv_b, norm_g):
    ng = jnp.pad(norm_g, ((0, 0), (0, SMALL_W - norm_g.shape[1])))
    return jnp.concatenate([conv_w[0], conv_b, ng, jnp.zeros((2, SMALL_W), F32)], axis=0)


def _unpack_small_shard(p):
    return p[0:4][None], p[4:5], p[5:6, :SSD_INNER // N_DEV]


REPL = ["norm_mix_g", "norm_mlp_g", "sb_q_norm_g", "sb_k_norm_g", "gm_v_norm_g", "gm_w_s", "gm_b_s",
        "ssd_dt_bias", "ssd_a_log", "ssd_d"]
BIG = ["sb_w_qkv", "sb_w_o", "gm_w_in", "gm_w_o", "ssd_w_in", "ssd_w_o", "mlp_w_in", "mlp_w_out"]
BIG_KIND = {"sb_w_qkv": "col", "sb_w_o": "row", "gm_w_in": "col", "gm_w_o": "row", "ssd_w_in": "slot",
            "ssd_w_o": "row", "mlp_w_in": "col", "mlp_w_out": "row"}
SMALL_SHARDED = ["ssd_conv_w", "ssd_conv_b", "ssd_norm_g"]
WEIGHTS = ["norm_mix_g", "norm_mlp_g", "sb_w_qkv", "sb_q_norm_g", "sb_k_norm_g", "sb_w_o", "gm_w_in",
           "gm_v_norm_g", "gm_w_s", "gm_b_s", "gm_w_o", "ssd_w_in", "ssd_conv_w", "ssd_conv_b", "ssd_dt_bias",
           "ssd_a_log", "ssd_d", "ssd_norm_g", "ssd_w_o", "mlp_w_in", "mlp_w_out"]


SHARD_KIND = dict(BIG_KIND, small="slot", repl="all")
AG_GROUPS = [
    [("sb_w_qkv", 0), ("sb_w_o", 0), ("mlp_w_in", 0), ("mlp_w_out", 0)],
    [("gm_w_in", 0), ("gm_w_o", 0), ("mlp_w_in", 1), ("mlp_w_out", 1), ("ssd_w_in", None), ("small", None),
     ("ssd_w_o", 0), ("mlp_w_in", 2), ("mlp_w_out", 2)],
    [("sb_w_qkv", 1), ("sb_w_o", 1), ("mlp_w_in", 3), ("mlp_w_out", 3)],
]


def _ssd_weights(w):
    win = jnp.moveaxis(w["ssd_w_in"][:, 0], 0, 1).reshape(D_MODEL, SSD_PROJ)
    small = w["small"]
    w["ssd_wz"] = win[:, :SSD_INNER]
    w["ssd_wxbc"] = win[:, SSD_INNER:SSD_INNER + SSD_CONV_DIM]
    w["ssd_wdt"] = jnp.pad(win[:, SSD_INNER + SSD_CONV_DIM:], ((0, 0), (0, DT_PAD - SSD_HEADS)))
    w["ssd_conv_w"] = small[:, 0:4].transpose(1, 0, 2).reshape(SSD_CONV, SSD_CONV_DIM)
    w["ssd_conv_b"] = small[:, 4].reshape(1, SSD_CONV_DIM)
    w["ssd_norm_g"] = small[:, 5:6, :SSD_INNER // N_DEV]


def _small_weights(p):
    w = {}
    w["norm_mix_g"] = [p["norm_mix_g"][i:i + 1] for i in range(DEPTH)]
    w["norm_mlp_g"] = [p["norm_mlp_g"][i:i + 1] for i in range(DEPTH)]
    w["sb_gqk"] = [jnp.stack([p["sb_q_norm_g"][j:j + 1], p["sb_k_norm_g"][j:j + 1]]) for j in range(2)]
    w["gm_v_norm_g"] = p["gm_v_norm_g"]
    w["gm_w_s"] = p["gm_w_s"][0]
    w["gm_b_s"] = p["gm_b_s"][0][:, :, None]
    w["ssd_dt_bias"] = jnp.pad(p["ssd_dt_bias"], ((0, 0), (0, DT_PAD - SSD_HEADS)))
    w["ssd_a_log"] = p["ssd_a_log"].reshape(SSD_HEADS, 1, 1)
    w["ssd_d"] = p["ssd_d"].reshape(SSD_HEADS, 1, 1)
    return w


def _repl_grads(g):
    return [jnp.concatenate(g["norm_mix_g"], axis=0), jnp.concatenate(g["norm_mlp_g"], axis=0),
            jnp.concatenate([g["sb_gqk"][0][0], g["sb_gqk"][1][0]], axis=0),
            jnp.concatenate([g["sb_gqk"][0][1], g["sb_gqk"][1][1]], axis=0),
            g["gm_v_norm_g"], g["gm_w_s"][None], g["gm_b_s"][None, :, :, 0],
            g["ssd_dt_bias"], g["ssd_a_log"].reshape(1, SSD_HEADS), g["ssd_d"].reshape(1, SSD_HEADS)]


def kernel(x, norm_mix_g, norm_mlp_g, sb_w_qkv, sb_q_norm_g, sb_k_norm_g, sb_w_o, gm_w_in, gm_v_norm_g, gm_w_s, gm_b_s, gm_w_o, ssd_w_in, ssd_conv_w, ssd_conv_b, ssd_dt_bias, ssd_a_log, ssd_d, ssd_norm_g, ssd_w_o, mlp_w_in, mlp_w_out, loss_target, m_norm_mix_g, m_norm_mlp_g, m_sb_w_qkv, m_sb_q_norm_g, m_sb_k_norm_g, m_sb_w_o, m_gm_w_in, m_gm_v_norm_g, m_gm_w_s, m_gm_b_s, m_gm_w_o, m_ssd_w_in, m_ssd_conv_w, m_ssd_conv_b, m_ssd_dt_bias, m_ssd_a_log, m_ssd_d, m_ssd_norm_g, m_ssd_w_o, m_mlp_w_in, m_mlp_w_out, v_norm_mix_g, v_norm_mlp_g, v_sb_w_qkv, v_sb_q_norm_g, v_sb_k_norm_g, v_sb_w_o, v_gm_w_in, v_gm_v_norm_g, v_gm_w_s, v_gm_b_s, v_gm_w_o, v_ssd_w_in, v_ssd_conv_w, v_ssd_conv_b, v_ssd_dt_bias, v_ssd_a_log, v_ssd_d, v_ssd_norm_g, v_ssd_w_o, v_mlp_w_in, v_mlp_w_out):
    args = dict(locals())
    p = {k: args[k] for k in WEIGHTS}
    pm = {k: args["m_" + k] for k in WEIGHTS}
    pv = {k: args["v_" + k] for k in WEIGHTS}

    shards = {k: p[k].astype(BF16) for k in BIG}
    shards["small"] = _pack_small_shard(*[p[k] for k in SMALL_SHARDED])
    repl_shapes = [p[k].shape for k in REPL]
    n_repl = sum(math.prod(s) for s in repl_shapes)
    repl_rows = -(-n_repl // (LANES * LANES)) * LANES
    shard_shapes = {k: p[k].shape for k in BIG}
    shard_shapes.update(small=(8, SMALL_W), repl=(repl_rows, LANES))

    w = Traffic(_small_weights(p), shards, shard_shapes)
    loss, grad_x = device_step(x[0], loss_target[0], w)
    loss = lax.psum(loss[0, 0], ("x", "y", "c"))

    out = {}
    for k in ["mlp_w_in", "mlp_w_out", "ssd_w_in", "ssd_w_o", "gm_w_in", "gm_w_o", "sb_w_o", "sb_w_qkv"]:
        w.need_grad(k)
        ex = w.host_bwd(p[k].size * ADAMW_BYTES_PER_ELEMENT / HBM_BYTES_PER_US, exclude=(k,))
        out[k], bufs = adamw(p[k], pm[k], pv[k], w.recv[k], f"adamw_{k}", ex)
        w.recv.update(bufs)
    w.need_grad("small")
    w.need_grad("repl")
    recv = w.recv
    sm, _ = adamw(*[_pack_small_shard(*[d[k] for k in SMALL_SHARDED])[None] for d in (p, pm, pv)],
                  recv["small"][:, None], "adamw_small")
    for k, vals in zip(SMALL_SHARDED, zip(*[_unpack_small_shard(r[0]) for r in sm])):
        out[k] = list(vals)
    rp, _ = adamw(*[_pack([d[k] for k in REPL], repl_rows)[None] for d in (p, pm, pv)],
                  recv["repl"][:, None], "adamw_repl")
    for k, vals in zip(REPL, zip(*[_unpack(r[0], repl_shapes) for r in rp])):
        out[k] = list(vals)
    res = [loss, grad_x[None]]
    for q in range(4):
        res += [out[k][q] for k in WEIGHTS]
    return tuple(res)
```

```python
import functools
import math

import jax
import jax.numpy as jnp
from jax import lax
from jax.experimental import pallas as pl
from jax.experimental.pallas import tpu as pltpu

F32 = jnp.float32
BF16 = jnp.bfloat16

D_MODEL = 2048
DEPTH = 4
N_MIXERS = 3
EPS = 1e-6
SB_HEADS = 16
HEAD_DIM = 128
GM_CHUNK = 128
GM_GROUPS = 16
GM_STEP_CHUNKS = 4
SSD_INNER = 4096
SSD_HEADS = 64
SSD_HEAD_DIM = 64
SSD_GROUPS = 8
SSD_HPG = 8
SSD_STATE = 128
SSD_CHUNK = 128
SSD_CONV = 4
SSD_BC = SSD_GROUPS * SSD_STATE
SSD_CONV_DIM = SSD_INNER + 2 * SSD_BC
SSD_PROJ = SSD_INNER + SSD_CONV_DIM + SSD_HEADS
DT_PAD = 128
N_DEV = 8
LANES = 128

ADAM_LR = 0.001
ADAM_B1 = 0.9
ADAM_B2 = 0.999
ADAM_EPS = 1e-08
ADAM_WD = 0.01
ADAM_STEP = 10

VMEM_LIMIT_BYTES = 56 * 1024 * 1024
MESH = pl.DeviceIdType.MESH
HBM_SPEC = pl.BlockSpec(memory_space=pl.ANY)


def _cparams(*sem):
    return pltpu.CompilerParams(dimension_semantics=sem, vmem_limit_bytes=VMEM_LIMIT_BYTES)


def _tile(dim, pref):
    t = min(dim, pref)
    assert dim % t == 0, (dim, pref)
    return t


_DOT_DIMS = {"nn": ((1,), (0,)), "nt": ((1,), (1,)), "tn": ((0,), (0,))}


def matmul(a, b, form, name, *, b_layer=None, out_dtypes=(F32,), epilogue=None, extras=(),
           tm=1024, tn=1024, tk=2048, ex=None):
    bs = b.shape[-2:]
    if form == "nn":
        (m, k), (k2, n) = a.shape, bs
    elif form == "nt":
        (m, k), (n, k2) = a.shape, bs
    else:
        (k, m), (k2, n) = a.shape, bs
    assert k == k2, (a.shape, b.shape, form)
    tm, tn, tk = _tile(m, tm), _tile(n, tn), _tile(k, tk)
    grid = (m // tm, n // tn, k // tk)
    nk = grid[2]
    if form == "tn":
        a_spec = pl.BlockSpec((tk, tm), lambda i, j, kk: (kk, i))
    else:
        a_spec = pl.BlockSpec((tm, tk), lambda i, j, kk: (i, kk))
    if form == "nt":
        b_blk, b_idx = (tn, tk), (lambda i, j, kk: (j, kk))
    else:
        b_blk, b_idx = (tk, tn), (lambda i, j, kk: (kk, j))
    if b.ndim == 3:
        b_spec = pl.BlockSpec((None,) + b_blk, lambda i, j, kk: (b_layer,) + b_idx(i, j, kk))
    else:
        b_spec = pl.BlockSpec(b_blk, b_idx)
    o_spec = pl.BlockSpec((tm, tn), lambda i, j, kk: (i, j))
    n_extra, n_out = len(extras), len(out_dtypes)
    dims = (_DOT_DIMS[form], ((), ()))

    def body(a_ref, b_ref, *rest):
        extra_refs, out_refs = rest[:n_extra], rest[n_extra:n_extra + n_out]
        kk = pl.program_id(2)

        def product():
            return lax.dot_general(a_ref[...], b_ref[...], dims, preferred_element_type=F32)

        def finish(acc):
            outs = (acc,) if epilogue is None else epilogue(acc, *[r[...] for r in extra_refs])
            for r, o in zip(out_refs, outs):
                r[...] = o.astype(r.dtype)

        if nk == 1:
            finish(product())
            return
        acc_ref = rest[-1]

        @pl.when(kk == 0)
        def _():
            acc_ref[...] = product()

        @pl.when(jnp.logical_and(kk > 0, kk < nk - 1))
        def _():
            acc_ref[...] += product()

        @pl.when(kk == nk - 1)
        def _():
            finish(acc_ref[...] + product())

    res, bufs = comm_call(
        body, name=name, grid=grid,
        in_specs=[a_spec, b_spec] + [o_spec] * n_extra,
        out_specs=[o_spec] * n_out,
        out_shape=[jax.ShapeDtypeStruct((m, n), dt) for dt in out_dtypes],
        scratch_shapes=[pltpu.VMEM((tm, tn), F32)] if nk > 1 else [],
        inputs=[a, b, *extras], sem=("parallel", "parallel", "arbitrary"), ex=ex)
    res = res[0] if n_out == 1 else res
    return res if ex is None else (res, bufs)


class Data:
    def __init__(self, array, block, imap):
        self.array, self.block, self.imap = array, block, imap


class Param:
    def __init__(self, array, block, imap, group):
        self.array, self.block, self.imap, self.group = array, block, imap, group


class Out:
    def __init__(self, shape, dtype, block, imap):
        self.shape, self.dtype, self.block, self.imap = shape, dtype, block, imap


def _f32(v):
    return v.astype(F32)


def blockwise_fwd(f, grid, data, params, outs, name):
    nd, npar = len(data), len(params)

    def body(*refs):
        vals = [_f32(r[...]) for r in refs[:nd + npar]]
        res = f(*vals)
        for r, o in zip(refs[nd + npar:], res):
            r[...] = o.astype(r.dtype)

    in_specs = [pl.BlockSpec(d.block, d.imap) for d in data]
    in_specs += [pl.BlockSpec(p.block, (lambda o, i, _m=p.imap: _m(o))) for p in params]
    res = pl.pallas_call(
        body, name=name, grid=grid, in_specs=in_specs,
        out_specs=[pl.BlockSpec(o.block, o.imap) for o in outs],
        out_shape=[jax.ShapeDtypeStruct(o.shape, o.dtype) for o in outs],
        compiler_params=_cparams("parallel", "parallel"),
    )(*[d.array for d in data], *[p.array for p in params])
    return res


def blockwise_bwd(f, grid, data, params, outs, cts, name, *, grad_dtypes, accum=None, also_bf16=()):
    nd, npar, no = len(data), len(params), len(outs)
    accum = accum or {}
    want = [k for k in range(nd) if grad_dtypes[k] is not None] + list(also_bf16)
    want_dtypes = [grad_dtypes[k] for k in want[:len(want) - len(also_bf16)]] + [BF16] * len(also_bf16)
    acc_keys = sorted(accum)

    def body(*refs):
        in_refs = refs[:nd + npar]
        ct_refs = refs[nd + npar:nd + npar + no]
        acc_refs = refs[nd + npar + no:nd + npar + no + len(acc_keys)]
        out_refs = refs[nd + npar + no + len(acc_keys):]
        dd_refs, dp_refs = out_refs[:len(want)], out_refs[len(want):]
        o, i = pl.program_id(0), pl.program_id(1)
        vals = [_f32(r[...]) for r in in_refs]
        _, vjp = jax.vjp(f, *vals)
        grads = vjp(tuple(_f32(r[...]) for r in ct_refs))
        for r, k in zip(dd_refs, want):
            g = grads[k]
            if k in accum:
                g = g + _f32(acc_refs[acc_keys.index(k)][...])
            r[...] = g.astype(r.dtype)
        for r, p, g in zip(dp_refs, params, grads[nd:]):
            first = jnp.logical_and(i == 0, o % p.group == 0)

            @pl.when(first)
            def _():
                r[...] = g.astype(r.dtype)

            @pl.when(jnp.logical_not(first))
            def _():
                r[...] += g.astype(r.dtype)

    d_specs = [pl.BlockSpec(d.block, d.imap) for d in data]
    p_specs = [pl.BlockSpec(p.block, (lambda o, i, _m=p.imap: _m(o))) for p in params]
    ct_specs = [pl.BlockSpec(o.block, o.imap) for o in outs]
    acc_specs = [d_specs[k] for k in acc_keys]
    res = pl.pallas_call(
        body, name=name, grid=grid,
        in_specs=d_specs + p_specs + ct_specs + acc_specs,
        out_specs=[d_specs[k] for k in want] + p_specs,
        out_shape=[jax.ShapeDtypeStruct(data[k].array.shape, dt) for k, dt in zip(want, want_dtypes)]
        + [jax.ShapeDtypeStruct(p.array.shape, F32) for p in params],
        compiler_params=_cparams("arbitrary", "arbitrary"),
    )(*[d.array for d in data], *[p.array for p in params], *cts, *[accum[k] for k in acc_keys])
    return list(res[:len(want)]), list(res[len(want):])


def _rmsnorm_f(x, g):
    return (x * lax.rsqrt(jnp.mean(x * x, axis=-1, keepdims=True) + EPS) * g,)


def _gelu(x):
    return 0.5 * x * (1.0 + lax.erf(x * (2.0 ** -0.5)))


def _silu(x):
    return x / (1.0 + jnp.exp(-x))


def _softplus(x):
    return jnp.maximum(x, 0.0) + jnp.log1p(jnp.exp(-jnp.abs(x)))


def _gm_act_f(pu, pv, gv):
    return (_gelu(pu),) + _rmsnorm_f(_gelu(pv), gv)


def _gm_gate_f(u, vn, w, b):
    n = w.shape[0]
    c = u.shape[0] // n
    row = lax.broadcasted_iota(jnp.int32, (n, n), 0)
    col = lax.broadcasted_iota(jnp.int32, (n, n), 1)
    wm = jnp.broadcast_to(jnp.where(row >= col, w, 0.0), (c, n, n))
    mixed = lax.dot_general(wm, vn.reshape(c, n, LANES), (((2,), (1,)), ((0,), (0,))),
                            preferred_element_type=F32)
    return (u * (mixed + b[None]).reshape(c * n, LANES),)


def _ssd_gate_f(y, z, g):
    return _rmsnorm_f(y * _silu(z), g)


def _dt_f(raw, bias):
    return (_softplus(raw + bias),)


def _row_tile(t):
    return _tile(t, 256)


def _norm_specs(h, g_row):
    t, d = h.shape
    tr = _row_tile(t)
    data = [Data(h, (tr, d), lambda o, i: (i, 0))]
    params = [Param(g_row, (1, d), lambda o: (0, 0), 1)]
    outs = [Out((t, d), BF16, (tr, d), lambda o, i: (i, 0))]
    return (1, t // tr), data, params, outs


def rmsnorm_fwd(h, g_row, name):
    grid, data, params, outs = _norm_specs(h, g_row)
    return blockwise_fwd(_rmsnorm_f, grid, data, params, outs, name)[0]


def rmsnorm_bwd(h, g_row, ct, skip, name, with_bf16=True):
    grid, data, params, outs = _norm_specs(h, g_row)
    dd, (dg,) = blockwise_bwd(_rmsnorm_f, grid, data, params, outs, [ct], name, grad_dtypes=[F32],
                              accum={0: skip}, also_bf16=(0,) if with_bf16 else ())
    return dd[0], (dd[1] if with_bf16 else None), dg


def loss_call(y, target):
    t, d = y.shape
    tr = _row_tile(t)

    def body(y_ref, t_ref, dy_ref, dyb_ref, loss_ref):
        e = y_ref[...] - t_ref[...]
        dy_ref[...] = e * (1.0 / d)
        dyb_ref[...] = (e * (1.0 / d)).astype(BF16)

        @pl.when(pl.program_id(0) == 0)
        def _():
            loss_ref[...] = jnp.zeros_like(loss_ref)

        loss_ref[...] += jnp.sum(e * e).reshape(1, 1) * (0.5 / d)

    row = pl.BlockSpec((tr, d), lambda i: (i, 0))
    return pl.pallas_call(
        body, name="loss", grid=(t // tr,), in_specs=[row, row],
        out_specs=[row, row, pl.BlockSpec((1, 1), lambda i: (0, 0))],
        out_shape=[jax.ShapeDtypeStruct((t, d), F32), jax.ShapeDtypeStruct((t, d), BF16),
                   jax.ShapeDtypeStruct((1, 1), F32)],
        compiler_params=_cparams("arbitrary"),
    )(y, target)


ATT_TQ = 1024
ATT_KB = 256
ATT_DIAG = ATT_TQ // ATT_KB


def _split2(x):
    hi = x.astype(BF16)
    return hi, (x - hi.astype(F32)).astype(BF16)


def _dot2(hi, lo, ones_bf):
    return (jnp.dot(hi, ones_bf, preferred_element_type=F32)
            + jnp.dot(lo, ones_bf, preferred_element_type=F32))


def _att_consts():
    r_ = lax.broadcasted_iota(jnp.int32, (ATT_KB, ATT_KB), 0)
    c_ = lax.broadcasted_iota(jnp.int32, (ATT_KB, ATT_KB), 1)
    upper = (r_ > c_).astype(BF16)
    lower = (r_ < c_).astype(BF16)
    return upper, lower


def _att_logits(q, k, scale, mask):
    z = lax.dot_general(q, k, (((1,), (1,)), ((), ())), preferred_element_type=F32) * scale
    lb = jnp.minimum(z, 0.0) - jnp.log(1.0 + jnp.exp(-jnp.abs(z)))
    l1 = lb - z
    if mask is not None:
        l1 = jnp.where(mask, l1, 0.0)
    return lb, l1


def _live_rows(d):
    if d is None:
        return pl.ds(0, ATT_TQ), None
    first, n = d * ATT_KB, ATT_TQ - d * ATT_KB
    row = first + lax.broadcasted_iota(jnp.int32, (n, ATT_KB), 0)
    col = first + lax.broadcasted_iota(jnp.int32, (n, ATT_KB), 1)
    return pl.ds(first, n), col < row


def _wide(c):
    return jnp.concatenate([c] * (ATT_KB // LANES), axis=1)


def attn_fwd(qk, qkv_bf, name, ex=None):
    t = qk.shape[0]
    tq = ATT_TQ
    assert t % tq == 0
    scale = HEAD_DIM ** -0.5

    def body(q_ref, k_ref, v_ref, o_ref, acc_ref, c_ref):
        qi = pl.program_id(1)
        upper, _ = _att_consts()
        acc_ref[...] = jnp.zeros_like(acc_ref)
        c_ref[...] = jnp.zeros_like(c_ref)

        def block(jb, diag):
            off = pl.multiple_of(jb * ATT_KB, ATT_KB)
            k = k_ref[pl.ds(off, ATT_KB), :]
            v = v_ref[pl.ds(off, ATT_KB), :]
            rows, mask = _live_rows(diag)
            lb, l1 = _att_logits(q_ref[rows, :], k, scale, mask)
            hi, lo = _split2(l1)
            c = c_ref[rows, :]
            a = jnp.exp(lb + _wide(c) + _dot2(hi, lo, upper))
            if mask is not None:
                a = jnp.where(mask, a, 0.0)
            acc_ref[rows, :] += jnp.dot(a.astype(BF16), v, preferred_element_type=F32)
            c_ref[rows, :] = c + jnp.sum(l1, axis=1, keepdims=True)

        for d in reversed(range(ATT_DIAG)):
            block(qi * ATT_DIAG + d, d)

        def step(s, carry):
            block(qi * ATT_DIAG - 1 - s, None)
            return carry

        lax.fori_loop(0, qi * ATT_DIAG, step, 0)
        o_ref[...] = acc_ref[...].astype(o_ref.dtype)

    (o,), bufs = comm_call(
        body, name=name, grid=(SB_HEADS, t // tq),
        in_specs=[pl.BlockSpec((tq, HEAD_DIM), lambda h, i: (i, h)),
                  pl.BlockSpec((t, HEAD_DIM), lambda h, i: (0, SB_HEADS + h)),
                  pl.BlockSpec((t, HEAD_DIM), lambda h, i: (0, 2 * SB_HEADS + h))],
        out_specs=[pl.BlockSpec((tq, HEAD_DIM), lambda h, i: (i, h))],
        out_shape=[jax.ShapeDtypeStruct((t, SB_HEADS * HEAD_DIM), BF16)],
        scratch_shapes=[pltpu.VMEM((tq, HEAD_DIM), F32), pltpu.VMEM((tq, LANES), F32)],
        inputs=[qk, qk, qkv_bf], sem=("parallel", "arbitrary"), ex=ex)
    return o, bufs


def attn_bwd(qk, qkv_bf, do, name, ex=None):
    t = qk.shape[0]
    tq = ATT_TQ
    assert t % tq == 0
    nq = t // tq
    nkb = t // ATT_KB
    scale = HEAD_DIM ** -0.5
    tn_dims = (((0,), (0,)), ((), ()))

    def body(q_ref, k_ref, v_ref, do_ref, dq_ref, dk_ref, dv_ref,
             cbuf, c_ref, ce_ref, dq_acc, dk_acc, dv_acc):
        qi = pl.program_id(1)
        upper, lower = _att_consts()

        @pl.when(qi == 0)
        def _():
            dk_acc[...] = jnp.zeros_like(dk_acc)
            dv_acc[...] = jnp.zeros_like(dv_acc)

        c_ref[...] = jnp.zeros_like(c_ref)
        ce_ref[...] = jnp.zeros_like(ce_ref)
        dq_acc[...] = jnp.zeros_like(dq_acc)

        def pass1(jb, diag):
            off = pl.multiple_of(jb * ATT_KB, ATT_KB)
            rows, mask = _live_rows(diag)
            _, l1 = _att_logits(q_ref[rows, :], k_ref[pl.ds(off, ATT_KB), :], scale, mask)
            c = c_ref[rows, :]
            cbuf[jb, rows, :] = c
            c_ref[rows, :] = c + jnp.sum(l1, axis=1, keepdims=True)

        for d in reversed(range(ATT_DIAG)):
            pass1(qi * ATT_DIAG + d, d)

        def step1(s, carry):
            pass1(qi * ATT_DIAG - 1 - s, None)
            return carry

        lax.fori_loop(0, qi * ATT_DIAG, step1, 0)

        def pass2(jb, diag):
            off = pl.multiple_of(jb * ATT_KB, ATT_KB)
            k = k_ref[pl.ds(off, ATT_KB), :]
            v = v_ref[pl.ds(off, ATT_KB), :]
            rows, mask = _live_rows(diag)
            q, do_ = q_ref[rows, :], do_ref[rows, :]
            lb, l1 = _att_logits(q, k, scale, mask)
            hi, lo = _split2(l1)
            a = jnp.exp(lb + _wide(cbuf[jb, rows, :]) + _dot2(hi, lo, upper))
            if mask is not None:
                a = jnp.where(mask, a, 0.0)
            da = lax.dot_general(do_, v, (((1,), (1,)), ((), ())), preferred_element_type=F32)
            e = da * a
            ehi, elo = _split2(e)
            ce = ce_ref[rows, :]
            big_e = _wide(ce) + _dot2(ehi, elo, lower)
            dz = (e - (e + big_e) * jnp.exp(lb)) * scale
            if mask is not None:
                dz = jnp.where(mask, dz, 0.0)
            dz = dz.astype(BF16)
            dq_acc[rows, :] += jnp.dot(dz, k, preferred_element_type=F32)
            ce_ref[rows, :] = ce + jnp.sum(e, axis=1, keepdims=True)
            dk_acc[pl.ds(off, ATT_KB), :] += lax.dot_general(dz, q, tn_dims, preferred_element_type=F32)
            dv_acc[pl.ds(off, ATT_KB), :] += lax.dot_general(a.astype(BF16), do_, tn_dims,
                                                             preferred_element_type=F32)

        def step2(jb, carry):
            pass2(jb, None)
            return carry

        lax.fori_loop(0, qi * ATT_DIAG, step2, 0)
        for d in range(ATT_DIAG):
            pass2(qi * ATT_DIAG + d, d)
        dq_ref[...] = dq_acc[...]

        @pl.when(qi == nq - 1)
        def _():
            dk_ref[...] = dk_acc[...]
            dv_ref[...] = dv_acc[...].astype(dv_ref.dtype)

    hd = SB_HEADS * HEAD_DIM
    (dq, dk, dv), bufs = comm_call(
        body, name=name, grid=(SB_HEADS, nq),
        in_specs=[pl.BlockSpec((tq, HEAD_DIM), lambda h, i: (i, h)),
                  pl.BlockSpec((t, HEAD_DIM), lambda h, i: (0, SB_HEADS + h)),
                  pl.BlockSpec((t, HEAD_DIM), lambda h, i: (0, 2 * SB_HEADS + h)),
                  pl.BlockSpec((tq, HEAD_DIM), lambda h, i: (i, h))],
        out_specs=[pl.BlockSpec((tq, HEAD_DIM), lambda h, i: (i, h)),
                   pl.BlockSpec((t, HEAD_DIM), lambda h, i: (0, h)),
                   pl.BlockSpec((t, HEAD_DIM), lambda h, i: (0, h))],
        out_shape=[jax.ShapeDtypeStruct((t, hd), F32), jax.ShapeDtypeStruct((t, hd), F32),
                   jax.ShapeDtypeStruct((t, hd), BF16)],
        scratch_shapes=[pltpu.VMEM((nkb, tq, LANES), F32), pltpu.VMEM((tq, LANES), F32),
                        pltpu.VMEM((tq, LANES), F32), pltpu.VMEM((tq, HEAD_DIM), F32),
                        pltpu.VMEM((t, HEAD_DIM), F32), pltpu.VMEM((t, HEAD_DIM), F32)],
        inputs=[qk, qk, qkv_bf, do], sem=("arbitrary", "arbitrary"), ex=ex)
    return jnp.concatenate([dq, dk], axis=1), dv, bufs


def _qknorm_specs(qkv, gqk):
    t = qkv.shape[0]
    tr = _tile(t, 1024)
    data = [Data(qkv, (tr, HEAD_DIM), lambda o, i: (i, o))]
    params = [Param(gqk, (None, 1, HEAD_DIM), lambda o: (o // SB_HEADS, 0, 0), SB_HEADS)]
    outs = [Out((t, 2 * SB_HEADS * HEAD_DIM), BF16, (tr, HEAD_DIM), lambda o, i: (i, o))]
    return (2 * SB_HEADS, t // tr), data, params, outs


CONV_COLS = 128


def _shift_down(x, s, rows):
    if s == 0:
        return x
    return jnp.where(rows < s, 0.0, pltpu.roll(x, s, 0))


def _shift_up(x, s, rows):
    if s == 0:
        return x
    n = x.shape[0]
    return jnp.where(rows >= n - s, 0.0, pltpu.roll(x, n - s, 0))


def conv_fwd(xbc, w, b, name):
    t, c = xbc.shape
    tc = _tile(c, CONV_COLS)

    def body(x_ref, w_ref, b_ref, o_ref):
        x = x_ref[...]
        rows = lax.broadcasted_iota(jnp.int32, x.shape, 0)
        y = jnp.broadcast_to(b_ref[...], x.shape)
        for kk in range(SSD_CONV):
            y = y + w_ref[kk:kk + 1, :] * _shift_down(x, SSD_CONV - 1 - kk, rows)
        o_ref[...] = _silu(y)

    col = pl.BlockSpec((t, tc), lambda j: (0, j))
    return pl.pallas_call(
        body, name=name, grid=(c // tc,),
        in_specs=[col, pl.BlockSpec((SSD_CONV, tc), lambda j: (0, j)), pl.BlockSpec((1, tc), lambda j: (0, j))],
        out_specs=col, out_shape=jax.ShapeDtypeStruct((t, c), F32),
        compiler_params=_cparams("parallel"),
    )(xbc, w, b)


def conv_bwd(xbc, w, b, dact, name):
    t, c = xbc.shape
    tc = _tile(c, CONV_COLS)

    def body(x_ref, w_ref, b_ref, g_ref, dx_ref, dw_ref, db_ref):
        x = x_ref[...]
        rows = lax.broadcasted_iota(jnp.int32, x.shape, 0)
        shifted = [_shift_down(x, SSD_CONV - 1 - kk, rows) for kk in range(SSD_CONV)]
        y = jnp.broadcast_to(b_ref[...], x.shape)
        for kk in range(SSD_CONV):
            y = y + w_ref[kk:kk + 1, :] * shifted[kk]
        sig = 1.0 / (1.0 + jnp.exp(-y))
        dy = g_ref[...] * (sig * (1.0 + y * (1.0 - sig)))
        dx = jnp.zeros_like(x)
        for kk in range(SSD_CONV):
            dx = dx + w_ref[kk:kk + 1, :] * _shift_up(dy, SSD_CONV - 1 - kk, rows)
            dw_ref[kk:kk + 1, :] = jnp.sum(dy * shifted[kk], axis=0, keepdims=True)
        dx_ref[...] = dx.astype(dx_ref.dtype)
        db_ref[...] = jnp.sum(dy, axis=0, keepdims=True)

    col = pl.BlockSpec((t, tc), lambda j: (0, j))
    wspec = pl.BlockSpec((SSD_CONV, tc), lambda j: (0, j))
    bspec = pl.BlockSpec((1, tc), lambda j: (0, j))
    return pl.pallas_call(
        body, name=name, grid=(c // tc,),
        in_specs=[col, wspec, bspec, col], out_specs=[col, wspec, bspec],
        out_shape=[jax.ShapeDtypeStruct((t, c), BF16), jax.ShapeDtypeStruct((SSD_CONV, c), F32),
                   jax.ShapeDtypeStruct((1, c), F32)],
        compiler_params=_cparams("parallel"),
    )(xbc, w, b, dact)


@functools.partial(jax.custom_vjp, nondiff_argnums=(2,))
def _bdot(a, b, dims):
    return lax.dot_general(a.astype(BF16), b.astype(BF16), (dims, ((), ())), preferred_element_type=F32)


def _bdot_fwd(a, b, dims):
    return _bdot(a, b, dims), (a, b)


def _bdot_bwd(dims, res, g):
    a, b = res
    (ca,), (cb,) = dims
    fa, fb = 1 - ca, 1 - cb
    gb, ab, bb = g.astype(BF16), a.astype(BF16), b.astype(BF16)

    def dn(u, v, cu, cv):
        return lax.dot_general(u, v, (((cu,), (cv,)), ((), ())), preferred_element_type=F32)

    da = dn(gb, bb, 1, fb) if ca == 1 else dn(bb, gb, fb, 1)
    db = dn(ab, gb, fa, 0) if cb == 0 else dn(gb, ab, 0, fa)
    return da, db


_bdot.defvjp(_bdot_fwd, _bdot_bwd)
_NN, _NT, _TN = ((1,), (0,)), ((1,), (1,)), ((0,), (0,))


def _split3_dot(m_bf, x, dims):
    out = None
    for _ in range(3):
        t = x.astype(BF16)
        part = lax.dot_general(m_bf, t, (dims, ((), ())), preferred_element_type=F32)
        out = part if out is None else out + part
        x = x - t.astype(F32)
    return out


@jax.custom_vjp
def _tri_dot(m_bf, x):
    return _split3_dot(m_bf, x, _NN)


def _tri_dot_fwd(m_bf, x):
    return _tri_dot(m_bf, x), m_bf


def _tri_dot_bwd(m_bf, g):
    return jnp.zeros_like(m_bf), _split3_dot(m_bf, g, _TN)


_tri_dot.defvjp(_tri_dot_fwd, _tri_dot_bwd)


def _ssd_chunk(x, dt, bm, cm, sprev, alog, dskip):
    n = x.shape[0]
    a = -jnp.exp(alog)
    da = dt * a
    row = lax.broadcasted_iota(jnp.int32, (n, n), 0)
    col = lax.broadcasted_iota(jnp.int32, (n, n), 1)
    causal = row >= col
    acum_l = _tri_dot(causal.astype(BF16), jnp.broadcast_to(da, (n, n)))
    acum_s = acum_l.T
    decay = jnp.exp(jnp.where(causal, acum_l - acum_s, -1e30))
    acum = jnp.sum(acum_l, axis=1, keepdims=True) * (1.0 / n)
    atot = jnp.sum(da, axis=0, keepdims=True)
    xs = x * dt
    y = _bdot(_bdot(cm, bm, _NT) * decay, xs, _NN)
    y = y + jnp.exp(acum) * _bdot(cm, sprev, _NT)
    y = y + dskip * x
    snew = jnp.exp(atot) * sprev + _bdot(xs * jnp.exp(atot - acum), bm, _TN)
    return y, snew


def _ssd_specs(t, rev):
    nc = t // SSD_CHUNK
    cidx = (lambda c: nc - 1 - c) if rev else (lambda c: c)
    x_spec = pl.BlockSpec((SSD_HPG, SSD_CHUNK, SSD_HEAD_DIM), lambda g, c: (g, cidx(c), 0))
    dt_spec = pl.BlockSpec((SSD_HPG, SSD_CHUNK, 1), lambda g, c: (g, cidx(c), 0))
    bc_spec = pl.BlockSpec((None, SSD_CHUNK, SSD_STATE), lambda g, c: (g, cidx(c), 0))
    sc_spec = pl.BlockSpec((SSD_HPG, 1, 1), lambda g, c: (g, 0, 0))
    st_spec = pl.BlockSpec((SSD_HPG, None, SSD_HEAD_DIM, SSD_STATE), lambda g, c: (g, cidx(c), 0, 0))
    return nc, x_spec, dt_spec, bc_spec, sc_spec, st_spec


def ssd_scan_fwd(x, dt, bm, cm, alog, dskip, name, ex=None):
    t = x.shape[1]
    nc, x_spec, dt_spec, bc_spec, sc_spec, st_spec = _ssd_specs(t, False)

    def body(x_ref, dt_ref, b_ref, c_ref, al_ref, d_ref, y_ref, sin_ref, state):
        @pl.when(pl.program_id(1) == 0)
        def _():
            state[...] = jnp.zeros_like(state)

        bm_, cm_ = b_ref[...], c_ref[...]

        def head(r, carry):
            sprev = state[r]
            sin_ref[r] = sprev
            y, snew = _ssd_chunk(x_ref[r], dt_ref[r], bm_, cm_, sprev, al_ref[r], d_ref[r])
            y_ref[r] = y
            state[r] = snew
            return carry

        lax.fori_loop(0, SSD_HPG, head, 0, unroll=SSD_HPG)

    (y, sin), bufs = comm_call(
        body, name=name, grid=(SSD_GROUPS, nc),
        in_specs=[x_spec, dt_spec, bc_spec, bc_spec, sc_spec, sc_spec],
        out_specs=[x_spec, st_spec],
        out_shape=[jax.ShapeDtypeStruct(x.shape, F32),
                   jax.ShapeDtypeStruct((SSD_HEADS, nc, SSD_HEAD_DIM, SSD_STATE), F32)],
        scratch_shapes=[pltpu.VMEM((SSD_HPG, SSD_HEAD_DIM, SSD_STATE), F32)],
        inputs=[x, dt, bm, cm, alog, dskip], sem=("parallel", "arbitrary"), ex=ex)
    return y, sin, bufs


def ssd_scan_bwd(x, dt, bm, cm, alog, dskip, sin, dy, name, ex=None):
    t = x.shape[1]
    nc, x_spec, dt_spec, bc_spec, sc_spec, st_spec = _ssd_specs(t, True)

    def body(x_ref, dt_ref, b_ref, c_ref, al_ref, d_ref, sin_ref, dy_ref,
             dx_ref, ddt_ref, db_ref, dc_ref, dal_ref, dd_ref, dstate):
        @pl.when(pl.program_id(1) == 0)
        def _():
            dstate[...] = jnp.zeros_like(dstate)
            dal_ref[...] = jnp.zeros_like(dal_ref)
            dd_ref[...] = jnp.zeros_like(dd_ref)

        db_ref[...] = jnp.zeros_like(db_ref)
        dc_ref[...] = jnp.zeros_like(dc_ref)
        bm_, cm_ = b_ref[...], c_ref[...]

        def head(r, carry):
            _, vjp = jax.vjp(_ssd_chunk, x_ref[r], dt_ref[r], bm_, cm_, sin_ref[r], al_ref[r], d_ref[r])
            dx, ddt, dbm, dcm, dsp, dal, dd = vjp((dy_ref[r], dstate[r]))
            dx_ref[r] = dx
            ddt_ref[r] = ddt
            db_ref[...] += dbm
            dc_ref[...] += dcm
            dstate[r] = dsp
            dal_ref[r] += dal
            dd_ref[r] += dd
            return carry

        lax.fori_loop(0, SSD_HPG, head, 0, unroll=SSD_HPG)

    return comm_call(
        body, name=name, grid=(SSD_GROUPS, nc),
        in_specs=[x_spec, dt_spec, bc_spec, bc_spec, sc_spec, sc_spec, st_spec, x_spec],
        out_specs=[x_spec, dt_spec, bc_spec, bc_spec, sc_spec, sc_spec],
        out_shape=[jax.ShapeDtypeStruct(x.shape, F32), jax.ShapeDtypeStruct(dt.shape, F32),
                   jax.ShapeDtypeStruct(bm.shape, F32), jax.ShapeDtypeStruct(cm.shape, F32),
                   jax.ShapeDtypeStruct(alog.shape, F32), jax.ShapeDtypeStruct(dskip.shape, F32)],
        scratch_shapes=[pltpu.VMEM((SSD_HPG, SSD_HEAD_DIM, SSD_STATE), F32)],
        inputs=[x, dt, bm, cm, alog, dskip, sin, dy], sem=("arbitrary", "arbitrary"), ex=ex)


def _to_heads(a, n, w):
    t = a.shape[0]
    return a.reshape(t, n, w).transpose(1, 0, 2)


def _from_heads(a):
    n, t, w = a.shape
    return a.transpose(1, 0, 2).reshape(t, n * w)


def _add(acc, h):
    return (acc + h,)


ATTN_FWD_US, ATTN_BWD_US, SCAN_FWD_US, SCAN_BWD_US = 700, 1600, 560, 1280


def sb_fwd(h, hn, w, j, tag):
    gqk = w["sb_gqk"][j]
    qkv, qkv_bf = w.mm_fwd(hn, "sb_w_qkv", j, "nn", f"{tag}_qkv", out_dtypes=(F32, BF16),
                           epilogue=lambda acc: (acc, acc))
    grid, data, params, outs = _qknorm_specs(qkv, gqk)
    qk = blockwise_fwd(_rmsnorm_f, grid, data, params, outs, f"{tag}_qknorm")[0]
    o, bufs = attn_fwd(qk, qkv_bf, f"{tag}_attn", w.host_fwd(ATTN_FWD_US))
    w.done_fwd(bufs)
    h1 = w.mm_fwd(o, "sb_w_o", j, "nn", f"{tag}_out", epilogue=_add, extras=(h,))
    return h1, (qkv, qkv_bf, qk, o)


def sb_bwd(g_bf, hn, saved, w, j, tag):
    qkv, qkv_bf, qk, o = saved
    gqk = w["sb_gqk"][j]
    w.add_grad("sb_w_o", j, w.mm_bwd(o, g_bf, "tn", f"{tag}_dwo", out_dtypes=(BF16,)))
    do = w.mm_bwd(g_bf, w["sb_w_o"], "nt", f"{tag}_do", b_layer=j, out_dtypes=(BF16,))
    dqk, dv, bufs = attn_bwd(qk, qkv_bf, do, f"{tag}_attn_bwd", w.host_bwd(ATTN_BWD_US))
    w.recv.update(bufs)
    grid, data, params, outs = _qknorm_specs(qkv, gqk)
    (draw,), (dgqk,) = blockwise_bwd(_rmsnorm_f, grid, data, params, outs, [dqk], f"{tag}_qknorm_bwd",
                                     grad_dtypes=[BF16])
    dqkv = jnp.concatenate([draw[:, :2 * SB_HEADS * HEAD_DIM], dv], axis=1)
    w.add_grad("sb_w_qkv", j, w.mm_bwd(hn, dqkv, "tn", f"{tag}_dwqkv", out_dtypes=(BF16,)))
    dhn = w.mm_bwd(dqkv, w["sb_w_qkv"], "nt", f"{tag}_dhn", b_layer=j)
    return dhn, {"sb_gqk": dgqk}


def _gm_act_specs(p, gv):
    t = p.shape[0]
    tr = _row_tile(t)
    d = D_MODEL
    data = [Data(p, (tr, d), lambda o, i: (i, 0)), Data(p, (tr, d), lambda o, i: (i, 1))]
    params = [Param(gv, (1, d), lambda o: (0, 0), 1)]
    outs = [Out((t, d), F32, (tr, d), lambda o, i: (i, 0)), Out((t, d), BF16, (tr, d), lambda o, i: (i, 0))]
    return (1, t // tr), data, params, outs


def _gm_gate_specs(u, vn, ws, bs):
    t = u.shape[0]
    rows = _tile(t, GM_STEP_CHUNKS * GM_CHUNK)
    blk = (rows, LANES)
    data = [Data(u, blk, lambda o, i: (i, o)), Data(vn, blk, lambda o, i: (i, o))]
    params = [Param(ws, (None, GM_CHUNK, GM_CHUNK), lambda o: (o, 0, 0), 1),
              Param(bs, (None, GM_CHUNK, 1), lambda o: (o, 0, 0), 1)]
    outs = [Out((t, D_MODEL), BF16, blk, lambda o, i: (i, o))]
    return (GM_GROUPS, t // rows), data, params, outs


def gm_fwd(h, hn, w, j, tag):
    p = w.mm_fwd(hn, "gm_w_in", j, "nn", f"{tag}_in")
    grid, data, params, outs = _gm_act_specs(p, w["gm_v_norm_g"])
    u, vn = blockwise_fwd(_gm_act_f, grid, data, params, outs, f"{tag}_act")
    grid, data, params, outs = _gm_gate_specs(u, vn, w["gm_w_s"], w["gm_b_s"])
    y = blockwise_fwd(_gm_gate_f, grid, data, params, outs, f"{tag}_gate")[0]
    h1 = w.mm_fwd(y, "gm_w_o", j, "nn", f"{tag}_out", epilogue=_add, extras=(h,))
    return h1, (p, u, vn, y)


def gm_bwd(g_bf, hn, saved, w, j, tag):
    p, u, vn, y = saved
    w.add_grad("gm_w_o", j, w.mm_bwd(y, g_bf, "tn", f"{tag}_dwo", out_dtypes=(BF16,)))
    dy = w.mm_bwd(g_bf, w["gm_w_o"], "nt", f"{tag}_dy", b_layer=j)
    grid, data, params, outs = _gm_gate_specs(u, vn, w["gm_w_s"], w["gm_b_s"])
    (du, dvn), (dws, dbs) = blockwise_bwd(_gm_gate_f, grid, data, params, outs, [dy], f"{tag}_gate_bwd",
                                          grad_dtypes=[F32, F32])
    grid, data, params, outs = _gm_act_specs(p, w["gm_v_norm_g"])
    (dpu, dpv), (dgv,) = blockwise_bwd(_gm_act_f, grid, data, params, outs, [du, dvn], f"{tag}_act_bwd",
                                       grad_dtypes=[BF16, BF16])
    dp = jnp.concatenate([dpu[:, :D_MODEL], dpv[:, D_MODEL:]], axis=1)
    w.add_grad("gm_w_in", j, w.mm_bwd(hn, dp, "tn", f"{tag}_dwin", out_dtypes=(BF16,)))
    dhn = w.mm_bwd(dp, w["gm_w_in"], "nt", f"{tag}_dhn", b_layer=j)
    return dhn, {"gm_v_norm_g": dgv, "gm_w_s": dws, "gm_b_s": dbs}


def _ssd_gate_specs(y, z, g):
    t = y.shape[0]
    tr = _row_tile(t)
    gw = SSD_INNER // SSD_GROUPS
    blk = (tr, gw)
    data = [Data(y, blk, lambda o, i: (i, o)), Data(z, blk, lambda o, i: (i, o))]
    params = [Param(g, (None, 1, gw), lambda o: (o, 0, 0), 1)]
    outs = [Out((t, SSD_INNER), BF16, blk, lambda o, i: (i, o))]
    return (SSD_GROUPS, t // tr), data, params, outs


def _dt_specs(raw, bias):
    t = raw.shape[0]
    tr = _row_tile(t)
    data = [Data(raw, (tr, DT_PAD), lambda o, i: (i, 0))]
    params = [Param(bias, (1, DT_PAD), lambda o: (0, 0), 1)]
    outs = [Out((t, DT_PAD), F32, (tr, DT_PAD), lambda o, i: (i, 0))]
    return (1, t // tr), data, params, outs


def ssd_fwd(h, hn, w, tag):
    w.need("ssd_w_in", None)
    z = w.mm_fwd(hn, "ssd_wz", None, "nn", f"{tag}_inz")
    xbc = w.mm_fwd(hn, "ssd_wxbc", None, "nn", f"{tag}_inx")
    raw = w.mm_fwd(hn, "ssd_wdt", None, "nn", f"{tag}_indt")
    act = conv_fwd(xbc, w["ssd_conv_w"], w["ssd_conv_b"], f"{tag}_conv")
    grid, data, params, outs = _dt_specs(raw, w["ssd_dt_bias"])
    dt = blockwise_fwd(_dt_f, grid, data, params, outs, f"{tag}_dt")[0]
    xh = _to_heads(act[:, :SSD_INNER], SSD_HEADS, SSD_HEAD_DIM)
    bh = _to_heads(act[:, SSD_INNER:SSD_INNER + SSD_BC], SSD_GROUPS, SSD_STATE)
    ch = _to_heads(act[:, SSD_INNER + SSD_BC:], SSD_GROUPS, SSD_STATE)
    dth = _to_heads(dt[:, :SSD_HEADS], SSD_HEADS, 1)
    yh, sin, bufs = ssd_scan_fwd(xh, dth, bh, ch, w["ssd_a_log"], w["ssd_d"], f"{tag}_scan",
                                 w.host_fwd(SCAN_FWD_US))
    w.done_fwd(bufs)
    y = _from_heads(yh)
    grid, data, params, outs = _ssd_gate_specs(y, z, w["ssd_norm_g"])
    yn = blockwise_fwd(_ssd_gate_f, grid, data, params, outs, f"{tag}_gate")[0]
    h1 = w.mm_fwd(yn, "ssd_w_o", 0, "nn", f"{tag}_out", epilogue=_add, extras=(h,))
    return h1, (z, xbc, raw, xh, bh, ch, dth, sin, y, yn)


def ssd_bwd(g_bf, hn, saved, w, tag):
    z, xbc, raw, xh, bh, ch, dth, sin, y, yn = saved
    w.add_grad("ssd_w_o", 0, w.mm_bwd(yn, g_bf, "tn", f"{tag}_dwo", out_dtypes=(BF16,)))
    dyn = w.mm_bwd(g_bf, w["ssd_w_o"], "nt", f"{tag}_dyn", b_layer=0)
    grid, data, params, outs = _ssd_gate_specs(y, z, w["ssd_norm_g"])
    (dy, dz), (dng,) = blockwise_bwd(_ssd_gate_f, grid, data, params, outs, [dyn], f"{tag}_gate_bwd",
                                     grad_dtypes=[F32, BF16])
    (dxh, ddth, dbh, dch, dal, ddk), bufs = ssd_scan_bwd(
        xh, dth, bh, ch, w["ssd_a_log"], w["ssd_d"], sin, _to_heads(dy, SSD_HEADS, SSD_HEAD_DIM),
        f"{tag}_scan_bwd", w.host_bwd(SCAN_BWD_US))
    w.recv.update(bufs)
    dact = jnp.concatenate([_from_heads(dxh), _from_heads(dbh), _from_heads(dch)], axis=1)
    dxbc, dcw, dcb = conv_bwd(xbc, w["ssd_conv_w"], w["ssd_conv_b"], dact, f"{tag}_conv_bwd")
    ddt = jnp.pad(_from_heads(ddth), ((0, 0), (0, DT_PAD - SSD_HEADS)))
    grid, data, params, outs = _dt_specs(raw, w["ssd_dt_bias"])
    (draw,), (dbias,) = blockwise_bwd(_dt_f, grid, data, params, outs, [ddt], f"{tag}_dt_bwd",
                                      grad_dtypes=[BF16])
    d_wz = w.mm_bwd(hn, dz, "tn", f"{tag}_dwz", out_dtypes=(BF16,))
    d_wx = w.mm_bwd(hn, dxbc, "tn", f"{tag}_dwx", out_dtypes=(BF16,))
    d_wdt = w.mm_bwd(hn, draw, "tn", f"{tag}_dwdt", out_dtypes=(BF16,))
    d_win = jnp.concatenate([d_wz, d_wx, d_wdt[:, :SSD_HEADS]], axis=1)
    d_win = d_win.reshape(D_MODEL, N_DEV, SSD_PROJ // N_DEV).transpose(1, 0, 2)
    w.add_grad("ssd_w_in", 0, d_win)
    dhn = w.mm_bwd(dz, w["ssd_wz"], "nt", f"{tag}_dhn_z")
    dhn = w.mm_bwd(dxbc, w["ssd_wxbc"], "nt", f"{tag}_dhn_x", epilogue=_add, extras=(dhn,))
    dhn = w.mm_bwd(draw, w["ssd_wdt"], "nt", f"{tag}_dhn_dt", epilogue=_add, extras=(dhn,))
    return dhn, {"ssd_conv_w": dcw, "ssd_conv_b": dcb, "ssd_dt_bias": dbias[:, :SSD_HEADS], "ssd_a_log": dal,
                 "ssd_d": ddk, "ssd_norm_g": dng}


def _relu2(acc):
    return (acc, jnp.square(jnp.maximum(acc, 0.0)))


def _relu2_bwd(acc, a):
    return (acc * (2.0 * jnp.maximum(a, 0.0)),)


def device_step(x, target, w):
    h = x
    saved = []
    for i in range(DEPTH):
        kind, j = i % N_MIXERS, i // N_MIXERS
        tag = f"l{i}"
        hn = rmsnorm_fwd(h, w["norm_mix_g"][i], f"{tag}_norm_mix")
        if kind == 0:
            h1, ms = sb_fwd(h, hn, w, j, f"{tag}_sb")
        elif kind == 1:
            h1, ms = gm_fwd(h, hn, w, j, f"{tag}_gm")
        else:
            h1, ms = ssd_fwd(h, hn, w, f"{tag}_ssd")
        hm = rmsnorm_fwd(h1, w["norm_mlp_g"][i], f"{tag}_norm_mlp")
        a, r = w.mm_fwd(hm, "mlp_w_in", i, "nn", f"{tag}_mlp_in", out_dtypes=(F32, BF16), epilogue=_relu2)
        h2 = w.mm_fwd(r, "mlp_w_out", i, "nn", f"{tag}_mlp_out", epilogue=_add, extras=(h1,))
        saved.append((h, hn, ms, h1, hm, a, r))
        h = h2
    dh, g_bf, loss = loss_call(h, target)

    small = {"norm_mix_g": [None] * DEPTH, "norm_mlp_g": [None] * DEPTH, "sb_gqk": [None] * 2}
    for i in reversed(range(DEPTH)):
        kind, j = i % N_MIXERS, i // N_MIXERS
        tag = f"l{i}"
        h0, hn, ms, h1, hm, a, r = saved[i]
        w.add_grad("mlp_w_out", i, w.mm_bwd(r, g_bf, "tn", f"{tag}_mlp_dwout", out_dtypes=(BF16,)))
        da = w.mm_bwd(g_bf, w["mlp_w_out"], "nt", f"{tag}_mlp_da", b_layer=i, out_dtypes=(BF16,),
                      epilogue=_relu2_bwd, extras=(a,))
        w.add_grad("mlp_w_in", i, w.mm_bwd(hm, da, "tn", f"{tag}_mlp_dwin", out_dtypes=(BF16,)))
        dhm = w.mm_bwd(da, w["mlp_w_in"], "nt", f"{tag}_mlp_dhm", b_layer=i)
        dh1, g_bf, small["norm_mlp_g"][i] = rmsnorm_bwd(h1, w["norm_mlp_g"][i], dhm, dh, f"{tag}_norm_mlp_bwd")
        if kind == 0:
            dhn, mg = sb_bwd(g_bf, hn, ms, w, j, f"{tag}_sb")
            small["sb_gqk"][j] = mg.pop("sb_gqk")
        elif kind == 1:
            dhn, mg = gm_bwd(g_bf, hn, ms, w, j, f"{tag}_gm")
        else:
            dhn, mg = ssd_bwd(g_bf, hn, ms, w, f"{tag}_ssd")
        small.update(mg)
        dh, g_bf, small["norm_mix_g"][i] = rmsnorm_bwd(h0, w["norm_mix_g"][i], dhn, dh1, f"{tag}_norm_mix_bwd",
                                                       with_bf16=i > 0)
    dcw = small["ssd_conv_w"].reshape(SSD_CONV, N_DEV, SMALL_W).transpose(1, 0, 2)
    dcb = small["ssd_conv_b"].reshape(N_DEV, 1, SMALL_W)
    dng = jnp.pad(small["ssd_norm_g"], ((0, 0), (0, 0), (0, SMALL_W - SSD_INNER // N_DEV)))
    w.add_grad("small", None, jnp.concatenate([dcw, dcb, dng, jnp.zeros((N_DEV, 2, SMALL_W), F32)], axis=1))
    w.add_grad("repl", None, _pack(_repl_grads(small), w.shapes["repl"][0]))
    return loss, dh


def _my_pos():
    return lax.axis_index("x"), lax.axis_index("y"), lax.axis_index("c")


def _lin(p):
    return 4 * p[0] + 2 * p[1] + p[2]


_FLIPS = [(0, 0, 1), (0, 1, 0), (0, 1, 1), (1, 0, 0), (1, 0, 1), (1, 1, 0), (1, 1, 1)]
_SIBLING = (0, 0, 1)
_AG_FLIPS = [_SIBLING, (1, 0, 0), (0, 1, 0), (1, 1, 0)]


def _flip(pos, f):
    return tuple((1 - p) if b else p for p, b in zip(pos, f))


def _other_chip(pos, j):
    x, y = pos[0], pos[1]
    return [(1 - x, y), (x, 1 - y), (1 - x, 1 - y)][j]


class Exchange:
    def __init__(self):
        self.sources, self.buffers, self.items, self.locals = [], [], [], []

    def source(self, array):
        self.sources.append(array)
        return len(self.sources) - 1

    def buffer(self, name, bufs, shape, dtype):
        if name not in [b[0] for b in self.buffers]:
            self.buffers.append((name, bufs.get(name), jax.ShapeDtypeStruct(shape, dtype)))

    def n_copies(self):
        return sum(len(flips) for _, _, flips in self.items)


def _exchange_fns(ex, srcs, bufs, send_sems, recv_sems, local_sems):
    def local(k):
        sf, df = ex.locals[k]
        me = _my_pos()
        return pltpu.make_async_copy(sf(srcs, bufs, me, me), df(bufs, me), local_sems.at[k])

    def remote(sending):
        me = _my_pos()
        k = 0
        for sf, df, flips in ex.items:
            for f in flips:
                peer = _flip(me, f)
                yield pltpu.make_async_remote_copy(
                    src_ref=sf(srcs, bufs, me, peer), dst_ref=df(bufs, me if sending else peer),
                    send_sem=send_sems.at[k], recv_sem=recv_sems.at[k], device_id=peer, device_id_type=MESH)
                k += 1

    def start():
        for k in range(len(ex.locals)):
            local(k).start()
        for cp in remote(True):
            cp.start()

    def wait():
        arrivals = list(remote(False))
        for cp in arrivals:
            cp.wait_recv()
        for cp in arrivals:
            cp.wait_send()
        for k in range(len(ex.locals)):
            local(k).wait()

    return start, wait


def comm_call(core, *, name, grid, in_specs, out_specs, out_shape, scratch_shapes, inputs, sem, ex=None):
    n_in, n_out, n_scr = len(in_specs), len(out_specs), len(scratch_shapes)
    if ex is None:
        res = pl.pallas_call(core, name=name, grid=grid, in_specs=in_specs, out_specs=out_specs,
                             out_shape=out_shape, scratch_shapes=scratch_shapes,
                             compiler_params=_cparams(*sem))(*inputs)
        return list(res), {}
    existing = [(k, b[1]) for k, b in enumerate(ex.buffers) if b[1] is not None]
    names = [b[0] for b in ex.buffers]
    n_src, n_old, n_buf = len(ex.sources), len(existing), len(names)

    def body(*refs):
        cuts = [n_in, n_src, n_old, n_out, n_buf, n_scr, 3]
        parts, pos = [], 0
        for c in cuts:
            parts.append(refs[pos:pos + c])
            pos += c
        core_in, srcs, _, core_out, buf_refs, core_scr, sems = parts
        start, wait = _exchange_fns(ex, srcs, dict(zip(names, buf_refs)), *sems)
        if not grid:
            start()
            wait()
            return
        ids = [pl.program_id(d) for d in range(len(grid))]
        pl.when(functools.reduce(jnp.logical_and, [i == 0 for i in ids]))(start)
        core(*core_in, *core_out, *core_scr)
        pl.when(functools.reduce(jnp.logical_and, [i == g - 1 for i, g in zip(ids, grid)]))(wait)

    dma = pltpu.SemaphoreType.DMA
    res = pl.pallas_call(
        body, name=name, **({"grid": grid} if grid else {}),
        in_specs=list(in_specs) + [HBM_SPEC] * (n_src + n_old),
        out_specs=list(out_specs) + [HBM_SPEC] * n_buf,
        out_shape=list(out_shape) + [b[2] for b in ex.buffers],
        scratch_shapes=list(scratch_shapes) + [dma((max(ex.n_copies(), 1),)), dma((max(ex.n_copies(), 1),)),
                                               dma((max(len(ex.locals), 1),))],
        input_output_aliases={n_in + n_src + e: n_out + k for e, (k, _) in enumerate(existing)},
        compiler_params=_cparams(*(["arbitrary"] * len(grid))),
    )(*inputs, *ex.sources, *[b for _, b in existing])
    return list(res[:n_out]), dict(zip(names, res[n_out:]))


def run_exchange(ex, name):
    return comm_call(None, name=name, grid=(), in_specs=[], out_specs=[], out_shape=[], scratch_shapes=[],
                     inputs=[], sem=(), ex=ex)[1]


def _gathered_shape(shard, kind):
    s = shard.shape
    if kind == "col":
        return s[:2] + (s[2] * N_DEV,)
    if kind == "row":
        return (s[0], s[1] * N_DEV, s[2])
    return (N_DEV,) + s


def _n_pieces(nbytes, rows, target):
    n = 1
    while nbytes > n * target and rows % (32 * n) == 0 and n < 8:
        n *= 2
    return n


def _ag_part(buf, tn, layer, r0, nr, who, shards):
    kind, shard, idx = SHARD_KIND[tn], shards[tn], _lin(who)
    if kind == "col":
        return buf.at[layer, pl.ds(r0, nr), pl.ds(pl.multiple_of(idx * shard.shape[2], LANES), shard.shape[2])]
    if kind == "row":
        return buf.at[layer, pl.ds(pl.multiple_of(idx * shard.shape[1] + r0, 16), nr), :]
    return buf.at[idx] if tn == "small" else buf.at[idx, 0, pl.ds(r0, nr), :]


def _ag_items(ex, piece, phase, shards, bufs):
    tn, layer, r0, nr, _ = piece
    shard = shards[tn]
    ex.buffer(tn, bufs, _gathered_shape(shard, SHARD_KIND[tn]), shard.dtype)

    def part(b, who):
        return _ag_part(b[tn], tn, layer, r0, nr, who, shards)

    if phase == "A":
        def src(s, b, me, peer, si=ex.source(shard)):
            return s[si] if tn == "small" else s[si].at[0 if layer is None else layer, pl.ds(r0, nr), :]

        ex.items.append((src, part, _AG_FLIPS))
        ex.locals.append((src, part))
        return
    for j in range(3):
        def held(who, j=j):
            return (*_other_chip(who, j), who[2])

        ex.items.append((lambda s, b, me, peer, held=held: part(b, held(me)),
                         lambda b, sender, held=held: part(b, held(sender)), [_SIBLING]))


def _rs_items(ex, piece, bufs, shapes):
    tn, layer, r0, nr, _, g = piece
    kind, shape = SHARD_KIND[tn], shapes[tn]
    ex.buffer(tn, bufs, (N_DEV,) + tuple(shape), g.dtype)

    def src(s, b, me, peer, si=ex.source(g)):
        idx = _lin(peer)
        if kind == "all":
            return s[si]
        if tn == "small":
            return s[si].at[idx]
        if kind == "slot":
            return s[si].at[idx, pl.ds(r0, nr), :]
        if kind == "col":
            return s[si].at[pl.ds(r0, nr), pl.ds(pl.multiple_of(idx * shape[2], LANES), shape[2])]
        return s[si].at[pl.ds(pl.multiple_of(idx * shape[1] + r0, 16), nr), :]

    def dst(b, sender):
        idx = _lin(sender)
        return b[tn].at[idx] if layer is None else b[tn].at[idx, layer, pl.ds(r0, nr), :]

    ex.items.append((src, dst, _FLIPS))
    ex.locals.append((src, dst))


AG_BYTES_PER_US = 22e3
RS_BYTES_PER_US = 75e3
AG_PIECE_BYTES = 2.3e6
RS_PIECE_BYTES = 4.5e6
MXU_FLOPS_PER_US = 8.6e8
MATMUL_HOST_MIN_US = {"fwd": 0, "bwd": 110}
MATMUL_HOST_SHARE = 1.0


class Traffic(dict):
    def __init__(self, small_weights, shards, shapes):
        super().__init__(small_weights)
        self.shards, self.shapes, self.recv = shards, shapes, {}
        self.wait_a, self.wait_b, self.riding, self.unfinished = [], [], ([], []), {}
        for group in AG_GROUPS:
            for tn, layer in group:
                shard = shards[tn]
                rows = shard.shape[-2]
                nbytes = rows * shard.shape[-1] * shard.dtype.itemsize
                n = 1 if tn == "small" else _n_pieces(nbytes, rows, AG_PIECE_BYTES)
                self.unfinished[(tn, layer)] = n
                self.wait_a += [(tn, layer, k * (rows // n), rows // n, nbytes / n) for k in range(n)]
        self.grads, self.grads_left, self.n_calls = [], {}, 0

    def host_fwd(self, us, exclude=(), only=None):
        ex, budget = Exchange(), us * AG_BYTES_PER_US
        ok = (lambda p: p[:2] == only) if only else (lambda p: p[0] not in exclude)
        take_b = [p for p in self.wait_b if ok(p)]
        take_a = []
        for p in [p for p in self.wait_a if ok(p)]:
            if p[4] > budget and not only:
                break
            take_a.append(p)
            budget -= p[4]
        for p in take_b:
            self.wait_b.remove(p)
            _ag_items(ex, p, "B", self.shards, self)
        for p in take_a:
            self.wait_a.remove(p)
            _ag_items(ex, p, "A", self.shards, self)
        self.riding = (take_a, take_b)
        return ex if ex.items else None

    def done_fwd(self, bufs):
        self.update(bufs)
        take_a, take_b = self.riding
        self.wait_b += take_a
        for p in take_b:
            self.unfinished[p[:2]] -= 1
        self.riding = ([], [])

    def need(self, tn, layer):
        while self.unfinished.get((tn, layer), 0) > 0:
            self.n_calls += 1
            self.done_fwd(run_exchange(self.host_fwd(0, only=(tn, layer)), f"gather_now{self.n_calls}"))
        if tn == "ssd_w_in" and "ssd_wz" not in self:
            self.need("small", None)
            _ssd_weights(self)

    def mm_fwd(self, a, tn, layer, form, name, **kw):
        self.need(tn, layer)
        ex = self.host_fwd(_host_us(a, self[tn], form, "fwd"), exclude=(tn,))
        if ex is None:
            return matmul(a, self[tn], form, name, b_layer=layer, **kw)
        out, bufs = matmul(a, self[tn], form, name, b_layer=layer, ex=ex, **kw)
        self.done_fwd(bufs)
        return out

    def add_grad(self, tn, layer, g):
        rows = self.shapes[tn][-2]
        nbytes = g.size * g.dtype.itemsize
        n = 1 if layer is None else _n_pieces(nbytes, rows, RS_PIECE_BYTES)
        self.grads += [(tn, layer, k * (rows // n), rows // n, nbytes / n, g) for k in range(n)]
        self.grads_left[tn] = self.grads_left.get(tn, 0) + n

    def host_bwd(self, us, exclude=(), only=None):
        ex, budget = Exchange(), us * RS_BYTES_PER_US
        for p in [p for p in self.grads if (p[0] == only if only else p[0] not in exclude)]:
            if p[4] > budget and not only:
                continue
            self.grads.remove(p)
            self.grads_left[p[0]] -= 1
            _rs_items(ex, p, self.recv, self.shapes)
            budget -= p[4]
        return ex if ex.items else None

    def need_grad(self, tn):
        if self.grads_left.get(tn, 0) > 0:
            self.n_calls += 1
            self.recv.update(run_exchange(self.host_bwd(0, only=tn), f"scatter_now{self.n_calls}"))

    def mm_bwd(self, a, b, form, name, **kw):
        ex = self.host_bwd(_host_us(a, b, form, "bwd"))
        if ex is None:
            return matmul(a, b, form, name, **kw)
        out, bufs = matmul(a, b, form, name, ex=ex, **kw)
        self.recv.update(bufs)
        return out


def _host_us(a, b, form, direction):
    k = a.shape[0] if form == "tn" else a.shape[1]
    us = 2.0 * a.size // k * b.shape[-2] * b.shape[-1] / MXU_FLOPS_PER_US
    return MATMUL_HOST_SHARE * us if us >= MATMUL_HOST_MIN_US[direction] else 0.0


_C1 = 1.0 - ADAM_B1 ** ADAM_STEP
_C2 = 1.0 - ADAM_B2 ** ADAM_STEP
ADAMW_BYTES_PER_ELEMENT = 7 * 4 + N_DEV * 2
HBM_BYTES_PER_US = 2.9e6


def adamw(wt, m, v, slots, name, ex=None):
    nl, r, c = wt.shape
    tr = _tile(r, 128)

    def body(w_ref, m_ref, v_ref, s_ref, g_ref, d_ref, nm_ref, nv_ref):
        g = s_ref[0].astype(F32)
        for k in range(1, N_DEV):
            g = g + s_ref[k].astype(F32)
        m_new = ADAM_B1 * m_ref[...] + (1.0 - ADAM_B1) * g
        v_new = ADAM_B2 * v_ref[...] + (1.0 - ADAM_B2) * jnp.square(g)
        m_hat = m_new / _C1
        v_hat = v_new / _C2
        g_ref[...] = g
        d_ref[...] = -ADAM_LR * (m_hat / (jnp.sqrt(v_hat) + ADAM_EPS) + ADAM_WD * w_ref[...])
        nm_ref[...] = m_new
        nv_ref[...] = v_new

    blk = pl.BlockSpec((None, tr, c), lambda l, i: (l, i, 0))
    sblk = pl.BlockSpec((N_DEV, None, tr, c), lambda l, i: (0, l, i, 0))
    return comm_call(
        body, name=name, grid=(nl, r // tr), in_specs=[blk, blk, blk, sblk], out_specs=[blk] * 4,
        out_shape=[jax.ShapeDtypeStruct(wt.shape, F32)] * 4, scratch_shapes=[],
        inputs=[wt, m, v, slots], sem=("parallel", "parallel"), ex=ex)


def _pack(arrays, rows):
    flat = jnp.concatenate([a.reshape(-1).astype(F32) for a in arrays])
    return jnp.pad(flat, (0, rows * LANES - flat.shape[0])).reshape(rows, LANES)


def _unpack(packed, shapes):
    flat = packed.reshape(-1)
    out, off = [], 0
    for s in shapes:
        sz = math.prod(s)
        out.append(flat[off:off + sz].reshape(s))
        off += sz
    return out


SMALL_W = 768


def _pack_small_shard(conv_w, conv_b, norm_g):
    ng = jnp.pad(norm_g, ((0, 0), (0, SMALL_W - norm_g.shape[1])))
    return jnp.concatenate([conv_w[0], conv_b, ng, jnp.zeros((2, SMALL_W), F32)], axis=0)


def _unpack_small_shard(p):
    return p[0:4][None], p[4:5], p[5:6, :SSD_INNER // N_DEV]


REPL = ["norm_mix_g", "norm_mlp_g", "sb_q_norm_g", "sb_k_norm_g", "gm_v_norm_g", "gm_w_s", "gm_b_s",
        "ssd_dt_bias", "ssd_a_log", "ssd_d"]
BIG = ["sb_w_qkv", "sb_w_o", "gm_w_in", "gm_w_o", "ssd_w_in", "ssd_w_o", "mlp_w_in", "mlp_w_out"]
BIG_KIND = {"sb_w_qkv": "col", "sb_w_o": "row", "gm_w_in": "col", "gm_w_o": "row", "ssd_w_in": "slot",
            "ssd_w_o": "row", "mlp_w_in": "col", "mlp_w_out": "row"}
SMALL_SHARDED = ["ssd_conv_w", "ssd_conv_b", "ssd_norm_g"]
WEIGHTS = ["norm_mix_g", "norm_mlp_g", "sb_w_qkv", "sb_q_norm_g", "sb_k_norm_g", "sb_w_o", "gm_w_in",
           "gm_v_norm_g", "gm_w_s", "gm_b_s", "gm_w_o", "ssd_w_in", "ssd_conv_w", "ssd_conv_b", "ssd_dt_bias",
           "ssd_a_log", "ssd_d", "ssd_norm_g", "ssd_w_o", "mlp_w_in", "mlp_w_out"]


SHARD_KIND = dict(BIG_KIND, small="slot", repl="all")
AG_GROUPS = [
    [("sb_w_qkv", 0), ("sb_w_o", 0), ("mlp_w_in", 0), ("mlp_w_out", 0)],
    [("gm_w_in", 0), ("gm_w_o", 0), ("mlp_w_in", 1), ("mlp_w_out", 1), ("ssd_w_in", None), ("small", None),
     ("ssd_w_o", 0), ("mlp_w_in", 2), ("mlp_w_out", 2)],
    [("sb_w_qkv", 1), ("sb_w_o", 1), ("mlp_w_in", 3), ("mlp_w_out", 3)],
]


def _ssd_weights(w):
    win = jnp.moveaxis(w["ssd_w_in"][:, 0], 0, 1).reshape(D_MODEL, SSD_PROJ)
    small = w["small"]
    w["ssd_wz"] = win[:, :SSD_INNER]
    w["ssd_wxbc"] = win[:, SSD_INNER:SSD_INNER + SSD_CONV_DIM]
    w["ssd_wdt"] = jnp.pad(win[:, SSD_INNER + SSD_CONV_DIM:], ((0, 0), (0, DT_PAD - SSD_HEADS)))
    w["ssd_conv_w"] = small[:, 0:4].transpose(1, 0, 2).reshape(SSD_CONV, SSD_CONV_DIM)
    w["ssd_conv_b"] = small[:, 4].reshape(1, SSD_CONV_DIM)
    w["ssd_norm_g"] = small[:, 5:6, :SSD_INNER // N_DEV]


def _small_weights(p):
    w = {}
    w["norm_mix_g"] = [p["norm_mix_g"][i:i + 1] for i in range(DEPTH)]
    w["norm_mlp_g"] = [p["norm_mlp_g"][i:i + 1] for i in range(DEPTH)]
    w["sb_gqk"] = [jnp.stack([p["sb_q_norm_g"][j:j + 1], p["sb_k_norm_g"][j:j + 1]]) for j in range(2)]
    w["gm_v_norm_g"] = p["gm_v_norm_g"]
    w["gm_w_s"] = p["gm_w_s"][0]
    w["gm_b_s"] = p["gm_b_s"][0][:, :, None]
    w["ssd_dt_bias"] = jnp.pad(p["ssd_dt_bias"], ((0, 0), (0, DT_PAD - SSD_HEADS)))
    w["ssd_a_log"] = p["ssd_a_log"].reshape(SSD_HEADS, 1, 1)
    w["ssd_d"] = p["ssd_d"].reshape(SSD_HEADS, 1, 1)
    return w


def _repl_grads(g):
    return [jnp.concatenate(g["norm_mix_g"], axis=0), jnp.concatenate(g["norm_mlp_g"], axis=0),
            jnp.concatenate([g["sb_gqk"][0][0], g["sb_gqk"][1][0]], axis=0),
            jnp.concatenate([g["sb_gqk"][0][1], g["sb_gqk"][1][1]], axis=0),
            g["gm_v_norm_g"], g["gm_w_s"][None], g["gm_b_s"][None, :, :, 0],
            g["ssd_dt_bias"], g["ssd_a_log"].reshape(1, SSD_HEADS), g["ssd_d"].reshape(1, SSD_HEADS)]


def kernel(x, norm_mix_g, norm_mlp_g, sb_w_qkv, sb_q_norm_g, sb_k_norm_g, sb_w_o, gm_w_in, gm_v_norm_g, gm_w_s, gm_b_s, gm_w_o, ssd_w_in, ssd_conv_w, ssd_conv_b, ssd_dt_bias, ssd_a_log, ssd_d, ssd_norm_g, ssd_w_o, mlp_w_in, mlp_w_out, loss_target, m_norm_mix_g, m_norm_mlp_g, m_sb_w_qkv, m_sb_q_norm_g, m_sb_k_norm_g, m_sb_w_o, m_gm_w_in, m_gm_v_norm_g, m_gm_w_s, m_gm_b_s, m_gm_w_o, m_ssd_w_in, m_ssd_conv_w, m_ssd_conv_b, m_ssd_dt_bias, m_ssd_a_log, m_ssd_d, m_ssd_norm_g, m_ssd_w_o, m_mlp_w_in, m_mlp_w_out, v_norm_mix_g, v_norm_mlp_g, v_sb_w_qkv, v_sb_q_norm_g, v_sb_k_norm_g, v_sb_w_o, v_gm_w_in, v_gm_v_norm_g, v_gm_w_s, v_gm_b_s, v_gm_w_o, v_ssd_w_in, v_ssd_conv_w, v_ssd_conv_b, v_ssd_dt_bias, v_ssd_a_log, v_ssd_d, v_ssd_norm_g, v_ssd_w_o, v_mlp_w_in, v_mlp_w_out):
    args = dict(locals())
    p = {k: args[k] for k in WEIGHTS}
    pm = {k: args["m_" + k] for k in WEIGHTS}
    pv = {k: args["v_" + k] for k in WEIGHTS}

    shards = {k: p[k].astype(BF16) for k in BIG}
    shards["small"] = _pack_small_shard(*[p[k] for k in SMALL_SHARDED])
    repl_shapes = [p[k].shape for k in REPL]
    n_repl = sum(math.prod(s) for s in repl_shapes)
    repl_rows = -(-n_repl // (LANES * LANES)) * LANES
    shard_shapes = {k: p[k].shape for k in BIG}
    shard_shapes.update(small=(8, SMALL_W), repl=(repl_rows, LANES))

    w = Traffic(_small_weights(p), shards, shard_shapes)
    loss, grad_x = device_step(x[0], loss_target[0], w)
    loss = lax.psum(loss[0, 0], ("x", "y", "c"))

    out = {}
    for k in sorted(BIG, key=lambda k: (w.grads_left[k], -p[k].size)):
        w.need_grad(k)
        ex = w.host_bwd(p[k].size * ADAMW_BYTES_PER_ELEMENT / HBM_BYTES_PER_US, exclude=(k,))
        out[k], bufs = adamw(p[k], pm[k], pv[k], w.recv[k], f"adamw_{k}", ex)
        w.recv.update(bufs)
    w.need_grad("small")
    w.need_grad("repl")
    recv = w.recv
    sm, _ = adamw(*[_pack_small_shard(*[d[k] for k in SMALL_SHARDED])[None] for d in (p, pm, pv)],
                  recv["small"][:, None], "adamw_small")
    for k, vals in zip(SMALL_SHARDED, zip(*[_unpack_small_shard(r[0]) for r in sm])):
        out[k] = list(vals)
    rp, _ = adamw(*[_pack([d[k] for k in REPL], repl_rows)[None] for d in (p, pm, pv)],
                  recv["repl"][:, None], "adamw_repl")
    for k, vals in zip(REPL, zip(*[_unpack(r[0], repl_shapes) for r in rp])):
        out[k] = list(vals)
    res = [loss, grad_x[None]]
    for q in range(4):
        res += [out[k][q] for k in WEIGHTS]
    return tuple(res)
```

```python
import functools
import math

import jax
import jax.numpy as jnp
from jax import lax
from jax.experimental import pallas as pl
from jax.experimental.pallas import tpu as pltpu

F32 = jnp.float32
BF16 = jnp.bfloat16

D_MODEL = 2048
DEPTH = 4
N_MIXERS = 3
EPS = 1e-6
SB_HEADS = 16
HEAD_DIM = 128
GM_CHUNK = 128
GM_GROUPS = 16
GM_STEP_CHUNKS = 4
SSD_INNER = 4096
SSD_HEADS = 64
SSD_HEAD_DIM = 64
SSD_GROUPS = 8
SSD_HPG = 8
SSD_STATE = 128
SSD_CHUNK = 128
SSD_CONV = 4
SSD_BC = SSD_GROUPS * SSD_STATE
SSD_CONV_DIM = SSD_INNER + 2 * SSD_BC
SSD_PROJ = SSD_INNER + SSD_CONV_DIM + SSD_HEADS
DT_PAD = 128
N_DEV = 8
LANES = 128

ADAM_LR = 0.001
ADAM_B1 = 0.9
ADAM_B2 = 0.999
ADAM_EPS = 1e-08
ADAM_WD = 0.01
ADAM_STEP = 10

VMEM_LIMIT_BYTES = 56 * 1024 * 1024
MESH = pl.DeviceIdType.MESH
HBM_SPEC = pl.BlockSpec(memory_space=pl.ANY)


def _cparams(*sem):
    return pltpu.CompilerParams(dimension_semantics=sem, vmem_limit_bytes=VMEM_LIMIT_BYTES)


def _tile(dim, pref):
    t = min(dim, pref)
    assert dim % t == 0, (dim, pref)
    return t


_DOT_DIMS = {"nn": ((1,), (0,)), "nt": ((1,), (1,)), "tn": ((0,), (0,))}


def matmul(a, b, form, name, *, b_layer=None, out_dtypes=(F32,), epilogue=None, extras=(),
           tm=1024, tn=1024, tk=2048, ex=None):
    bs = b.shape[-2:]
    if form == "nn":
        (m, k), (k2, n) = a.shape, bs
    elif form == "nt":
        (m, k), (n, k2) = a.shape, bs
    else:
        (k, m), (k2, n) = a.shape, bs
    assert k == k2, (a.shape, b.shape, form)
    tm, tn, tk = _tile(m, tm), _tile(n, tn), _tile(k, tk)
    grid = (m // tm, n // tn, k // tk)
    nk = grid[2]
    if form == "tn":
        a_spec = pl.BlockSpec((tk, tm), lambda i, j, kk: (kk, i))
    else:
        a_spec = pl.BlockSpec((tm, tk), lambda i, j, kk: (i, kk))
    if form == "nt":
        b_blk, b_idx = (tn, tk), (lambda i, j, kk: (j, kk))
    else:
        b_blk, b_idx = (tk, tn), (lambda i, j, kk: (kk, j))
    if b.ndim == 3:
        b_spec = pl.BlockSpec((None,) + b_blk, lambda i, j, kk: (b_layer,) + b_idx(i, j, kk))
    else:
        b_spec = pl.BlockSpec(b_blk, b_idx)
    o_spec = pl.BlockSpec((tm, tn), lambda i, j, kk: (i, j))
    n_extra, n_out = len(extras), len(out_dtypes)
    dims = (_DOT_DIMS[form], ((), ()))

    def body(a_ref, b_ref, *rest):
        extra_refs, out_refs = rest[:n_extra], rest[n_extra:n_extra + n_out]
        kk = pl.program_id(2)

        def product():
            return lax.dot_general(a_ref[...], b_ref[...], dims, preferred_element_type=F32)

        def finish(acc):
            outs = (acc,) if epilogue is None else epilogue(acc, *[r[...] for r in extra_refs])
            for r, o in zip(out_refs, outs):
                r[...] = o.astype(r.dtype)

        if nk == 1:
            finish(product())
            return
        acc_ref = rest[-1]

        @pl.when(kk == 0)
        def _():
            acc_ref[...] = product()

        @pl.when(jnp.logical_and(kk > 0, kk < nk - 1))
        def _():
            acc_ref[...] += product()

        @pl.when(kk == nk - 1)
        def _():
            finish(acc_ref[...] + product())

    res, bufs = comm_call(
        body, name=name, grid=grid,
        in_specs=[a_spec, b_spec] + [o_spec] * n_extra,
        out_specs=[o_spec] * n_out,
        out_shape=[jax.ShapeDtypeStruct((m, n), dt) for dt in out_dtypes],
        scratch_shapes=[pltpu.VMEM((tm, tn), F32)] if nk > 1 else [],
        inputs=[a, b, *extras], sem=("parallel", "parallel", "arbitrary"), ex=ex)
    res = res[0] if n_out == 1 else res
    return res if ex is None else (res, bufs)


class Data:
    def __init__(self, array, block, imap):
        self.array, self.block, self.imap = array, block, imap


class Param:
    def __init__(self, array, block, imap, group):
        self.array, self.block, self.imap, self.group = array, block, imap, group


class Out:
    def __init__(self, shape, dtype, block, imap):
        self.shape, self.dtype, self.block, self.imap = shape, dtype, block, imap


def _f32(v):
    return v.astype(F32)


def blockwise_fwd(f, grid, data, params, outs, name):
    nd, npar = len(data), len(params)

    def body(*refs):
        vals = [_f32(r[...]) for r in refs[:nd + npar]]
        res = f(*vals)
        for r, o in zip(refs[nd + npar:], res):
            r[...] = o.astype(r.dtype)

    in_specs = [pl.BlockSpec(d.block, d.imap) for d in data]
    in_specs += [pl.BlockSpec(p.block, (lambda o, i, _m=p.imap: _m(o))) for p in params]
    res = pl.pallas_call(
        body, name=name, grid=grid, in_specs=in_specs,
        out_specs=[pl.BlockSpec(o.block, o.imap) for o in outs],
        out_shape=[jax.ShapeDtypeStruct(o.shape, o.dtype) for o in outs],
        compiler_params=_cparams("parallel", "parallel"),
    )(*[d.array for d in data], *[p.array for p in params])
    return res


def blockwise_bwd(f, grid, data, params, outs, cts, name, *, grad_dtypes, accum=None, also_bf16=()):
    nd, npar, no = len(data), len(params), len(outs)
    accum = accum or {}
    want = [k for k in range(nd) if grad_dtypes[k] is not None] + list(also_bf16)
    want_dtypes = [grad_dtypes[k] for k in want[:len(want) - len(also_bf16)]] + [BF16] * len(also_bf16)
    acc_keys = sorted(accum)

    def body(*refs):
        in_refs = refs[:nd + npar]
        ct_refs = refs[nd + npar:nd + npar + no]
        acc_refs = refs[nd + npar + no:nd + npar + no + len(acc_keys)]
        out_refs = refs[nd + npar + no + len(acc_keys):]
        dd_refs, dp_refs = out_refs[:len(want)], out_refs[len(want):]
        o, i = pl.program_id(0), pl.program_id(1)
        vals = [_f32(r[...]) for r in in_refs]
        _, vjp = jax.vjp(f, *vals)
        grads = vjp(tuple(_f32(r[...]) for r in ct_refs))
        for r, k in zip(dd_refs, want):
            g = grads[k]
            if k in accum:
                g = g + _f32(acc_refs[acc_keys.index(k)][...])
            r[...] = g.astype(r.dtype)
        for r, p, g in zip(dp_refs, params, grads[nd:]):
            first = jnp.logical_and(i == 0, o % p.group == 0)

            @pl.when(first)
            def _():
                r[...] = g.astype(r.dtype)

            @pl.when(jnp.logical_not(first))
            def _():
                r[...] += g.astype(r.dtype)

    d_specs = [pl.BlockSpec(d.block, d.imap) for d in data]
    p_specs = [pl.BlockSpec(p.block, (lambda o, i, _m=p.imap: _m(o))) for p in params]
    ct_specs = [pl.BlockSpec(o.block, o.imap) for o in outs]
    acc_specs = [d_specs[k] for k in acc_keys]
    res = pl.pallas_call(
        body, name=name, grid=grid,
        in_specs=d_specs + p_specs + ct_specs + acc_specs,
        out_specs=[d_specs[k] for k in want] + p_specs,
        out_shape=[jax.ShapeDtypeStruct(data[k].array.shape, dt) for k, dt in zip(want, want_dtypes)]
        + [jax.ShapeDtypeStruct(p.array.shape, F32) for p in params],
        compiler_params=_cparams("arbitrary", "arbitrary"),
    )(*[d.array for d in data], *[p.array for p in params], *cts, *[accum[k] for k in acc_keys])
    return list(res[:len(want)]), list(res[len(want):])


def _rmsnorm_f(x, g):
    return (x * lax.rsqrt(jnp.mean(x * x, axis=-1, keepdims=True) + EPS) * g,)


def _gelu(x):
    return 0.5 * x * (1.0 + lax.erf(x * (2.0 ** -0.5)))


def _silu(x):
    return x / (1.0 + jnp.exp(-x))


def _softplus(x):
    return jnp.maximum(x, 0.0) + jnp.log1p(jnp.exp(-jnp.abs(x)))


def _gm_act_f(pu, pv, gv):
    return (_gelu(pu),) + _rmsnorm_f(_gelu(pv), gv)


def _gm_gate_f(u, vn, w, b):
    n = w.shape[0]
    c = u.shape[0] // n
    row = lax.broadcasted_iota(jnp.int32, (n, n), 0)
    col = lax.broadcasted_iota(jnp.int32, (n, n), 1)
    wm = jnp.broadcast_to(jnp.where(row >= col, w, 0.0), (c, n, n))
    mixed = lax.dot_general(wm, vn.reshape(c, n, LANES), (((2,), (1,)), ((0,), (0,))),
                            preferred_element_type=F32)
    return (u * (mixed + b[None]).reshape(c * n, LANES),)


def _ssd_gate_f(y, z, g):
    return _rmsnorm_f(y * _silu(z), g)


def _dt_f(raw, bias):
    return (_softplus(raw + bias),)


def _row_tile(t):
    return _tile(t, 256)


def _norm_specs(h, g_row):
    t, d = h.shape
    tr = _row_tile(t)
    data = [Data(h, (tr, d), lambda o, i: (i, 0))]
    params = [Param(g_row, (1, d), lambda o: (0, 0), 1)]
    outs = [Out((t, d), BF16, (tr, d), lambda o, i: (i, 0))]
    return (1, t // tr), data, params, outs


def rmsnorm_fwd(h, g_row, name):
    grid, data, params, outs = _norm_specs(h, g_row)
    return blockwise_fwd(_rmsnorm_f, grid, data, params, outs, name)[0]


def rmsnorm_bwd(h, g_row, ct, skip, name, with_bf16=True):
    grid, data, params, outs = _norm_specs(h, g_row)
    dd, (dg,) = blockwise_bwd(_rmsnorm_f, grid, data, params, outs, [ct], name, grad_dtypes=[F32],
                              accum={0: skip}, also_bf16=(0,) if with_bf16 else ())
    return dd[0], (dd[1] if with_bf16 else None), dg


def loss_call(y, target):
    t, d = y.shape
    tr = _row_tile(t)

    def body(y_ref, t_ref, dy_ref, dyb_ref, loss_ref):
        e = y_ref[...] - t_ref[...]
        dy_ref[...] = e * (1.0 / d)
        dyb_ref[...] = (e * (1.0 / d)).astype(BF16)

        @pl.when(pl.program_id(0) == 0)
        def _():
            loss_ref[...] = jnp.zeros_like(loss_ref)

        loss_ref[...] += jnp.sum(e * e).reshape(1, 1) * (0.5 / d)

    row = pl.BlockSpec((tr, d), lambda i: (i, 0))
    return pl.pallas_call(
        body, name="loss", grid=(t // tr,), in_specs=[row, row],
        out_specs=[row, row, pl.BlockSpec((1, 1), lambda i: (0, 0))],
        out_shape=[jax.ShapeDtypeStruct((t, d), F32), jax.ShapeDtypeStruct((t, d), BF16),
                   jax.ShapeDtypeStruct((1, 1), F32)],
        compiler_params=_cparams("arbitrary"),
    )(y, target)


ATT_TQ = 1024
ATT_KB = 256
ATT_DIAG = ATT_TQ // ATT_KB


def _split2(x):
    hi = x.astype(BF16)
    return hi, (x - hi.astype(F32)).astype(BF16)


def _dot2(hi, lo, ones_bf):
    return (jnp.dot(hi, ones_bf, preferred_element_type=F32)
            + jnp.dot(lo, ones_bf, preferred_element_type=F32))


def _att_consts():
    r_ = lax.broadcasted_iota(jnp.int32, (ATT_KB, ATT_KB), 0)
    c_ = lax.broadcasted_iota(jnp.int32, (ATT_KB, ATT_KB), 1)
    upper = (r_ > c_).astype(BF16)
    lower = (r_ < c_).astype(BF16)
    return upper, lower


def _att_logits(q, k, scale, mask):
    z = lax.dot_general(q, k, (((1,), (1,)), ((), ())), preferred_element_type=F32) * scale
    lb = jnp.minimum(z, 0.0) - jnp.log(1.0 + jnp.exp(-jnp.abs(z)))
    l1 = lb - z
    if mask is not None:
        l1 = jnp.where(mask, l1, 0.0)
    return lb, l1


def _live_rows(d):
    if d is None:
        return pl.ds(0, ATT_TQ), None
    first, n = d * ATT_KB, ATT_TQ - d * ATT_KB
    row = first + lax.broadcasted_iota(jnp.int32, (n, ATT_KB), 0)
    col = first + lax.broadcasted_iota(jnp.int32, (n, ATT_KB), 1)
    return pl.ds(first, n), col < row


def _wide(c):
    return jnp.concatenate([c] * (ATT_KB // LANES), axis=1)


def attn_fwd(qk, qkv_bf, name, ex=None):
    t = qk.shape[0]
    tq = ATT_TQ
    assert t % tq == 0
    scale = HEAD_DIM ** -0.5

    def body(q_ref, k_ref, v_ref, o_ref, acc_ref, c_ref):
        qi = pl.program_id(1)
        upper, _ = _att_consts()
        acc_ref[...] = jnp.zeros_like(acc_ref)
        c_ref[...] = jnp.zeros_like(c_ref)

        def block(jb, diag):
            off = pl.multiple_of(jb * ATT_KB, ATT_KB)
            k = k_ref[pl.ds(off, ATT_KB), :]
            v = v_ref[pl.ds(off, ATT_KB), :]
            rows, mask = _live_rows(diag)
            lb, l1 = _att_logits(q_ref[rows, :], k, scale, mask)
            hi, lo = _split2(l1)
            c = c_ref[rows, :]
            a = jnp.exp(lb + _wide(c) + _dot2(hi, lo, upper))
            if mask is not None:
                a = jnp.where(mask, a, 0.0)
            acc_ref[rows, :] += jnp.dot(a.astype(BF16), v, preferred_element_type=F32)
            c_ref[rows, :] = c + jnp.sum(l1, axis=1, keepdims=True)

        for d in reversed(range(ATT_DIAG)):
            block(qi * ATT_DIAG + d, d)

        def step(s, carry):
            block(qi * ATT_DIAG - 1 - s, None)
            return carry

        lax.fori_loop(0, qi * ATT_DIAG, step, 0)
        o_ref[...] = acc_ref[...].astype(o_ref.dtype)

    (o,), bufs = comm_call(
        body, name=name, grid=(SB_HEADS, t // tq),
        in_specs=[pl.BlockSpec((tq, HEAD_DIM), lambda h, i: (i, h)),
                  pl.BlockSpec((t, HEAD_DIM), lambda h, i: (0, SB_HEADS + h)),
                  pl.BlockSpec((t, HEAD_DIM), lambda h, i: (0, 2 * SB_HEADS + h))],
        out_specs=[pl.BlockSpec((tq, HEAD_DIM), lambda h, i: (i, h))],
        out_shape=[jax.ShapeDtypeStruct((t, SB_HEADS * HEAD_DIM), BF16)],
        scratch_shapes=[pltpu.VMEM((tq, HEAD_DIM), F32), pltpu.VMEM((tq, LANES), F32)],
        inputs=[qk, qk, qkv_bf], sem=("parallel", "arbitrary"), ex=ex)
    return o, bufs


def attn_bwd(qk, qkv_bf, do, name, ex=None):
    t = qk.shape[0]
    tq = ATT_TQ
    assert t % tq == 0
    nq = t // tq
    nkb = t // ATT_KB
    scale = HEAD_DIM ** -0.5
    tn_dims = (((0,), (0,)), ((), ()))

    def body(q_ref, k_ref, v_ref, do_ref, dq_ref, dk_ref, dv_ref,
             cbuf, c_ref, ce_ref, dq_acc, dk_acc, dv_acc):
        qi = pl.program_id(1)
        upper, lower = _att_consts()

        @pl.when(qi == 0)
        def _():
            dk_acc[...] = jnp.zeros_like(dk_acc)
            dv_acc[...] = jnp.zeros_like(dv_acc)

        c_ref[...] = jnp.zeros_like(c_ref)
        ce_ref[...] = jnp.zeros_like(ce_ref)
        dq_acc[...] = jnp.zeros_like(dq_acc)

        def pass1(jb, diag):
            off = pl.multiple_of(jb * ATT_KB, ATT_KB)
            rows, mask = _live_rows(diag)
            _, l1 = _att_logits(q_ref[rows, :], k_ref[pl.ds(off, ATT_KB), :], scale, mask)
            c = c_ref[rows, :]
            cbuf[jb, rows, :] = c
            c_ref[rows, :] = c + jnp.sum(l1, axis=1, keepdims=True)

        for d in reversed(range(ATT_DIAG)):
            pass1(qi * ATT_DIAG + d, d)

        def step1(s, carry):
            pass1(qi * ATT_DIAG - 1 - s, None)
            return carry

        lax.fori_loop(0, qi * ATT_DIAG, step1, 0)

        def pass2(jb, diag):
            off = pl.multiple_of(jb * ATT_KB, ATT_KB)
            k = k_ref[pl.ds(off, ATT_KB), :]
            v = v_ref[pl.ds(off, ATT_KB), :]
            rows, mask = _live_rows(diag)
            q, do_ = q_ref[rows, :], do_ref[rows, :]
            lb, l1 = _att_logits(q, k, scale, mask)
            hi, lo = _split2(l1)
            a = jnp.exp(lb + _wide(cbuf[jb, rows, :]) + _dot2(hi, lo, upper))
            if mask is not None:
                a = jnp.where(mask, a, 0.0)
            da = lax.dot_general(do_, v, (((1,), (1,)), ((), ())), preferred_element_type=F32)
            e = da * a
            ehi, elo = _split2(e)
            ce = ce_ref[rows, :]
            big_e = _wide(ce) + _dot2(ehi, elo, lower)
            dz = (e - (e + big_e) * jnp.exp(lb)) * scale
            if mask is not None:
                dz = jnp.where(mask, dz, 0.0)
            dz = dz.astype(BF16)
            dq_acc[rows, :] += jnp.dot(dz, k, preferred_element_type=F32)
            ce_ref[rows, :] = ce + jnp.sum(e, axis=1, keepdims=True)
            dk_acc[pl.ds(off, ATT_KB), :] += lax.dot_general(dz, q, tn_dims, preferred_element_type=F32)
            dv_acc[pl.ds(off, ATT_KB), :] += lax.dot_general(a.astype(BF16), do_, tn_dims,
                                                             preferred_element_type=F32)

        def step2(jb, carry):
            pass2(jb, None)
            return carry

        lax.fori_loop(0, qi * ATT_DIAG, step2, 0)
        for d in range(ATT_DIAG):
            pass2(qi * ATT_DIAG + d, d)
        dq_ref[...] = dq_acc[...]

        @pl.when(qi == nq - 1)
        def _():
            dk_ref[...] = dk_acc[...]
            dv_ref[...] = dv_acc[...].astype(dv_ref.dtype)

    hd = SB_HEADS * HEAD_DIM
    (dq, dk, dv), bufs = comm_call(
        body, name=name, grid=(SB_HEADS, nq),
        in_specs=[pl.BlockSpec((tq, HEAD_DIM), lambda h, i: (i, h)),
                  pl.BlockSpec((t, HEAD_DIM), lambda h, i: (0, SB_HEADS + h)),
                  pl.BlockSpec((t, HEAD_DIM), lambda h, i: (0, 2 * SB_HEADS + h)),
                  pl.BlockSpec((tq, HEAD_DIM), lambda h, i: (i, h))],
        out_specs=[pl.BlockSpec((tq, HEAD_DIM), lambda h, i: (i, h)),
                   pl.BlockSpec((t, HEAD_DIM), lambda h, i: (0, h)),
                   pl.BlockSpec((t, HEAD_DIM), lambda h, i: (0, h))],
        out_shape=[jax.ShapeDtypeStruct((t, hd), F32), jax.ShapeDtypeStruct((t, hd), F32),
                   jax.ShapeDtypeStruct((t, hd), BF16)],
        scratch_shapes=[pltpu.VMEM((nkb, tq, LANES), F32), pltpu.VMEM((tq, LANES), F32),
                        pltpu.VMEM((tq, LANES), F32), pltpu.VMEM((tq, HEAD_DIM), F32),
                        pltpu.VMEM((t, HEAD_DIM), F32), pltpu.VMEM((t, HEAD_DIM), F32)],
        inputs=[qk, qk, qkv_bf, do], sem=("arbitrary", "arbitrary"), ex=ex)
    return jnp.concatenate([dq, dk], axis=1), dv, bufs


def _qknorm_specs(qkv, gqk):
    t = qkv.shape[0]
    tr = _tile(t, 1024)
    data = [Data(qkv, (tr, HEAD_DIM), lambda o, i: (i, o))]
    params = [Param(gqk, (None, 1, HEAD_DIM), lambda o: (o // SB_HEADS, 0, 0), SB_HEADS)]
    outs = [Out((t, 2 * SB_HEADS * HEAD_DIM), BF16, (tr, HEAD_DIM), lambda o, i: (i, o))]
    return (2 * SB_HEADS, t // tr), data, params, outs


CONV_COLS = 128


def _shift_down(x, s, rows):
    if s == 0:
        return x
    return jnp.where(rows < s, 0.0, pltpu.roll(x, s, 0))


def _shift_up(x, s, rows):
    if s == 0:
        return x
    n = x.shape[0]
    return jnp.where(rows >= n - s, 0.0, pltpu.roll(x, n - s, 0))


def conv_fwd(xbc, w, b, name):
    t, c = xbc.shape
    tc = _tile(c, CONV_COLS)

    def body(x_ref, w_ref, b_ref, o_ref):
        x = x_ref[...]
        rows = lax.broadcasted_iota(jnp.int32, x.shape, 0)
        y = jnp.broadcast_to(b_ref[...], x.shape)
        for kk in range(SSD_CONV):
            y = y + w_ref[kk:kk + 1, :] * _shift_down(x, SSD_CONV - 1 - kk, rows)
        o_ref[...] = _silu(y)

    col = pl.BlockSpec((t, tc), lambda j: (0, j))
    return pl.pallas_call(
        body, name=name, grid=(c // tc,),
        in_specs=[col, pl.BlockSpec((SSD_CONV, tc), lambda j: (0, j)), pl.BlockSpec((1, tc), lambda j: (0, j))],
        out_specs=col, out_shape=jax.ShapeDtypeStruct((t, c), F32),
        compiler_params=_cparams("parallel"),
    )(xbc, w, b)


def conv_bwd(xbc, w, b, dact, name):
    t, c = xbc.shape
    tc = _tile(c, CONV_COLS)

    def body(x_ref, w_ref, b_ref, g_ref, dx_ref, dw_ref, db_ref):
        x = x_ref[...]
        rows = lax.broadcasted_iota(jnp.int32, x.shape, 0)
        shifted = [_shift_down(x, SSD_CONV - 1 - kk, rows) for kk in range(SSD_CONV)]
        y = jnp.broadcast_to(b_ref[...], x.shape)
        for kk in range(SSD_CONV):
            y = y + w_ref[kk:kk + 1, :] * shifted[kk]
        sig = 1.0 / (1.0 + jnp.exp(-y))
        dy = g_ref[...] * (sig * (1.0 + y * (1.0 - sig)))
        dx = jnp.zeros_like(x)
        for kk in range(SSD_CONV):
            dx = dx + w_ref[kk:kk + 1, :] * _shift_up(dy, SSD_CONV - 1 - kk, rows)
            dw_ref[kk:kk + 1, :] = jnp.sum(dy * shifted[kk], axis=0, keepdims=True)
        dx_ref[...] = dx.astype(dx_ref.dtype)
        db_ref[...] = jnp.sum(dy, axis=0, keepdims=True)

    col = pl.BlockSpec((t, tc), lambda j: (0, j))
    wspec = pl.BlockSpec((SSD_CONV, tc), lambda j: (0, j))
    bspec = pl.BlockSpec((1, tc), lambda j: (0, j))
    return pl.pallas_call(
        body, name=name, grid=(c // tc,),
        in_specs=[col, wspec, bspec, col], out_specs=[col, wspec, bspec],
        out_shape=[jax.ShapeDtypeStruct((t, c), BF16), jax.ShapeDtypeStruct((SSD_CONV, c), F32),
                   jax.ShapeDtypeStruct((1, c), F32)],
        compiler_params=_cparams("parallel"),
    )(xbc, w, b, dact)


@functools.partial(jax.custom_vjp, nondiff_argnums=(2,))
def _bdot(a, b, dims):
    return lax.dot_general(a.astype(BF16), b.astype(BF16), (dims, ((), ())), preferred_element_type=F32)


def _bdot_fwd(a, b, dims):
    return _bdot(a, b, dims), (a, b)


def _bdot_bwd(dims, res, g):
    a, b = res
    (ca,), (cb,) = dims
    fa, fb = 1 - ca, 1 - cb
    gb, ab, bb = g.astype(BF16), a.astype(BF16), b.astype(BF16)

    def dn(u, v, cu, cv):
        return lax.dot_general(u, v, (((cu,), (cv,)), ((), ())), preferred_element_type=F32)

    da = dn(gb, bb, 1, fb) if ca == 1 else dn(bb, gb, fb, 1)
    db = dn(ab, gb, fa, 0) if cb == 0 else dn(gb, ab, 0, fa)
    return da, db


_bdot.defvjp(_bdot_fwd, _bdot_bwd)
_NN, _NT, _TN = ((1,), (0,)), ((1,), (1,)), ((0,), (0,))


def _split3_dot(x, m_bf, dims):
    out = None
    for _ in range(3):
        t = x.astype(BF16)
        part = lax.dot_general(t, m_bf, (dims, ((), ())), preferred_element_type=F32)
        out = part if out is None else out + part
        x = x - t.astype(F32)
    return out


@jax.custom_vjp
def _tri_dot(x, m_bf):
    return _split3_dot(x, m_bf, _NN)


def _tri_dot_fwd(x, m_bf):
    return _tri_dot(x, m_bf), m_bf


def _tri_dot_bwd(m_bf, g):
    return _split3_dot(g, m_bf, _NT), jnp.zeros_like(m_bf)


_tri_dot.defvjp(_tri_dot_fwd, _tri_dot_bwd)


def _ssd_chunk(x, dt, bm, cm, sprev, alog, dskip):
    n = x.shape[0]
    a = -jnp.exp(alog)
    da = dt * a
    row = lax.broadcasted_iota(jnp.int32, (n, n), 0)
    col = lax.broadcasted_iota(jnp.int32, (n, n), 1)
    causal = row >= col
    acum_s = _tri_dot(jnp.broadcast_to(da, (n, n)), (row <= col).astype(BF16))
    acum_l = acum_s.T
    decay = jnp.exp(jnp.where(causal, acum_l - acum_s, -1e30))
    acum = jnp.sum(acum_l, axis=1, keepdims=True) * (1.0 / n)
    atot = jnp.sum(da, axis=1, keepdims=True)
    xs = x * jnp.sum(jnp.where(row == col, jnp.broadcast_to(dt, (n, n)), 0.0), axis=1, keepdims=True)
    y = _bdot(_bdot(cm, bm, _NT) * decay, xs, _NN)
    y = y + jnp.exp(acum) * _bdot(cm, sprev, _NT)
    y = y + dskip * x
    snew = jnp.exp(atot) * sprev + _bdot(xs * jnp.exp(atot - acum), bm, _TN)
    return y, snew


def _ssd_specs(t, rev):
    nc = t // SSD_CHUNK
    cidx = (lambda c: nc - 1 - c) if rev else (lambda c: c)
    x_spec = pl.BlockSpec((SSD_HPG, SSD_CHUNK, SSD_HEAD_DIM), lambda g, c: (g, cidx(c), 0))
    dt_spec = pl.BlockSpec((None, SSD_HPG, SSD_CHUNK), lambda g, c: (cidx(c), g, 0))
    bc_spec = pl.BlockSpec((None, SSD_CHUNK, SSD_STATE), lambda g, c: (g, cidx(c), 0))
    sc_spec = pl.BlockSpec((SSD_HPG, 1, 1), lambda g, c: (g, 0, 0))
    st_spec = pl.BlockSpec((SSD_HPG, None, SSD_HEAD_DIM, SSD_STATE), lambda g, c: (g, cidx(c), 0, 0))
    return nc, x_spec, dt_spec, bc_spec, sc_spec, st_spec


def ssd_scan_fwd(x, dt, bm, cm, alog, dskip, name, ex=None):
    t = x.shape[1]
    nc, x_spec, dt_spec, bc_spec, sc_spec, st_spec = _ssd_specs(t, False)

    def body(x_ref, dt_ref, b_ref, c_ref, al_ref, d_ref, y_ref, sin_ref, state):
        @pl.when(pl.program_id(1) == 0)
        def _():
            state[...] = jnp.zeros_like(state)

        bm_, cm_ = b_ref[...], c_ref[...]

        for r in range(SSD_HPG):
            sprev = state[r]
            sin_ref[r] = sprev
            y, snew = _ssd_chunk(x_ref[r], dt_ref[r:r + 1, :], bm_, cm_, sprev, al_ref[r], d_ref[r])
            y_ref[r] = y
            state[r] = snew

    (y, sin), bufs = comm_call(
        body, name=name, grid=(SSD_GROUPS, nc),
        in_specs=[x_spec, dt_spec, bc_spec, bc_spec, sc_spec, sc_spec],
        out_specs=[x_spec, st_spec],
        out_shape=[jax.ShapeDtypeStruct(x.shape, F32),
                   jax.ShapeDtypeStruct((SSD_HEADS, nc, SSD_HEAD_DIM, SSD_STATE), F32)],
        scratch_shapes=[pltpu.VMEM((SSD_HPG, SSD_HEAD_DIM, SSD_STATE), F32)],
        inputs=[x, dt, bm, cm, alog, dskip], sem=("parallel", "arbitrary"), ex=ex)
    return y, sin, bufs


def ssd_scan_bwd(x, dt, bm, cm, alog, dskip, sin, dy, name, ex=None):
    t = x.shape[1]
    nc, x_spec, dt_spec, bc_spec, sc_spec, st_spec = _ssd_specs(t, True)

    def body(x_ref, dt_ref, b_ref, c_ref, al_ref, d_ref, sin_ref, dy_ref,
             dx_ref, ddt_ref, db_ref, dc_ref, dal_ref, dd_ref, dstate):
        @pl.when(pl.program_id(1) == 0)
        def _():
            dstate[...] = jnp.zeros_like(dstate)
            dal_ref[...] = jnp.zeros_like(dal_ref)
            dd_ref[...] = jnp.zeros_like(dd_ref)

        db_ref[...] = jnp.zeros_like(db_ref)
        dc_ref[...] = jnp.zeros_like(dc_ref)
        bm_, cm_ = b_ref[...], c_ref[...]

        for r in range(SSD_HPG):
            _, vjp = jax.vjp(_ssd_chunk, x_ref[r], dt_ref[r:r + 1, :], bm_, cm_, sin_ref[r], al_ref[r], d_ref[r])
            dx, ddt, dbm, dcm, dsp, dal, dd = vjp((dy_ref[r], dstate[r]))
            dx_ref[r] = dx
            ddt_ref[r:r + 1, :] = ddt
            db_ref[...] += dbm
            dc_ref[...] += dcm
            dstate[r] = dsp
            dal_ref[r] += dal
            dd_ref[r] += dd

    return comm_call(
        body, name=name, grid=(SSD_GROUPS, nc),
        in_specs=[x_spec, dt_spec, bc_spec, bc_spec, sc_spec, sc_spec, st_spec, x_spec],
        out_specs=[x_spec, dt_spec, bc_spec, bc_spec, sc_spec, sc_spec],
        out_shape=[jax.ShapeDtypeStruct(x.shape, F32), jax.ShapeDtypeStruct(dt.shape, F32),
                   jax.ShapeDtypeStruct(bm.shape, F32), jax.ShapeDtypeStruct(cm.shape, F32),
                   jax.ShapeDtypeStruct(alog.shape, F32), jax.ShapeDtypeStruct(dskip.shape, F32)],
        scratch_shapes=[pltpu.VMEM((SSD_HPG, SSD_HEAD_DIM, SSD_STATE), F32)],
        inputs=[x, dt, bm, cm, alog, dskip, sin, dy], sem=("arbitrary", "arbitrary"), ex=ex)


def _to_heads(a, n, w):
    t = a.shape[0]
    return a.reshape(t, n, w).transpose(1, 0, 2)


def _from_heads(a):
    n, t, w = a.shape
    return a.transpose(1, 0, 2).reshape(t, n * w)


def _add(acc, h):
    return (acc + h,)


ATTN_FWD_US, ATTN_BWD_US, SCAN_FWD_US, SCAN_BWD_US = 700, 1600, 560, 1280


def sb_fwd(h, hn, w, j, tag):
    gqk = w["sb_gqk"][j]
    qkv, qkv_bf = w.mm_fwd(hn, "sb_w_qkv", j, "nn", f"{tag}_qkv", out_dtypes=(F32, BF16),
                           epilogue=lambda acc: (acc, acc))
    grid, data, params, outs = _qknorm_specs(qkv, gqk)
    qk = blockwise_fwd(_rmsnorm_f, grid, data, params, outs, f"{tag}_qknorm")[0]
    o, bufs = attn_fwd(qk, qkv_bf, f"{tag}_attn", w.host_fwd(ATTN_FWD_US))
    w.done_fwd(bufs)
    h1 = w.mm_fwd(o, "sb_w_o", j, "nn", f"{tag}_out", epilogue=_add, extras=(h,))
    return h1, (qkv, qkv_bf, qk, o)


def sb_bwd(g_bf, hn, saved, w, j, tag):
    qkv, qkv_bf, qk, o = saved
    gqk = w["sb_gqk"][j]
    w.add_grad("sb_w_o", j, w.mm_bwd(o, g_bf, "tn", f"{tag}_dwo", out_dtypes=(BF16,)))
    do = w.mm_bwd(g_bf, w["sb_w_o"], "nt", f"{tag}_do", b_layer=j, out_dtypes=(BF16,))
    dqk, dv, bufs = attn_bwd(qk, qkv_bf, do, f"{tag}_attn_bwd", w.host_bwd(ATTN_BWD_US))
    w.recv.update(bufs)
    grid, data, params, outs = _qknorm_specs(qkv, gqk)
    (draw,), (dgqk,) = blockwise_bwd(_rmsnorm_f, grid, data, params, outs, [dqk], f"{tag}_qknorm_bwd",
                                     grad_dtypes=[BF16])
    dqkv = jnp.concatenate([draw[:, :2 * SB_HEADS * HEAD_DIM], dv], axis=1)
    w.add_grad("sb_w_qkv", j, w.mm_bwd(hn, dqkv, "tn", f"{tag}_dwqkv", out_dtypes=(BF16,)))
    dhn = w.mm_bwd(dqkv, w["sb_w_qkv"], "nt", f"{tag}_dhn", b_layer=j)
    return dhn, {"sb_gqk": dgqk}


def _gm_act_specs(p, gv):
    t = p.shape[0]
    tr = _row_tile(t)
    d = D_MODEL
    data = [Data(p, (tr, d), lambda o, i: (i, 0)), Data(p, (tr, d), lambda o, i: (i, 1))]
    params = [Param(gv, (1, d), lambda o: (0, 0), 1)]
    outs = [Out((t, d), F32, (tr, d), lambda o, i: (i, 0)), Out((t, d), BF16, (tr, d), lambda o, i: (i, 0))]
    return (1, t // tr), data, params, outs


def _gm_gate_specs(u, vn, ws, bs):
    t = u.shape[0]
    rows = _tile(t, GM_STEP_CHUNKS * GM_CHUNK)
    blk = (rows, LANES)
    data = [Data(u, blk, lambda o, i: (i, o)), Data(vn, blk, lambda o, i: (i, o))]
    params = [Param(ws, (None, GM_CHUNK, GM_CHUNK), lambda o: (o, 0, 0), 1),
              Param(bs, (None, GM_CHUNK, 1), lambda o: (o, 0, 0), 1)]
    outs = [Out((t, D_MODEL), BF16, blk, lambda o, i: (i, o))]
    return (GM_GROUPS, t // rows), data, params, outs


def gm_fwd(h, hn, w, j, tag):
    p = w.mm_fwd(hn, "gm_w_in", j, "nn", f"{tag}_in")
    grid, data, params, outs = _gm_act_specs(p, w["gm_v_norm_g"])
    u, vn = blockwise_fwd(_gm_act_f, grid, data, params, outs, f"{tag}_act")
    grid, data, params, outs = _gm_gate_specs(u, vn, w["gm_w_s"], w["gm_b_s"])
    y = blockwise_fwd(_gm_gate_f, grid, data, params, outs, f"{tag}_gate")[0]
    h1 = w.mm_fwd(y, "gm_w_o", j, "nn", f"{tag}_out", epilogue=_add, extras=(h,))
    return h1, (p, u, vn, y)


def gm_bwd(g_bf, hn, saved, w, j, tag):
    p, u, vn, y = saved
    w.add_grad("gm_w_o", j, w.mm_bwd(y, g_bf, "tn", f"{tag}_dwo", out_dtypes=(BF16,)))
    dy = w.mm_bwd(g_bf, w["gm_w_o"], "nt", f"{tag}_dy", b_layer=j)
    grid, data, params, outs = _gm_gate_specs(u, vn, w["gm_w_s"], w["gm_b_s"])
    (du, dvn), (dws, dbs) = blockwise_bwd(_gm_gate_f, grid, data, params, outs, [dy], f"{tag}_gate_bwd",
                                          grad_dtypes=[F32, F32])
    grid, data, params, outs = _gm_act_specs(p, w["gm_v_norm_g"])
    (dpu, dpv), (dgv,) = blockwise_bwd(_gm_act_f, grid, data, params, outs, [du, dvn], f"{tag}_act_bwd",
                                       grad_dtypes=[BF16, BF16])
    dp = jnp.concatenate([dpu[:, :D_MODEL], dpv[:, D_MODEL:]], axis=1)
    w.add_grad("gm_w_in", j, w.mm_bwd(hn, dp, "tn", f"{tag}_dwin", out_dtypes=(BF16,)))
    dhn = w.mm_bwd(dp, w["gm_w_in"], "nt", f"{tag}_dhn", b_layer=j)
    return dhn, {"gm_v_norm_g": dgv, "gm_w_s": dws, "gm_b_s": dbs}


def _ssd_gate_specs(y, z, g):
    t = y.shape[0]
    tr = _row_tile(t)
    gw = SSD_INNER // SSD_GROUPS
    blk = (tr, gw)
    data = [Data(y, blk, lambda o, i: (i, o)), Data(z, blk, lambda o, i: (i, o))]
    params = [Param(g, (None, 1, gw), lambda o: (o, 0, 0), 1)]
    outs = [Out((t, SSD_INNER), BF16, blk, lambda o, i: (i, o))]
    return (SSD_GROUPS, t // tr), data, params, outs


def _dt_specs(raw, bias):
    t = raw.shape[0]
    tr = _row_tile(t)
    data = [Data(raw, (tr, DT_PAD), lambda o, i: (i, 0))]
    params = [Param(bias, (1, DT_PAD), lambda o: (0, 0), 1)]
    outs = [Out((t, DT_PAD), F32, (tr, DT_PAD), lambda o, i: (i, 0))]
    return (1, t // tr), data, params, outs


def ssd_fwd(h, hn, w, tag):
    w.need("ssd_w_in", None)
    z = w.mm_fwd(hn, "ssd_wz", None, "nn", f"{tag}_inz")
    xbc = w.mm_fwd(hn, "ssd_wxbc", None, "nn", f"{tag}_inx")
    raw = w.mm_fwd(hn, "ssd_wdt", None, "nn", f"{tag}_indt")
    act = conv_fwd(xbc, w["ssd_conv_w"], w["ssd_conv_b"], f"{tag}_conv")
    grid, data, params, outs = _dt_specs(raw, w["ssd_dt_bias"])
    dt = blockwise_fwd(_dt_f, grid, data, params, outs, f"{tag}_dt")[0]
    xh = _to_heads(act[:, :SSD_INNER], SSD_HEADS, SSD_HEAD_DIM)
    bh = _to_heads(act[:, SSD_INNER:SSD_INNER + SSD_BC], SSD_GROUPS, SSD_STATE)
    ch = _to_heads(act[:, SSD_INNER + SSD_BC:], SSD_GROUPS, SSD_STATE)
    dth = dt[:, :SSD_HEADS].reshape(-1, SSD_CHUNK, SSD_HEADS).transpose(0, 2, 1)
    yh, sin, bufs = ssd_scan_fwd(xh, dth, bh, ch, w["ssd_a_log"], w["ssd_d"], f"{tag}_scan",
                                 w.host_fwd(SCAN_FWD_US))
    w.done_fwd(bufs)
    y = _from_heads(yh)
    grid, data, params, outs = _ssd_gate_specs(y, z, w["ssd_norm_g"])
    yn = blockwise_fwd(_ssd_gate_f, grid, data, params, outs, f"{tag}_gate")[0]
    h1 = w.mm_fwd(yn, "ssd_w_o", 0, "nn", f"{tag}_out", epilogue=_add, extras=(h,))
    return h1, (z, xbc, raw, xh, bh, ch, dth, sin, y, yn)


def ssd_bwd(g_bf, hn, saved, w, tag):
    z, xbc, raw, xh, bh, ch, dth, sin, y, yn = saved
    w.add_grad("ssd_w_o", 0, w.mm_bwd(yn, g_bf, "tn", f"{tag}_dwo", out_dtypes=(BF16,)))
    dyn = w.mm_bwd(g_bf, w["ssd_w_o"], "nt", f"{tag}_dyn", b_layer=0)
    grid, data, params, outs = _ssd_gate_specs(y, z, w["ssd_norm_g"])
    (dy, dz), (dng,) = blockwise_bwd(_ssd_gate_f, grid, data, params, outs, [dyn], f"{tag}_gate_bwd",
                                     grad_dtypes=[F32, BF16])
    (dxh, ddth, dbh, dch, dal, ddk), bufs = ssd_scan_bwd(
        xh, dth, bh, ch, w["ssd_a_log"], w["ssd_d"], sin, _to_heads(dy, SSD_HEADS, SSD_HEAD_DIM),
        f"{tag}_scan_bwd", w.host_bwd(SCAN_BWD_US))
    w.recv.update(bufs)
    dact = jnp.concatenate([_from_heads(dxh), _from_heads(dbh), _from_heads(dch)], axis=1)
    dxbc, dcw, dcb = conv_bwd(xbc, w["ssd_conv_w"], w["ssd_conv_b"], dact, f"{tag}_conv_bwd")
    ddt = jnp.pad(ddth.transpose(0, 2, 1).reshape(-1, SSD_HEADS), ((0, 0), (0, DT_PAD - SSD_HEADS)))
    grid, data, params, outs = _dt_specs(raw, w["ssd_dt_bias"])
    (draw,), (dbias,) = blockwise_bwd(_dt_f, grid, data, params, outs, [ddt], f"{tag}_dt_bwd",
                                      grad_dtypes=[BF16])
    d_wz = w.mm_bwd(hn, dz, "tn", f"{tag}_dwz", out_dtypes=(BF16,))
    d_wx = w.mm_bwd(hn, dxbc, "tn", f"{tag}_dwx", out_dtypes=(BF16,))
    d_wdt = w.mm_bwd(hn, draw, "tn", f"{tag}_dwdt", out_dtypes=(BF16,))
    d_win = jnp.concatenate([d_wz, d_wx, d_wdt[:, :SSD_HEADS]], axis=1)
    d_win = d_win.reshape(D_MODEL, N_DEV, SSD_PROJ // N_DEV).transpose(1, 0, 2)
    w.add_grad("ssd_w_in", 0, d_win)
    dhn = w.mm_bwd(dz, w["ssd_wz"], "nt", f"{tag}_dhn_z")
    dhn = w.mm_bwd(dxbc, w["ssd_wxbc"], "nt", f"{tag}_dhn_x", epilogue=_add, extras=(dhn,))
    dhn = w.mm_bwd(draw, w["ssd_wdt"], "nt", f"{tag}_dhn_dt", epilogue=_add, extras=(dhn,))
    return dhn, {"ssd_conv_w": dcw, "ssd_conv_b": dcb, "ssd_dt_bias": dbias[:, :SSD_HEADS], "ssd_a_log": dal,
                 "ssd_d": ddk, "ssd_norm_g": dng}


def _relu2(acc):
    return (acc, jnp.square(jnp.maximum(acc, 0.0)))


def _relu2_bwd(acc, a):
    return (acc * (2.0 * jnp.maximum(a, 0.0)),)


def device_step(x, target, w):
    h = x
    saved = []
    for i in range(DEPTH):
        kind, j = i % N_MIXERS, i // N_MIXERS
        tag = f"l{i}"
        hn = rmsnorm_fwd(h, w["norm_mix_g"][i], f"{tag}_norm_mix")
        if kind == 0:
            h1, ms = sb_fwd(h, hn, w, j, f"{tag}_sb")
        elif kind == 1:
            h1, ms = gm_fwd(h, hn, w, j, f"{tag}_gm")
        else:
            h1, ms = ssd_fwd(h, hn, w, f"{tag}_ssd")
        hm = rmsnorm_fwd(h1, w["norm_mlp_g"][i], f"{tag}_norm_mlp")
        a, r = w.mm_fwd(hm, "mlp_w_in", i, "nn", f"{tag}_mlp_in", out_dtypes=(F32, BF16), epilogue=_relu2)
        h2 = w.mm_fwd(r, "mlp_w_out", i, "nn", f"{tag}_mlp_out", epilogue=_add, extras=(h1,))
        saved.append((h, hn, ms, h1, hm, a, r))
        h = h2
    dh, g_bf, loss = loss_call(h, target)

    small = {"norm_mix_g": [None] * DEPTH, "norm_mlp_g": [None] * DEPTH, "sb_gqk": [None] * 2}
    for i in reversed(range(DEPTH)):
        kind, j = i % N_MIXERS, i // N_MIXERS
        tag = f"l{i}"
        h0, hn, ms, h1, hm, a, r = saved[i]
        w.add_grad("mlp_w_out", i, w.mm_bwd(r, g_bf, "tn", f"{tag}_mlp_dwout", out_dtypes=(BF16,)))
        da = w.mm_bwd(g_bf, w["mlp_w_out"], "nt", f"{tag}_mlp_da", b_layer=i, out_dtypes=(BF16,),
                      epilogue=_relu2_bwd, extras=(a,))
        w.add_grad("mlp_w_in", i, w.mm_bwd(hm, da, "tn", f"{tag}_mlp_dwin", out_dtypes=(BF16,)))
        dhm = w.mm_bwd(da, w["mlp_w_in"], "nt", f"{tag}_mlp_dhm", b_layer=i)
        dh1, g_bf, small["norm_mlp_g"][i] = rmsnorm_bwd(h1, w["norm_mlp_g"][i], dhm, dh, f"{tag}_norm_mlp_bwd")
        if kind == 0:
            dhn, mg = sb_bwd(g_bf, hn, ms, w, j, f"{tag}_sb")
            small["sb_gqk"][j] = mg.pop("sb_gqk")
        elif kind == 1:
            dhn, mg = gm_bwd(g_bf, hn, ms, w, j, f"{tag}_gm")
        else:
            dhn, mg = ssd_bwd(g_bf, hn, ms, w, f"{tag}_ssd")
        small.update(mg)
        dh, g_bf, small["norm_mix_g"][i] = rmsnorm_bwd(h0, w["norm_mix_g"][i], dhn, dh1, f"{tag}_norm_mix_bwd",
                                                       with_bf16=i > 0)
    dcw = small["ssd_conv_w"].reshape(SSD_CONV, N_DEV, SMALL_W).transpose(1, 0, 2)
    dcb = small["ssd_conv_b"].reshape(N_DEV, 1, SMALL_W)
    dng = jnp.pad(small["ssd_norm_g"], ((0, 0), (0, 0), (0, SMALL_W - SSD_INNER // N_DEV)))
    w.add_grad("small", None, jnp.concatenate([dcw, dcb, dng, jnp.zeros((N_DEV, 2, SMALL_W), F32)], axis=1))
    w.add_grad("repl", None, _pack(_repl_grads(small), w.shapes["repl"][0]))
    return loss, dh


def _my_pos():
    return lax.axis_index("x"), lax.axis_index("y"), lax.axis_index("c")


def _lin(p):
    return 4 * p[0] + 2 * p[1] + p[2]


_FLIPS = [(0, 0, 1), (0, 1, 0), (0, 1, 1), (1, 0, 0), (1, 0, 1), (1, 1, 0), (1, 1, 1)]
_SIBLING = (0, 0, 1)
_AG_FLIPS = [_SIBLING, (1, 0, 0), (0, 1, 0), (1, 1, 0)]


def _flip(pos, f):
    return tuple((1 - p) if b else p for p, b in zip(pos, f))


def _other_chip(pos, j):
    x, y = pos[0], pos[1]
    return [(1 - x, y), (x, 1 - y), (1 - x, 1 - y)][j]


class Exchange:
    def __init__(self):
        self.sources, self.buffers, self.items, self.locals = [], [], [], []

    def source(self, array):
        self.sources.append(array)
        return len(self.sources) - 1

    def buffer(self, name, bufs, shape, dtype):
        if name not in [b[0] for b in self.buffers]:
            self.buffers.append((name, bufs.get(name), jax.ShapeDtypeStruct(shape, dtype)))

    def n_copies(self):
        return sum(len(flips) for _, _, flips in self.items)


def _exchange_fns(ex, srcs, bufs, send_sems, recv_sems, local_sems):
    def local(k):
        sf, df = ex.locals[k]
        me = _my_pos()
        return pltpu.make_async_copy(sf(srcs, bufs, me, me), df(bufs, me), local_sems.at[k])

    def remote(sending):
        me = _my_pos()
        k = 0
        for sf, df, flips in ex.items:
            for f in flips:
                peer = _flip(me, f)
                yield pltpu.make_async_remote_copy(
                    src_ref=sf(srcs, bufs, me, peer), dst_ref=df(bufs, me if sending else peer),
                    send_sem=send_sems.at[k], recv_sem=recv_sems.at[k], device_id=peer, device_id_type=MESH)
                k += 1

    def start():
        for k in range(len(ex.locals)):
            local(k).start()
        for cp in remote(True):
            cp.start()

    def wait():
        arrivals = list(remote(False))
        for cp in arrivals:
            cp.wait_recv()
        for cp in arrivals:
            cp.wait_send()
        for k in range(len(ex.locals)):
            local(k).wait()

    return start, wait


def comm_call(core, *, name, grid, in_specs, out_specs, out_shape, scratch_shapes, inputs, sem, ex=None):
    n_in, n_out, n_scr = len(in_specs), len(out_specs), len(scratch_shapes)
    if ex is None:
        res = pl.pallas_call(core, name=name, grid=grid, in_specs=in_specs, out_specs=out_specs,
                             out_shape=out_shape, scratch_shapes=scratch_shapes,
                             compiler_params=_cparams(*sem))(*inputs)
        return list(res), {}
    existing = [(k, b[1]) for k, b in enumerate(ex.buffers) if b[1] is not None]
    names = [b[0] for b in ex.buffers]
    n_src, n_old, n_buf = len(ex.sources), len(existing), len(names)

    def body(*refs):
        cuts = [n_in, n_src, n_old, n_out, n_buf, n_scr, 3]
        parts, pos = [], 0
        for c in cuts:
            parts.append(refs[pos:pos + c])
            pos += c
        core_in, srcs, _, core_out, buf_refs, core_scr, sems = parts
        start, wait = _exchange_fns(ex, srcs, dict(zip(names, buf_refs)), *sems)
        if not grid:
            start()
            wait()
            return
        ids = [pl.program_id(d) for d in range(len(grid))]
        pl.when(functools.reduce(jnp.logical_and, [i == 0 for i in ids]))(start)
        core(*core_in, *core_out, *core_scr)
        pl.when(functools.reduce(jnp.logical_and, [i == g - 1 for i, g in zip(ids, grid)]))(wait)

    dma = pltpu.SemaphoreType.DMA
    res = pl.pallas_call(
        body, name=name, **({"grid": grid} if grid else {}),
        in_specs=list(in_specs) + [HBM_SPEC] * (n_src + n_old),
        out_specs=list(out_specs) + [HBM_SPEC] * n_buf,
        out_shape=list(out_shape) + [b[2] for b in ex.buffers],
        scratch_shapes=list(scratch_shapes) + [dma((max(ex.n_copies(), 1),)), dma((max(ex.n_copies(), 1),)),
                                               dma((max(len(ex.locals), 1),))],
        input_output_aliases={n_in + n_src + e: n_out + k for e, (k, _) in enumerate(existing)},
        compiler_params=_cparams(*(["arbitrary"] * len(grid))),
    )(*inputs, *ex.sources, *[b for _, b in existing])
    return list(res[:n_out]), dict(zip(names, res[n_out:]))


def run_exchange(ex, name):
    return comm_call(None, name=name, grid=(), in_specs=[], out_specs=[], out_shape=[], scratch_shapes=[],
                     inputs=[], sem=(), ex=ex)[1]


def _gathered_shape(shard, kind):
    s = shard.shape
    if kind == "col":
        return s[:2] + (s[2] * N_DEV,)
    if kind == "row":
        return (s[0], s[1] * N_DEV, s[2])
    return (N_DEV,) + s


def _n_pieces(nbytes, rows, target):
    n = 1
    while nbytes > n * target and rows % (32 * n) == 0 and n < 8:
        n *= 2
    return n


def _ag_part(buf, tn, layer, r0, nr, who, shards):
    kind, shard, idx = SHARD_KIND[tn], shards[tn], _lin(who)
    if kind == "col":
        return buf.at[layer, pl.ds(r0, nr), pl.ds(pl.multiple_of(idx * shard.shape[2], LANES), shard.shape[2])]
    if kind == "row":
        return buf.at[layer, pl.ds(pl.multiple_of(idx * shard.shape[1] + r0, 16), nr), :]
    return buf.at[idx] if tn == "small" else buf.at[idx, 0, pl.ds(r0, nr), :]


def _ag_items(ex, piece, phase, shards, bufs):
    tn, layer, r0, nr, _ = piece
    shard = shards[tn]
    ex.buffer(tn, bufs, _gathered_shape(shard, SHARD_KIND[tn]), shard.dtype)

    def part(b, who):
        return _ag_part(b[tn], tn, layer, r0, nr, who, shards)

    if phase == "A":
        def src(s, b, me, peer, si=ex.source(shard)):
            return s[si] if tn == "small" else s[si].at[0 if layer is None else layer, pl.ds(r0, nr), :]

        ex.items.append((src, part, _AG_FLIPS))
        ex.locals.append((src, part))
        return
    for j in range(3):
        def held(who, j=j):
            return (*_other_chip(who, j), who[2])

        ex.items.append((lambda s, b, me, peer, held=held: part(b, held(me)),
                         lambda b, sender, held=held: part(b, held(sender)), [_SIBLING]))


def _rs_items(ex, piece, bufs, shapes):
    tn, layer, r0, nr, _, g = piece
    kind, shape = SHARD_KIND[tn], shapes[tn]
    ex.buffer(tn, bufs, (N_DEV,) + tuple(shape), g.dtype)

    def src(s, b, me, peer, si=ex.source(g)):
        idx = _lin(peer)
        if kind == "all":
            return s[si]
        if tn == "small":
            return s[si].at[idx]
        if kind == "slot":
            return s[si].at[idx, pl.ds(r0, nr), :]
        if kind == "col":
            return s[si].at[pl.ds(r0, nr), pl.ds(pl.multiple_of(idx * shape[2], LANES), shape[2])]
        return s[si].at[pl.ds(pl.multiple_of(idx * shape[1] + r0, 16), nr), :]

    def dst(b, sender):
        idx = _lin(sender)
        return b[tn].at[idx] if layer is None else b[tn].at[idx, layer, pl.ds(r0, nr), :]

    ex.items.append((src, dst, _FLIPS))
    ex.locals.append((src, dst))


AG_BYTES_PER_US = 22e3
RS_BYTES_PER_US = 75e3
AG_PIECE_BYTES = 2.3e6
RS_PIECE_BYTES = 4.5e6
MXU_FLOPS_PER_US = 8.6e8
MATMUL_HOST_MIN_US = {"fwd": 0, "bwd": 110}
MATMUL_HOST_SHARE = 1.0


class Traffic(dict):
    def __init__(self, small_weights, shards, shapes):
        super().__init__(small_weights)
        self.shards, self.shapes, self.recv = shards, shapes, {}
        self.wait_a, self.wait_b, self.riding, self.unfinished = [], [], ([], []), {}
        for group in AG_GROUPS:
            for tn, layer in group:
                shard = shards[tn]
                rows = shard.shape[-2]
                nbytes = rows * shard.shape[-1] * shard.dtype.itemsize
                n = 1 if tn == "small" else _n_pieces(nbytes, rows, AG_PIECE_BYTES)
                self.unfinished[(tn, layer)] = n
                self.wait_a += [(tn, layer, k * (rows // n), rows // n, nbytes / n) for k in range(n)]
        self.grads, self.grads_left, self.n_calls = [], {}, 0

    def host_fwd(self, us, exclude=(), only=None):
        ex, budget = Exchange(), us * AG_BYTES_PER_US
        ok = (lambda p: p[:2] == only) if only else (lambda p: p[0] not in exclude)
        take_b = [p for p in self.wait_b if ok(p)]
        take_a = []
        for p in [p for p in self.wait_a if ok(p)]:
            if p[4] > budget and not only:
                break
            take_a.append(p)
            budget -= p[4]
        for p in take_b:
            self.wait_b.remove(p)
            _ag_items(ex, p, "B", self.shards, self)
        for p in take_a:
            self.wait_a.remove(p)
            _ag_items(ex, p, "A", self.shards, self)
        self.riding = (take_a, take_b)
        return ex if ex.items else None

    def done_fwd(self, bufs):
        self.update(bufs)
        take_a, take_b = self.riding
        self.wait_b += take_a
        for p in take_b:
            self.unfinished[p[:2]] -= 1
        self.riding = ([], [])

    def need(self, tn, layer):
        while self.unfinished.get((tn, layer), 0) > 0:
            self.n_calls += 1
            self.done_fwd(run_exchange(self.host_fwd(0, only=(tn, layer)), f"gather_now{self.n_calls}"))
        if tn == "ssd_w_in" and "ssd_wz" not in self:
            self.need("small", None)
            _ssd_weights(self)

    def mm_fwd(self, a, tn, layer, form, name, **kw):
        self.need(tn, layer)
        ex = self.host_fwd(_host_us(a, self[tn], form, "fwd"), exclude=(tn,))
        if ex is None:
            return matmul(a, self[tn], form, name, b_layer=layer, **kw)
        out, bufs = matmul(a, self[tn], form, name, b_layer=layer, ex=ex, **kw)
        self.done_fwd(bufs)
        return out

    def add_grad(self, tn, layer, g):
        rows = self.shapes[tn][-2]
        nbytes = g.size * g.dtype.itemsize
        n = 1 if layer is None else _n_pieces(nbytes, rows, RS_PIECE_BYTES)
        self.grads += [(tn, layer, k * (rows // n), rows // n, nbytes / n, g) for k in range(n)]
        self.grads_left[tn] = self.grads_left.get(tn, 0) + n

    def host_bwd(self, us, exclude=(), only=None):
        ex, budget = Exchange(), us * RS_BYTES_PER_US
        for p in [p for p in self.grads if (p[0] == only if only else p[0] not in exclude)]:
            if p[4] > budget and not only:
                continue
            self.grads.remove(p)
            self.grads_left[p[0]] -= 1
            _rs_items(ex, p, self.recv, self.shapes)
            budget -= p[4]
        return ex if ex.items else None

    def need_grad(self, tn):
        if self.grads_left.get(tn, 0) > 0:
            self.n_calls += 1
            self.recv.update(run_exchange(self.host_bwd(0, only=tn), f"scatter_now{self.n_calls}"))

    def mm_bwd(self, a, b, form, name, **kw):
        ex = self.host_bwd(_host_us(a, b, form, "bwd"))
        if ex is None:
            return matmul(a, b, form, name, **kw)
        out, bufs = matmul(a, b, form, name, ex=ex, **kw)
        self.recv.update(bufs)
        return out


def _host_us(a, b, form, direction):
    k = a.shape[0] if form == "tn" else a.shape[1]
    us = 2.0 * a.size // k * b.shape[-2] * b.shape[-1] / MXU_FLOPS_PER_US
    return MATMUL_HOST_SHARE * us if us >= MATMUL_HOST_MIN_US[direction] else 0.0


_C1 = 1.0 - ADAM_B1 ** ADAM_STEP
_C2 = 1.0 - ADAM_B2 ** ADAM_STEP
ADAMW_BYTES_PER_ELEMENT = 7 * 4 + N_DEV * 2
HBM_BYTES_PER_US = 2.9e6


def adamw(wt, m, v, slots, name, ex=None):
    nl, r, c = wt.shape
    tr = _tile(r, 128)

    def body(w_ref, m_ref, v_ref, s_ref, g_ref, d_ref, nm_ref, nv_ref):
        g = s_ref[0].astype(F32)
        for k in range(1, N_DEV):
            g = g + s_ref[k].astype(F32)
        m_new = ADAM_B1 * m_ref[...] + (1.0 - ADAM_B1) * g
        v_new = ADAM_B2 * v_ref[...] + (1.0 - ADAM_B2) * jnp.square(g)
        m_hat = m_new / _C1
        v_hat = v_new / _C2
        g_ref[...] = g
        d_ref[...] = -ADAM_LR * (m_hat / (jnp.sqrt(v_hat) + ADAM_EPS) + ADAM_WD * w_ref[...])
        nm_ref[...] = m_new
        nv_ref[...] = v_new

    blk = pl.BlockSpec((None, tr, c), lambda l, i: (l, i, 0))
    sblk = pl.BlockSpec((N_DEV, None, tr, c), lambda l, i: (0, l, i, 0))
    return comm_call(
        body, name=name, grid=(nl, r // tr), in_specs=[blk, blk, blk, sblk], out_specs=[blk] * 4,
        out_shape=[jax.ShapeDtypeStruct(wt.shape, F32)] * 4, scratch_shapes=[],
        inputs=[wt, m, v, slots], sem=("parallel", "parallel"), ex=ex)


def _pack(arrays, rows):
    flat = jnp.concatenate([a.reshape(-1).astype(F32) for a in arrays])
    return jnp.pad(flat, (0, rows * LANES - flat.shape[0])).reshape(rows, LANES)


def _unpack(packed, shapes):
    flat = packed.reshape(-1)
    out, off = [], 0
    for s in shapes:
        sz = math.prod(s)
        out.append(flat[off:off + sz].reshape(s))
        off += sz
    return out


SMALL_W = 768


def _pack_small_shard(conv_w, conv_b, norm_g):
    ng = jnp.pad(norm_g, ((0, 0), (0, SMALL_W - norm_g.shape[1])))
    return jnp.concatenate([conv_w[0], conv_b, ng, jnp.zeros((2, SMALL_W), F32)], axis=0)


def _unpack_small_shard(p):
    return p[0:4][None], p[4:5], p[5:6, :SSD_INNER // N_DEV]


REPL = ["norm_mix_g", "norm_mlp_g", "sb_q_norm_g", "sb_k_norm_g", "gm_v_norm_g", "gm_w_s", "gm_b_s",
        "ssd_dt_bias", "ssd_a_log", "ssd_d"]
BIG = ["sb_w_qkv", "sb_w_o", "gm_w_in", "gm_w_o", "ssd_w_in", "ssd_w_o", "mlp_w_in", "mlp_w_out"]
BIG_KIND = {"sb_w_qkv": "col", "sb_w_o": "row", "gm_w_in": "col", "gm_w_o": "row", "ssd_w_in": "slot",
            "ssd_w_o": "row", "mlp_w_in": "col", "mlp_w_out": "row"}
SMALL_SHARDED = ["ssd_conv_w", "ssd_conv_b", "ssd_norm_g"]
WEIGHTS = ["norm_mix_g", "norm_mlp_g", "sb_w_qkv", "sb_q_norm_g", "sb_k_norm_g", "sb_w_o", "gm_w_in",
           "gm_v_norm_g", "gm_w_s", "gm_b_s", "gm_w_o", "ssd_w_in", "ssd_conv_w", "ssd_conv_b", "ssd_dt_bias",
           "ssd_a_log", "ssd_d", "ssd_norm_g", "ssd_w_o", "mlp_w_in", "mlp_w_out"]


SHARD_KIND = dict(BIG_KIND, small="slot", repl="all")
AG_GROUPS = [
    [("sb_w_qkv", 0), ("sb_w_o", 0), ("mlp_w_in", 0), ("mlp_w_out", 0)],
    [("gm_w_in", 0), ("gm_w_o", 0), ("mlp_w_in", 1), ("mlp_w_out", 1), ("ssd_w_in", None), ("small", None),
     ("ssd_w_o", 0), ("mlp_w_in", 2), ("mlp_w_out", 2)],
    [("sb_w_qkv", 1), ("sb_w_o", 1), ("mlp_w_in", 3), ("mlp_w_out", 3)],
]


def _ssd_weights(w):
    win = jnp.moveaxis(w["ssd_w_in"][:, 0], 0, 1).reshape(D_MODEL, SSD_PROJ)
    small = w["small"]
    w["ssd_wz"] = win[:, :SSD_INNER]
    w["ssd_wxbc"] = win[:, SSD_INNER:SSD_INNER + SSD_CONV_DIM]
    w["ssd_wdt"] = jnp.pad(win[:, SSD_INNER + SSD_CONV_DIM:], ((0, 0), (0, DT_PAD - SSD_HEADS)))
    w["ssd_conv_w"] = small[:, 0:4].transpose(1, 0, 2).reshape(SSD_CONV, SSD_CONV_DIM)
    w["ssd_conv_b"] = small[:, 4].reshape(1, SSD_CONV_DIM)
    w["ssd_norm_g"] = small[:, 5:6, :SSD_INNER // N_DEV]


def _small_weights(p):
    w = {}
    w["norm_mix_g"] = [p["norm_mix_g"][i:i + 1] for i in range(DEPTH)]
    w["norm_mlp_g"] = [p["norm_mlp_g"][i:i + 1] for i in range(DEPTH)]
    w["sb_gqk"] = [jnp.stack([p["sb_q_norm_g"][j:j + 1], p["sb_k_norm_g"][j:j + 1]]) for j in range(2)]
    w["gm_v_norm_g"] = p["gm_v_norm_g"]
    w["gm_w_s"] = p["gm_w_s"][0]
    w["gm_b_s"] = p["gm_b_s"][0][:, :, None]
    w["ssd_dt_bias"] = jnp.pad(p["ssd_dt_bias"], ((0, 0), (0, DT_PAD - SSD_HEADS)))
    w["ssd_a_log"] = p["ssd_a_log"].reshape(SSD_HEADS, 1, 1)
    w["ssd_d"] = p["ssd_d"].reshape(SSD_HEADS, 1, 1)
    return w


def _repl_grads(g):
    return [jnp.concatenate(g["norm_mix_g"], axis=0), jnp.concatenate(g["norm_mlp_g"], axis=0),
            jnp.concatenate([g["sb_gqk"][0][0], g["sb_gqk"][1][0]], axis=0),
            jnp.concatenate([g["sb_gqk"][0][1], g["sb_gqk"][1][1]], axis=0),
            g["gm_v_norm_g"], g["gm_w_s"][None], g["gm_b_s"][None, :, :, 0],
            g["ssd_dt_bias"], g["ssd_a_log"].reshape(1, SSD_HEADS), g["ssd_d"].reshape(1, SSD_HEADS)]


def kernel(x, norm_mix_g, norm_mlp_g, sb_w_qkv, sb_q_norm_g, sb_k_norm_g, sb_w_o, gm_w_in, gm_v_norm_g, gm_w_s, gm_b_s, gm_w_o, ssd_w_in, ssd_conv_w, ssd_conv_b, ssd_dt_bias, ssd_a_log, ssd_d, ssd_norm_g, ssd_w_o, mlp_w_in, mlp_w_out, loss_target, m_norm_mix_g, m_norm_mlp_g, m_sb_w_qkv, m_sb_q_norm_g, m_sb_k_norm_g, m_sb_w_o, m_gm_w_in, m_gm_v_norm_g, m_gm_w_s, m_gm_b_s, m_gm_w_o, m_ssd_w_in, m_ssd_conv_w, m_ssd_conv_b, m_ssd_dt_bias, m_ssd_a_log, m_ssd_d, m_ssd_norm_g, m_ssd_w_o, m_mlp_w_in, m_mlp_w_out, v_norm_mix_g, v_norm_mlp_g, v_sb_w_qkv, v_sb_q_norm_g, v_sb_k_norm_g, v_sb_w_o, v_gm_w_in, v_gm_v_norm_g, v_gm_w_s, v_gm_b_s, v_gm_w_o, v_ssd_w_in, v_ssd_conv_w, v_ssd_conv_b, v_ssd_dt_bias, v_ssd_a_log, v_ssd_d, v_ssd_norm_g, v_ssd_w_o, v_mlp_w_in, v_mlp_w_out):
    args = dict(locals())
    p = {k: args[k] for k in WEIGHTS}
    pm = {k: args["m_" + k] for k in WEIGHTS}
    pv = {k: args["v_" + k] for k in WEIGHTS}

    shards = {k: p[k].astype(BF16) for k in BIG}
    shards["small"] = _pack_small_shard(*[p[k] for k in SMALL_SHARDED])
    repl_shapes = [p[k].shape for k in REPL]
    n_repl = sum(math.prod(s) for s in repl_shapes)
    repl_rows = -(-n_repl // (LANES * LANES)) * LANES
    shard_shapes = {k: p[k].shape for k in BIG}
    shard_shapes.update(small=(8, SMALL_W), repl=(repl_rows, LANES))

    w = Traffic(_small_weights(p), shards, shard_shapes)
    loss, grad_x = device_step(x[0], loss_target[0], w)
    loss = lax.psum(loss[0, 0], ("x", "y", "c"))

    out = {}
    for k in sorted(BIG, key=lambda k: (w.grads_left[k], -p[k].size)):
        w.need_grad(k)
        ex = w.host_bwd(p[k].size * ADAMW_BYTES_PER_ELEMENT / HBM_BYTES_PER_US, exclude=(k,))
        out[k], bufs = adamw(p[k], pm[k], pv[k], w.recv[k], f"adamw_{k}", ex)
        w.recv.update(bufs)
    w.need_grad("small")
    w.need_grad("repl")
    recv = w.recv
    sm, _ = adamw(*[_pack_small_shard(*[d[k] for k in SMALL_SHARDED])[None] for d in (p, pm, pv)],
                  recv["small"][:, None], "adamw_small")
    for k, vals in zip(SMALL_SHARDED, zip(*[_unpack_small_shard(r[0]) for r in sm])):
        out[k] = list(vals)
    rp, _ = adamw(*[_pack([d[k] for k in REPL], repl_rows)[None] for d in (p, pm, pv)],
                  recv["repl"][:, None], "adamw_repl")
    for k, vals in zip(REPL, zip(*[_unpack(r[0], repl_shapes) for r in rp])):
        out[k] = list(vals)
    res = [loss, grad_x[None]]
    for q in range(4):
        res += [out[k][q] for k in WEIGHTS]
    return tuple(res)
```

```python
import functools
import math

import jax
import jax.numpy as jnp
from jax import lax
from jax.experimental import pallas as pl
from jax.experimental.pallas import tpu as pltpu

F32 = jnp.float32
BF16 = jnp.bfloat16

D_MODEL = 2048
DEPTH = 4
N_MIXERS = 3
EPS = 1e-6
SB_HEADS = 16
HEAD_DIM = 128
GM_CHUNK = 128
GM_GROUPS = 16
GM_STEP_CHUNKS = 4
SSD_INNER = 4096
SSD_HEADS = 64
SSD_HEAD_DIM = 64
SSD_GROUPS = 8
SSD_HPG = 8
SSD_STATE = 128
SSD_CHUNK = 128
SSD_CONV = 4
SSD_BC = SSD_GROUPS * SSD_STATE
SSD_CONV_DIM = SSD_INNER + 2 * SSD_BC
SSD_PROJ = SSD_INNER + SSD_CONV_DIM + SSD_HEADS
DT_PAD = 128
N_DEV = 8
LANES = 128

ADAM_LR = 0.001
ADAM_B1 = 0.9
ADAM_B2 = 0.999
ADAM_EPS = 1e-08
ADAM_WD = 0.01
ADAM_STEP = 10

VMEM_LIMIT_BYTES = 56 * 1024 * 1024
MESH = pl.DeviceIdType.MESH
HBM_SPEC = pl.BlockSpec(memory_space=pl.ANY)


def _cparams(*sem):
    return pltpu.CompilerParams(dimension_semantics=sem, vmem_limit_bytes=VMEM_LIMIT_BYTES)


def _tile(dim, pref):
    t = min(dim, pref)
    assert dim % t == 0, (dim, pref)
    return t


_DOT_DIMS = {"nn": ((1,), (0,)), "nt": ((1,), (1,)), "tn": ((0,), (0,))}


def matmul(a, b, form, name, *, b_layer=None, out_dtypes=(F32,), epilogue=None, extras=(),
           tm=1024, tn=1024, tk=2048, ex=None):
    bs = b.shape[-2:]
    if form == "nn":
        (m, k), (k2, n) = a.shape, bs
    elif form == "nt":
        (m, k), (n, k2) = a.shape, bs
    else:
        (k, m), (k2, n) = a.shape, bs
    assert k == k2, (a.shape, b.shape, form)
    tm, tn, tk = _tile(m, tm), _tile(n, tn), _tile(k, tk)
    grid = (m // tm, n // tn, k // tk)
    nk = grid[2]
    if form == "tn":
        a_spec = pl.BlockSpec((tk, tm), lambda i, j, kk: (kk, i))
    else:
        a_spec = pl.BlockSpec((tm, tk), lambda i, j, kk: (i, kk))
    if form == "nt":
        b_blk, b_idx = (tn, tk), (lambda i, j, kk: (j, kk))
    else:
        b_blk, b_idx = (tk, tn), (lambda i, j, kk: (kk, j))
    if b.ndim == 3:
        b_spec = pl.BlockSpec((None,) + b_blk, lambda i, j, kk: (b_layer,) + b_idx(i, j, kk))
    else:
        b_spec = pl.BlockSpec(b_blk, b_idx)
    o_spec = pl.BlockSpec((tm, tn), lambda i, j, kk: (i, j))
    n_extra, n_out = len(extras), len(out_dtypes)
    dims = (_DOT_DIMS[form], ((), ()))

    def body(a_ref, b_ref, *rest):
        extra_refs, out_refs = rest[:n_extra], rest[n_extra:n_extra + n_out]
        kk = pl.program_id(2)

        def product():
            return lax.dot_general(a_ref[...], b_ref[...], dims, preferred_element_type=F32)

        def finish(acc):
            outs = (acc,) if epilogue is None else epilogue(acc, *[r[...] for r in extra_refs])
            for r, o in zip(out_refs, outs):
                r[...] = o.astype(r.dtype)

        if nk == 1:
            finish(product())
            return
        acc_ref = rest[-1]

        @pl.when(kk == 0)
        def _():
            acc_ref[...] = product()

        @pl.when(jnp.logical_and(kk > 0, kk < nk - 1))
        def _():
            acc_ref[...] += product()

        @pl.when(kk == nk - 1)
        def _():
            finish(acc_ref[...] + product())

    res, bufs = comm_call(
        body, name=name, grid=grid,
        in_specs=[a_spec, b_spec] + [o_spec] * n_extra,
        out_specs=[o_spec] * n_out,
        out_shape=[jax.ShapeDtypeStruct((m, n), dt) for dt in out_dtypes],
        scratch_shapes=[pltpu.VMEM((tm, tn), F32)] if nk > 1 else [],
        inputs=[a, b, *extras], sem=("parallel", "parallel", "arbitrary"), ex=ex)
    res = res[0] if n_out == 1 else res
    return res if ex is None else (res, bufs)


class Data:
    def __init__(self, array, block, imap):
        self.array, self.block, self.imap = array, block, imap


class Param:
    def __init__(self, array, block, imap, group):
        self.array, self.block, self.imap, self.group = array, block, imap, group


class Out:
    def __init__(self, shape, dtype, block, imap):
        self.shape, self.dtype, self.block, self.imap = shape, dtype, block, imap


def _f32(v):
    return v.astype(F32)


def blockwise_fwd(f, grid, data, params, outs, name):
    nd, npar = len(data), len(params)

    def body(*refs):
        vals = [_f32(r[...]) for r in refs[:nd + npar]]
        res = f(*vals)
        for r, o in zip(refs[nd + npar:], res):
            r[...] = o.astype(r.dtype)

    in_specs = [pl.BlockSpec(d.block, d.imap) for d in data]
    in_specs += [pl.BlockSpec(p.block, (lambda o, i, _m=p.imap: _m(o))) for p in params]
    res = pl.pallas_call(
        body, name=name, grid=grid, in_specs=in_specs,
        out_specs=[pl.BlockSpec(o.block, o.imap) for o in outs],
        out_shape=[jax.ShapeDtypeStruct(o.shape, o.dtype) for o in outs],
        compiler_params=_cparams("parallel", "parallel"),
    )(*[d.array for d in data], *[p.array for p in params])
    return res


def blockwise_bwd(f, grid, data, params, outs, cts, name, *, grad_dtypes, accum=None, also_bf16=()):
    nd, npar, no = len(data), len(params), len(outs)
    accum = accum or {}
    want = [k for k in range(nd) if grad_dtypes[k] is not None] + list(also_bf16)
    want_dtypes = [grad_dtypes[k] for k in want[:len(want) - len(also_bf16)]] + [BF16] * len(also_bf16)
    acc_keys = sorted(accum)

    def body(*refs):
        in_refs = refs[:nd + npar]
        ct_refs = refs[nd + npar:nd + npar + no]
        acc_refs = refs[nd + npar + no:nd + npar + no + len(acc_keys)]
        out_refs = refs[nd + npar + no + len(acc_keys):]
        dd_refs, dp_refs = out_refs[:len(want)], out_refs[len(want):]
        o, i = pl.program_id(0), pl.program_id(1)
        vals = [_f32(r[...]) for r in in_refs]
        _, vjp = jax.vjp(f, *vals)
        grads = vjp(tuple(_f32(r[...]) for r in ct_refs))
        for r, k in zip(dd_refs, want):
            g = grads[k]
            if k in accum:
                g = g + _f32(acc_refs[acc_keys.index(k)][...])
            r[...] = g.astype(r.dtype)
        for r, p, g in zip(dp_refs, params, grads[nd:]):
            first = jnp.logical_and(i == 0, o % p.group == 0)

            @pl.when(first)
            def _():
                r[...] = g.astype(r.dtype)

            @pl.when(jnp.logical_not(first))
            def _():
                r[...] += g.astype(r.dtype)

    d_specs = [pl.BlockSpec(d.block, d.imap) for d in data]
    p_specs = [pl.BlockSpec(p.block, (lambda o, i, _m=p.imap: _m(o))) for p in params]
    ct_specs = [pl.BlockSpec(o.block, o.imap) for o in outs]
    acc_specs = [d_specs[k] for k in acc_keys]
    res = pl.pallas_call(
        body, name=name, grid=grid,
        in_specs=d_specs + p_specs + ct_specs + acc_specs,
        out_specs=[d_specs[k] for k in want] + p_specs,
        out_shape=[jax.ShapeDtypeStruct(data[k].array.shape, dt) for k, dt in zip(want, want_dtypes)]
        + [jax.ShapeDtypeStruct(p.array.shape, F32) for p in params],
        compiler_params=_cparams("arbitrary", "arbitrary"),
    )(*[d.array for d in data], *[p.array for p in params], *cts, *[accum[k] for k in acc_keys])
    return list(res[:len(want)]), list(res[len(want):])


def _rmsnorm_f(x, g):
    return (x * lax.rsqrt(jnp.mean(x * x, axis=-1, keepdims=True) + EPS) * g,)


def _gelu(x):
    return 0.5 * x * (1.0 + lax.erf(x * (2.0 ** -0.5)))


def _silu(x):
    return x / (1.0 + jnp.exp(-x))


def _softplus(x):
    return jnp.maximum(x, 0.0) + jnp.log1p(jnp.exp(-jnp.abs(x)))


def _gm_act_f(pu, pv, gv):
    return (_gelu(pu),) + _rmsnorm_f(_gelu(pv), gv)


def _gm_gate_f(u, vn, w, b):
    n = w.shape[0]
    c = u.shape[0] // n
    row = lax.broadcasted_iota(jnp.int32, (n, n), 0)
    col = lax.broadcasted_iota(jnp.int32, (n, n), 1)
    wm = jnp.broadcast_to(jnp.where(row >= col, w, 0.0), (c, n, n))
    mixed = lax.dot_general(wm, vn.reshape(c, n, LANES), (((2,), (1,)), ((0,), (0,))),
                            preferred_element_type=F32)
    return (u * (mixed + b[None]).reshape(c * n, LANES),)


def _ssd_gate_f(y, z, g):
    return _rmsnorm_f(y * _silu(z), g)


def _dt_f(raw, bias):
    return (_softplus(raw + bias),)


def _row_tile(t):
    return _tile(t, 256)


def _norm_specs(h, g_row):
    t, d = h.shape
    tr = _row_tile(t)
    data = [Data(h, (tr, d), lambda o, i: (i, 0))]
    params = [Param(g_row, (1, d), lambda o: (0, 0), 1)]
    outs = [Out((t, d), BF16, (tr, d), lambda o, i: (i, 0))]
    return (1, t // tr), data, params, outs


def rmsnorm_fwd(h, g_row, name):
    grid, data, params, outs = _norm_specs(h, g_row)
    return blockwise_fwd(_rmsnorm_f, grid, data, params, outs, name)[0]


def rmsnorm_bwd(h, g_row, ct, skip, name, with_bf16=True):
    grid, data, params, outs = _norm_specs(h, g_row)
    dd, (dg,) = blockwise_bwd(_rmsnorm_f, grid, data, params, outs, [ct], name, grad_dtypes=[F32],
                              accum={0: skip}, also_bf16=(0,) if with_bf16 else ())
    return dd[0], (dd[1] if with_bf16 else None), dg


def loss_call(y, target):
    t, d = y.shape
    tr = _row_tile(t)

    def body(y_ref, t_ref, dy_ref, dyb_ref, loss_ref):
        e = y_ref[...] - t_ref[...]
        dy_ref[...] = e * (1.0 / d)
        dyb_ref[...] = (e * (1.0 / d)).astype(BF16)

        @pl.when(pl.program_id(0) == 0)
        def _():
            loss_ref[...] = jnp.zeros_like(loss_ref)

        loss_ref[...] += jnp.sum(e * e).reshape(1, 1) * (0.5 / d)

    row = pl.BlockSpec((tr, d), lambda i: (i, 0))
    return pl.pallas_call(
        body, name="loss", grid=(t // tr,), in_specs=[row, row],
        out_specs=[row, row, pl.BlockSpec((1, 1), lambda i: (0, 0))],
        out_shape=[jax.ShapeDtypeStruct((t, d), F32), jax.ShapeDtypeStruct((t, d), BF16),
                   jax.ShapeDtypeStruct((1, 1), F32)],
        compiler_params=_cparams("arbitrary"),
    )(y, target)


ATT_TQ = 2048
ATT_KB = 256
ATT_DIAG = ATT_TQ // ATT_KB


def _split2(x):
    hi = x.astype(BF16)
    return hi, (x - hi.astype(F32)).astype(BF16)


def _dot2(hi, lo, ones_bf):
    return (jnp.dot(hi, ones_bf, preferred_element_type=F32)
            + jnp.dot(lo, ones_bf, preferred_element_type=F32))


def _att_consts():
    r_ = lax.broadcasted_iota(jnp.int32, (ATT_KB, ATT_KB), 0)
    c_ = lax.broadcasted_iota(jnp.int32, (ATT_KB, ATT_KB), 1)
    upper = (r_ > c_).astype(BF16)
    lower = (r_ < c_).astype(BF16)
    return upper, lower


def _att_logits(q, k, scale, mask):
    z = lax.dot_general(q, k, (((1,), (1,)), ((), ())), preferred_element_type=F32) * scale
    lb = jnp.minimum(z, 0.0) - jnp.log(1.0 + jnp.exp(-jnp.abs(z)))
    l1 = lb - z
    if mask is not None:
        l1 = jnp.where(mask, l1, 0.0)
    return lb, l1


def _live_rows(d):
    if d is None:
        return pl.ds(0, ATT_TQ), None
    first, n = d * ATT_KB, ATT_TQ - d * ATT_KB
    row = first + lax.broadcasted_iota(jnp.int32, (n, ATT_KB), 0)
    col = first + lax.broadcasted_iota(jnp.int32, (n, ATT_KB), 1)
    return pl.ds(first, n), col < row


def _wide(c):
    return jnp.concatenate([c] * (ATT_KB // LANES), axis=1)


def attn_fwd(qk, qkv_bf, name, ex=None):
    t = qk.shape[0]
    tq = ATT_TQ
    assert t % tq == 0
    scale = HEAD_DIM ** -0.5

    def body(q_ref, k_ref, v_ref, o_ref, acc_ref, c_ref):
        qi = pl.program_id(1)
        upper, _ = _att_consts()
        acc_ref[...] = jnp.zeros_like(acc_ref)
        c_ref[...] = jnp.zeros_like(c_ref)

        def block(jb, diag):
            off = pl.multiple_of(jb * ATT_KB, ATT_KB)
            k = k_ref[pl.ds(off, ATT_KB), :]
            v = v_ref[pl.ds(off, ATT_KB), :]
            rows, mask = _live_rows(diag)
            lb, l1 = _att_logits(q_ref[rows, :], k, scale, mask)
            hi, lo = _split2(l1)
            c = c_ref[rows, :]
            a = jnp.exp(lb + _wide(c) + _dot2(hi, lo, upper))
            if mask is not None:
                a = jnp.where(mask, a, 0.0)
            acc_ref[rows, :] += jnp.dot(a.astype(BF16), v, preferred_element_type=F32)
            c_ref[rows, :] = c + jnp.sum(l1, axis=1, keepdims=True)

        for d in reversed(range(ATT_DIAG)):
            block(qi * ATT_DIAG + d, d)

        def step(s, carry):
            block(qi * ATT_DIAG - 1 - s, None)
            return carry

        lax.fori_loop(0, qi * ATT_DIAG, step, 0)
        o_ref[...] = acc_ref[...].astype(o_ref.dtype)

    (o,), bufs = comm_call(
        body, name=name, grid=(SB_HEADS, t // tq),
        in_specs=[pl.BlockSpec((tq, HEAD_DIM), lambda h, i: (i, h)),
                  pl.BlockSpec((t, HEAD_DIM), lambda h, i: (0, SB_HEADS + h)),
                  pl.BlockSpec((t, HEAD_DIM), lambda h, i: (0, 2 * SB_HEADS + h))],
        out_specs=[pl.BlockSpec((tq, HEAD_DIM), lambda h, i: (i, h))],
        out_shape=[jax.ShapeDtypeStruct((t, SB_HEADS * HEAD_DIM), BF16)],
        scratch_shapes=[pltpu.VMEM((tq, HEAD_DIM), F32), pltpu.VMEM((tq, LANES), F32)],
        inputs=[qk, qk, qkv_bf], sem=("parallel", "arbitrary"), ex=ex)
    return o, bufs


def attn_bwd(qk, qkv_bf, do, name, ex=None):
    t = qk.shape[0]
    tq = ATT_TQ
    assert t % tq == 0
    nq = t // tq
    nkb = t // ATT_KB
    scale = HEAD_DIM ** -0.5
    tn_dims = (((0,), (0,)), ((), ()))

    def body(q_ref, k_ref, v_ref, do_ref, dq_ref, dk_ref, dv_ref,
             cbuf, c_ref, ce_ref, dq_acc, dk_acc, dv_acc):
        qi = pl.program_id(1)
        upper, lower = _att_consts()

        @pl.when(qi == 0)
        def _():
            dk_acc[...] = jnp.zeros_like(dk_acc)
            dv_acc[...] = jnp.zeros_like(dv_acc)

        c_ref[...] = jnp.zeros_like(c_ref)
        ce_ref[...] = jnp.zeros_like(ce_ref)
        dq_acc[...] = jnp.zeros_like(dq_acc)

        def pass1(jb, diag):
            off = pl.multiple_of(jb * ATT_KB, ATT_KB)
            rows, mask = _live_rows(diag)
            _, l1 = _att_logits(q_ref[rows, :], k_ref[pl.ds(off, ATT_KB), :], scale, mask)
            c = c_ref[rows, :]
            cbuf[jb, rows, :] = c
            c_ref[rows, :] = c + jnp.sum(l1, axis=1, keepdims=True)

        for d in reversed(range(ATT_DIAG)):
            pass1(qi * ATT_DIAG + d, d)

        def step1(s, carry):
            pass1(qi * ATT_DIAG - 1 - s, None)
            return carry

        lax.fori_loop(0, qi * ATT_DIAG, step1, 0)

        def pass2(jb, diag):
            off = pl.multiple_of(jb * ATT_KB, ATT_KB)
            k = k_ref[pl.ds(off, ATT_KB), :]
            v = v_ref[pl.ds(off, ATT_KB), :]
            rows, mask = _live_rows(diag)
            q, do_ = q_ref[rows, :], do_ref[rows, :]
            lb, l1 = _att_logits(q, k, scale, mask)
            hi, lo = _split2(l1)
            a = jnp.exp(lb + _wide(cbuf[jb, rows, :]) + _dot2(hi, lo, upper))
            if mask is not None:
                a = jnp.where(mask, a, 0.0)
            da = lax.dot_general(do_, v, (((1,), (1,)), ((), ())), preferred_element_type=F32)
            e = da * a
            ehi, elo = _split2(e)
            ce = ce_ref[rows, :]
            big_e = _wide(ce) + _dot2(ehi, elo, lower)
            dz = (e - (e + big_e) * jnp.exp(lb)) * scale
            if mask is not None:
                dz = jnp.where(mask, dz, 0.0)
            dz = dz.astype(BF16)
            dq_acc[rows, :] += jnp.dot(dz, k, preferred_element_type=F32)
            ce_ref[rows, :] = ce + jnp.sum(e, axis=1, keepdims=True)
            dk_acc[pl.ds(off, ATT_KB), :] += lax.dot_general(dz, q, tn_dims, preferred_element_type=F32)
            dv_acc[pl.ds(off, ATT_KB), :] += lax.dot_general(a.astype(BF16), do_, tn_dims,
                                                             preferred_element_type=F32)

        def step2(jb, carry):
            pass2(jb, None)
            return carry

        lax.fori_loop(0, qi * ATT_DIAG, step2, 0)
        for d in range(ATT_DIAG):
            pass2(qi * ATT_DIAG + d, d)
        dq_ref[...] = dq_acc[...]

        @pl.when(qi == nq - 1)
        def _():
            dk_ref[...] = dk_acc[...]
            dv_ref[...] = dv_acc[...].astype(dv_ref.dtype)

    hd = SB_HEADS * HEAD_DIM
    (dq, dk, dv), bufs = comm_call(
        body, name=name, grid=(SB_HEADS, nq),
        in_specs=[pl.BlockSpec((tq, HEAD_DIM), lambda h, i: (i, h)),
                  pl.BlockSpec((t, HEAD_DIM), lambda h, i: (0, SB_HEADS + h)),
                  pl.BlockSpec((t, HEAD_DIM), lambda h, i: (0, 2 * SB_HEADS + h)),
                  pl.BlockSpec((tq, HEAD_DIM), lambda h, i: (i, h))],
        out_specs=[pl.BlockSpec((tq, HEAD_DIM), lambda h, i: (i, h)),
                   pl.BlockSpec((t, HEAD_DIM), lambda h, i: (0, h)),
                   pl.BlockSpec((t, HEAD_DIM), lambda h, i: (0, h))],
        out_shape=[jax.ShapeDtypeStruct((t, hd), F32), jax.ShapeDtypeStruct((t, hd), F32),
                   jax.ShapeDtypeStruct((t, hd), BF16)],
        scratch_shapes=[pltpu.VMEM((nkb, tq, LANES), F32), pltpu.VMEM((tq, LANES), F32),
                        pltpu.VMEM((tq, LANES), F32), pltpu.VMEM((tq, HEAD_DIM), F32),
                        pltpu.VMEM((t, HEAD_DIM), F32), pltpu.VMEM((t, HEAD_DIM), F32)],
        inputs=[qk, qk, qkv_bf, do], sem=("arbitrary", "arbitrary"), ex=ex)
    return jnp.concatenate([dq, dk], axis=1), dv, bufs


def _qknorm_specs(qkv, gqk):
    t = qkv.shape[0]
    tr = _tile(t, 1024)
    data = [Data(qkv, (tr, HEAD_DIM), lambda o, i: (i, o))]
    params = [Param(gqk, (None, 1, HEAD_DIM), lambda o: (o // SB_HEADS, 0, 0), SB_HEADS)]
    outs = [Out((t, 2 * SB_HEADS * HEAD_DIM), BF16, (tr, HEAD_DIM), lambda o, i: (i, o))]
    return (2 * SB_HEADS, t // tr), data, params, outs


CONV_COLS = 128


def _shift_down(x, s, rows):
    if s == 0:
        return x
    return jnp.where(rows < s, 0.0, pltpu.roll(x, s, 0))


def _shift_up(x, s, rows):
    if s == 0:
        return x
    n = x.shape[0]
    return jnp.where(rows >= n - s, 0.0, pltpu.roll(x, n - s, 0))


def conv_fwd(xbc, w, b, name):
    t, c = xbc.shape
    tc = _tile(c, CONV_COLS)

    def body(x_ref, w_ref, b_ref, o_ref):
        x = x_ref[...]
        rows = lax.broadcasted_iota(jnp.int32, x.shape, 0)
        y = jnp.broadcast_to(b_ref[...], x.shape)
        for kk in range(SSD_CONV):
            y = y + w_ref[kk:kk + 1, :] * _shift_down(x, SSD_CONV - 1 - kk, rows)
        o_ref[...] = _silu(y)

    col = pl.BlockSpec((t, tc), lambda j: (0, j))
    return pl.pallas_call(
        body, name=name, grid=(c // tc,),
        in_specs=[col, pl.BlockSpec((SSD_CONV, tc), lambda j: (0, j)), pl.BlockSpec((1, tc), lambda j: (0, j))],
        out_specs=col, out_shape=jax.ShapeDtypeStruct((t, c), F32),
        compiler_params=_cparams("parallel"),
    )(xbc, w, b)


def conv_bwd(xbc, w, b, dact, name):
    t, c = xbc.shape
    tc = _tile(c, CONV_COLS)

    def body(x_ref, w_ref, b_ref, g_ref, dx_ref, dw_ref, db_ref):
        x = x_ref[...]
        rows = lax.broadcasted_iota(jnp.int32, x.shape, 0)
        shifted = [_shift_down(x, SSD_CONV - 1 - kk, rows) for kk in range(SSD_CONV)]
        y = jnp.broadcast_to(b_ref[...], x.shape)
        for kk in range(SSD_CONV):
            y = y + w_ref[kk:kk + 1, :] * shifted[kk]
        sig = 1.0 / (1.0 + jnp.exp(-y))
        dy = g_ref[...] * (sig * (1.0 + y * (1.0 - sig)))
        dx = jnp.zeros_like(x)
        for kk in range(SSD_CONV):
            dx = dx + w_ref[kk:kk + 1, :] * _shift_up(dy, SSD_CONV - 1 - kk, rows)
            dw_ref[kk:kk + 1, :] = jnp.sum(dy * shifted[kk], axis=0, keepdims=True)
        dx_ref[...] = dx.astype(dx_ref.dtype)
        db_ref[...] = jnp.sum(dy, axis=0, keepdims=True)

    col = pl.BlockSpec((t, tc), lambda j: (0, j))
    wspec = pl.BlockSpec((SSD_CONV, tc), lambda j: (0, j))
    bspec = pl.BlockSpec((1, tc), lambda j: (0, j))
    return pl.pallas_call(
        body, name=name, grid=(c // tc,),
        in_specs=[col, wspec, bspec, col], out_specs=[col, wspec, bspec],
        out_shape=[jax.ShapeDtypeStruct((t, c), BF16), jax.ShapeDtypeStruct((SSD_CONV, c), F32),
                   jax.ShapeDtypeStruct((1, c), F32)],
        compiler_params=_cparams("parallel"),
    )(xbc, w, b, dact)


@functools.partial(jax.custom_vjp, nondiff_argnums=(2,))
def _bdot(a, b, dims):
    return lax.dot_general(a.astype(BF16), b.astype(BF16), (dims, ((), ())), preferred_element_type=F32)


def _bdot_fwd(a, b, dims):
    return _bdot(a, b, dims), (a, b)


def _bdot_bwd(dims, res, g):
    a, b = res
    (ca,), (cb,) = dims
    fa, fb = 1 - ca, 1 - cb
    gb, ab, bb = g.astype(BF16), a.astype(BF16), b.astype(BF16)

    def dn(u, v, cu, cv):
        return lax.dot_general(u, v, (((cu,), (cv,)), ((), ())), preferred_element_type=F32)

    da = dn(gb, bb, 1, fb) if ca == 1 else dn(bb, gb, fb, 1)
    db = dn(ab, gb, fa, 0) if cb == 0 else dn(gb, ab, 0, fa)
    return da, db


_bdot.defvjp(_bdot_fwd, _bdot_bwd)
_NN, _NT, _TN = ((1,), (0,)), ((1,), (1,)), ((0,), (0,))


def _split3_dot(x, m_bf, dims):
    out = None
    for _ in range(3):
        t = x.astype(BF16)
        part = lax.dot_general(t, m_bf, (dims, ((), ())), preferred_element_type=F32)
        out = part if out is None else out + part
        x = x - t.astype(F32)
    return out


@jax.custom_vjp
def _tri_dot(x, m_bf):
    return _split3_dot(x, m_bf, _NN)


def _tri_dot_fwd(x, m_bf):
    return _tri_dot(x, m_bf), m_bf


def _tri_dot_bwd(m_bf, g):
    return _split3_dot(g, m_bf, _NT), jnp.zeros_like(m_bf)


_tri_dot.defvjp(_tri_dot_fwd, _tri_dot_bwd)


def _ssd_chunk(x, dt, bm, cm, sprev, alog, dskip):
    n = x.shape[0]
    a = -jnp.exp(alog)
    da = dt * a
    row = lax.broadcasted_iota(jnp.int32, (n, n), 0)
    col = lax.broadcasted_iota(jnp.int32, (n, n), 1)
    causal = row >= col
    acum_s = _tri_dot(jnp.broadcast_to(da, (n, n)), (row <= col).astype(BF16))
    acum_l = acum_s.T
    decay = jnp.exp(jnp.where(causal, acum_l - acum_s, -1e30))
    acum = jnp.sum(acum_l, axis=1, keepdims=True) * (1.0 / n)
    atot = jnp.sum(da, axis=1, keepdims=True)
    xs = x * jnp.sum(jnp.where(row == col, jnp.broadcast_to(dt, (n, n)), 0.0), axis=1, keepdims=True)
    y = _bdot(_bdot(cm, bm, _NT) * decay, xs, _NN)
    y = y + jnp.exp(acum) * _bdot(cm, sprev, _NT)
    y = y + dskip * x
    snew = jnp.exp(atot) * sprev + _bdot(xs * jnp.exp(atot - acum), bm, _TN)
    return y, snew


def _ssd_specs(t, rev):
    nc = t // SSD_CHUNK
    cidx = (lambda c: nc - 1 - c) if rev else (lambda c: c)
    x_spec = pl.BlockSpec((SSD_HPG, SSD_CHUNK, SSD_HEAD_DIM), lambda g, c: (g, cidx(c), 0))
    dt_spec = pl.BlockSpec((None, SSD_HPG, SSD_CHUNK), lambda g, c: (cidx(c), g, 0))
    bc_spec = pl.BlockSpec((None, SSD_CHUNK, SSD_STATE), lambda g, c: (g, cidx(c), 0))
    sc_spec = pl.BlockSpec((SSD_HPG, 1, 1), lambda g, c: (g, 0, 0))
    st_spec = pl.BlockSpec((SSD_HPG, None, SSD_HEAD_DIM, SSD_STATE), lambda g, c: (g, cidx(c), 0, 0))
    return nc, x_spec, dt_spec, bc_spec, sc_spec, st_spec


def ssd_scan_fwd(x, dt, bm, cm, alog, dskip, name, ex=None):
    t = x.shape[1]
    nc, x_spec, dt_spec, bc_spec, sc_spec, st_spec = _ssd_specs(t, False)

    def body(x_ref, dt_ref, b_ref, c_ref, al_ref, d_ref, y_ref, sin_ref, state):
        @pl.when(pl.program_id(1) == 0)
        def _():
            state[...] = jnp.zeros_like(state)

        bm_, cm_ = b_ref[...], c_ref[...]

        for r in range(SSD_HPG):
            sprev = state[r]
            sin_ref[r] = sprev
            y, snew = _ssd_chunk(x_ref[r], dt_ref[r:r + 1, :], bm_, cm_, sprev, al_ref[r], d_ref[r])
            y_ref[r] = y
            state[r] = snew

    (y, sin), bufs = comm_call(
        body, name=name, grid=(SSD_GROUPS, nc),
        in_specs=[x_spec, dt_spec, bc_spec, bc_spec, sc_spec, sc_spec],
        out_specs=[x_spec, st_spec],
        out_shape=[jax.ShapeDtypeStruct(x.shape, F32),
                   jax.ShapeDtypeStruct((SSD_HEADS, nc, SSD_HEAD_DIM, SSD_STATE), F32)],
        scratch_shapes=[pltpu.VMEM((SSD_HPG, SSD_HEAD_DIM, SSD_STATE), F32)],
        inputs=[x, dt, bm, cm, alog, dskip], sem=("parallel", "arbitrary"), ex=ex)
    return y, sin, bufs


def ssd_scan_bwd(x, dt, bm, cm, alog, dskip, sin, dy, name, ex=None):
    t = x.shape[1]
    nc, x_spec, dt_spec, bc_spec, sc_spec, st_spec = _ssd_specs(t, True)

    def body(x_ref, dt_ref, b_ref, c_ref, al_ref, d_ref, sin_ref, dy_ref,
             dx_ref, ddt_ref, db_ref, dc_ref, dal_ref, dd_ref, dstate):
        @pl.when(pl.program_id(1) == 0)
        def _():
            dstate[...] = jnp.zeros_like(dstate)
            dal_ref[...] = jnp.zeros_like(dal_ref)
            dd_ref[...] = jnp.zeros_like(dd_ref)

        db_ref[...] = jnp.zeros_like(db_ref)
        dc_ref[...] = jnp.zeros_like(dc_ref)
        bm_, cm_ = b_ref[...], c_ref[...]

        for r in range(SSD_HPG):
            _, vjp = jax.vjp(_ssd_chunk, x_ref[r], dt_ref[r:r + 1, :], bm_, cm_, sin_ref[r], al_ref[r], d_ref[r])
            dx, ddt, dbm, dcm, dsp, dal, dd = vjp((dy_ref[r], dstate[r]))
            dx_ref[r] = dx
            ddt_ref[r:r + 1, :] = ddt
            db_ref[...] += dbm
            dc_ref[...] += dcm
            dstate[r] = dsp
            dal_ref[r] += dal
            dd_ref[r] += dd

    return comm_call(
        body, name=name, grid=(SSD_GROUPS, nc),
        in_specs=[x_spec, dt_spec, bc_spec, bc_spec, sc_spec, sc_spec, st_spec, x_spec],
        out_specs=[x_spec, dt_spec, bc_spec, bc_spec, sc_spec, sc_spec],
        out_shape=[jax.ShapeDtypeStruct(x.shape, F32), jax.ShapeDtypeStruct(dt.shape, F32),
                   jax.ShapeDtypeStruct(bm.shape, F32), jax.ShapeDtypeStruct(cm.shape, F32),
                   jax.ShapeDtypeStruct(alog.shape, F32), jax.ShapeDtypeStruct(dskip.shape, F32)],
        scratch_shapes=[pltpu.VMEM((SSD_HPG, SSD_HEAD_DIM, SSD_STATE), F32)],
        inputs=[x, dt, bm, cm, alog, dskip, sin, dy], sem=("arbitrary", "arbitrary"), ex=ex)


def _to_heads(a, n, w):
    t = a.shape[0]
    return a.reshape(t, n, w).transpose(1, 0, 2)


def _from_heads(a):
    n, t, w = a.shape
    return a.transpose(1, 0, 2).reshape(t, n * w)


def _add(acc, h):
    return (acc + h,)


ATTN_FWD_US, ATTN_BWD_US, SCAN_FWD_US, SCAN_BWD_US = 700, 1600, 560, 1280


def sb_fwd(h, hn, w, j, tag):
    gqk = w["sb_gqk"][j]
    qkv, qkv_bf = w.mm_fwd(hn, "sb_w_qkv", j, "nn", f"{tag}_qkv", out_dtypes=(F32, BF16),
                           epilogue=lambda acc: (acc, acc))
    grid, data, params, outs = _qknorm_specs(qkv, gqk)
    qk = blockwise_fwd(_rmsnorm_f, grid, data, params, outs, f"{tag}_qknorm")[0]
    o, bufs = attn_fwd(qk, qkv_bf, f"{tag}_attn", w.host_fwd(ATTN_FWD_US))
    w.done_fwd(bufs)
    h1 = w.mm_fwd(o, "sb_w_o", j, "nn", f"{tag}_out", epilogue=_add, extras=(h,))
    return h1, (qkv, qkv_bf, qk, o)


def sb_bwd(g_bf, hn, saved, w, j, tag):
    qkv, qkv_bf, qk, o = saved
    gqk = w["sb_gqk"][j]
    w.add_grad("sb_w_o", j, w.mm_bwd(o, g_bf, "tn", f"{tag}_dwo", out_dtypes=(BF16,)))
    do = w.mm_bwd(g_bf, w["sb_w_o"], "nt", f"{tag}_do", b_layer=j, out_dtypes=(BF16,))
    dqk, dv, bufs = attn_bwd(qk, qkv_bf, do, f"{tag}_attn_bwd", w.host_bwd(ATTN_BWD_US))
    w.recv.update(bufs)
    grid, data, params, outs = _qknorm_specs(qkv, gqk)
    (draw,), (dgqk,) = blockwise_bwd(_rmsnorm_f, grid, data, params, outs, [dqk], f"{tag}_qknorm_bwd",
                                     grad_dtypes=[BF16])
    dqkv = jnp.concatenate([draw[:, :2 * SB_HEADS * HEAD_DIM], dv], axis=1)
    w.add_grad("sb_w_qkv", j, w.mm_bwd(hn, dqkv, "tn", f"{tag}_dwqkv", out_dtypes=(BF16,)))
    dhn = w.mm_bwd(dqkv, w["sb_w_qkv"], "nt", f"{tag}_dhn", b_layer=j)
    return dhn, {"sb_gqk": dgqk}


def _gm_act_specs(p, gv):
    t = p.shape[0]
    tr = _row_tile(t)
    d = D_MODEL
    data = [Data(p, (tr, d), lambda o, i: (i, 0)), Data(p, (tr, d), lambda o, i: (i, 1))]
    params = [Param(gv, (1, d), lambda o: (0, 0), 1)]
    outs = [Out((t, d), F32, (tr, d), lambda o, i: (i, 0)), Out((t, d), BF16, (tr, d), lambda o, i: (i, 0))]
    return (1, t // tr), data, params, outs


def _gm_gate_specs(u, vn, ws, bs):
    t = u.shape[0]
    rows = _tile(t, GM_STEP_CHUNKS * GM_CHUNK)
    blk = (rows, LANES)
    data = [Data(u, blk, lambda o, i: (i, o)), Data(vn, blk, lambda o, i: (i, o))]
    params = [Param(ws, (None, GM_CHUNK, GM_CHUNK), lambda o: (o, 0, 0), 1),
              Param(bs, (None, GM_CHUNK, 1), lambda o: (o, 0, 0), 1)]
    outs = [Out((t, D_MODEL), BF16, blk, lambda o, i: (i, o))]
    return (GM_GROUPS, t // rows), data, params, outs


def gm_fwd(h, hn, w, j, tag):
    p = w.mm_fwd(hn, "gm_w_in", j, "nn", f"{tag}_in")
    grid, data, params, outs = _gm_act_specs(p, w["gm_v_norm_g"])
    u, vn = blockwise_fwd(_gm_act_f, grid, data, params, outs, f"{tag}_act")
    grid, data, params, outs = _gm_gate_specs(u, vn, w["gm_w_s"], w["gm_b_s"])
    y = blockwise_fwd(_gm_gate_f, grid, data, params, outs, f"{tag}_gate")[0]
    h1 = w.mm_fwd(y, "gm_w_o", j, "nn", f"{tag}_out", epilogue=_add, extras=(h,))
    return h1, (p, u, vn, y)


def gm_bwd(g_bf, hn, saved, w, j, tag):
    p, u, vn, y = saved
    w.add_grad("gm_w_o", j, w.mm_bwd(y, g_bf, "tn", f"{tag}_dwo", out_dtypes=(BF16,)))
    dy = w.mm_bwd(g_bf, w["gm_w_o"], "nt", f"{tag}_dy", b_layer=j)
    grid, data, params, outs = _gm_gate_specs(u, vn, w["gm_w_s"], w["gm_b_s"])
    (du, dvn), (dws, dbs) = blockwise_bwd(_gm_gate_f, grid, data, params, outs, [dy], f"{tag}_gate_bwd",
                                          grad_dtypes=[F32, F32])
    grid, data, params, outs = _gm_act_specs(p, w["gm_v_norm_g"])
    (dpu, dpv), (dgv,) = blockwise_bwd(_gm_act_f, grid, data, params, outs, [du, dvn], f"{tag}_act_bwd",
                                       grad_dtypes=[BF16, BF16])
    dp = jnp.concatenate([dpu[:, :D_MODEL], dpv[:, D_MODEL:]], axis=1)
    w.add_grad("gm_w_in", j, w.mm_bwd(hn, dp, "tn", f"{tag}_dwin", out_dtypes=(BF16,)))
    dhn = w.mm_bwd(dp, w["gm_w_in"], "nt", f"{tag}_dhn", b_layer=j)
    return dhn, {"gm_v_norm_g": dgv, "gm_w_s": dws, "gm_b_s": dbs}


def _ssd_gate_specs(y, z, g):
    t = y.shape[0]
    tr = _row_tile(t)
    gw = SSD_INNER // SSD_GROUPS
    blk = (tr, gw)
    data = [Data(y, blk, lambda o, i: (i, o)), Data(z, blk, lambda o, i: (i, o))]
    params = [Param(g, (None, 1, gw), lambda o: (o, 0, 0), 1)]
    outs = [Out((t, SSD_INNER), BF16, blk, lambda o, i: (i, o))]
    return (SSD_GROUPS, t // tr), data, params, outs


def _dt_specs(raw, bias):
    t = raw.shape[0]
    tr = _row_tile(t)
    data = [Data(raw, (tr, DT_PAD), lambda o, i: (i, 0))]
    params = [Param(bias, (1, DT_PAD), lambda o: (0, 0), 1)]
    outs = [Out((t, DT_PAD), F32, (tr, DT_PAD), lambda o, i: (i, 0))]
    return (1, t // tr), data, params, outs


def ssd_fwd(h, hn, w, tag):
    w.need("ssd_w_in", None)
    z = w.mm_fwd(hn, "ssd_wz", None, "nn", f"{tag}_inz")
    xbc = w.mm_fwd(hn, "ssd_wxbc", None, "nn", f"{tag}_inx")
    raw = w.mm_fwd(hn, "ssd_wdt", None, "nn", f"{tag}_indt")
    act = conv_fwd(xbc, w["ssd_conv_w"], w["ssd_conv_b"], f"{tag}_conv")
    grid, data, params, outs = _dt_specs(raw, w["ssd_dt_bias"])
    dt = blockwise_fwd(_dt_f, grid, data, params, outs, f"{tag}_dt")[0]
    xh = _to_heads(act[:, :SSD_INNER], SSD_HEADS, SSD_HEAD_DIM)
    bh = _to_heads(act[:, SSD_INNER:SSD_INNER + SSD_BC], SSD_GROUPS, SSD_STATE)
    ch = _to_heads(act[:, SSD_INNER + SSD_BC:], SSD_GROUPS, SSD_STATE)
    dth = dt[:, :SSD_HEADS].reshape(-1, SSD_CHUNK, SSD_HEADS).transpose(0, 2, 1)
    yh, sin, bufs = ssd_scan_fwd(xh, dth, bh, ch, w["ssd_a_log"], w["ssd_d"], f"{tag}_scan",
                                 w.host_fwd(SCAN_FWD_US))
    w.done_fwd(bufs)
    y = _from_heads(yh)
    grid, data, params, outs = _ssd_gate_specs(y, z, w["ssd_norm_g"])
    yn = blockwise_fwd(_ssd_gate_f, grid, data, params, outs, f"{tag}_gate")[0]
    h1 = w.mm_fwd(yn, "ssd_w_o", 0, "nn", f"{tag}_out", epilogue=_add, extras=(h,))
    return h1, (z, xbc, raw, xh, bh, ch, dth, sin, y, yn)


def ssd_bwd(g_bf, hn, saved, w, tag):
    z, xbc, raw, xh, bh, ch, dth, sin, y, yn = saved
    w.add_grad("ssd_w_o", 0, w.mm_bwd(yn, g_bf, "tn", f"{tag}_dwo", out_dtypes=(BF16,)))
    dyn = w.mm_bwd(g_bf, w["ssd_w_o"], "nt", f"{tag}_dyn", b_layer=0)
    grid, data, params, outs = _ssd_gate_specs(y, z, w["ssd_norm_g"])
    (dy, dz), (dng,) = blockwise_bwd(_ssd_gate_f, grid, data, params, outs, [dyn], f"{tag}_gate_bwd",
                                     grad_dtypes=[F32, BF16])
    (dxh, ddth, dbh, dch, dal, ddk), bufs = ssd_scan_bwd(
        xh, dth, bh, ch, w["ssd_a_log"], w["ssd_d"], sin, _to_heads(dy, SSD_HEADS, SSD_HEAD_DIM),
        f"{tag}_scan_bwd", w.host_bwd(SCAN_BWD_US))
    w.recv.update(bufs)
    dact = jnp.concatenate([_from_heads(dxh), _from_heads(dbh), _from_heads(dch)], axis=1)
    dxbc, dcw, dcb = conv_bwd(xbc, w["ssd_conv_w"], w["ssd_conv_b"], dact, f"{tag}_conv_bwd")
    ddt = jnp.pad(ddth.transpose(0, 2, 1).reshape(-1, SSD_HEADS), ((0, 0), (0, DT_PAD - SSD_HEADS)))
    grid, data, params, outs = _dt_specs(raw, w["ssd_dt_bias"])
    (draw,), (dbias,) = blockwise_bwd(_dt_f, grid, data, params, outs, [ddt], f"{tag}_dt_bwd",
                                      grad_dtypes=[BF16])
    d_wz = w.mm_bwd(hn, dz, "tn", f"{tag}_dwz", out_dtypes=(BF16,))
    d_wx = w.mm_bwd(hn, dxbc, "tn", f"{tag}_dwx", out_dtypes=(BF16,))
    d_wdt = w.mm_bwd(hn, draw, "tn", f"{tag}_dwdt", out_dtypes=(BF16,))
    d_win = jnp.concatenate([d_wz, d_wx, d_wdt[:, :SSD_HEADS]], axis=1)
    d_win = d_win.reshape(D_MODEL, N_DEV, SSD_PROJ // N_DEV).transpose(1, 0, 2)
    w.add_grad("ssd_w_in", 0, d_win)
    dhn = w.mm_bwd(dz, w["ssd_wz"], "nt", f"{tag}_dhn_z")
    dhn = w.mm_bwd(dxbc, w["ssd_wxbc"], "nt", f"{tag}_dhn_x", epilogue=_add, extras=(dhn,))
    dhn = w.mm_bwd(draw, w["ssd_wdt"], "nt", f"{tag}_dhn_dt", epilogue=_add, extras=(dhn,))
    return dhn, {"ssd_conv_w": dcw, "ssd_conv_b": dcb, "ssd_dt_bias": dbias[:, :SSD_HEADS], "ssd_a_log": dal,
                 "ssd_d": ddk, "ssd_norm_g": dng}


def _relu2(acc):
    return (acc, jnp.square(jnp.maximum(acc, 0.0)))


def _relu2_bwd(acc, a):
    return (acc * (2.0 * jnp.maximum(a, 0.0)),)


def device_step(x, target, w):
    h = x
    saved = []
    for i in range(DEPTH):
        kind, j = i % N_MIXERS, i // N_MIXERS
        tag = f"l{i}"
        hn = rmsnorm_fwd(h, w["norm_mix_g"][i], f"{tag}_norm_mix")
        if kind == 0:
            h1, ms = sb_fwd(h, hn, w, j, f"{tag}_sb")
        elif kind == 1:
            h1, ms = gm_fwd(h, hn, w, j, f"{tag}_gm")
        else:
            h1, ms = ssd_fwd(h, hn, w, f"{tag}_ssd")
        hm = rmsnorm_fwd(h1, w["norm_mlp_g"][i], f"{tag}_norm_mlp")
        a, r = w.mm_fwd(hm, "mlp_w_in", i, "nn", f"{tag}_mlp_in", out_dtypes=(F32, BF16), epilogue=_relu2)
        h2 = w.mm_fwd(r, "mlp_w_out", i, "nn", f"{tag}_mlp_out", epilogue=_add, extras=(h1,))
        saved.append((h, hn, ms, h1, hm, a, r))
        h = h2
    dh, g_bf, loss = loss_call(h, target)

    small = {"norm_mix_g": [None] * DEPTH, "norm_mlp_g": [None] * DEPTH, "sb_gqk": [None] * 2}
    for i in reversed(range(DEPTH)):
        kind, j = i % N_MIXERS, i // N_MIXERS
        tag = f"l{i}"
        h0, hn, ms, h1, hm, a, r = saved[i]
        w.add_grad("mlp_w_out", i, w.mm_bwd(r, g_bf, "tn", f"{tag}_mlp_dwout", out_dtypes=(BF16,)))
        da = w.mm_bwd(g_bf, w["mlp_w_out"], "nt", f"{tag}_mlp_da", b_layer=i, out_dtypes=(BF16,),
                      epilogue=_relu2_bwd, extras=(a,))
        w.add_grad("mlp_w_in", i, w.mm_bwd(hm, da, "tn", f"{tag}_mlp_dwin", out_dtypes=(BF16,)))
        dhm = w.mm_bwd(da, w["mlp_w_in"], "nt", f"{tag}_mlp_dhm", b_layer=i)
        dh1, g_bf, small["norm_mlp_g"][i] = rmsnorm_bwd(h1, w["norm_mlp_g"][i], dhm, dh, f"{tag}_norm_mlp_bwd")
        if kind == 0:
            dhn, mg = sb_bwd(g_bf, hn, ms, w, j, f"{tag}_sb")
            small["sb_gqk"][j] = mg.pop("sb_gqk")
        elif kind == 1:
            dhn, mg = gm_bwd(g_bf, hn, ms, w, j, f"{tag}_gm")
        else:
            dhn, mg = ssd_bwd(g_bf, hn, ms, w, f"{tag}_ssd")
        small.update(mg)
        dh, g_bf, small["norm_mix_g"][i] = rmsnorm_bwd(h0, w["norm_mix_g"][i], dhn, dh1, f"{tag}_norm_mix_bwd",
                                                       with_bf16=i > 0)
    dcw = small["ssd_conv_w"].reshape(SSD_CONV, N_DEV, SMALL_W).transpose(1, 0, 2)
    dcb = small["ssd_conv_b"].reshape(N_DEV, 1, SMALL_W)
    dng = jnp.pad(small["ssd_norm_g"], ((0, 0), (0, 0), (0, SMALL_W - SSD_INNER // N_DEV)))
    w.add_grad("small", None, jnp.concatenate([dcw, dcb, dng, jnp.zeros((N_DEV, 2, SMALL_W), F32)], axis=1))
    w.add_grad("repl", None, _pack(_repl_grads(small), w.shapes["repl"][0]))
    return loss, dh


def _my_pos():
    return lax.axis_index("x"), lax.axis_index("y"), lax.axis_index("c")


def _lin(p):
    return 4 * p[0] + 2 * p[1] + p[2]


_FLIPS = [(0, 0, 1), (0, 1, 0), (0, 1, 1), (1, 0, 0), (1, 0, 1), (1, 1, 0), (1, 1, 1)]
_SIBLING = (0, 0, 1)
_AG_FLIPS = [_SIBLING, (1, 0, 0), (0, 1, 0), (1, 1, 0)]


def _flip(pos, f):
    return tuple((1 - p) if b else p for p, b in zip(pos, f))


def _other_chip(pos, j):
    x, y = pos[0], pos[1]
    return [(1 - x, y), (x, 1 - y), (1 - x, 1 - y)][j]


class Exchange:
    def __init__(self):
        self.sources, self.buffers, self.items, self.locals = [], [], [], []

    def source(self, array):
        self.sources.append(array)
        return len(self.sources) - 1

    def buffer(self, name, bufs, shape, dtype):
        if name not in [b[0] for b in self.buffers]:
            self.buffers.append((name, bufs.get(name), jax.ShapeDtypeStruct(shape, dtype)))

    def n_copies(self):
        return sum(len(flips) for _, _, flips in self.items)


def _exchange_fns(ex, srcs, bufs, send_sems, recv_sems, local_sems):
    def local(k):
        sf, df = ex.locals[k]
        me = _my_pos()
        return pltpu.make_async_copy(sf(srcs, bufs, me, me), df(bufs, me), local_sems.at[k])

    def remote(sending):
        me = _my_pos()
        k = 0
        for sf, df, flips in ex.items:
            for f in flips:
                peer = _flip(me, f)
                yield pltpu.make_async_remote_copy(
                    src_ref=sf(srcs, bufs, me, peer), dst_ref=df(bufs, me if sending else peer),
                    send_sem=send_sems.at[k], recv_sem=recv_sems.at[k], device_id=peer, device_id_type=MESH)
                k += 1

    def start():
        for k in range(len(ex.locals)):
            local(k).start()
        for cp in remote(True):
            cp.start()

    def wait():
        arrivals = list(remote(False))
        for cp in arrivals:
            cp.wait_recv()
        for cp in arrivals:
            cp.wait_send()
        for k in range(len(ex.locals)):
            local(k).wait()

    return start, wait


def comm_call(core, *, name, grid, in_specs, out_specs, out_shape, scratch_shapes, inputs, sem, ex=None):
    n_in, n_out, n_scr = len(in_specs), len(out_specs), len(scratch_shapes)
    if ex is None:
        res = pl.pallas_call(core, name=name, grid=grid, in_specs=in_specs, out_specs=out_specs,
                             out_shape=out_shape, scratch_shapes=scratch_shapes,
                             compiler_params=_cparams(*sem))(*inputs)
        return list(res), {}
    existing = [(k, b[1]) for k, b in enumerate(ex.buffers) if b[1] is not None]
    names = [b[0] for b in ex.buffers]
    n_src, n_old, n_buf = len(ex.sources), len(existing), len(names)

    def body(*refs):
        cuts = [n_in, n_src, n_old, n_out, n_buf, n_scr, 3]
        parts, pos = [], 0
        for c in cuts:
            parts.append(refs[pos:pos + c])
            pos += c
        core_in, srcs, _, core_out, buf_refs, core_scr, sems = parts
        start, wait = _exchange_fns(ex, srcs, dict(zip(names, buf_refs)), *sems)
        if not grid:
            start()
            wait()
            return
        ids = [pl.program_id(d) for d in range(len(grid))]
        pl.when(functools.reduce(jnp.logical_and, [i == 0 for i in ids]))(start)
        core(*core_in, *core_out, *core_scr)
        pl.when(functools.reduce(jnp.logical_and, [i == g - 1 for i, g in zip(ids, grid)]))(wait)

    dma = pltpu.SemaphoreType.DMA
    res = pl.pallas_call(
        body, name=name, **({"grid": grid} if grid else {}),
        in_specs=list(in_specs) + [HBM_SPEC] * (n_src + n_old),
        out_specs=list(out_specs) + [HBM_SPEC] * n_buf,
        out_shape=list(out_shape) + [b[2] for b in ex.buffers],
        scratch_shapes=list(scratch_shapes) + [dma((max(ex.n_copies(), 1),)), dma((max(ex.n_copies(), 1),)),
                                               dma((max(len(ex.locals), 1),))],
        input_output_aliases={n_in + n_src + e: n_out + k for e, (k, _) in enumerate(existing)},
        compiler_params=_cparams(*(["arbitrary"] * len(grid))),
    )(*inputs, *ex.sources, *[b for _, b in existing])
    return list(res[:n_out]), dict(zip(names, res[n_out:]))


def run_exchange(ex, name):
    return comm_call(None, name=name, grid=(), in_specs=[], out_specs=[], out_shape=[], scratch_shapes=[],
                     inputs=[], sem=(), ex=ex)[1]


def _gathered_shape(shard, kind):
    s = shard.shape
    if kind == "col":
        return s[:2] + (s[2] * N_DEV,)
    if kind == "row":
        return (s[0], s[1] * N_DEV, s[2])
    return (N_DEV,) + s


def _n_pieces(nbytes, rows, target):
    n = 1
    while nbytes > n * target and rows % (32 * n) == 0 and n < 8:
        n *= 2
    return n


def _ag_part(buf, tn, layer, r0, nr, who, shards):
    kind, shard, idx = SHARD_KIND[tn], shards[tn], _lin(who)
    if kind == "col":
        return buf.at[layer, pl.ds(r0, nr), pl.ds(pl.multiple_of(idx * shard.shape[2], LANES), shard.shape[2])]
    if kind == "row":
        return buf.at[layer, pl.ds(pl.multiple_of(idx * shard.shape[1] + r0, 16), nr), :]
    return buf.at[idx] if tn == "small" else buf.at[idx, 0, pl.ds(r0, nr), :]


def _ag_items(ex, piece, phase, shards, bufs):
    tn, layer, r0, nr, _ = piece
    shard = shards[tn]
    ex.buffer(tn, bufs, _gathered_shape(shard, SHARD_KIND[tn]), shard.dtype)

    def part(b, who):
        return _ag_part(b[tn], tn, layer, r0, nr, who, shards)

    if phase == "A":
        def src(s, b, me, peer, si=ex.source(shard)):
            return s[si] if tn == "small" else s[si].at[0 if layer is None else layer, pl.ds(r0, nr), :]

        ex.items.append((src, part, _AG_FLIPS))
        ex.locals.append((src, part))
        return
    for j in range(3):
        def held(who, j=j):
            return (*_other_chip(who, j), who[2])

        ex.items.append((lambda s, b, me, peer, held=held: part(b, held(me)),
                         lambda b, sender, held=held: part(b, held(sender)), [_SIBLING]))


def _rs_items(ex, piece, bufs, shapes):
    tn, layer, r0, nr, _, g = piece
    kind, shape = SHARD_KIND[tn], shapes[tn]
    ex.buffer(tn, bufs, (N_DEV,) + tuple(shape), g.dtype)

    def src(s, b, me, peer, si=ex.source(g)):
        idx = _lin(peer)
        if kind == "all":
            return s[si]
        if tn == "small":
            return s[si].at[idx]
        if kind == "slot":
            return s[si].at[idx, pl.ds(r0, nr), :]
        if kind == "col":
            return s[si].at[pl.ds(r0, nr), pl.ds(pl.multiple_of(idx * shape[2], LANES), shape[2])]
        return s[si].at[pl.ds(pl.multiple_of(idx * shape[1] + r0, 16), nr), :]

    def dst(b, sender):
        idx = _lin(sender)
        return b[tn].at[idx] if layer is None else b[tn].at[idx, layer, pl.ds(r0, nr), :]

    ex.items.append((src, dst, _FLIPS))
    ex.locals.append((src, dst))


AG_BYTES_PER_US = 22e3
RS_BYTES_PER_US = 75e3
AG_PIECE_BYTES = 2.3e6
RS_PIECE_BYTES = 4.5e6
MXU_FLOPS_PER_US = 8.6e8
MATMUL_HOST_MIN_US = {"fwd": 0, "bwd": 110}
MATMUL_HOST_SHARE = 1.0


class Traffic(dict):
    def __init__(self, small_weights, shards, shapes):
        super().__init__(small_weights)
        self.shards, self.shapes, self.recv = shards, shapes, {}
        self.wait_a, self.wait_b, self.riding, self.unfinished = [], [], ([], []), {}
        for group in AG_GROUPS:
            for tn, layer in group:
                shard = shards[tn]
                rows = shard.shape[-2]
                nbytes = rows * shard.shape[-1] * shard.dtype.itemsize
                n = 1 if tn == "small" else _n_pieces(nbytes, rows, AG_PIECE_BYTES)
                self.unfinished[(tn, layer)] = n
                self.wait_a += [(tn, layer, k * (rows // n), rows // n, nbytes / n) for k in range(n)]
        self.grads, self.grads_left, self.n_calls = [], {}, 0

    def host_fwd(self, us, exclude=(), only=None):
        ex, budget = Exchange(), us * AG_BYTES_PER_US
        ok = (lambda p: p[:2] == only) if only else (lambda p: p[0] not in exclude)
        take_b = [p for p in self.wait_b if ok(p)]
        take_a = []
        for p in [p for p in self.wait_a if ok(p)]:
            if p[4] > budget and not only:
                break
            take_a.append(p)
            budget -= p[4]
        for p in take_b:
            self.wait_b.remove(p)
            _ag_items(ex, p, "B", self.shards, self)
        for p in take_a:
            self.wait_a.remove(p)
            _ag_items(ex, p, "A", self.shards, self)
        self.riding = (take_a, take_b)
        return ex if ex.items else None

    def done_fwd(self, bufs):
        self.update(bufs)
        take_a, take_b = self.riding
        self.wait_b += take_a
        for p in take_b:
            self.unfinished[p[:2]] -= 1
        self.riding = ([], [])

    def need(self, tn, layer):
        while self.unfinished.get((tn, layer), 0) > 0:
            self.n_calls += 1
            self.done_fwd(run_exchange(self.host_fwd(0, only=(tn, layer)), f"gather_now{self.n_calls}"))
        if tn == "ssd_w_in" and "ssd_wz" not in self:
            self.need("small", None)
            _ssd_weights(self)

    def mm_fwd(self, a, tn, layer, form, name, **kw):
        self.need(tn, layer)
        ex = self.host_fwd(_host_us(a, self[tn], form, "fwd"), exclude=(tn,))
        if ex is None:
            return matmul(a, self[tn], form, name, b_layer=layer, **kw)
        out, bufs = matmul(a, self[tn], form, name, b_layer=layer, ex=ex, **kw)
        self.done_fwd(bufs)
        return out

    def add_grad(self, tn, layer, g):
        rows = self.shapes[tn][-2]
        nbytes = g.size * g.dtype.itemsize
        n = 1 if layer is None else _n_pieces(nbytes, rows, RS_PIECE_BYTES)
        self.grads += [(tn, layer, k * (rows // n), rows // n, nbytes / n, g) for k in range(n)]
        self.grads_left[tn] = self.grads_left.get(tn, 0) + n

    def host_bwd(self, us, exclude=(), only=None):
        ex, budget = Exchange(), us * RS_BYTES_PER_US
        for p in [p for p in self.grads if (p[0] == only if only else p[0] not in exclude)]:
            if p[4] > budget and not only:
                continue
            self.grads.remove(p)
            self.grads_left[p[0]] -= 1
            _rs_items(ex, p, self.recv, self.shapes)
            budget -= p[4]
        return ex if ex.items else None

    def need_grad(self, tn):
        if self.grads_left.get(tn, 0) > 0:
            self.n_calls += 1
            self.recv.update(run_exchange(self.host_bwd(0, only=tn), f"scatter_now{self.n_calls}"))

    def mm_bwd(self, a, b, form, name, **kw):
        ex = self.host_bwd(_host_us(a, b, form, "bwd"))
        if ex is None:
            return matmul(a, b, form, name, **kw)
        out, bufs = matmul(a, b, form, name, ex=ex, **kw)
        self.recv.update(bufs)
        return out


def _host_us(a, b, form, direction):
    k = a.shape[0] if form == "tn" else a.shape[1]
    us = 2.0 * a.size // k * b.shape[-2] * b.shape[-1] / MXU_FLOPS_PER_US
    return MATMUL_HOST_SHARE * us if us >= MATMUL_HOST_MIN_US[direction] else 0.0


_C1 = 1.0 - ADAM_B1 ** ADAM_STEP
_C2 = 1.0 - ADAM_B2 ** ADAM_STEP
ADAMW_BYTES_PER_ELEMENT = 7 * 4 + N_DEV * 2
HBM_BYTES_PER_US = 2.9e6


def adamw(wt, m, v, slots, name, ex=None):
    nl, r, c = wt.shape
    tr = _tile(r, 128)

    def body(w_ref, m_ref, v_ref, s_ref, g_ref, d_ref, nm_ref, nv_ref):
        g = s_ref[0].astype(F32)
        for k in range(1, N_DEV):
            g = g + s_ref[k].astype(F32)
        m_new = ADAM_B1 * m_ref[...] + (1.0 - ADAM_B1) * g
        v_new = ADAM_B2 * v_ref[...] + (1.0 - ADAM_B2) * jnp.square(g)
        m_hat = m_new / _C1
        v_hat = v_new / _C2
        g_ref[...] = g
        d_ref[...] = -ADAM_LR * (m_hat / (jnp.sqrt(v_hat) + ADAM_EPS) + ADAM_WD * w_ref[...])
        nm_ref[...] = m_new
        nv_ref[...] = v_new

    blk = pl.BlockSpec((None, tr, c), lambda l, i: (l, i, 0))
    sblk = pl.BlockSpec((N_DEV, None, tr, c), lambda l, i: (0, l, i, 0))
    return comm_call(
        body, name=name, grid=(nl, r // tr), in_specs=[blk, blk, blk, sblk], out_specs=[blk] * 4,
        out_shape=[jax.ShapeDtypeStruct(wt.shape, F32)] * 4, scratch_shapes=[],
        inputs=[wt, m, v, slots], sem=("parallel", "parallel"), ex=ex)


def _pack(arrays, rows):
    flat = jnp.concatenate([a.reshape(-1).astype(F32) for a in arrays])
    return jnp.pad(flat, (0, rows * LANES - flat.shape[0])).reshape(rows, LANES)


def _unpack(packed, shapes):
    flat = packed.reshape(-1)
    out, off = [], 0
    for s in shapes:
        sz = math.prod(s)
        out.append(flat[off:off + sz].reshape(s))
        off += sz
    return out


SMALL_W = 768


def _pack_small_shard(conv_w, conv_b, norm_g):
    ng = jnp.pad(norm_g, ((0, 0), (0, SMALL_W - norm_g.shape[1])))
    return jnp.concatenate([conv_w[0], conv_b, ng, jnp.zeros((2, SMALL_W), F32)], axis=0)


def _unpack_small_shard(p):
    return p[0:4][None], p[4:5], p[5:6, :SSD_INNER // N_DEV]


REPL = ["norm_mix_g", "norm_mlp_g", "sb_q_norm_g", "sb_k_norm_g", "gm_v_norm_g", "gm_w_s", "gm_b_s",
        "ssd_dt_bias", "ssd_a_log", "ssd_d"]
BIG = ["sb_w_qkv", "sb_w_o", "gm_w_in", "gm_w_o", "ssd_w_in", "ssd_w_o", "mlp_w_in", "mlp_w_out"]
BIG_KIND = {"sb_w_qkv": "col", "sb_w_o": "row", "gm_w_in": "col", "gm_w_o": "row", "ssd_w_in": "slot",
            "ssd_w_o": "row", "mlp_w_in": "col", "mlp_w_out": "row"}
SMALL_SHARDED = ["ssd_conv_w", "ssd_conv_b", "ssd_norm_g"]
WEIGHTS = ["norm_mix_g", "norm_mlp_g", "sb_w_qkv", "sb_q_norm_g", "sb_k_norm_g", "sb_w_o", "gm_w_in",
           "gm_v_norm_g", "gm_w_s", "gm_b_s", "gm_w_o", "ssd_w_in", "ssd_conv_w", "ssd_conv_b", "ssd_dt_bias",
           "ssd_a_log", "ssd_d", "ssd_norm_g", "ssd_w_o", "mlp_w_in", "mlp_w_out"]


SHARD_KIND = dict(BIG_KIND, small="slot", repl="all")
AG_GROUPS = [
    [("sb_w_qkv", 0), ("sb_w_o", 0), ("mlp_w_in", 0), ("mlp_w_out", 0)],
    [("gm_w_in", 0), ("gm_w_o", 0), ("mlp_w_in", 1), ("mlp_w_out", 1), ("ssd_w_in", None), ("small", None),
     ("ssd_w_o", 0), ("mlp_w_in", 2), ("mlp_w_out", 2)],
    [("sb_w_qkv", 1), ("sb_w_o", 1), ("mlp_w_in", 3), ("mlp_w_out", 3)],
]


def _ssd_weights(w):
    win = jnp.moveaxis(w["ssd_w_in"][:, 0], 0, 1).reshape(D_MODEL, SSD_PROJ)
    small = w["small"]
    w["ssd_wz"] = win[:, :SSD_INNER]
    w["ssd_wxbc"] = win[:, SSD_INNER:SSD_INNER + SSD_CONV_DIM]
    w["ssd_wdt"] = jnp.pad(win[:, SSD_INNER + SSD_CONV_DIM:], ((0, 0), (0, DT_PAD - SSD_HEADS)))
    w["ssd_conv_w"] = small[:, 0:4].transpose(1, 0, 2).reshape(SSD_CONV, SSD_CONV_DIM)
    w["ssd_conv_b"] = small[:, 4].reshape(1, SSD_CONV_DIM)
    w["ssd_norm_g"] = small[:, 5:6, :SSD_INNER // N_DEV]


def _small_weights(p):
    w = {}
    w["norm_mix_g"] = [p["norm_mix_g"][i:i + 1] for i in range(DEPTH)]
    w["norm_mlp_g"] = [p["norm_mlp_g"][i:i + 1] for i in range(DEPTH)]
    w["sb_gqk"] = [jnp.stack([p["sb_q_norm_g"][j:j + 1], p["sb_k_norm_g"][j:j + 1]]) for j in range(2)]
    w["gm_v_norm_g"] = p["gm_v_norm_g"]
    w["gm_w_s"] = p["gm_w_s"][0]
    w["gm_b_s"] = p["gm_b_s"][0][:, :, None]
    w["ssd_dt_bias"] = jnp.pad(p["ssd_dt_bias"], ((0, 0), (0, DT_PAD - SSD_HEADS)))
    w["ssd_a_log"] = p["ssd_a_log"].reshape(SSD_HEADS, 1, 1)
    w["ssd_d"] = p["ssd_d"].reshape(SSD_HEADS, 1, 1)
    return w


def _repl_grads(g):
    return [jnp.concatenate(g["norm_mix_g"], axis=0), jnp.concatenate(g["norm_mlp_g"], axis=0),
            jnp.concatenate([g["sb_gqk"][0][0], g["sb_gqk"][1][0]], axis=0),
            jnp.concatenate([g["sb_gqk"][0][1], g["sb_gqk"][1][1]], axis=0),
            g["gm_v_norm_g"], g["gm_w_s"][None], g["gm_b_s"][None, :, :, 0],
            g["ssd_dt_bias"], g["ssd_a_log"].reshape(1, SSD_HEADS), g["ssd_d"].reshape(1, SSD_HEADS)]


def kernel(x, norm_mix_g, norm_mlp_g, sb_w_qkv, sb_q_norm_g, sb_k_norm_g, sb_w_o, gm_w_in, gm_v_norm_g, gm_w_s, gm_b_s, gm_w_o, ssd_w_in, ssd_conv_w, ssd_conv_b, ssd_dt_bias, ssd_a_log, ssd_d, ssd_norm_g, ssd_w_o, mlp_w_in, mlp_w_out, loss_target, m_norm_mix_g, m_norm_mlp_g, m_sb_w_qkv, m_sb_q_norm_g, m_sb_k_norm_g, m_sb_w_o, m_gm_w_in, m_gm_v_norm_g, m_gm_w_s, m_gm_b_s, m_gm_w_o, m_ssd_w_in, m_ssd_conv_w, m_ssd_conv_b, m_ssd_dt_bias, m_ssd_a_log, m_ssd_d, m_ssd_norm_g, m_ssd_w_o, m_mlp_w_in, m_mlp_w_out, v_norm_mix_g, v_norm_mlp_g, v_sb_w_qkv, v_sb_q_norm_g, v_sb_k_norm_g, v_sb_w_o, v_gm_w_in, v_gm_v_norm_g, v_gm_w_s, v_gm_b_s, v_gm_w_o, v_ssd_w_in, v_ssd_conv_w, v_ssd_conv_b, v_ssd_dt_bias, v_ssd_a_log, v_ssd_d, v_ssd_norm_g, v_ssd_w_o, v_mlp_w_in, v_mlp_w_out):
    args = dict(locals())
    p = {k: args[k] for k in WEIGHTS}
    pm = {k: args["m_" + k] for k in WEIGHTS}
    pv = {k: args["v_" + k] for k in WEIGHTS}

    shards = {k: p[k].astype(BF16) for k in BIG}
    shards["small"] = _pack_small_shard(*[p[k] for k in SMALL_SHARDED])
    repl_shapes = [p[k].shape for k in REPL]
    n_repl = sum(math.prod(s) for s in repl_shapes)
    repl_rows = -(-n_repl // (LANES * LANES)) * LANES
    shard_shapes = {k: p[k].shape for k in BIG}
    shard_shapes.update(small=(8, SMALL_W), repl=(repl_rows, LANES))

    w = Traffic(_small_weights(p), shards, shard_shapes)
    loss, grad_x = device_step(x[0], loss_target[0], w)
    loss = lax.psum(loss[0, 0], ("x", "y", "c"))

    out = {}
    for k in sorted(BIG, key=lambda k: (w.grads_left[k], -p[k].size)):
        w.need_grad(k)
        ex = w.host_bwd(p[k].size * ADAMW_BYTES_PER_ELEMENT / HBM_BYTES_PER_US, exclude=(k,))
        out[k], bufs = adamw(p[k], pm[k], pv[k], w.recv[k], f"adamw_{k}", ex)
        w.recv.update(bufs)
    w.need_grad("small")
    w.need_grad("repl")
    recv = w.recv
    sm, _ = adamw(*[_pack_small_shard(*[d[k] for k in SMALL_SHARDED])[None] for d in (p, pm, pv)],
                  recv["small"][:, None], "adamw_small")
    for k, vals in zip(SMALL_SHARDED, zip(*[_unpack_small_shard(r[0]) for r in sm])):
        out[k] = list(vals)
    rp, _ = adamw(*[_pack([d[k] for k in REPL], repl_rows)[None] for d in (p, pm, pv)],
                  recv["repl"][:, None], "adamw_repl")
    for k, vals in zip(REPL, zip(*[_unpack(r[0], repl_shapes) for r in rp])):
        out[k] = list(vals)
    res = [loss, grad_x[None]]
    for q in range(4):
        res += [out[k][q] for k in WEIGHTS]
    return tuple(res)
```

```python
import functools
import math

import jax
import jax.numpy as jnp
from jax import lax
from jax.experimental import pallas as pl
from jax.experimental.pallas import tpu as pltpu

F32 = jnp.float32
BF16 = jnp.bfloat16

D_MODEL = 2048
DEPTH = 4
N_MIXERS = 3
EPS = 1e-6
SB_HEADS = 16
HEAD_DIM = 128
GM_CHUNK = 128
GM_GROUPS = 16
GM_STEP_CHUNKS = 8
SSD_INNER = 4096
SSD_HEADS = 64
SSD_HEAD_DIM = 64
SSD_GROUPS = 8
SSD_HPG = 8
SSD_STATE = 128
SSD_CHUNK = 128
SSD_CONV = 4
SSD_BC = SSD_GROUPS * SSD_STATE
SSD_CONV_DIM = SSD_INNER + 2 * SSD_BC
SSD_PROJ = SSD_INNER + SSD_CONV_DIM + SSD_HEADS
DT_PAD = 128
N_DEV = 8
LANES = 128

ADAM_LR = 0.001
ADAM_B1 = 0.9
ADAM_B2 = 0.999
ADAM_EPS = 1e-08
ADAM_WD = 0.01
ADAM_STEP = 10

VMEM_LIMIT_BYTES = 56 * 1024 * 1024
MESH = pl.DeviceIdType.MESH
HBM_SPEC = pl.BlockSpec(memory_space=pl.ANY)


def _cparams(*sem):
    return pltpu.CompilerParams(dimension_semantics=sem, vmem_limit_bytes=VMEM_LIMIT_BYTES)


def _tile(dim, pref):
    t = min(dim, pref)
    assert dim % t == 0, (dim, pref)
    return t


_DOT_DIMS = {"nn": ((1,), (0,)), "nt": ((1,), (1,)), "tn": ((0,), (0,))}


def matmul(a, b, form, name, *, b_layer=None, out_dtypes=(F32,), epilogue=None, extras=(),
           tm=1024, tn=1024, tk=2048, ex=None):
    bs = b.shape[-2:]
    if form == "nn":
        (m, k), (k2, n) = a.shape, bs
    elif form == "nt":
        (m, k), (n, k2) = a.shape, bs
    else:
        (k, m), (k2, n) = a.shape, bs
    assert k == k2, (a.shape, b.shape, form)
    tm, tn, tk = _tile(m, tm), _tile(n, tn), _tile(k, tk)
    grid = (m // tm, n // tn, k // tk)
    nk = grid[2]
    if form == "tn":
        a_spec = pl.BlockSpec((tk, tm), lambda i, j, kk: (kk, i))
    else:
        a_spec = pl.BlockSpec((tm, tk), lambda i, j, kk: (i, kk))
    if form == "nt":
        b_blk, b_idx = (tn, tk), (lambda i, j, kk: (j, kk))
    else:
        b_blk, b_idx = (tk, tn), (lambda i, j, kk: (kk, j))
    if b.ndim == 3:
        b_spec = pl.BlockSpec((None,) + b_blk, lambda i, j, kk: (b_layer,) + b_idx(i, j, kk))
    else:
        b_spec = pl.BlockSpec(b_blk, b_idx)
    o_spec = pl.BlockSpec((tm, tn), lambda i, j, kk: (i, j))
    n_extra, n_out = len(extras), len(out_dtypes)
    dims = (_DOT_DIMS[form], ((), ()))

    def body(a_ref, b_ref, *rest):
        extra_refs, out_refs = rest[:n_extra], rest[n_extra:n_extra + n_out]
        kk = pl.program_id(2)

        def product():
            return lax.dot_general(a_ref[...], b_ref[...], dims, preferred_element_type=F32)

        def finish(acc):
            outs = (acc,) if epilogue is None else epilogue(acc, *[r[...] for r in extra_refs])
            for r, o in zip(out_refs, outs):
                r[...] = o.astype(r.dtype)

        if nk == 1:
            finish(product())
            return
        acc_ref = rest[-1]

        @pl.when(kk == 0)
        def _():
            acc_ref[...] = product()

        @pl.when(jnp.logical_and(kk > 0, kk < nk - 1))
        def _():
            acc_ref[...] += product()

        @pl.when(kk == nk - 1)
        def _():
            finish(acc_ref[...] + product())

    res, bufs = comm_call(
        body, name=name, grid=grid,
        in_specs=[a_spec, b_spec] + [o_spec] * n_extra,
        out_specs=[o_spec] * n_out,
        out_shape=[jax.ShapeDtypeStruct((m, n), dt) for dt in out_dtypes],
        scratch_shapes=[pltpu.VMEM((tm, tn), F32)] if nk > 1 else [],
        inputs=[a, b, *extras], sem=("parallel", "parallel", "arbitrary"), ex=ex)
    res = res[0] if n_out == 1 else res
    return res if ex is None else (res, bufs)


class Data:
    def __init__(self, array, block, imap):
        self.array, self.block, self.imap = array, block, imap


class Param:
    def __init__(self, array, block, imap, group):
        self.array, self.block, self.imap, self.group = array, block, imap, group


class Out:
    def __init__(self, shape, dtype, block, imap):
        self.shape, self.dtype, self.block, self.imap = shape, dtype, block, imap


def _f32(v):
    return v.astype(F32)


def blockwise_fwd(f, grid, data, params, outs, name):
    nd, npar = len(data), len(params)

    def body(*refs):
        vals = [_f32(r[...]) for r in refs[:nd + npar]]
        res = f(*vals)
        for r, o in zip(refs[nd + npar:], res):
            r[...] = o.astype(r.dtype)

    in_specs = [pl.BlockSpec(d.block, d.imap) for d in data]
    in_specs += [pl.BlockSpec(p.block, (lambda o, i, _m=p.imap: _m(o))) for p in params]
    res = pl.pallas_call(
        body, name=name, grid=grid, in_specs=in_specs,
        out_specs=[pl.BlockSpec(o.block, o.imap) for o in outs],
        out_shape=[jax.ShapeDtypeStruct(o.shape, o.dtype) for o in outs],
        compiler_params=_cparams("parallel", "parallel"),
    )(*[d.array for d in data], *[p.array for p in params])
    return res


def blockwise_bwd(f, grid, data, params, outs, cts, name, *, grad_dtypes, accum=None, also_bf16=()):
    nd, npar, no = len(data), len(params), len(outs)
    accum = accum or {}
    want = [k for k in range(nd) if grad_dtypes[k] is not None] + list(also_bf16)
    want_dtypes = [grad_dtypes[k] for k in want[:len(want) - len(also_bf16)]] + [BF16] * len(also_bf16)
    acc_keys = sorted(accum)

    def body(*refs):
        in_refs = refs[:nd + npar]
        ct_refs = refs[nd + npar:nd + npar + no]
        acc_refs = refs[nd + npar + no:nd + npar + no + len(acc_keys)]
        out_refs = refs[nd + npar + no + len(acc_keys):]
        dd_refs, dp_refs = out_refs[:len(want)], out_refs[len(want):]
        o, i = pl.program_id(0), pl.program_id(1)
        vals = [_f32(r[...]) for r in in_refs]
        _, vjp = jax.vjp(f, *vals)
        grads = vjp(tuple(_f32(r[...]) for r in ct_refs))
        for r, k in zip(dd_refs, want):
            g = grads[k]
            if k in accum:
                g = g + _f32(acc_refs[acc_keys.index(k)][...])
            r[...] = g.astype(r.dtype)
        for r, p, g in zip(dp_refs, params, grads[nd:]):
            first = jnp.logical_and(i == 0, o % p.group == 0)

            @pl.when(first)
            def _():
                r[...] = g.astype(r.dtype)

            @pl.when(jnp.logical_not(first))
            def _():
                r[...] += g.astype(r.dtype)

    d_specs = [pl.BlockSpec(d.block, d.imap) for d in data]
    p_specs = [pl.BlockSpec(p.block, (lambda o, i, _m=p.imap: _m(o))) for p in params]
    ct_specs = [pl.BlockSpec(o.block, o.imap) for o in outs]
    acc_specs = [d_specs[k] for k in acc_keys]
    res = pl.pallas_call(
        body, name=name, grid=grid,
        in_specs=d_specs + p_specs + ct_specs + acc_specs,
        out_specs=[d_specs[k] for k in want] + p_specs,
        out_shape=[jax.ShapeDtypeStruct(data[k].array.shape, dt) for k, dt in zip(want, want_dtypes)]
        + [jax.ShapeDtypeStruct(p.array.shape, F32) for p in params],
        compiler_params=_cparams("arbitrary", "arbitrary"),
    )(*[d.array for d in data], *[p.array for p in params], *cts, *[accum[k] for k in acc_keys])
    return list(res[:len(want)]), list(res[len(want):])


def _rmsnorm_f(x, g):
    return (x * lax.rsqrt(jnp.mean(x * x, axis=-1, keepdims=True) + EPS) * g,)


def _gelu(x):
    return 0.5 * x * (1.0 + lax.erf(x * (2.0 ** -0.5)))


def _silu(x):
    return x / (1.0 + jnp.exp(-x))


def _softplus(x):
    return jnp.maximum(x, 0.0) + jnp.log1p(jnp.exp(-jnp.abs(x)))


def _gm_act_f(pu, pv, gv):
    return (_gelu(pu),) + _rmsnorm_f(_gelu(pv), gv)


def _gm_gate_f(u, vn, w, b):
    n = w.shape[0]
    c = u.shape[0] // n
    row = lax.broadcasted_iota(jnp.int32, (n, n), 0)
    col = lax.broadcasted_iota(jnp.int32, (n, n), 1)
    wm = jnp.broadcast_to(jnp.where(row >= col, w, 0.0), (c, n, n))
    mixed = lax.dot_general(wm, vn.reshape(c, n, LANES), (((2,), (1,)), ((0,), (0,))),
                            preferred_element_type=F32)
    return (u * (mixed + b[None]).reshape(c * n, LANES),)


def _ssd_gate_f(y, z, g):
    return _rmsnorm_f(y * _silu(z), g)


def _dt_f(raw, bias):
    return (_softplus(raw + bias),)


def _row_tile(t):
    return _tile(t, 256)


def _norm_specs(h, g_row):
    t, d = h.shape
    tr = _row_tile(t)
    data = [Data(h, (tr, d), lambda o, i: (i, 0))]
    params = [Param(g_row, (1, d), lambda o: (0, 0), 1)]
    outs = [Out((t, d), BF16, (tr, d), lambda o, i: (i, 0))]
    return (1, t // tr), data, params, outs


def rmsnorm_fwd(h, g_row, name):
    grid, data, params, outs = _norm_specs(h, g_row)
    return blockwise_fwd(_rmsnorm_f, grid, data, params, outs, name)[0]


def rmsnorm_bwd(h, g_row, ct, skip, name, with_bf16=True):
    grid, data, params, outs = _norm_specs(h, g_row)
    dd, (dg,) = blockwise_bwd(_rmsnorm_f, grid, data, params, outs, [ct], name, grad_dtypes=[F32],
                              accum={0: skip}, also_bf16=(0,) if with_bf16 else ())
    return dd[0], (dd[1] if with_bf16 else None), dg


def loss_call(y, target):
    t, d = y.shape
    tr = _row_tile(t)

    def body(y_ref, t_ref, dy_ref, dyb_ref, loss_ref):
        e = y_ref[...] - t_ref[...]
        dy_ref[...] = e * (1.0 / d)
        dyb_ref[...] = (e * (1.0 / d)).astype(BF16)

        @pl.when(pl.program_id(0) == 0)
        def _():
            loss_ref[...] = jnp.zeros_like(loss_ref)

        loss_ref[...] += jnp.sum(e * e).reshape(1, 1) * (0.5 / d)

    row = pl.BlockSpec((tr, d), lambda i: (i, 0))
    return pl.pallas_call(
        body, name="loss", grid=(t // tr,), in_specs=[row, row],
        out_specs=[row, row, pl.BlockSpec((1, 1), lambda i: (0, 0))],
        out_shape=[jax.ShapeDtypeStruct((t, d), F32), jax.ShapeDtypeStruct((t, d), BF16),
                   jax.ShapeDtypeStruct((1, 1), F32)],
        compiler_params=_cparams("arbitrary"),
    )(y, target)


ATT_TQ = 2048
ATT_KB = 256
ATT_DIAG = ATT_TQ // ATT_KB


def _split2(x):
    hi = x.astype(BF16)
    return hi, (x - hi.astype(F32)).astype(BF16)


def _dot2(hi, lo, ones_bf):
    return (jnp.dot(hi, ones_bf, preferred_element_type=F32)
            + jnp.dot(lo, ones_bf, preferred_element_type=F32))


def _att_consts():
    r_ = lax.broadcasted_iota(jnp.int32, (ATT_KB, ATT_KB), 0)
    c_ = lax.broadcasted_iota(jnp.int32, (ATT_KB, ATT_KB), 1)
    upper = (r_ > c_).astype(BF16)
    lower = (r_ < c_).astype(BF16)
    return upper, lower


def _att_logits(q, k, scale, mask):
    z = lax.dot_general(q, k, (((1,), (1,)), ((), ())), preferred_element_type=F32) * scale
    lb = jnp.minimum(z, 0.0) - jnp.log(1.0 + jnp.exp(-jnp.abs(z)))
    l1 = lb - z
    if mask is not None:
        l1 = jnp.where(mask, l1, 0.0)
    return lb, l1


def _live_rows(d):
    if d is None:
        return pl.ds(0, ATT_TQ), None
    first, n = d * ATT_KB, ATT_TQ - d * ATT_KB
    row = first + lax.broadcasted_iota(jnp.int32, (n, ATT_KB), 0)
    col = first + lax.broadcasted_iota(jnp.int32, (n, ATT_KB), 1)
    return pl.ds(first, n), col < row


def _wide(c):
    return jnp.concatenate([c] * (ATT_KB // LANES), axis=1)


def attn_fwd(qk, qkv_bf, name, ex=None):
    t = qk.shape[0]
    tq = ATT_TQ
    assert t % tq == 0
    scale = HEAD_DIM ** -0.5

    def body(q_ref, k_ref, v_ref, o_ref, acc_ref, c_ref):
        qi = pl.program_id(1)
        upper, _ = _att_consts()
        acc_ref[...] = jnp.zeros_like(acc_ref)
        c_ref[...] = jnp.zeros_like(c_ref)

        def block(jb, diag):
            off = pl.multiple_of(jb * ATT_KB, ATT_KB)
            k = k_ref[pl.ds(off, ATT_KB), :]
            v = v_ref[pl.ds(off, ATT_KB), :]
            rows, mask = _live_rows(diag)
            lb, l1 = _att_logits(q_ref[rows, :], k, scale, mask)
            hi, lo = _split2(l1)
            c = c_ref[rows, :]
            a = jnp.exp(lb + _wide(c) + _dot2(hi, lo, upper))
            if mask is not None:
                a = jnp.where(mask, a, 0.0)
            acc_ref[rows, :] += jnp.dot(a.astype(BF16), v, preferred_element_type=F32)
            c_ref[rows, :] = c + jnp.sum(l1, axis=1, keepdims=True)

        for d in reversed(range(ATT_DIAG)):
            block(qi * ATT_DIAG + d, d)

        def step(s, carry):
            block(qi * ATT_DIAG - 1 - s, None)
            return carry

        lax.fori_loop(0, qi * ATT_DIAG, step, 0)
        o_ref[...] = acc_ref[...].astype(o_ref.dtype)

    (o,), bufs = comm_call(
        body, name=name, grid=(SB_HEADS, t // tq),
        in_specs=[pl.BlockSpec((tq, HEAD_DIM), lambda h, i: (i, h)),
                  pl.BlockSpec((t, HEAD_DIM), lambda h, i: (0, SB_HEADS + h)),
                  pl.BlockSpec((t, HEAD_DIM), lambda h, i: (0, 2 * SB_HEADS + h))],
        out_specs=[pl.BlockSpec((tq, HEAD_DIM), lambda h, i: (i, h))],
        out_shape=[jax.ShapeDtypeStruct((t, SB_HEADS * HEAD_DIM), BF16)],
        scratch_shapes=[pltpu.VMEM((tq, HEAD_DIM), F32), pltpu.VMEM((tq, LANES), F32)],
        inputs=[qk, qk, qkv_bf], sem=("parallel", "arbitrary"), ex=ex)
    return o, bufs


def attn_bwd(qk, qkv_bf, do, name, ex=None):
    t = qk.shape[0]
    tq = ATT_TQ
    assert t % tq == 0
    nq = t // tq
    nkb = t // ATT_KB
    scale = HEAD_DIM ** -0.5
    tn_dims = (((0,), (0,)), ((), ()))

    def body(q_ref, k_ref, v_ref, do_ref, dq_ref, dk_ref, dv_ref,
             cbuf, c_ref, ce_ref, dq_acc, dk_acc, dv_acc):
        qi = pl.program_id(1)
        upper, lower = _att_consts()

        @pl.when(qi == 0)
        def _():
            dk_acc[...] = jnp.zeros_like(dk_acc)
            dv_acc[...] = jnp.zeros_like(dv_acc)

        c_ref[...] = jnp.zeros_like(c_ref)
        ce_ref[...] = jnp.zeros_like(ce_ref)
        dq_acc[...] = jnp.zeros_like(dq_acc)

        def pass1(jb, diag):
            off = pl.multiple_of(jb * ATT_KB, ATT_KB)
            rows, mask = _live_rows(diag)
            _, l1 = _att_logits(q_ref[rows, :], k_ref[pl.ds(off, ATT_KB), :], scale, mask)
            c = c_ref[rows, :]
            cbuf[jb, rows, :] = c
            c_ref[rows, :] = c + jnp.sum(l1, axis=1, keepdims=True)

        for d in reversed(range(ATT_DIAG)):
            pass1(qi * ATT_DIAG + d, d)

        def step1(s, carry):
            pass1(qi * ATT_DIAG - 1 - s, None)
            return carry

        lax.fori_loop(0, qi * ATT_DIAG, step1, 0)

        def pass2(jb, diag):
            off = pl.multiple_of(jb * ATT_KB, ATT_KB)
            k = k_ref[pl.ds(off, ATT_KB), :]
            v = v_ref[pl.ds(off, ATT_KB), :]
            rows, mask = _live_rows(diag)
            q, do_ = q_ref[rows, :], do_ref[rows, :]
            lb, l1 = _att_logits(q, k, scale, mask)
            hi, lo = _split2(l1)
            a = jnp.exp(lb + _wide(cbuf[jb, rows, :]) + _dot2(hi, lo, upper))
            if mask is not None:
                a = jnp.where(mask, a, 0.0)
            da = lax.dot_general(do_, v, (((1,), (1,)), ((), ())), preferred_element_type=F32)
            e = da * a
            ehi, elo = _split2(e)
            ce = ce_ref[rows, :]
            big_e = _wide(ce) + _dot2(ehi, elo, lower)
            dz = (e - (e + big_e) * jnp.exp(lb)) * scale
            if mask is not None:
                dz = jnp.where(mask, dz, 0.0)
            dz = dz.astype(BF16)
            dq_acc[rows, :] += jnp.dot(dz, k, preferred_element_type=F32)
            ce_ref[rows, :] = ce + jnp.sum(e, axis=1, keepdims=True)
            dk_acc[pl.ds(off, ATT_KB), :] += lax.dot_general(dz, q, tn_dims, preferred_element_type=F32)
            dv_acc[pl.ds(off, ATT_KB), :] += lax.dot_general(a.astype(BF16), do_, tn_dims,
                                                             preferred_element_type=F32)

        def step2(jb, carry):
            pass2(jb, None)
            return carry

        lax.fori_loop(0, qi * ATT_DIAG, step2, 0)
        for d in range(ATT_DIAG):
            pass2(qi * ATT_DIAG + d, d)
        dq_ref[...] = dq_acc[...]

        @pl.when(qi == nq - 1)
        def _():
            dk_ref[...] = dk_acc[...]
            dv_ref[...] = dv_acc[...].astype(dv_ref.dtype)

    hd = SB_HEADS * HEAD_DIM
    (dq, dk, dv), bufs = comm_call(
        body, name=name, grid=(SB_HEADS, nq),
        in_specs=[pl.BlockSpec((tq, HEAD_DIM), lambda h, i: (i, h)),
                  pl.BlockSpec((t, HEAD_DIM), lambda h, i: (0, SB_HEADS + h)),
                  pl.BlockSpec((t, HEAD_DIM), lambda h, i: (0, 2 * SB_HEADS + h)),
                  pl.BlockSpec((tq, HEAD_DIM), lambda h, i: (i, h))],
        out_specs=[pl.BlockSpec((tq, HEAD_DIM), lambda h, i: (i, h)),
                   pl.BlockSpec((t, HEAD_DIM), lambda h, i: (0, h)),
                   pl.BlockSpec((t, HEAD_DIM), lambda h, i: (0, h))],
        out_shape=[jax.ShapeDtypeStruct((t, hd), F32), jax.ShapeDtypeStruct((t, hd), F32),
                   jax.ShapeDtypeStruct((t, hd), BF16)],
        scratch_shapes=[pltpu.VMEM((nkb, tq, LANES), F32), pltpu.VMEM((tq, LANES), F32),
                        pltpu.VMEM((tq, LANES), F32), pltpu.VMEM((tq, HEAD_DIM), F32),
                        pltpu.VMEM((t, HEAD_DIM), F32), pltpu.VMEM((t, HEAD_DIM), F32)],
        inputs=[qk, qk, qkv_bf, do], sem=("arbitrary", "arbitrary"), ex=ex)
    return jnp.concatenate([dq, dk], axis=1), dv, bufs


def _qknorm_specs(qkv, gqk):
    t = qkv.shape[0]
    tr = _tile(t, 1024)
    data = [Data(qkv, (tr, HEAD_DIM), lambda o, i: (i, o))]
    params = [Param(gqk, (None, 1, HEAD_DIM), lambda o: (o // SB_HEADS, 0, 0), SB_HEADS)]
    outs = [Out((t, 2 * SB_HEADS * HEAD_DIM), BF16, (tr, HEAD_DIM), lambda o, i: (i, o))]
    return (2 * SB_HEADS, t // tr), data, params, outs


CONV_COLS = 128


def _shift_down(x, s, rows):
    if s == 0:
        return x
    return jnp.where(rows < s, 0.0, pltpu.roll(x, s, 0))


def _shift_up(x, s, rows):
    if s == 0:
        return x
    n = x.shape[0]
    return jnp.where(rows >= n - s, 0.0, pltpu.roll(x, n - s, 0))


def conv_fwd(xbc, w, b, name):
    t, c = xbc.shape
    tc = _tile(c, CONV_COLS)

    def body(x_ref, w_ref, b_ref, o_ref):
        x = x_ref[...]
        rows = lax.broadcasted_iota(jnp.int32, x.shape, 0)
        y = jnp.broadcast_to(b_ref[...], x.shape)
        for kk in range(SSD_CONV):
            y = y + w_ref[kk:kk + 1, :] * _shift_down(x, SSD_CONV - 1 - kk, rows)
        o_ref[...] = _silu(y)

    col = pl.BlockSpec((t, tc), lambda j: (0, j))
    return pl.pallas_call(
        body, name=name, grid=(c // tc,),
        in_specs=[col, pl.BlockSpec((SSD_CONV, tc), lambda j: (0, j)), pl.BlockSpec((1, tc), lambda j: (0, j))],
        out_specs=col, out_shape=jax.ShapeDtypeStruct((t, c), F32),
        compiler_params=_cparams("parallel"),
    )(xbc, w, b)


def conv_bwd(xbc, w, b, dact, name):
    t, c = xbc.shape
    tc = _tile(c, CONV_COLS)

    def body(x_ref, w_ref, b_ref, g_ref, dx_ref, dw_ref, db_ref):
        x = x_ref[...]
        rows = lax.broadcasted_iota(jnp.int32, x.shape, 0)
        shifted = [_shift_down(x, SSD_CONV - 1 - kk, rows) for kk in range(SSD_CONV)]
        y = jnp.broadcast_to(b_ref[...], x.shape)
        for kk in range(SSD_CONV):
            y = y + w_ref[kk:kk + 1, :] * shifted[kk]
        sig = 1.0 / (1.0 + jnp.exp(-y))
        dy = g_ref[...] * (sig * (1.0 + y * (1.0 - sig)))
        dx = jnp.zeros_like(x)
        for kk in range(SSD_CONV):
            dx = dx + w_ref[kk:kk + 1, :] * _shift_up(dy, SSD_CONV - 1 - kk, rows)
            dw_ref[kk:kk + 1, :] = jnp.sum(dy * shifted[kk], axis=0, keepdims=True)
        dx_ref[...] = dx.astype(dx_ref.dtype)
        db_ref[...] = jnp.sum(dy, axis=0, keepdims=True)

    col = pl.BlockSpec((t, tc), lambda j: (0, j))
    wspec = pl.BlockSpec((SSD_CONV, tc), lambda j: (0, j))
    bspec = pl.BlockSpec((1, tc), lambda j: (0, j))
    return pl.pallas_call(
        body, name=name, grid=(c // tc,),
        in_specs=[col, wspec, bspec, col], out_specs=[col, wspec, bspec],
        out_shape=[jax.ShapeDtypeStruct((t, c), BF16), jax.ShapeDtypeStruct((SSD_CONV, c), F32),
                   jax.ShapeDtypeStruct((1, c), F32)],
        compiler_params=_cparams("parallel"),
    )(xbc, w, b, dact)


@functools.partial(jax.custom_vjp, nondiff_argnums=(2,))
def _bdot(a, b, dims):
    return lax.dot_general(a.astype(BF16), b.astype(BF16), (dims, ((), ())), preferred_element_type=F32)


def _bdot_fwd(a, b, dims):
    return _bdot(a, b, dims), (a, b)


def _bdot_bwd(dims, res, g):
    a, b = res
    (ca,), (cb,) = dims
    fa, fb = 1 - ca, 1 - cb
    gb, ab, bb = g.astype(BF16), a.astype(BF16), b.astype(BF16)

    def dn(u, v, cu, cv):
        return lax.dot_general(u, v, (((cu,), (cv,)), ((), ())), preferred_element_type=F32)

    da = dn(gb, bb, 1, fb) if ca == 1 else dn(bb, gb, fb, 1)
    db = dn(ab, gb, fa, 0) if cb == 0 else dn(gb, ab, 0, fa)
    return da, db


_bdot.defvjp(_bdot_fwd, _bdot_bwd)
_NN, _NT, _TN = ((1,), (0,)), ((1,), (1,)), ((0,), (0,))


def _split3_dot(x, m_bf, dims):
    out = None
    for _ in range(3):
        t = x.astype(BF16)
        part = lax.dot_general(t, m_bf, (dims, ((), ())), preferred_element_type=F32)
        out = part if out is None else out + part
        x = x - t.astype(F32)
    return out


@jax.custom_vjp
def _tri_dot(x, m_bf):
    return _split3_dot(x, m_bf, _NN)


def _tri_dot_fwd(x, m_bf):
    return _tri_dot(x, m_bf), m_bf


def _tri_dot_bwd(m_bf, g):
    return _split3_dot(g, m_bf, _NT), jnp.zeros_like(m_bf)


_tri_dot.defvjp(_tri_dot_fwd, _tri_dot_bwd)


def _ssd_chunk(x, dt, bm, cm, sprev, alog, dskip):
    n = x.shape[0]
    a = -jnp.exp(alog)
    da = dt * a
    row = lax.broadcasted_iota(jnp.int32, (n, n), 0)
    col = lax.broadcasted_iota(jnp.int32, (n, n), 1)
    causal = row >= col
    acum_s = _tri_dot(jnp.broadcast_to(da, (n, n)), (row <= col).astype(BF16))
    acum_l = acum_s.T
    decay = jnp.exp(jnp.where(causal, acum_l - acum_s, -1e30))
    acum = jnp.sum(acum_l, axis=1, keepdims=True) * (1.0 / n)
    atot = jnp.sum(da, axis=1, keepdims=True)
    xs = x * jnp.sum(jnp.where(row == col, jnp.broadcast_to(dt, (n, n)), 0.0), axis=1, keepdims=True)
    y = _bdot(_bdot(cm, bm, _NT) * decay, xs, _NN)
    y = y + jnp.exp(acum) * _bdot(cm, sprev, _NT)
    y = y + dskip * x
    snew = jnp.exp(atot) * sprev + _bdot(xs * jnp.exp(atot - acum), bm, _TN)
    return y, snew


def _ssd_specs(t, rev):
    nc = t // SSD_CHUNK
    cidx = (lambda c: nc - 1 - c) if rev else (lambda c: c)
    x_spec = pl.BlockSpec((SSD_HPG, SSD_CHUNK, SSD_HEAD_DIM), lambda g, c: (g, cidx(c), 0))
    dt_spec = pl.BlockSpec((None, SSD_HPG, SSD_CHUNK), lambda g, c: (cidx(c), g, 0))
    bc_spec = pl.BlockSpec((None, SSD_CHUNK, SSD_STATE), lambda g, c: (g, cidx(c), 0))
    sc_spec = pl.BlockSpec((SSD_HPG, 1, 1), lambda g, c: (g, 0, 0))
    st_spec = pl.BlockSpec((SSD_HPG, None, SSD_HEAD_DIM, SSD_STATE), lambda g, c: (g, cidx(c), 0, 0))
    return nc, x_spec, dt_spec, bc_spec, sc_spec, st_spec


def ssd_scan_fwd(x, dt, bm, cm, alog, dskip, name, ex=None):
    t = x.shape[1]
    nc, x_spec, dt_spec, bc_spec, sc_spec, st_spec = _ssd_specs(t, False)

    def body(x_ref, dt_ref, b_ref, c_ref, al_ref, d_ref, y_ref, sin_ref, state):
        @pl.when(pl.program_id(1) == 0)
        def _():
            state[...] = jnp.zeros_like(state)

        bm_, cm_ = b_ref[...], c_ref[...]

        for r in range(SSD_HPG):
            sprev = state[r]
            sin_ref[r] = sprev
            y, snew = _ssd_chunk(x_ref[r], dt_ref[r:r + 1, :], bm_, cm_, sprev, al_ref[r], d_ref[r])
            y_ref[r] = y
            state[r] = snew

    (y, sin), bufs = comm_call(
        body, name=name, grid=(SSD_GROUPS, nc),
        in_specs=[x_spec, dt_spec, bc_spec, bc_spec, sc_spec, sc_spec],
        out_specs=[x_spec, st_spec],
        out_shape=[jax.ShapeDtypeStruct(x.shape, F32),
                   jax.ShapeDtypeStruct((SSD_HEADS, nc, SSD_HEAD_DIM, SSD_STATE), F32)],
        scratch_shapes=[pltpu.VMEM((SSD_HPG, SSD_HEAD_DIM, SSD_STATE), F32)],
        inputs=[x, dt, bm, cm, alog, dskip], sem=("parallel", "arbitrary"), ex=ex)
    return y, sin, bufs


def ssd_scan_bwd(x, dt, bm, cm, alog, dskip, sin, dy, name, ex=None):
    t = x.shape[1]
    nc, x_spec, dt_spec, bc_spec, sc_spec, st_spec = _ssd_specs(t, True)

    def body(x_ref, dt_ref, b_ref, c_ref, al_ref, d_ref, sin_ref, dy_ref,
             dx_ref, ddt_ref, db_ref, dc_ref, dal_ref, dd_ref, dstate):
        @pl.when(pl.program_id(1) == 0)
        def _():
            dstate[...] = jnp.zeros_like(dstate)
            dal_ref[...] = jnp.zeros_like(dal_ref)
            dd_ref[...] = jnp.zeros_like(dd_ref)

        db_ref[...] = jnp.zeros_like(db_ref)
        dc_ref[...] = jnp.zeros_like(dc_ref)
        bm_, cm_ = b_ref[...], c_ref[...]

        for r in range(SSD_HPG):
            _, vjp = jax.vjp(_ssd_chunk, x_ref[r], dt_ref[r:r + 1, :], bm_, cm_, sin_ref[r], al_ref[r], d_ref[r])
            dx, ddt, dbm, dcm, dsp, dal, dd = vjp((dy_ref[r], dstate[r]))
            dx_ref[r] = dx
            ddt_ref[r:r + 1, :] = ddt
            db_ref[...] += dbm
            dc_ref[...] += dcm
            dstate[r] = dsp
            dal_ref[r] += dal
            dd_ref[r] += dd

    return comm_call(
        body, name=name, grid=(SSD_GROUPS, nc),
        in_specs=[x_spec, dt_spec, bc_spec, bc_spec, sc_spec, sc_spec, st_spec, x_spec],
        out_specs=[x_spec, dt_spec, bc_spec, bc_spec, sc_spec, sc_spec],
        out_shape=[jax.ShapeDtypeStruct(x.shape, F32), jax.ShapeDtypeStruct(dt.shape, F32),
                   jax.ShapeDtypeStruct(bm.shape, F32), jax.ShapeDtypeStruct(cm.shape, F32),
                   jax.ShapeDtypeStruct(alog.shape, F32), jax.ShapeDtypeStruct(dskip.shape, F32)],
        scratch_shapes=[pltpu.VMEM((SSD_HPG, SSD_HEAD_DIM, SSD_STATE), F32)],
        inputs=[x, dt, bm, cm, alog, dskip, sin, dy], sem=("arbitrary", "arbitrary"), ex=ex)


def _to_heads(a, n, w):
    t = a.shape[0]
    return a.reshape(t, n, w).transpose(1, 0, 2)


def _from_heads(a):
    n, t, w = a.shape
    return a.transpose(1, 0, 2).reshape(t, n * w)


def _add(acc, h):
    return (acc + h,)


ATTN_FWD_US, ATTN_BWD_US, SCAN_FWD_US, SCAN_BWD_US = 700, 1600, 560, 1280


def sb_fwd(h, hn, w, j, tag):
    gqk = w["sb_gqk"][j]
    qkv, qkv_bf = w.mm_fwd(hn, "sb_w_qkv", j, "nn", f"{tag}_qkv", out_dtypes=(F32, BF16),
                           epilogue=lambda acc: (acc, acc))
    grid, data, params, outs = _qknorm_specs(qkv, gqk)
    qk = blockwise_fwd(_rmsnorm_f, grid, data, params, outs, f"{tag}_qknorm")[0]
    o, bufs = attn_fwd(qk, qkv_bf, f"{tag}_attn", w.host_fwd(ATTN_FWD_US))
    w.done_fwd(bufs)
    h1 = w.mm_fwd(o, "sb_w_o", j, "nn", f"{tag}_out", epilogue=_add, extras=(h,))
    return h1, (qkv, qkv_bf, qk, o)


def sb_bwd(g_bf, hn, saved, w, j, tag):
    qkv, qkv_bf, qk, o = saved
    gqk = w["sb_gqk"][j]
    w.add_grad("sb_w_o", j, w.mm_bwd(o, g_bf, "tn", f"{tag}_dwo", out_dtypes=(BF16,)))
    do = w.mm_bwd(g_bf, w["sb_w_o"], "nt", f"{tag}_do", b_layer=j, out_dtypes=(BF16,))
    dqk, dv, bufs = attn_bwd(qk, qkv_bf, do, f"{tag}_attn_bwd", w.host_bwd(ATTN_BWD_US))
    w.recv.update(bufs)
    grid, data, params, outs = _qknorm_specs(qkv, gqk)
    (draw,), (dgqk,) = blockwise_bwd(_rmsnorm_f, grid, data, params, outs, [dqk], f"{tag}_qknorm_bwd",
                                     grad_dtypes=[BF16])
    dqkv = jnp.concatenate([draw[:, :2 * SB_HEADS * HEAD_DIM], dv], axis=1)
    w.add_grad("sb_w_qkv", j, w.mm_bwd(hn, dqkv, "tn", f"{tag}_dwqkv", out_dtypes=(BF16,)))
    dhn = w.mm_bwd(dqkv, w["sb_w_qkv"], "nt", f"{tag}_dhn", b_layer=j)
    return dhn, {"sb_gqk": dgqk}


def _gm_act_specs(p, gv):
    t = p.shape[0]
    tr = _row_tile(t)
    d = D_MODEL
    data = [Data(p, (tr, d), lambda o, i: (i, 0)), Data(p, (tr, d), lambda o, i: (i, 1))]
    params = [Param(gv, (1, d), lambda o: (0, 0), 1)]
    outs = [Out((t, d), F32, (tr, d), lambda o, i: (i, 0)), Out((t, d), BF16, (tr, d), lambda o, i: (i, 0))]
    return (1, t // tr), data, params, outs


def _gm_gate_specs(u, vn, ws, bs):
    t = u.shape[0]
    rows = _tile(t, GM_STEP_CHUNKS * GM_CHUNK)
    blk = (rows, LANES)
    data = [Data(u, blk, lambda o, i: (i, o)), Data(vn, blk, lambda o, i: (i, o))]
    params = [Param(ws, (None, GM_CHUNK, GM_CHUNK), lambda o: (o, 0, 0), 1),
              Param(bs, (None, GM_CHUNK, 1), lambda o: (o, 0, 0), 1)]
    outs = [Out((t, D_MODEL), BF16, blk, lambda o, i: (i, o))]
    return (GM_GROUPS, t // rows), data, params, outs


def gm_fwd(h, hn, w, j, tag):
    p = w.mm_fwd(hn, "gm_w_in", j, "nn", f"{tag}_in")
    grid, data, params, outs = _gm_act_specs(p, w["gm_v_norm_g"])
    u, vn = blockwise_fwd(_gm_act_f, grid, data, params, outs, f"{tag}_act")
    grid, data, params, outs = _gm_gate_specs(u, vn, w["gm_w_s"], w["gm_b_s"])
    y = blockwise_fwd(_gm_gate_f, grid, data, params, outs, f"{tag}_gate")[0]
    h1 = w.mm_fwd(y, "gm_w_o", j, "nn", f"{tag}_out", epilogue=_add, extras=(h,))
    return h1, (p, u, vn, y)


def gm_bwd(g_bf, hn, saved, w, j, tag):
    p, u, vn, y = saved
    w.add_grad("gm_w_o", j, w.mm_bwd(y, g_bf, "tn", f"{tag}_dwo", out_dtypes=(BF16,)))
    dy = w.mm_bwd(g_bf, w["gm_w_o"], "nt", f"{tag}_dy", b_layer=j)
    grid, data, params, outs = _gm_gate_specs(u, vn, w["gm_w_s"], w["gm_b_s"])
    (du, dvn), (dws, dbs) = blockwise_bwd(_gm_gate_f, grid, data, params, outs, [dy], f"{tag}_gate_bwd",
                                          grad_dtypes=[F32, F32])
    grid, data, params, outs = _gm_act_specs(p, w["gm_v_norm_g"])
    (dpu, dpv), (dgv,) = blockwise_bwd(_gm_act_f, grid, data, params, outs, [du, dvn], f"{tag}_act_bwd",
                                       grad_dtypes=[BF16, BF16])
    dp = jnp.concatenate([dpu[:, :D_MODEL], dpv[:, D_MODEL:]], axis=1)
    w.add_grad("gm_w_in", j, w.mm_bwd(hn, dp, "tn", f"{tag}_dwin", out_dtypes=(BF16,)))
    dhn = w.mm_bwd(dp, w["gm_w_in"], "nt", f"{tag}_dhn", b_layer=j)
    return dhn, {"gm_v_norm_g": dgv, "gm_w_s": dws, "gm_b_s": dbs}


def _ssd_gate_specs(y, z, g):
    t = y.shape[0]
    tr = _tile(t, 1024)
    gw = SSD_INNER // SSD_GROUPS
    blk = (tr, gw)
    data = [Data(y, blk, lambda o, i: (i, o)), Data(z, blk, lambda o, i: (i, o))]
    params = [Param(g, (None, 1, gw), lambda o: (o, 0, 0), 1)]
    outs = [Out((t, SSD_INNER), BF16, blk, lambda o, i: (i, o))]
    return (SSD_GROUPS, t // tr), data, params, outs


def _dt_specs(raw, bias):
    t = raw.shape[0]
    tr = _row_tile(t)
    data = [Data(raw, (tr, DT_PAD), lambda o, i: (i, 0))]
    params = [Param(bias, (1, DT_PAD), lambda o: (0, 0), 1)]
    outs = [Out((t, DT_PAD), F32, (tr, DT_PAD), lambda o, i: (i, 0))]
    return (1, t // tr), data, params, outs


def ssd_fwd(h, hn, w, tag):
    w.need("ssd_w_in", None)
    z = w.mm_fwd(hn, "ssd_wz", None, "nn", f"{tag}_inz")
    xbc = w.mm_fwd(hn, "ssd_wxbc", None, "nn", f"{tag}_inx")
    raw = w.mm_fwd(hn, "ssd_wdt", None, "nn", f"{tag}_indt")
    act = conv_fwd(xbc, w["ssd_conv_w"], w["ssd_conv_b"], f"{tag}_conv")
    grid, data, params, outs = _dt_specs(raw, w["ssd_dt_bias"])
    dt = blockwise_fwd(_dt_f, grid, data, params, outs, f"{tag}_dt")[0]
    xh = _to_heads(act[:, :SSD_INNER], SSD_HEADS, SSD_HEAD_DIM)
    bh = _to_heads(act[:, SSD_INNER:SSD_INNER + SSD_BC], SSD_GROUPS, SSD_STATE)
    ch = _to_heads(act[:, SSD_INNER + SSD_BC:], SSD_GROUPS, SSD_STATE)
    dth = dt[:, :SSD_HEADS].reshape(-1, SSD_CHUNK, SSD_HEADS).transpose(0, 2, 1)
    yh, sin, bufs = ssd_scan_fwd(xh, dth, bh, ch, w["ssd_a_log"], w["ssd_d"], f"{tag}_scan",
                                 w.host_fwd(SCAN_FWD_US))
    w.done_fwd(bufs)
    y = _from_heads(yh)
    grid, data, params, outs = _ssd_gate_specs(y, z, w["ssd_norm_g"])
    yn = blockwise_fwd(_ssd_gate_f, grid, data, params, outs, f"{tag}_gate")[0]
    h1 = w.mm_fwd(yn, "ssd_w_o", 0, "nn", f"{tag}_out", epilogue=_add, extras=(h,))
    return h1, (z, xbc, raw, xh, bh, ch, dth, sin, y, yn)


def ssd_bwd(g_bf, hn, saved, w, tag):
    z, xbc, raw, xh, bh, ch, dth, sin, y, yn = saved
    w.add_grad("ssd_w_o", 0, w.mm_bwd(yn, g_bf, "tn", f"{tag}_dwo", out_dtypes=(BF16,)))
    dyn = w.mm_bwd(g_bf, w["ssd_w_o"], "nt", f"{tag}_dyn", b_layer=0)
    grid, data, params, outs = _ssd_gate_specs(y, z, w["ssd_norm_g"])
    (dy, dz), (dng,) = blockwise_bwd(_ssd_gate_f, grid, data, params, outs, [dyn], f"{tag}_gate_bwd",
                                     grad_dtypes=[F32, BF16])
    (dxh, ddth, dbh, dch, dal, ddk), bufs = ssd_scan_bwd(
        xh, dth, bh, ch, w["ssd_a_log"], w["ssd_d"], sin, _to_heads(dy, SSD_HEADS, SSD_HEAD_DIM),
        f"{tag}_scan_bwd", w.host_bwd(SCAN_BWD_US))
    w.recv.update(bufs)
    dact = jnp.concatenate([_from_heads(dxh), _from_heads(dbh), _from_heads(dch)], axis=1)
    dxbc, dcw, dcb = conv_bwd(xbc, w["ssd_conv_w"], w["ssd_conv_b"], dact, f"{tag}_conv_bwd")
    ddt = jnp.pad(ddth.transpose(0, 2, 1).reshape(-1, SSD_HEADS), ((0, 0), (0, DT_PAD - SSD_HEADS)))
    grid, data, params, outs = _dt_specs(raw, w["ssd_dt_bias"])
    (draw,), (dbias,) = blockwise_bwd(_dt_f, grid, data, params, outs, [ddt], f"{tag}_dt_bwd",
                                      grad_dtypes=[BF16])
    d_wz = w.mm_bwd(hn, dz, "tn", f"{tag}_dwz", out_dtypes=(BF16,))
    d_wx = w.mm_bwd(hn, dxbc, "tn", f"{tag}_dwx", out_dtypes=(BF16,))
    d_wdt = w.mm_bwd(hn, draw, "tn", f"{tag}_dwdt", out_dtypes=(BF16,))
    d_win = jnp.concatenate([d_wz, d_wx, d_wdt[:, :SSD_HEADS]], axis=1)
    d_win = d_win.reshape(D_MODEL, N_DEV, SSD_PROJ // N_DEV).transpose(1, 0, 2)
    w.add_grad("ssd_w_in", 0, d_win)
    dhn = w.mm_bwd(dz, w["ssd_wz"], "nt", f"{tag}_dhn_z")
    dhn = w.mm_bwd(dxbc, w["ssd_wxbc"], "nt", f"{tag}_dhn_x", epilogue=_add, extras=(dhn,))
    dhn = w.mm_bwd(draw, w["ssd_wdt"], "nt", f"{tag}_dhn_dt", epilogue=_add, extras=(dhn,))
    return dhn, {"ssd_conv_w": dcw, "ssd_conv_b": dcb, "ssd_dt_bias": dbias[:, :SSD_HEADS], "ssd_a_log": dal,
                 "ssd_d": ddk, "ssd_norm_g": dng}


def _relu2(acc):
    return (acc, jnp.square(jnp.maximum(acc, 0.0)))


def _relu2_bwd(acc, a):
    return (acc * (2.0 * jnp.maximum(a, 0.0)),)


def device_step(x, target, w):
    h = x
    saved = []
    for i in range(DEPTH):
        kind, j = i % N_MIXERS, i // N_MIXERS
        tag = f"l{i}"
        hn = rmsnorm_fwd(h, w["norm_mix_g"][i], f"{tag}_norm_mix")
        if kind == 0:
            h1, ms = sb_fwd(h, hn, w, j, f"{tag}_sb")
        elif kind == 1:
            h1, ms = gm_fwd(h, hn, w, j, f"{tag}_gm")
        else:
            h1, ms = ssd_fwd(h, hn, w, f"{tag}_ssd")
        hm = rmsnorm_fwd(h1, w["norm_mlp_g"][i], f"{tag}_norm_mlp")
        a, r = w.mm_fwd(hm, "mlp_w_in", i, "nn", f"{tag}_mlp_in", out_dtypes=(F32, BF16), epilogue=_relu2)
        h2 = w.mm_fwd(r, "mlp_w_out", i, "nn", f"{tag}_mlp_out", epilogue=_add, extras=(h1,))
        saved.append((h, hn, ms, h1, hm, a, r))
        h = h2
    dh, g_bf, loss = loss_call(h, target)

    small = {"norm_mix_g": [None] * DEPTH, "norm_mlp_g": [None] * DEPTH, "sb_gqk": [None] * 2}
    for i in reversed(range(DEPTH)):
        kind, j = i % N_MIXERS, i // N_MIXERS
        tag = f"l{i}"
        h0, hn, ms, h1, hm, a, r = saved[i]
        w.add_grad("mlp_w_out", i, w.mm_bwd(r, g_bf, "tn", f"{tag}_mlp_dwout", out_dtypes=(BF16,)))
        da = w.mm_bwd(g_bf, w["mlp_w_out"], "nt", f"{tag}_mlp_da", b_layer=i, out_dtypes=(BF16,),
                      epilogue=_relu2_bwd, extras=(a,))
        w.add_grad("mlp_w_in", i, w.mm_bwd(hm, da, "tn", f"{tag}_mlp_dwin", out_dtypes=(BF16,)))
        dhm = w.mm_bwd(da, w["mlp_w_in"], "nt", f"{tag}_mlp_dhm", b_layer=i)
        dh1, g_bf, small["norm_mlp_g"][i] = rmsnorm_bwd(h1, w["norm_mlp_g"][i], dhm, dh, f"{tag}_norm_mlp_bwd")
        if kind == 0:
            dhn, mg = sb_bwd(g_bf, hn, ms, w, j, f"{tag}_sb")
            small["sb_gqk"][j] = mg.pop("sb_gqk")
        elif kind == 1:
            dhn, mg = gm_bwd(g_bf, hn, ms, w, j, f"{tag}_gm")
        else:
            dhn, mg = ssd_bwd(g_bf, hn, ms, w, f"{tag}_ssd")
        small.update(mg)
        dh, g_bf, small["norm_mix_g"][i] = rmsnorm_bwd(h0, w["norm_mix_g"][i], dhn, dh1, f"{tag}_norm_mix_bwd",
                                                       with_bf16=i > 0)
    dcw = small["ssd_conv_w"].reshape(SSD_CONV, N_DEV, SMALL_W).transpose(1, 0, 2)
    dcb = small["ssd_conv_b"].reshape(N_DEV, 1, SMALL_W)
    dng = jnp.pad(small["ssd_norm_g"], ((0, 0), (0, 0), (0, SMALL_W - SSD_INNER // N_DEV)))
    w.add_grad("small", None, jnp.concatenate([dcw, dcb, dng, jnp.zeros((N_DEV, 2, SMALL_W), F32)], axis=1))
    w.add_grad("repl", None, _pack(_repl_grads(small), w.shapes["repl"][0]))
    return loss, dh


def _my_pos():
    return lax.axis_index("x"), lax.axis_index("y"), lax.axis_index("c")


def _lin(p):
    return 4 * p[0] + 2 * p[1] + p[2]


_FLIPS = [(0, 0, 1), (0, 1, 0), (0, 1, 1), (1, 0, 0), (1, 0, 1), (1, 1, 0), (1, 1, 1)]
_SIBLING = (0, 0, 1)
_AG_FLIPS = [_SIBLING, (1, 0, 0), (0, 1, 0), (1, 1, 0)]


def _flip(pos, f):
    return tuple((1 - p) if b else p for p, b in zip(pos, f))


def _other_chip(pos, j):
    x, y = pos[0], pos[1]
    return [(1 - x, y), (x, 1 - y), (1 - x, 1 - y)][j]


class Exchange:
    def __init__(self):
        self.sources, self.buffers, self.items, self.locals = [], [], [], []

    def source(self, array):
        self.sources.append(array)
        return len(self.sources) - 1

    def buffer(self, name, bufs, shape, dtype):
        if name not in [b[0] for b in self.buffers]:
            self.buffers.append((name, bufs.get(name), jax.ShapeDtypeStruct(shape, dtype)))

    def n_copies(self):
        return sum(len(flips) for _, _, flips in self.items)


def _exchange_fns(ex, srcs, bufs, send_sems, recv_sems, local_sems):
    def local(k):
        sf, df = ex.locals[k]
        me = _my_pos()
        return pltpu.make_async_copy(sf(srcs, bufs, me, me), df(bufs, me), local_sems.at[k])

    def remote(sending):
        me = _my_pos()
        k = 0
        for sf, df, flips in ex.items:
            for f in flips:
                peer = _flip(me, f)
                yield pltpu.make_async_remote_copy(
                    src_ref=sf(srcs, bufs, me, peer), dst_ref=df(bufs, me if sending else peer),
                    send_sem=send_sems.at[k], recv_sem=recv_sems.at[k], device_id=peer, device_id_type=MESH)
                k += 1

    def start():
        for k in range(len(ex.locals)):
            local(k).start()
        for cp in remote(True):
            cp.start()

    def wait():
        arrivals = list(remote(False))
        for cp in arrivals:
            cp.wait_recv()
        for cp in arrivals:
            cp.wait_send()
        for k in range(len(ex.locals)):
            local(k).wait()

    return start, wait


def comm_call(core, *, name, grid, in_specs, out_specs, out_shape, scratch_shapes, inputs, sem, ex=None):
    n_in, n_out, n_scr = len(in_specs), len(out_specs), len(scratch_shapes)
    if ex is None:
        res = pl.pallas_call(core, name=name, grid=grid, in_specs=in_specs, out_specs=out_specs,
                             out_shape=out_shape, scratch_shapes=scratch_shapes,
                             compiler_params=_cparams(*sem))(*inputs)
        return list(res), {}
    existing = [(k, b[1]) for k, b in enumerate(ex.buffers) if b[1] is not None]
    names = [b[0] for b in ex.buffers]
    n_src, n_old, n_buf = len(ex.sources), len(existing), len(names)

    def body(*refs):
        cuts = [n_in, n_src, n_old, n_out, n_buf, n_scr, 3]
        parts, pos = [], 0
        for c in cuts:
            parts.append(refs[pos:pos + c])
            pos += c
        core_in, srcs, _, core_out, buf_refs, core_scr, sems = parts
        start, wait = _exchange_fns(ex, srcs, dict(zip(names, buf_refs)), *sems)
        if not grid:
            start()
            wait()
            return
        ids = [pl.program_id(d) for d in range(len(grid))]
        pl.when(functools.reduce(jnp.logical_and, [i == 0 for i in ids]))(start)
        core(*core_in, *core_out, *core_scr)
        pl.when(functools.reduce(jnp.logical_and, [i == g - 1 for i, g in zip(ids, grid)]))(wait)

    dma = pltpu.SemaphoreType.DMA
    res = pl.pallas_call(
        body, name=name, **({"grid": grid} if grid else {}),
        in_specs=list(in_specs) + [HBM_SPEC] * (n_src + n_old),
        out_specs=list(out_specs) + [HBM_SPEC] * n_buf,
        out_shape=list(out_shape) + [b[2] for b in ex.buffers],
        scratch_shapes=list(scratch_shapes) + [dma((max(ex.n_copies(), 1),)), dma((max(ex.n_copies(), 1),)),
                                               dma((max(len(ex.locals), 1),))],
        input_output_aliases={n_in + n_src + e: n_out + k for e, (k, _) in enumerate(existing)},
        compiler_params=_cparams(*(["arbitrary"] * len(grid))),
    )(*inputs, *ex.sources, *[b for _, b in existing])
    return list(res[:n_out]), dict(zip(names, res[n_out:]))


def run_exchange(ex, name):
    return comm_call(None, name=name, grid=(), in_specs=[], out_specs=[], out_shape=[], scratch_shapes=[],
                     inputs=[], sem=(), ex=ex)[1]


def _gathered_shape(shard, kind):
    s = shard.shape
    if kind == "col":
        return s[:2] + (s[2] * N_DEV,)
    if kind == "row":
        return (s[0], s[1] * N_DEV, s[2])
    return (N_DEV,) + s


def _n_pieces(nbytes, rows, target):
    n = 1
    while nbytes > n * target and rows % (32 * n) == 0 and n < 8:
        n *= 2
    return n


def _ag_part(buf, tn, layer, r0, nr, who, shards):
    kind, shard, idx = SHARD_KIND[tn], shards[tn], _lin(who)
    if kind == "col":
        return buf.at[layer, pl.ds(r0, nr), pl.ds(pl.multiple_of(idx * shard.shape[2], LANES), shard.shape[2])]
    if kind == "row":
        return buf.at[layer, pl.ds(pl.multiple_of(idx * shard.shape[1] + r0, 16), nr), :]
    return buf.at[idx] if tn == "small" else buf.at[idx, 0, pl.ds(r0, nr), :]


def _ag_items(ex, piece, phase, shards, bufs):
    tn, layer, r0, nr, _ = piece
    shard = shards[tn]
    ex.buffer(tn, bufs, _gathered_shape(shard, SHARD_KIND[tn]), shard.dtype)

    def part(b, who):
        return _ag_part(b[tn], tn, layer, r0, nr, who, shards)

    if phase == "A":
        def src(s, b, me, peer, si=ex.source(shard)):
            return s[si] if tn == "small" else s[si].at[0 if layer is None else layer, pl.ds(r0, nr), :]

        ex.items.append((src, part, _AG_FLIPS))
        ex.locals.append((src, part))
        return
    for j in range(3):
        def held(who, j=j):
            return (*_other_chip(who, j), who[2])

        ex.items.append((lambda s, b, me, peer, held=held: part(b, held(me)),
                         lambda b, sender, held=held: part(b, held(sender)), [_SIBLING]))


def _rs_items(ex, piece, bufs, shapes):
    tn, layer, r0, nr, _, g = piece
    kind, shape = SHARD_KIND[tn], shapes[tn]
    ex.buffer(tn, bufs, (N_DEV,) + tuple(shape), g.dtype)

    def src(s, b, me, peer, si=ex.source(g)):
        idx = _lin(peer)
        if kind == "all":
            return s[si]
        if tn == "small":
            return s[si].at[idx]
        if kind == "slot":
            return s[si].at[idx, pl.ds(r0, nr), :]
        if kind == "col":
            return s[si].at[pl.ds(r0, nr), pl.ds(pl.multiple_of(idx * shape[2], LANES), shape[2])]
        return s[si].at[pl.ds(pl.multiple_of(idx * shape[1] + r0, 16), nr), :]

    def dst(b, sender):
        idx = _lin(sender)
        return b[tn].at[idx] if layer is None else b[tn].at[idx, layer, pl.ds(r0, nr), :]

    ex.items.append((src, dst, _FLIPS))
    ex.locals.append((src, dst))


AG_BYTES_PER_US = 22e3
RS_BYTES_PER_US = 75e3
AG_PIECE_BYTES = 2.3e6
RS_PIECE_BYTES = 4.5e6
MXU_FLOPS_PER_US = 8.6e8
MATMUL_HOST_MIN_US = {"fwd": 0, "bwd": 110}
MATMUL_HOST_SHARE = 1.0


class Traffic(dict):
    def __init__(self, small_weights, shards, shapes):
        super().__init__(small_weights)
        self.shards, self.shapes, self.recv = shards, shapes, {}
        self.wait_a, self.wait_b, self.riding, self.unfinished = [], [], ([], []), {}
        for group in AG_GROUPS:
            for tn, layer in group:
                shard = shards[tn]
                rows = shard.shape[-2]
                nbytes = rows * shard.shape[-1] * shard.dtype.itemsize
                n = 1 if tn == "small" else _n_pieces(nbytes, rows, AG_PIECE_BYTES)
                self.unfinished[(tn, layer)] = n
                self.wait_a += [(tn, layer, k * (rows // n), rows // n, nbytes / n) for k in range(n)]
        self.grads, self.grads_left, self.n_calls = [], {}, 0

    def host_fwd(self, us, exclude=(), only=None):
        ex, budget = Exchange(), us * AG_BYTES_PER_US
        ok = (lambda p: p[:2] == only) if only else (lambda p: p[0] not in exclude)
        take_b = [p for p in self.wait_b if ok(p)]
        take_a = []
        for p in [p for p in self.wait_a if ok(p)]:
            if p[4] > budget and not only:
                break
            take_a.append(p)
            budget -= p[4]
        for p in take_b:
            self.wait_b.remove(p)
            _ag_items(ex, p, "B", self.shards, self)
        for p in take_a:
            self.wait_a.remove(p)
            _ag_items(ex, p, "A", self.shards, self)
        self.riding = (take_a, take_b)
        return ex if ex.items else None

    def done_fwd(self, bufs):
        self.update(bufs)
        take_a, take_b = self.riding
        self.wait_b += take_a
        for p in take_b:
            self.unfinished[p[:2]] -= 1
        self.riding = ([], [])

    def need(self, tn, layer):
        while self.unfinished.get((tn, layer), 0) > 0:
            self.n_calls += 1
            self.done_fwd(run_exchange(self.host_fwd(0, only=(tn, layer)), f"gather_now{self.n_calls}"))
        if tn == "ssd_w_in" and "ssd_wz" not in self:
            self.need("small", None)
            _ssd_weights(self)

    def mm_fwd(self, a, tn, layer, form, name, **kw):
        self.need(tn, layer)
        ex = self.host_fwd(_host_us(a, self[tn], form, "fwd"), exclude=(tn,))
        if ex is None:
            return matmul(a, self[tn], form, name, b_layer=layer, **kw)
        out, bufs = matmul(a, self[tn], form, name, b_layer=layer, ex=ex, **kw)
        self.done_fwd(bufs)
        return out

    def add_grad(self, tn, layer, g):
        rows = self.shapes[tn][-2]
        nbytes = g.size * g.dtype.itemsize
        n = 1 if layer is None else _n_pieces(nbytes, rows, RS_PIECE_BYTES)
        self.grads += [(tn, layer, k * (rows // n), rows // n, nbytes / n, g) for k in range(n)]
        self.grads_left[tn] = self.grads_left.get(tn, 0) + n

    def host_bwd(self, us, exclude=(), only=None):
        ex, budget = Exchange(), us * RS_BYTES_PER_US
        for p in [p for p in self.grads if (p[0] == only if only else p[0] not in exclude)]:
            if p[4] > budget and not only:
                continue
            self.grads.remove(p)
            self.grads_left[p[0]] -= 1
            _rs_items(ex, p, self.recv, self.shapes)
            budget -= p[4]
        return ex if ex.items else None

    def need_grad(self, tn):
        if self.grads_left.get(tn, 0) > 0:
            self.n_calls += 1
            self.recv.update(run_exchange(self.host_bwd(0, only=tn), f"scatter_now{self.n_calls}"))

    def mm_bwd(self, a, b, form, name, **kw):
        ex = self.host_bwd(_host_us(a, b, form, "bwd"))
        if ex is None:
            return matmul(a, b, form, name, **kw)
        out, bufs = matmul(a, b, form, name, ex=ex, **kw)
        self.recv.update(bufs)
        return out


def _host_us(a, b, form, direction):
    k = a.shape[0] if form == "tn" else a.shape[1]
    us = 2.0 * a.size // k * b.shape[-2] * b.shape[-1] / MXU_FLOPS_PER_US
    return MATMUL_HOST_SHARE * us if us >= MATMUL_HOST_MIN_US[direction] else 0.0


_C1 = 1.0 - ADAM_B1 ** ADAM_STEP
_C2 = 1.0 - ADAM_B2 ** ADAM_STEP
ADAMW_BYTES_PER_ELEMENT = 7 * 4 + N_DEV * 2
HBM_BYTES_PER_US = 2.9e6


def adamw(wt, m, v, slots, name, ex=None):
    nl, r, c = wt.shape
    tr = _tile(r, 128)

    def body(w_ref, m_ref, v_ref, s_ref, g_ref, d_ref, nm_ref, nv_ref):
        g = s_ref[0].astype(F32)
        for k in range(1, N_DEV):
            g = g + s_ref[k].astype(F32)
        m_new = ADAM_B1 * m_ref[...] + (1.0 - ADAM_B1) * g
        v_new = ADAM_B2 * v_ref[...] + (1.0 - ADAM_B2) * jnp.square(g)
        m_hat = m_new / _C1
        v_hat = v_new / _C2
        g_ref[...] = g
        d_ref[...] = -ADAM_LR * (m_hat / (jnp.sqrt(v_hat) + ADAM_EPS) + ADAM_WD * w_ref[...])
        nm_ref[...] = m_new
        nv_ref[...] = v_new

    blk = pl.BlockSpec((None, tr, c), lambda l, i: (l, i, 0))
    sblk = pl.BlockSpec((N_DEV, None, tr, c), lambda l, i: (0, l, i, 0))
    return comm_call(
        body, name=name, grid=(nl, r // tr), in_specs=[blk, blk, blk, sblk], out_specs=[blk] * 4,
        out_shape=[jax.ShapeDtypeStruct(wt.shape, F32)] * 4, scratch_shapes=[],
        inputs=[wt, m, v, slots], sem=("parallel", "parallel"), ex=ex)


def _pack(arrays, rows):
    flat = jnp.concatenate([a.reshape(-1).astype(F32) for a in arrays])
    return jnp.pad(flat, (0, rows * LANES - flat.shape[0])).reshape(rows, LANES)


def _unpack(packed, shapes):
    flat = packed.reshape(-1)
    out, off = [], 0
    for s in shapes:
        sz = math.prod(s)
        out.append(flat[off:off + sz].reshape(s))
        off += sz
    return out


SMALL_W = 768


def _pack_small_shard(conv_w, conv_b, norm_g):
    ng = jnp.pad(norm_g, ((0, 0), (0, SMALL_W - norm_g.shape[1])))
    return jnp.concatenate([conv_w[0], conv_b, ng, jnp.zeros((2, SMALL_W), F32)], axis=0)


def _unpack_small_shard(p):
    return p[0:4][None], p[4:5], p[5:6, :SSD_INNER // N_DEV]


REPL = ["norm_mix_g", "norm_mlp_g", "sb_q_norm_g", "sb_k_norm_g", "gm_v_norm_g", "gm_w_s", "gm_b_s",
        "ssd_dt_bias", "ssd_a_log", "ssd_d"]
BIG = ["sb_w_qkv", "sb_w_o", "gm_w_in", "gm_w_o", "ssd_w_in", "ssd_w_o", "mlp_w_in", "mlp_w_out"]
BIG_KIND = {"sb_w_qkv": "col", "sb_w_o": "row", "gm_w_in": "col", "gm_w_o": "row", "ssd_w_in": "slot",
            "ssd_w_o": "row", "mlp_w_in": "col", "mlp_w_out": "row"}
SMALL_SHARDED = ["ssd_conv_w", "ssd_conv_b", "ssd_norm_g"]
WEIGHTS = ["norm_mix_g", "norm_mlp_g", "sb_w_qkv", "sb_q_norm_g", "sb_k_norm_g", "sb_w_o", "gm_w_in",
           "gm_v_norm_g", "gm_w_s", "gm_b_s", "gm_w_o", "ssd_w_in", "ssd_conv_w", "ssd_conv_b", "ssd_dt_bias",
           "ssd_a_log", "ssd_d", "ssd_norm_g", "ssd_w_o", "mlp_w_in", "mlp_w_out"]


SHARD_KIND = dict(BIG_KIND, small="slot", repl="all")
AG_GROUPS = [
    [("sb_w_qkv", 0), ("sb_w_o", 0), ("mlp_w_in", 0), ("mlp_w_out", 0)],
    [("gm_w_in", 0), ("gm_w_o", 0), ("mlp_w_in", 1), ("mlp_w_out", 1), ("ssd_w_in", None), ("small", None),
     ("ssd_w_o", 0), ("mlp_w_in", 2), ("mlp_w_out", 2)],
    [("sb_w_qkv", 1), ("sb_w_o", 1), ("mlp_w_in", 3), ("mlp_w_out", 3)],
]


def _ssd_weights(w):
    win = jnp.moveaxis(w["ssd_w_in"][:, 0], 0, 1).reshape(D_MODEL, SSD_PROJ)
    small = w["small"]
    w["ssd_wz"] = win[:, :SSD_INNER]
    w["ssd_wxbc"] = win[:, SSD_INNER:SSD_INNER + SSD_CONV_DIM]
    w["ssd_wdt"] = jnp.pad(win[:, SSD_INNER + SSD_CONV_DIM:], ((0, 0), (0, DT_PAD - SSD_HEADS)))
    w["ssd_conv_w"] = small[:, 0:4].transpose(1, 0, 2).reshape(SSD_CONV, SSD_CONV_DIM)
    w["ssd_conv_b"] = small[:, 4].reshape(1, SSD_CONV_DIM)
    w["ssd_norm_g"] = small[:, 5:6, :SSD_INNER // N_DEV]


def _small_weights(p):
    w = {}
    w["norm_mix_g"] = [p["norm_mix_g"][i:i + 1] for i in range(DEPTH)]
    w["norm_mlp_g"] = [p["norm_mlp_g"][i:i + 1] for i in range(DEPTH)]
    w["sb_gqk"] = [jnp.stack([p["sb_q_norm_g"][j:j + 1], p["sb_k_norm_g"][j:j + 1]]) for j in range(2)]
    w["gm_v_norm_g"] = p["gm_v_norm_g"]
    w["gm_w_s"] = p["gm_w_s"][0]
    w["gm_b_s"] = p["gm_b_s"][0][:, :, None]
    w["ssd_dt_bias"] = jnp.pad(p["ssd_dt_bias"], ((0, 0), (0, DT_PAD - SSD_HEADS)))
    w["ssd_a_log"] = p["ssd_a_log"].reshape(SSD_HEADS, 1, 1)
    w["ssd_d"] = p["ssd_d"].reshape(SSD_HEADS, 1, 1)
    return w


def _repl_grads(g):
    return [jnp.concatenate(g["norm_mix_g"], axis=0), jnp.concatenate(g["norm_mlp_g"], axis=0),
            jnp.concatenate([g["sb_gqk"][0][0], g["sb_gqk"][1][0]], axis=0),
            jnp.concatenate([g["sb_gqk"][0][1], g["sb_gqk"][1][1]], axis=0),
            g["gm_v_norm_g"], g["gm_w_s"][None], g["gm_b_s"][None, :, :, 0],
            g["ssd_dt_bias"], g["ssd_a_log"].reshape(1, SSD_HEADS), g["ssd_d"].reshape(1, SSD_HEADS)]


def kernel(x, norm_mix_g, norm_mlp_g, sb_w_qkv, sb_q_norm_g, sb_k_norm_g, sb_w_o, gm_w_in, gm_v_norm_g, gm_w_s, gm_b_s, gm_w_o, ssd_w_in, ssd_conv_w, ssd_conv_b, ssd_dt_bias, ssd_a_log, ssd_d, ssd_norm_g, ssd_w_o, mlp_w_in, mlp_w_out, loss_target, m_norm_mix_g, m_norm_mlp_g, m_sb_w_qkv, m_sb_q_norm_g, m_sb_k_norm_g, m_sb_w_o, m_gm_w_in, m_gm_v_norm_g, m_gm_w_s, m_gm_b_s, m_gm_w_o, m_ssd_w_in, m_ssd_conv_w, m_ssd_conv_b, m_ssd_dt_bias, m_ssd_a_log, m_ssd_d, m_ssd_norm_g, m_ssd_w_o, m_mlp_w_in, m_mlp_w_out, v_norm_mix_g, v_norm_mlp_g, v_sb_w_qkv, v_sb_q_norm_g, v_sb_k_norm_g, v_sb_w_o, v_gm_w_in, v_gm_v_norm_g, v_gm_w_s, v_gm_b_s, v_gm_w_o, v_ssd_w_in, v_ssd_conv_w, v_ssd_conv_b, v_ssd_dt_bias, v_ssd_a_log, v_ssd_d, v_ssd_norm_g, v_ssd_w_o, v_mlp_w_in, v_mlp_w_out):
    args = dict(locals())
    p = {k: args[k] for k in WEIGHTS}
    pm = {k: args["m_" + k] for k in WEIGHTS}
    pv = {k: args["v_" + k] for k in WEIGHTS}

    shards = {k: p[k].astype(BF16) for k in BIG}
    shards["small"] = _pack_small_shard(*[p[k] for k in SMALL_SHARDED])
    repl_shapes = [p[k].shape for k in REPL]
    n_repl = sum(math.prod(s) for s in repl_shapes)
    repl_rows = -(-n_repl // (LANES * LANES)) * LANES
    shard_shapes = {k: p[k].shape for k in BIG}
    shard_shapes.update(small=(8, SMALL_W), repl=(repl_rows, LANES))

    w = Traffic(_small_weights(p), shards, shard_shapes)
    loss, grad_x = device_step(x[0], loss_target[0], w)
    loss = lax.psum(loss[0, 0], ("x", "y", "c"))

    out = {}
    for k in sorted(BIG, key=lambda k: (w.grads_left[k], -p[k].size)):
        w.need_grad(k)
        ex = w.host_bwd(p[k].size * ADAMW_BYTES_PER_ELEMENT / HBM_BYTES_PER_US, exclude=(k,))
        out[k], bufs = adamw(p[k], pm[k], pv[k], w.recv[k], f"adamw_{k}", ex)
        w.recv.update(bufs)
    w.need_grad("small")
    w.need_grad("repl")
    recv = w.recv
    sm, _ = adamw(*[_pack_small_shard(*[d[k] for k in SMALL_SHARDED])[None] for d in (p, pm, pv)],
                  recv["small"][:, None], "adamw_small")
    for k, vals in zip(SMALL_SHARDED, zip(*[_unpack_small_shard(r[0]) for r in sm])):
        out[k] = list(vals)
    rp, _ = adamw(*[_pack([d[k] for k in REPL], repl_rows)[None] for d in (p, pm, pv)],
                  recv["repl"][:, None], "adamw_repl")
    for k, vals in zip(REPL, zip(*[_unpack(r[0], repl_shapes) for r in rp])):
        out[k] = list(vals)
    res = [loss, grad_x[None]]
    for q in range(4):
        res += [out[k][q] for k in WEIGHTS]
    return tuple(res)
```
